```python
import jax
import jax.numpy as jnp
from jax import lax

D_MODEL = 1024
BATCH = 4
SEQ = 8192
DEPTH = 2

MEM_LEN = 256
GRID_W = 64
Q_BLOCK = 128
ROPE_THETA = 500000.0
NORM_EPS = 1e-6

MLA_HEADS = 8
MLA_Q_LORA = 384
MLA_KV_LORA = 256
MLA_NOPE_DIM = 64
MLA_ROPE_DIM = 32
MLA_V_DIM = 64

GQA_Q_HEADS = 8
GQA_KV_HEADS = 2
GQA_GROUP = GQA_Q_HEADS // GQA_KV_HEADS
GQA_HEAD_DIM = 64

DIFF_HEADS = 8
DIFF_QK_DIM = 32
DIFF_V_DIM = 2 * DIFF_QK_DIM
DIFF_ROPE_DIM = DIFF_QK_DIM // 4

N_BRANCHES = 3
BRANCH_WIDTH = 512

MEM_HEADS = 4
MEM_HEAD_DIM = D_MODEL // MEM_HEADS

D_FF = 2816
N_EXPERTS = 8
TOP_K = 2

IN_SPLITS = (MLA_Q_LORA, MLA_KV_LORA, MLA_ROPE_DIM,
             GQA_Q_HEADS * GQA_HEAD_DIM, GQA_KV_HEADS * GQA_HEAD_DIM, GQA_KV_HEADS * GQA_HEAD_DIM,
             DIFF_HEADS * 2 * DIFF_QK_DIM, DIFF_HEADS * 2 * DIFF_QK_DIM, DIFF_HEADS * DIFF_V_DIM,
             N_BRANCHES * D_MODEL)
IN_COLS = sum(IN_SPLITS)

kernel_name = 'hybrid_gated_mla_gqa_diff_moe_encoder'


def rms_norm(x, g):
    xf = x.astype(jnp.float32)
    y = xf * lax.rsqrt(jnp.mean(xf * xf, axis=-1, keepdims=True) + NORM_EPS)
    return (y * g.astype(jnp.float32)).astype(x.dtype)


def rope_cos_sin(pos, rot_dim):
    inv = ROPE_THETA ** (-jnp.arange(0, rot_dim, 2, dtype=jnp.float32) / rot_dim)
    ang = pos.astype(jnp.float32)[:, None] * inv[None, :]
    return jnp.cos(ang), jnp.sin(ang)


def apply_rotary(x, cos, sin):
    cos = cos.astype(x.dtype)
    sin = sin.astype(x.dtype)
    x1, x2 = jnp.split(x, 2, axis=-1)
    return jnp.concatenate([x1 * cos - x2 * sin, x2 * cos + x1 * sin], axis=-1)


def split_cols(z, sizes):
    out, start = [], 0
    for n in sizes:
        out.append(z[..., start:start + n])
        start += n
    return out


def blocked_attention(q, k, v, scale):
    B, S = q.shape[0], q.shape[1]
    nb = S // Q_BLOCK
    qb = q.reshape(B, nb, Q_BLOCK, *q.shape[2:]).swapaxes(0, 1)

    def one_block(q_blk):
        s = jnp.einsum('bqgrd,bkgd->bgrqk', q_blk, k, preferred_element_type=jnp.float32) * scale
        p = jax.nn.softmax(s, axis=-1).astype(v.dtype)
        return jnp.einsum('bgrqk,bkgv->bqgrv', p, v)

    out = lax.map(one_block, qb)
    return out.swapaxes(0, 1).reshape(B, S, *out.shape[3:])


def blocked_diff_attention(q, k, v, lam, scale):
    B, S = q.shape[0], q.shape[1]
    nb = S // Q_BLOCK
    qb = q.reshape(B, nb, Q_BLOCK, *q.shape[2:]).swapaxes(0, 1)

    def one_block(q_blk):
        s = jnp.einsum('bqhcd,bkhcd->bhcqk', q_blk, k, preferred_element_type=jnp.float32) * scale
        p = jax.nn.softmax(s, axis=-1)
        a = (p[:, :, 0] - lam * p[:, :, 1]).astype(v.dtype)
        return jnp.einsum('bhqk,bkhv->bqhv', a, v)

    out = lax.map(one_block, qb)
    return out.swapaxes(0, 1).reshape(B, S, *out.shape[3:])


def mla_branch(c_q, c_kv, k_rope, q_norm_g, w_q_up, kv_norm_g, w_kv_up, cos, sin):
    B, S, _ = c_q.shape
    q = (rms_norm(c_q, q_norm_g) @ w_q_up).reshape(B, S, MLA_HEADS, MLA_NOPE_DIM + MLA_ROPE_DIM)
    kv = (rms_norm(c_kv, kv_norm_g) @ w_kv_up).reshape(B, S, MLA_HEADS, MLA_NOPE_DIM + MLA_V_DIM)
    q_nope, q_rot = q[..., :MLA_NOPE_DIM], q[..., MLA_NOPE_DIM:]
    k_nope, v = kv[..., :MLA_NOPE_DIM], kv[..., MLA_NOPE_DIM:]
    q_rot = apply_rotary(q_rot, cos[:, None, :], sin[:, None, :])
    k_rot = apply_rotary(k_rope, cos, sin)
    k_rot = jnp.broadcast_to(k_rot[:, :, None, :], (B, S, MLA_HEADS, MLA_ROPE_DIM))
    q_full = jnp.concatenate([q_nope, q_rot], axis=-1)[:, :, :, None, :]
    k_full = jnp.concatenate([k_nope, k_rot], axis=-1)
    o = blocked_attention(q_full, k_full, v, (MLA_NOPE_DIM + MLA_ROPE_DIM) ** -0.5)
    return o.reshape(B, S, MLA_HEADS * MLA_V_DIM)


def gqa_branch(q, k, v, q_norm_g, k_norm_g, cos_ax, sin_ax):
    B, S, _ = q.shape
    q = rms_norm(q.reshape(B, S, GQA_Q_HEADS, GQA_HEAD_DIM), q_norm_g)
    k = rms_norm(k.reshape(B, S, GQA_KV_HEADS, GQA_HEAD_DIM), k_norm_g)
    v = v.reshape(B, S, GQA_KV_HEADS, GQA_HEAD_DIM)
    q = apply_rotary(q, cos_ax[:, None, :], sin_ax[:, None, :])
    k = apply_rotary(k, cos_ax[:, None, :], sin_ax[:, None, :])
    q = q.reshape(B, S, GQA_KV_HEADS, GQA_GROUP, GQA_HEAD_DIM)
    o = blocked_attention(q, k, v, GQA_HEAD_DIM ** -0.5)
    return o.reshape(B, S, GQA_Q_HEADS * GQA_HEAD_DIM)


def diff_branch(q, k, v, lq1, lk1, lq2, lk2, subln_g, lambda_init, cos_p, sin_p):
    B, S, _ = q.shape
    q = q.reshape(B, S, DIFF_HEADS, 2, DIFF_QK_DIM)
    k = k.reshape(B, S, DIFF_HEADS, 2, DIFF_QK_DIM)
    v = v.reshape(B, S, DIFF_HEADS, DIFF_V_DIM)
    c4, s4 = cos_p[:, None, None, :], sin_p[:, None, None, :]
    q = jnp.concatenate([apply_rotary(q[..., :DIFF_ROPE_DIM], c4, s4), q[..., DIFF_ROPE_DIM:]], axis=-1)
    k = jnp.concatenate([apply_rotary(k[..., :DIFF_ROPE_DIM], c4, s4), k[..., DIFF_ROPE_DIM:]], axis=-1)
    f32 = jnp.float32
    lam = (jnp.exp(jnp.sum(lq1.astype(f32) * lk1.astype(f32)))
           - jnp.exp(jnp.sum(lq2.astype(f32) * lk2.astype(f32))) + lambda_init)
    o = blocked_diff_attention(q, k, v, lam, DIFF_QK_DIM ** -0.5)
    o = rms_norm(o, subln_g) * (1.0 - lambda_init)
    return o.reshape(B, S, DIFF_HEADS * DIFF_V_DIM)


def hybrid_mixer(h, w_in, mla_q_norm_g, mla_w_q_up, mla_kv_norm_g, mla_w_kv_up,
                 gqa_q_norm_g, gqa_k_norm_g, lq1, lk1, lq2, lk2, diff_subln_g,
                 w_branch, w_out, lambda_init, rope):
    cos_m, sin_m, cos_ax, sin_ax, cos_p, sin_p = rope
    (c_q, c_kv, k_rope, g_q, g_k, g_v, d_q, d_k, d_v, gate_pre) = split_cols(h @ w_in, IN_SPLITS)
    a = mla_branch(c_q, c_kv, k_rope, mla_q_norm_g, mla_w_q_up, mla_kv_norm_g, mla_w_kv_up, cos_m, sin_m)
    b = gqa_branch(g_q, g_k, g_v, gqa_q_norm_g, gqa_k_norm_g, cos_ax, sin_ax)
    c = diff_branch(d_q, d_k, d_v, lq1, lk1, lq2, lk2, diff_subln_g, lambda_init, cos_p, sin_p)
    branches = jnp.stack([a, b, c], axis=2)
    proj = jnp.einsum('bsnw,nwd->bsnd', branches, w_branch)
    gates = jax.nn.sigmoid(gate_pre.reshape(*gate_pre.shape[:2], N_BRANCHES, D_MODEL))
    merged = jnp.sum(gates * proj, axis=2)
    return merged @ w_out


def memory_cross_attention(h, m, wq, wkv, wo):
    B, S, _ = h.shape
    M = m.shape[1]
    q = (h @ wq).reshape(B, S, MEM_HEADS, MEM_HEAD_DIM)
    kv = (m @ wkv).reshape(B, M, 2, MEM_HEADS, MEM_HEAD_DIM)
    s = jnp.einsum('bshd,bmhd->bhsm', q, kv[:, :, 0], preferred_element_type=jnp.float32) * MEM_HEAD_DIM ** -0.5
    p = jax.nn.softmax(s, axis=-1).astype(h.dtype)
    o = jnp.einsum('bhsm,bmhd->bshd', p, kv[:, :, 1]).reshape(B, S, MEM_HEADS * MEM_HEAD_DIM)
    return o @ wo


def swiglu(h, w_gate, w_up, w_down):
    return (jax.nn.silu(h @ w_gate) * (h @ w_up)) @ w_down


def moe_swiglu(h, w_router, b_router, w_gate, w_up, w_down):
    logits = jnp.einsum('bsd,de->bse', h, w_router, preferred_element_type=jnp.float32) + b_router.astype(jnp.float32)
    top_val, top_idx = lax.top_k(logits, TOP_K)
    top_w = jax.nn.softmax(top_val, axis=-1)
    gates = jnp.einsum('bsk,bske->bse', top_w, jax.nn.one_hot(top_idx, N_EXPERTS, dtype=jnp.float32)).astype(h.dtype)
    y = jnp.zeros_like(h)
    for e in range(N_EXPERTS):
        y = y + gates[..., e:e + 1] * swiglu(h, w_gate[e], w_up[e], w_down[e])
    return y


def setup_inputs(seed: int = 0) -> dict:
    key = jax.random.key(seed)
    keys = iter(jax.random.split(key, 64))
    L = DEPTH
    n_dense = (DEPTH + 1) // 2
    n_moe = DEPTH // 2

    def normal(shape, scale):
        return jax.random.normal(next(keys), shape, jnp.float32) * scale

    def gain(shape):
        return 1.0 + normal(shape, 0.02)

    return {
        'x': normal((BATCH, SEQ, D_MODEL), 1.0),
        'mem': normal((BATCH, MEM_LEN, D_MODEL), 1.0),
        'mix_pre_g': gain((L, D_MODEL)),
        'mix_post_g': gain((L, D_MODEL)),
        'w_in': normal((L, D_MODEL, IN_COLS), D_MODEL ** -0.5),
        'mla_q_norm_g': gain((L, MLA_Q_LORA)),
        'mla_w_q_up': normal((L, MLA_Q_LORA, MLA_HEADS * (MLA_NOPE_DIM + MLA_ROPE_DIM)), MLA_Q_LORA ** -0.5),
        'mla_kv_norm_g': gain((L, MLA_KV_LORA)),
        'mla_w_kv_up': normal((L, MLA_KV_LORA, MLA_HEADS * (MLA_NOPE_DIM + MLA_V_DIM)), MLA_KV_LORA ** -0.5),
        'gqa_q_norm_g': gain((L, GQA_HEAD_DIM)),
        'gqa_k_norm_g': gain((L, GQA_HEAD_DIM)),
        'diff_lambda_q1': normal((L, DIFF_QK_DIM), 0.1),
        'diff_lambda_k1': normal((L, DIFF_QK_DIM), 0.1),
        'diff_lambda_q2': normal((L, DIFF_QK_DIM), 0.1),
        'diff_lambda_k2': normal((L, DIFF_QK_DIM), 0.1),
        'diff_subln_g': gain((L, DIFF_V_DIM)),
        'w_branch': normal((L, N_BRANCHES, BRANCH_WIDTH, D_MODEL), BRANCH_WIDTH ** -0.5),
        'w_out': normal((L, D_MODEL, D_MODEL), D_MODEL ** -0.5),
        'mem_pre_g': gain((L, D_MODEL)),
        'mem_post_g': gain((L, D_MODEL)),
        'mem_norm_g': gain((L, D_MODEL)),
        'mem_wq': normal((L, D_MODEL, MEM_HEADS * MEM_HEAD_DIM), D_MODEL ** -0.5),
        'mem_wkv': normal((L, D_MODEL, 2 * MEM_HEADS * MEM_HEAD_DIM), D_MODEL ** -0.5),
        'mem_wo': normal((L, MEM_HEADS * MEM_HEAD_DIM, D_MODEL), (MEM_HEADS * MEM_HEAD_DIM) ** -0.5),
        'ffn_pre_g': gain((L, D_MODEL)),
        'ffn_post_g': gain((L, D_MODEL)),
        'dense_w_gate': normal((n_dense, D_MODEL, D_FF), D_MODEL ** -0.5),
        'dense_w_up': normal((n_dense, D_MODEL, D_FF), D_MODEL ** -0.5),
        'dense_w_down': normal((n_dense, D_FF, D_MODEL), D_FF ** -0.5),
        'moe_w_router': normal((n_moe, D_MODEL, N_EXPERTS), D_MODEL ** -0.5),
        'moe_b_router': normal((n_moe, N_EXPERTS), 0.01),
        'moe_w_gate': normal((n_moe, N_EXPERTS, D_MODEL, D_FF), D_MODEL ** -0.5),
        'moe_w_up': normal((n_moe, N_EXPERTS, D_MODEL, D_FF), D_MODEL ** -0.5),
        'moe_w_down': normal((n_moe, N_EXPERTS, D_FF, D_MODEL), D_FF ** -0.5),
    }


def reference(x, mem, mix_pre_g, mix_post_g, w_in, mla_q_norm_g, mla_w_q_up, mla_kv_norm_g,
              mla_w_kv_up, gqa_q_norm_g, gqa_k_norm_g, diff_lambda_q1, diff_lambda_k1,
              diff_lambda_q2, diff_lambda_k2, diff_subln_g, w_branch, w_out, mem_pre_g,
              mem_post_g, mem_norm_g, mem_wq, mem_wkv, mem_wo, ffn_pre_g, ffn_post_g,
              dense_w_gate, dense_w_up, dense_w_down, moe_w_router, moe_b_router,
              moe_w_gate, moe_w_up, moe_w_down):
    S = x.shape[1]
    rows = S // GRID_W
    pos = jnp.arange(S)
    row = jnp.repeat(jnp.arange(rows), GRID_W)
    col = jnp.tile(jnp.arange(GRID_W), rows)
    cos_m, sin_m = rope_cos_sin(pos, MLA_ROPE_DIM)
    cos_r, sin_r = rope_cos_sin(row, GQA_HEAD_DIM // 2)
    cos_c, sin_c = rope_cos_sin(col, GQA_HEAD_DIM // 2)
    cos_ax = jnp.concatenate([cos_r, cos_c], axis=-1)
    sin_ax = jnp.concatenate([sin_r, sin_c], axis=-1)
    cos_p, sin_p = rope_cos_sin(pos, DIFF_ROPE_DIM)
    rope = (cos_m, sin_m, cos_ax, sin_ax, cos_p, sin_p)

    for layer in range(DEPTH):
        lambda_init = 0.8 - 0.6 * float(jnp.exp(-0.3 * layer)) if False else 0.8 - 0.6 * (2.718281828459045 ** (-0.3 * layer))
        h = rms_norm(x, mix_pre_g[layer])
        y = hybrid_mixer(h, w_in[layer], mla_q_norm_g[layer], mla_w_q_up[layer], mla_kv_norm_g[layer],
                         mla_w_kv_up[layer], gqa_q_norm_g[layer], gqa_k_norm_g[layer],
                         diff_lambda_q1[layer], diff_lambda_k1[layer], diff_lambda_q2[layer],
                         diff_lambda_k2[layer], diff_subln_g[layer], w_branch[layer], w_out[layer],
                         lambda_init, rope)
        x = x + rms_norm(y, mix_post_g[layer])

        h = rms_norm(x, mem_pre_g[layer])
        m = rms_norm(mem, mem_norm_g[layer])
        y = memory_cross_attention(h, m, mem_wq[layer], mem_wkv[layer], mem_wo[layer])
        x = x + rms_norm(y, mem_post_g[layer])

        h = rms_norm(x, ffn_pre_g[layer])
        if layer % 2 == 0:
            i = layer // 2
            y = swiglu(h, dense_w_gate[i], dense_w_up[i], dense_w_down[i])
        else:
            i = layer // 2
            y = moe_swiglu(h, moe_w_router[i], moe_b_router[i], moe_w_gate[i], moe_w_up[i], moe_w_down[i])
        x = x + rms_norm(y, ffn_post_g[layer])
    return x
```

```python
import functools
import math

import numpy as np
import jax
import jax.numpy as jnp
from jax import lax
from jax.experimental import pallas as pl
from jax.experimental.pallas import tpu as pltpu

F32 = jnp.float32
BF16 = jnp.bfloat16

D_MODEL = 1024
GRID_W = 64
ROPE_THETA = 500000.0
NORM_EPS = 1e-6

MLA_HEADS = 8
MLA_Q_LORA = 384
MLA_KV_LORA = 256
MLA_NOPE_DIM = 64
MLA_ROPE_DIM = 32
MLA_V_DIM = 64

GQA_Q_HEADS = 8
GQA_KV_HEADS = 2
GQA_GROUP = GQA_Q_HEADS // GQA_KV_HEADS
GQA_HEAD_DIM = 64

DIFF_HEADS = 8
DIFF_QK_DIM = 32
DIFF_V_DIM = 2 * DIFF_QK_DIM
DIFF_ROPE_DIM = DIFF_QK_DIM // 4

N_BRANCHES = 3
BRANCH_WIDTH = 512
HEAD_V = 64

MEM_HEADS = 4
MEM_HEAD_DIM = D_MODEL // MEM_HEADS

D_FF = 2816
N_EXPERTS = 8
TOP_K = 2

IN_SPLITS = (MLA_Q_LORA, MLA_KV_LORA, MLA_ROPE_DIM,
             GQA_Q_HEADS * GQA_HEAD_DIM, GQA_KV_HEADS * GQA_HEAD_DIM, GQA_KV_HEADS * GQA_HEAD_DIM,
             DIFF_HEADS * 2 * DIFF_QK_DIM, DIFF_HEADS * 2 * DIFF_QK_DIM, DIFF_HEADS * DIFF_V_DIM,
             N_BRANCHES * D_MODEL)
IN_OFFS = tuple(int(v) for v in np.cumsum((0,) + IN_SPLITS))
N_FEAT = IN_OFFS[9]
N_GATE = IN_SPLITS[9]

LOG2E = 1.4426950408889634
QK_PAD = 128
ONES_ROWS = 16

V7X_VMEM_LIMIT_BYTES = 56 * 1024 * 1024

PREP_TM = 256
ATTN_TQ = 512
ATTN_TK = 512
MERGE_TM = 256
MEM_TM = 512
FFN_TM = 512
FFN_TF = 1408
MOE_TM = 1024
KV_TM = 256


def _cparams(sem):
    return pltpu.CompilerParams(dimension_semantics=sem, vmem_limit_bytes=V7X_VMEM_LIMIT_BYTES)


def _const_spec(shape):
    nd = len(shape)
    return pl.BlockSpec(shape, lambda *_: (0,) * nd, pipeline_mode=pl.Buffered(1))


def _rms_rows(x, g_row):
    ms = jnp.mean(x * x, axis=-1, keepdims=True)
    return x * lax.rsqrt(ms + NORM_EPS) * g_row


def _rms_cols(x, g_col):
    ms = jnp.mean(x * x, axis=0, keepdims=True)
    return x * lax.rsqrt(ms + NORM_EPS) * g_col


def _dot(a, b):
    return jnp.dot(a, b, preferred_element_type=F32)


def _dot_nt(a, b):
    return lax.dot_general(a, b, (((1,), (1,)), ((), ())), preferred_element_type=F32)


_S_CQ, _S_CKV, _S_KR, _S_GQ, _S_GK, _S_GV, _S_DQ, _S_DK, _S_DV = IN_OFFS[:9]


def _rope_rows(x1, x2, c, s):
    return x1 * c - x2 * s, x2 * c + x1 * s


def _mixer_prep_kernel(x_ref, g_ref, wt_ref, wg_ref, wqu_ref, wkvu_ref,
                       mqg_ref, mkvg_ref, gqg_ref, gkg_ref,
                       cm_ref, sm_ref, ca_ref, sa_ref, cp_ref, sp_ref,
                       qm_ref, km_ref, vm_ref, qg_ref, kg_ref, vg_ref,
                       qd_ref, kd_ref, vd_ref, gate_ref):
    tm = x_ref.shape[1]
    hn = _rms_rows(x_ref[0], g_ref[...]).astype(BF16)

    def feat(lo, n):
        return _dot_nt(wt_ref[lo:lo + n, :], hn)

    gchunk = 1024
    for j in range(N_GATE // gchunk):
        z = _dot(hn, wg_ref[:, j * gchunk:(j + 1) * gchunk])
        gate_ref[0, :, j * gchunk:(j + 1) * gchunk] = jax.nn.sigmoid(z).astype(gate_ref.dtype)

    zeros32 = jnp.zeros((32, tm), F32)
    zeros64 = jnp.zeros((64, tm), F32)

    c_mla = (MLA_NOPE_DIM + MLA_ROPE_DIM) ** -0.5 * LOG2E
    cm = cm_ref[...]
    sm = sm_ref[...]
    cqn = _rms_cols(feat(_S_CQ, MLA_Q_LORA), mqg_ref[...]).astype(BF16)
    q_all = _dot(wqu_ref[...], cqn)
    q_nope = q_all[0:512] * c_mla
    q_r1, q_r2 = _rope_rows(q_all[512:640], q_all[640:768], cm, sm)
    q_r1 = q_r1 * c_mla
    q_r2 = q_r2 * c_mla
    for h in range(MLA_HEADS):
        qm_ref[0, h] = jnp.concatenate(
            [q_nope[64 * h:64 * h + 64], q_r1[16 * h:16 * h + 16], q_r2[16 * h:16 * h + 16], zeros32],
            axis=0).astype(qm_ref.dtype)

    ckvn = _rms_cols(feat(_S_CKV, MLA_KV_LORA), mkvg_ref[...]).astype(BF16)
    kv_all = _dot(wkvu_ref[...], ckvn)
    vm_ref[0] = kv_all[512:1024].astype(vm_ref.dtype)
    kr = feat(_S_KR, MLA_ROPE_DIM)
    k_r1, k_r2 = _rope_rows(kr[0:16], kr[16:32], cm[0:16], sm[0:16])
    for h in range(MLA_HEADS):
        kt = jnp.concatenate([kv_all[64 * h:64 * h + 64], k_r1, k_r2, zeros32], axis=0)
        km_ref[0, :, 128 * h:128 * h + 128] = kt.T.astype(km_ref.dtype)

    c_gqa = GQA_HEAD_DIM ** -0.5 * LOG2E
    ca = ca_ref[...]
    sa = sa_ref[...]
    gq = feat(_S_GQ, GQA_Q_HEADS * GQA_HEAD_DIM)
    for h in range(GQA_Q_HEADS):
        qn = _rms_cols(gq[64 * h:64 * h + 64], gqg_ref[...])
        r1, r2 = _rope_rows(qn[0:32], qn[32:64], ca, sa)
        q64 = jnp.concatenate([r1, r2], axis=0) * c_gqa
        parts = [q64, zeros64] if h // GQA_GROUP == 0 else [zeros64, q64]
        qg_ref[0, h] = jnp.concatenate(parts, axis=0).astype(qg_ref.dtype)
    gk = feat(_S_GK, GQA_KV_HEADS * GQA_HEAD_DIM)
    kparts = []
    for g in range(GQA_KV_HEADS):
        kn = _rms_cols(gk[64 * g:64 * g + 64], gkg_ref[...])
        r1, r2 = _rope_rows(kn[0:32], kn[32:64], ca, sa)
        kparts += [r1, r2]
    kg_ref[0] = jnp.concatenate(kparts, axis=0).T.astype(kg_ref.dtype)
    vg_ref[0] = feat(_S_GV, GQA_KV_HEADS * GQA_HEAD_DIM).astype(vg_ref.dtype)

    c_diff = DIFF_QK_DIM ** -0.5 * LOG2E
    cp = cp_ref[...]
    sp = sp_ref[...]
    row = lax.broadcasted_iota(jnp.int32, (64, 1), 0)
    in_c0 = (row < 4) | ((row >= 8) & (row < 12)) | ((row >= 16) & (row < 40))

    def diff_heads(lo):
        z = feat(lo, DIFF_HEADS * 2 * DIFF_QK_DIM)
        r1, r2 = _rope_rows(z[0:64], z[64:128], cp, sp)
        rest = z[128:512]
        return [jnp.concatenate([r1[8 * h:8 * h + 8], r2[8 * h:8 * h + 8], rest[48 * h:48 * h + 48]], axis=0)
                for h in range(DIFF_HEADS)]

    for h, q64 in enumerate(diff_heads(_S_DQ)):
        q64 = q64 * c_diff
        for c in range(2):
            qc = jnp.where(in_c0 if c == 0 else jnp.logical_not(in_c0), q64, 0.0)
            parts = [qc, zeros64] if h % 2 == 0 else [zeros64, qc]
            qd_ref[0, 2 * h + c] = jnp.concatenate(parts, axis=0).astype(qd_ref.dtype)
    k_heads = diff_heads(_S_DK)
    for p in range(DIFF_HEADS // 2):
        kt = jnp.concatenate([k_heads[2 * p], k_heads[2 * p + 1]], axis=0)
        kd_ref[0, :, 128 * p:128 * p + 128] = kt.T.astype(kd_ref.dtype)
    vd_ref[0] = feat(_S_DV, DIFF_HEADS * DIFF_V_DIM).astype(vd_ref.dtype)


def _mixer_prep(x, g, wt, wg, wqu, wkvu, mqg, mkvg, gqg, gkg, tables):
    B, S, D = x.shape
    tm = min(PREP_TM, S)
    cm, sm, ca, sa, cp, sp = tables
    grid = (B, S // tm)

    def tok(width):
        return pl.BlockSpec((1, tm, width), lambda b, i: (b, i, 0))

    def featm(rows):
        return pl.BlockSpec((1, rows, tm), lambda b, i: (b, 0, i))

    def heads(n):
        return pl.BlockSpec((1, n, QK_PAD, tm), lambda b, i: (b, 0, 0, i))

    def table(rows):
        return pl.BlockSpec((rows, tm), lambda b, i: (0, i))

    in_specs = [
        pl.BlockSpec((1, tm, D), lambda b, i: (b, i, 0)),
        _const_spec((1, D)),
        _const_spec(wt.shape), _const_spec(wg.shape), _const_spec(wqu.shape), _const_spec(wkvu.shape),
        _const_spec(mqg.shape), _const_spec(mkvg.shape), _const_spec(gqg.shape), _const_spec(gkg.shape),
        table(128), table(128), table(32), table(32), table(64), table(64),
    ]
    out_shape = [
        jax.ShapeDtypeStruct((B, MLA_HEADS, QK_PAD, S), BF16),
        jax.ShapeDtypeStruct((B, S, MLA_HEADS * QK_PAD), BF16),
        jax.ShapeDtypeStruct((B, MLA_HEADS * HEAD_V, S), BF16),
        jax.ShapeDtypeStruct((B, GQA_Q_HEADS, QK_PAD, S), BF16),
        jax.ShapeDtypeStruct((B, S, QK_PAD), BF16),
        jax.ShapeDtypeStruct((B, GQA_KV_HEADS * HEAD_V, S), BF16),
        jax.ShapeDtypeStruct((B, 2 * DIFF_HEADS, QK_PAD, S), BF16),
        jax.ShapeDtypeStruct((B, S, DIFF_HEADS // 2 * QK_PAD), BF16),
        jax.ShapeDtypeStruct((B, DIFF_HEADS * HEAD_V, S), BF16),
        jax.ShapeDtypeStruct((B, S, N_GATE), BF16),
    ]
    out_specs = [
        heads(MLA_HEADS), tok(MLA_HEADS * QK_PAD), featm(MLA_HEADS * HEAD_V),
        heads(GQA_Q_HEADS), tok(QK_PAD), featm(GQA_KV_HEADS * HEAD_V),
        heads(2 * DIFF_HEADS), tok(DIFF_HEADS // 2 * QK_PAD), featm(DIFF_HEADS * HEAD_V),
        tok(N_GATE),
    ]
    return pl.pallas_call(
        _mixer_prep_kernel, grid=grid, in_specs=in_specs, out_specs=out_specs, out_shape=out_shape,
        compiler_params=_cparams(("parallel", "parallel")), name="mixer_prep",
    )(x, g, wt, wg, wqu, wkvu, mqg, mkvg, gqg, gkg, cm, sm, ca, sa, cp, sp)


def _softmax_pv_step(s_t, v_aug, m, acc):
    m_new = jnp.maximum(m, jnp.max(s_t, axis=0, keepdims=True))
    alpha = jnp.exp2(m - m_new)
    p = jnp.exp2(s_t - m_new).astype(BF16)
    return m_new, acc * alpha + _dot(v_aug, p)


def _attn_kernel(q_ref, k_ref, v_ref, o_ref, *, tk):
    S = k_ref.shape[1]
    tq = q_ref.shape[3]
    q_t = q_ref[0, 0]
    ones = jnp.ones((ONES_ROWS, tk), BF16)

    def body(c, carry):
        m, acc = carry
        start = pl.multiple_of(c * tk, tk)
        s_t = _dot(k_ref[0, pl.ds(start, tk), :], q_t)
        v_aug = jnp.concatenate([v_ref[0, :, pl.ds(start, tk)], ones], axis=0)
        return _softmax_pv_step(s_t, v_aug, m, acc)

    m0 = jnp.full((1, tq), -jnp.inf, F32)
    acc0 = jnp.zeros((HEAD_V + ONES_ROWS, tq), F32)
    _, acc = lax.fori_loop(0, S // tk, body, (m0, acc0))
    o_ref[0] = (acc[:HEAD_V] / acc[HEAD_V:HEAD_V + 1]).astype(o_ref.dtype)


def _attention(q_t, k, v_t, *, kv_of_head, kcol_of_head, name):
    B, H, _, S = q_t.shape
    tq = min(ATTN_TQ, S)
    tk = min(ATTN_TK, S)
    return pl.pallas_call(
        functools.partial(_attn_kernel, tk=tk),
        grid=(B, H, S // tq),
        in_specs=[
            pl.BlockSpec((1, 1, QK_PAD, tq), lambda b, h, i: (b, h, 0, i)),
            pl.BlockSpec((1, S, QK_PAD), lambda b, h, i: (b, 0, kcol_of_head(h))),
            pl.BlockSpec((1, HEAD_V, S), lambda b, h, i: (b, kv_of_head(h), 0)),
        ],
        out_specs=pl.BlockSpec((1, HEAD_V, tq), lambda b, h, i: (b, h, i)),
        out_shape=jax.ShapeDtypeStruct((B, H * HEAD_V, S), F32),
        compiler_params=_cparams(("parallel", "parallel", "arbitrary")), name=name,
    )(q_t, k, v_t)


def _diff_attn_kernel(lq1_ref, lk1_ref, lq2_ref, lk2_ref, g_ref, q_ref, k_ref, v_ref, o_ref, *,
                      tk, lambda_init):
    S = k_ref.shape[1]
    tq = q_ref.shape[3]
    q1_t = q_ref[0, 0]
    q2_t = q_ref[0, 1]
    ones = jnp.ones((ONES_ROWS, tk), BF16)

    def body(c, carry):
        m1, acc1, m2, acc2 = carry
        start = pl.multiple_of(c * tk, tk)
        kc = k_ref[0, pl.ds(start, tk), :]
        v_aug = jnp.concatenate([v_ref[0, :, pl.ds(start, tk)], ones], axis=0)
        m1, acc1 = _softmax_pv_step(_dot(kc, q1_t), v_aug, m1, acc1)
        m2, acc2 = _softmax_pv_step(_dot(kc, q2_t), v_aug, m2, acc2)
        return m1, acc1, m2, acc2

    m0 = jnp.full((1, tq), -jnp.inf, F32)
    acc0 = jnp.zeros((HEAD_V + ONES_ROWS, tq), F32)
    _, acc1, _, acc2 = lax.fori_loop(0, S // tk, body, (m0, acc0, m0, acc0))
    lam = (jnp.exp(jnp.sum(lq1_ref[...] * lk1_ref[...], axis=-1, keepdims=True))
           - jnp.exp(jnp.sum(lq2_ref[...] * lk2_ref[...], axis=-1, keepdims=True)) + lambda_init)
    o = acc1[:HEAD_V] / acc1[HEAD_V:HEAD_V + 1] - lam * (acc2[:HEAD_V] / acc2[HEAD_V:HEAD_V + 1])
    o = _rms_cols(o, g_ref[...]) * (1.0 - lambda_init)
    o_ref[0] = o.astype(o_ref.dtype)


def _diff_attention(q_t, k, v_t, lq1, lk1, lq2, lk2, subln_g, lambda_init):
    B, H2, _, S = q_t.shape
    H = H2 // 2
    tq = min(ATTN_TQ, S)
    tk = min(ATTN_TK, S)
    vec = _const_spec((1, DIFF_QK_DIM))
    return pl.pallas_call(
        functools.partial(_diff_attn_kernel, tk=tk, lambda_init=lambda_init),
        grid=(B, H, S // tq),
        in_specs=[
            vec, vec, vec, vec, _const_spec((DIFF_V_DIM, 1)),
            pl.BlockSpec((1, 2, QK_PAD, tq), lambda b, h, i: (b, h, 0, i)),
            pl.BlockSpec((1, S, QK_PAD), lambda b, h, i: (b, 0, h // 2)),
            pl.BlockSpec((1, HEAD_V, S), lambda b, h, i: (b, h, 0)),
        ],
        out_specs=pl.BlockSpec((1, HEAD_V, tq), lambda b, h, i: (b, h, i)),
        out_shape=jax.ShapeDtypeStruct((B, H * HEAD_V, S), F32),
        compiler_params=_cparams(("parallel", "parallel", "arbitrary")), name="diff_attn",
    )(lq1, lk1, lq2, lk2, subln_g, q_t, k, v_t)


def _mixer_merge_kernel(x_ref, a_ref, b_ref, c_ref, gate_ref, wb_ref, wo_ref, g_ref, o_ref):
    merged = None
    for n, br_ref in enumerate((a_ref, b_ref, c_ref)):
        br = br_ref[0].T.astype(BF16)
        proj = _dot(br, wb_ref[n])
        term = gate_ref[0, :, n * D_MODEL:(n + 1) * D_MODEL].astype(F32) * proj
        merged = term if merged is None else merged + term
    y = _dot(merged.astype(BF16), wo_ref[...])
    o_ref[0] = x_ref[0] + _rms_rows(y, g_ref[...])


def _mixer_merge(x, a_t, b_t, c_t, gates, wb, wo, g):
    B, S, D = x.shape
    tm = min(MERGE_TM, S)
    xspec = pl.BlockSpec((1, tm, D), lambda b, i: (b, i, 0))
    brspec = pl.BlockSpec((1, BRANCH_WIDTH, tm), lambda b, i: (b, 0, i))
    return pl.pallas_call(
        _mixer_merge_kernel, grid=(B, S // tm),
        in_specs=[xspec, brspec, brspec, brspec,
                  pl.BlockSpec((1, tm, N_GATE), lambda b, i: (b, i, 0)),
                  _const_spec(wb.shape), _const_spec(wo.shape), _const_spec((1, D))],
        out_specs=xspec, out_shape=jax.ShapeDtypeStruct(x.shape, x.dtype),
        compiler_params=_cparams(("parallel", "parallel")), name="mixer_merge",
    )(x, a_t, b_t, c_t, gates, wb, wo, g)


def _norm_matmul_kernel(x_ref, g_ref, w_ref, o_ref):
    hn = _rms_rows(x_ref[...], g_ref[...]).astype(BF16)
    o_ref[...] = _dot(hn, w_ref[...]).astype(o_ref.dtype)


def _norm_matmul(x2d, g, w, out_dtype):
    T, D = x2d.shape
    N = w.shape[1]
    tm = min(KV_TM, T)
    return pl.pallas_call(
        _norm_matmul_kernel, grid=(T // tm,),
        in_specs=[pl.BlockSpec((tm, D), lambda i: (i, 0)), _const_spec((1, D)), _const_spec(w.shape)],
        out_specs=pl.BlockSpec((tm, N), lambda i: (i, 0)),
        out_shape=jax.ShapeDtypeStruct((T, N), out_dtype),
        compiler_params=_cparams(("parallel",)), name="mem_kv",
    )(x2d, g, w)


def _mem_attn_kernel(x_ref, kv_ref, gpre_ref, wq_ref, wo_ref, gpost_ref, o_ref):
    x = x_ref[0]
    hn = _rms_rows(x, gpre_ref[...]).astype(BF16)
    q = (_dot(hn, wq_ref[...]) * (MEM_HEAD_DIM ** -0.5 * LOG2E)).astype(BF16)
    outs = []
    for h in range(MEM_HEADS):
        lo = h * MEM_HEAD_DIM
        k_h = kv_ref[0, :, lo:lo + MEM_HEAD_DIM]
        v_h = kv_ref[0, :, D_MODEL + lo:D_MODEL + lo + MEM_HEAD_DIM]
        s = _dot_nt(q[:, lo:lo + MEM_HEAD_DIM], k_h)
        e = jnp.exp2(s - jnp.max(s, axis=-1, keepdims=True))
        p = (e / jnp.sum(e, axis=-1, keepdims=True)).astype(BF16)
        outs.append(_dot(p, v_h))
    o = jnp.concatenate(outs, axis=-1).astype(BF16)
    y = _dot(o, wo_ref[...])
    o_ref[0] = x + _rms_rows(y, gpost_ref[...])


def _mem_attention(x, kv, gpre, wq, wo, gpost):
    B, S, D = x.shape
    M = kv.shape[1]
    tm = min(MEM_TM, S)
    xspec = pl.BlockSpec((1, tm, D), lambda b, i: (b, i, 0))
    return pl.pallas_call(
        _mem_attn_kernel, grid=(B, S // tm),
        in_specs=[xspec, pl.BlockSpec((1, M, 2 * D), lambda b, i: (b, 0, 0)),
                  _const_spec((1, D)), _const_spec(wq.shape), _const_spec(wo.shape), _const_spec((1, D))],
        out_specs=xspec, out_shape=jax.ShapeDtypeStruct(x.shape, x.dtype),
        compiler_params=_cparams(("parallel", "parallel")), name="mem_attn",
    )(x, kv, gpre, wq, wo, gpost)


def _swiglu_chunk(hn, wg, wu, wd):
    gate = _dot(hn, wg)
    act = (gate * jax.nn.sigmoid(gate) * _dot(hn, wu)).astype(BF16)
    return _dot(act, wd)


def _ffn_kernel(x_ref, gpre_ref, wg_ref, wu_ref, wd_ref, gpost_ref, o_ref, hn_ref, acc_ref):
    j = pl.program_id(1)

    @pl.when(j == 0)
    def _():
        hn_ref[...] = _rms_rows(x_ref[...], gpre_ref[...]).astype(BF16)

    y = _swiglu_chunk(hn_ref[...], wg_ref[...], wu_ref[...], wd_ref[...])

    @pl.when(j == 0)
    def _():
        acc_ref[...] = y

    @pl.when(j > 0)
    def _():
        acc_ref[...] += y

    @pl.when(j == pl.num_programs(1) - 1)
    def _():
        o_ref[...] = x_ref[...] + _rms_rows(acc_ref[...], gpost_ref[...])


def _ffn(x2d, gpre, wg, wu, wd, gpost):
    T, D = x2d.shape
    F = wg.shape[1]
    tm = min(FFN_TM, T)
    tf = FFN_TF
    xspec = pl.BlockSpec((tm, D), lambda i, j: (i, 0))
    return pl.pallas_call(
        _ffn_kernel, grid=(T // tm, F // tf),
        in_specs=[xspec, _const_spec((1, D)),
                  pl.BlockSpec((D, tf), lambda i, j: (0, j)),
                  pl.BlockSpec((D, tf), lambda i, j: (0, j)),
                  pl.BlockSpec((tf, D), lambda i, j: (j, 0)),
                  _const_spec((1, D))],
        out_specs=xspec, out_shape=jax.ShapeDtypeStruct(x2d.shape, x2d.dtype),
        scratch_shapes=[pltpu.VMEM((tm, D), BF16), pltpu.VMEM((tm, D), F32)],
        compiler_params=_cparams(("parallel", "arbitrary")), name="ffn",
    )(x2d, gpre, wg, wu, wd, gpost)


def _router_kernel(x_ref, gpre_ref, wr_ref, br_ref, o_ref):
    hn = _rms_rows(x_ref[...], gpre_ref[...])
    logits = lax.dot_general(wr_ref[...], hn, (((1,), (1,)), ((), ())),
                             precision=lax.Precision.HIGHEST,
                             preferred_element_type=F32) + br_ref[...]
    e_idx = lax.broadcasted_iota(jnp.int32, logits.shape, 0)
    m1 = jnp.max(logits, axis=0, keepdims=True)
    i1 = jnp.min(jnp.where(logits == m1, e_idx, N_EXPERTS), axis=0, keepdims=True)
    rest = jnp.where(e_idx == i1, -jnp.inf, logits)
    m2 = jnp.max(rest, axis=0, keepdims=True)
    i2 = jnp.min(jnp.where(rest == m2, e_idx, N_EXPERTS), axis=0, keepdims=True)
    e2 = jnp.exp(m2 - m1)
    w1 = 1.0 / (1.0 + e2)
    w2 = e2 / (1.0 + e2)
    o_ref[...] = jnp.where(e_idx == i1, w1, 0.0) + jnp.where(e_idx == i2, w2, 0.0)


def _router(x2d, gpre, wr_t, br_col):
    T, D = x2d.shape
    tm = min(FFN_TM, T)
    return pl.pallas_call(
        _router_kernel, grid=(T // tm,),
        in_specs=[pl.BlockSpec((tm, D), lambda i: (i, 0)), _const_spec((1, D)),
                  _const_spec(wr_t.shape), _const_spec(br_col.shape)],
        out_specs=pl.BlockSpec((N_EXPERTS, tm), lambda i: (0, i)),
        out_shape=jax.ShapeDtypeStruct((N_EXPERTS, T), F32),
        compiler_params=_cparams(("parallel",)), name="router",
    )(x2d, gpre, wr_t, br_col)


def _moe_ffn_kernel(x_ref, gate_ref, gpre_ref, wg_ref, wu_ref, wd_ref, gpost_ref, o_ref, hn_ref, acc_ref):
    e = pl.program_id(1)
    j = pl.program_id(2)
    first = (e == 0) & (j == 0)
    last = (e == pl.num_programs(1) - 1) & (j == pl.num_programs(2) - 1)

    @pl.when(first)
    def _():
        hn_ref[...] = _rms_rows(x_ref[...], gpre_ref[...]).astype(BF16)

    y = gate_ref[0] * _swiglu_chunk(hn_ref[...], wg_ref[0], wu_ref[0], wd_ref[0])

    @pl.when(first)
    def _():
        acc_ref[...] = y

    @pl.when(jnp.logical_not(first))
    def _():
        acc_ref[...] += y

    @pl.when(last)
    def _():
        o_ref[...] = x_ref[...] + _rms_rows(acc_ref[...], gpost_ref[...])


def _moe_ffn(x2d, gates_col, gpre, wg, wu, wd, gpost):
    T, D = x2d.shape
    E, _, F = wg.shape
    tm = min(MOE_TM, T)
    tf = FFN_TF
    xspec = pl.BlockSpec((tm, D), lambda i, e, j: (i, 0))
    return pl.pallas_call(
        _moe_ffn_kernel, grid=(T // tm, E, F // tf),
        in_specs=[xspec, pl.BlockSpec((1, tm, 1), lambda i, e, j: (e, i, 0)), _const_spec((1, D)),
                  pl.BlockSpec((1, D, tf), lambda i, e, j: (e, 0, j)),
                  pl.BlockSpec((1, D, tf), lambda i, e, j: (e, 0, j)),
                  pl.BlockSpec((1, tf, D), lambda i, e, j: (e, j, 0)),
                  _const_spec((1, D))],
        out_specs=xspec, out_shape=jax.ShapeDtypeStruct(x2d.shape, x2d.dtype),
        scratch_shapes=[pltpu.VMEM((tm, D), BF16), pltpu.VMEM((tm, D), F32)],
        compiler_params=_cparams(("parallel", "arbitrary", "arbitrary")), name="moe_ffn",
    )(x2d, gates_col, gpre, wg, wu, wd, gpost)


def _feature_row_order():
    def diff_block(base):
        x1, x2, rest = [], [], []
        half = DIFF_ROPE_DIM // 2
        for h in range(DIFF_HEADS):
            for c in range(2):
                lo = base + h * 2 * DIFF_QK_DIM + c * DIFF_QK_DIM
                x1 += range(lo, lo + half)
                x2 += range(lo + half, lo + DIFF_ROPE_DIM)
                rest += range(lo + DIFF_ROPE_DIM, lo + DIFF_QK_DIM)
        return x1 + x2 + rest

    order = list(range(0, _S_DQ)) + diff_block(_S_DQ) + diff_block(_S_DK) + list(range(_S_DV, N_FEAT))
    return np.asarray(order, np.int32)


def _mla_q_up_order():
    w = MLA_NOPE_DIM + MLA_ROPE_DIM
    half = MLA_ROPE_DIM // 2
    nope = [h * w + j for h in range(MLA_HEADS) for j in range(MLA_NOPE_DIM)]
    x1 = [h * w + MLA_NOPE_DIM + j for h in range(MLA_HEADS) for j in range(half)]
    x2 = [h * w + MLA_NOPE_DIM + half + j for h in range(MLA_HEADS) for j in range(half)]
    return np.asarray(nope + x1 + x2, np.int32)


def _mla_kv_up_order():
    w = MLA_NOPE_DIM + MLA_V_DIM
    k = [h * w + j for h in range(MLA_HEADS) for j in range(MLA_NOPE_DIM)]
    v = [h * w + MLA_NOPE_DIM + j for h in range(MLA_HEADS) for j in range(MLA_V_DIM)]
    return np.asarray(k + v, np.int32)


def _rope_tables(S):
    def cos_sin(pos, rot_dim):
        inv = ROPE_THETA ** (-jnp.arange(0, rot_dim, 2, dtype=F32) / rot_dim)
        ang = pos.astype(F32)[:, None] * inv[None, :]
        return jnp.cos(ang).T, jnp.sin(ang).T

    rows = S // GRID_W
    pos = jnp.arange(S)
    row = jnp.repeat(jnp.arange(rows), GRID_W)
    col = jnp.tile(jnp.arange(GRID_W), rows)
    cm, sm = cos_sin(pos, MLA_ROPE_DIM)
    cr, sr = cos_sin(row, GQA_HEAD_DIM // 2)
    cc, sc = cos_sin(col, GQA_HEAD_DIM // 2)
    cp, sp = cos_sin(pos, DIFF_ROPE_DIM)
    return (jnp.tile(cm, (MLA_HEADS, 1)), jnp.tile(sm, (MLA_HEADS, 1)),
            jnp.concatenate([cr, cc], axis=0), jnp.concatenate([sr, sc], axis=0),
            jnp.tile(cp, (2 * DIFF_HEADS, 1)), jnp.tile(sp, (2 * DIFF_HEADS, 1)))


def _row(v):
    return v.reshape(1, -1).astype(F32)


def _col(v):
    return v.reshape(-1, 1).astype(F32)


def kernel(x, mem, mix_pre_g, mix_post_g, w_in, mla_q_norm_g, mla_w_q_up, mla_kv_norm_g, mla_w_kv_up,
           gqa_q_norm_g, gqa_k_norm_g, diff_lambda_q1, diff_lambda_k1, diff_lambda_q2, diff_lambda_k2,
           diff_subln_g, w_branch, w_out, mem_pre_g, mem_post_g, mem_norm_g, mem_wq, mem_wkv, mem_wo,
           ffn_pre_g, ffn_post_g, dense_w_gate, dense_w_up, dense_w_down, moe_w_router, moe_b_router,
           moe_w_gate, moe_w_up, moe_w_down):
    B, S, D = x.shape
    M = mem.shape[1]
    depth = w_in.shape[0]
    tables = _rope_tables(S)
    feat_order = _feature_row_order()
    q_up_order = _mla_q_up_order()
    kv_up_order = _mla_kv_up_order()

    for layer in range(depth):
        lambda_init = 0.8 - 0.6 * math.exp(-0.3 * layer)

        w_l = w_in[layer]
        wt = w_l[:, feat_order].T.astype(BF16)
        wg = w_l[:, N_FEAT:].astype(BF16)
        wqu = mla_w_q_up[layer][:, q_up_order].T.astype(BF16)
        wkvu = mla_w_kv_up[layer][:, kv_up_order].T.astype(BF16)
        (qm, km, vm, qg, kg, vg, qd, kd, vd, gates) = _mixer_prep(
            x, _row(mix_pre_g[layer]), wt, wg, wqu, wkvu,
            _col(mla_q_norm_g[layer]), _col(mla_kv_norm_g[layer]),
            _col(gqa_q_norm_g[layer]), _col(gqa_k_norm_g[layer]), tables)
        a_t = _attention(qm, km, vm, kv_of_head=lambda h: h, kcol_of_head=lambda h: h, name="mla_attn")
        b_t = _attention(qg, kg, vg, kv_of_head=lambda h: h // GQA_GROUP, kcol_of_head=lambda h: 0,
                         name="gqa_attn")
        c_t = _diff_attention(qd, kd, vd, _row(diff_lambda_q1[layer]), _row(diff_lambda_k1[layer]),
                              _row(diff_lambda_q2[layer]), _row(diff_lambda_k2[layer]),
                              _col(diff_subln_g[layer]), lambda_init)
        x = _mixer_merge(x, a_t, b_t, c_t, gates, w_branch[layer].astype(BF16),
                         w_out[layer].astype(BF16), _row(mix_post_g[layer]))

        kv = _norm_matmul(mem.reshape(B * M, D), _row(mem_norm_g[layer]), mem_wkv[layer].astype(BF16), BF16)
        x = _mem_attention(x, kv.reshape(B, M, 2 * D), _row(mem_pre_g[layer]), mem_wq[layer].astype(BF16),
                           mem_wo[layer].astype(BF16), _row(mem_post_g[layer]))

        x2d = x.reshape(B * S, D)
        i = layer // 2
        if layer % 2 == 0:
            x2d = _ffn(x2d, _row(ffn_pre_g[layer]), dense_w_gate[i].astype(BF16), dense_w_up[i].astype(BF16),
                       dense_w_down[i].astype(BF16), _row(ffn_post_g[layer]))
        else:
            gates_e = _router(x2d, _row(ffn_pre_g[layer]), moe_w_router[i].T.astype(F32),
                              _col(moe_b_router[i]))
            x2d = _moe_ffn(x2d, gates_e.reshape(N_EXPERTS, B * S, 1), _row(ffn_pre_g[layer]),
                           moe_w_gate[i].astype(BF16), moe_w_up[i].astype(BF16), moe_w_down[i].astype(BF16),
                           _row(ffn_post_g[layer]))
        x = x2d.reshape(B, S, D)
    return x
```

```python
import functools
import math

import numpy as np
import jax
import jax.numpy as jnp
from jax import lax
from jax.experimental import pallas as pl
from jax.experimental.pallas import tpu as pltpu

F32 = jnp.float32
BF16 = jnp.bfloat16

D_MODEL = 1024
GRID_W = 64
ROPE_THETA = 500000.0
NORM_EPS = 1e-6

MLA_HEADS = 8
MLA_Q_LORA = 384
MLA_KV_LORA = 256
MLA_NOPE_DIM = 64
MLA_ROPE_DIM = 32
MLA_V_DIM = 64

GQA_Q_HEADS = 8
GQA_KV_HEADS = 2
GQA_GROUP = GQA_Q_HEADS // GQA_KV_HEADS
GQA_HEAD_DIM = 64

DIFF_HEADS = 8
DIFF_QK_DIM = 32
DIFF_V_DIM = 2 * DIFF_QK_DIM
DIFF_ROPE_DIM = DIFF_QK_DIM // 4

N_BRANCHES = 3
BRANCH_WIDTH = 512
HEAD_V = 64

MEM_HEADS = 4
MEM_HEAD_DIM = D_MODEL // MEM_HEADS

D_FF = 2816
N_EXPERTS = 8
TOP_K = 2

IN_SPLITS = (MLA_Q_LORA, MLA_KV_LORA, MLA_ROPE_DIM,
             GQA_Q_HEADS * GQA_HEAD_DIM, GQA_KV_HEADS * GQA_HEAD_DIM, GQA_KV_HEADS * GQA_HEAD_DIM,
             DIFF_HEADS * 2 * DIFF_QK_DIM, DIFF_HEADS * 2 * DIFF_QK_DIM, DIFF_HEADS * DIFF_V_DIM,
             N_BRANCHES * D_MODEL)
IN_OFFS = tuple(int(v) for v in np.cumsum((0,) + IN_SPLITS))
N_FEAT = IN_OFFS[9]
N_GATE = IN_SPLITS[9]

LOG2E = 1.4426950408889634
QK_PAD = 128
ONES_ROWS = 16

V7X_VMEM_LIMIT_BYTES = 56 * 1024 * 1024

PREP_TM = 256
ATTN_TQ = 1024
ATTN_TK = 512
ATTN_UNROLL = 4
MERGE_TM = 256
MEM_TM = 512
FFN_TM = 512
FFN_TF = 1408
MOE_TM = 1024
KV_TM = 256


def _cparams(sem):
    return pltpu.CompilerParams(dimension_semantics=sem, vmem_limit_bytes=V7X_VMEM_LIMIT_BYTES)


def _const_spec(shape):
    nd = len(shape)
    return pl.BlockSpec(shape, lambda *_: (0,) * nd, pipeline_mode=pl.Buffered(1))


def _rms_rows(x, g_row):
    ms = jnp.mean(x * x, axis=-1, keepdims=True)
    return x * lax.rsqrt(ms + NORM_EPS) * g_row


def _rms_cols(x, g_col):
    ms = jnp.mean(x * x, axis=0, keepdims=True)
    return x * lax.rsqrt(ms + NORM_EPS) * g_col


def _dot(a, b):
    return jnp.dot(a, b, preferred_element_type=F32)


def _dot_nt(a, b):
    return lax.dot_general(a, b, (((1,), (1,)), ((), ())), preferred_element_type=F32)


_S_CQ, _S_CKV, _S_KR, _S_GQ, _S_GK, _S_GV, _S_DQ, _S_DK, _S_DV = IN_OFFS[:9]


def _rope_rows(x1, x2, c, s):
    return x1 * c - x2 * s, x2 * c + x1 * s


def _mixer_prep_kernel(x_ref, g_ref, wt_ref, wg_ref, wqu_ref, wkvu_ref,
                       mqg_ref, mkvg_ref, gqg_ref, gkg_ref,
                       cm_ref, sm_ref, ca_ref, sa_ref, cp_ref, sp_ref,
                       qm_ref, km_ref, vm_ref, qg_ref, kg_ref, vg_ref,
                       qd_ref, kd_ref, vd_ref, gate_ref):
    tm = x_ref.shape[1]
    hn = _rms_rows(x_ref[0], g_ref[...]).astype(BF16)

    def feat(lo, n):
        return _dot_nt(wt_ref[lo:lo + n, :], hn)

    gchunk = 1024
    for j in range(N_GATE // gchunk):
        z = _dot(hn, wg_ref[:, j * gchunk:(j + 1) * gchunk])
        gate_ref[0, :, j * gchunk:(j + 1) * gchunk] = jax.nn.sigmoid(z).astype(gate_ref.dtype)

    zeros32 = jnp.zeros((32, tm), F32)
    zeros64 = jnp.zeros((64, tm), F32)

    c_mla = (MLA_NOPE_DIM + MLA_ROPE_DIM) ** -0.5 * LOG2E
    cm = cm_ref[...]
    sm = sm_ref[...]
    cqn = _rms_cols(feat(_S_CQ, MLA_Q_LORA), mqg_ref[...]).astype(BF16)
    q_all = _dot(wqu_ref[...], cqn)
    q_nope = q_all[0:512] * c_mla
    q_r1, q_r2 = _rope_rows(q_all[512:640], q_all[640:768], cm, sm)
    q_r1 = q_r1 * c_mla
    q_r2 = q_r2 * c_mla
    for h in range(MLA_HEADS):
        qm_ref[0, h] = jnp.concatenate(
            [q_nope[64 * h:64 * h + 64], q_r1[16 * h:16 * h + 16], q_r2[16 * h:16 * h + 16], zeros32],
            axis=0).astype(qm_ref.dtype)

    ckvn = _rms_cols(feat(_S_CKV, MLA_KV_LORA), mkvg_ref[...]).astype(BF16)
    kv_all = _dot(wkvu_ref[...], ckvn)
    vm_ref[0] = kv_all[512:1024].astype(vm_ref.dtype)
    kr = feat(_S_KR, MLA_ROPE_DIM)
    k_r1, k_r2 = _rope_rows(kr[0:16], kr[16:32], cm[0:16], sm[0:16])
    for h in range(MLA_HEADS):
        kt = jnp.concatenate([kv_all[64 * h:64 * h + 64], k_r1, k_r2, zeros32], axis=0)
        km_ref[0, :, 128 * h:128 * h + 128] = kt.T.astype(km_ref.dtype)

    c_gqa = GQA_HEAD_DIM ** -0.5 * LOG2E
    ca = ca_ref[...]
    sa = sa_ref[...]
    gq = feat(_S_GQ, GQA_Q_HEADS * GQA_HEAD_DIM)
    for h in range(GQA_Q_HEADS):
        qn = _rms_cols(gq[64 * h:64 * h + 64], gqg_ref[...])
        r1, r2 = _rope_rows(qn[0:32], qn[32:64], ca, sa)
        q64 = jnp.concatenate([r1, r2], axis=0) * c_gqa
        parts = [q64, zeros64] if h // GQA_GROUP == 0 else [zeros64, q64]
        qg_ref[0, h] = jnp.concatenate(parts, axis=0).astype(qg_ref.dtype)
    gk = feat(_S_GK, GQA_KV_HEADS * GQA_HEAD_DIM)
    kparts = []
    for g in range(GQA_KV_HEADS):
        kn = _rms_cols(gk[64 * g:64 * g + 64], gkg_ref[...])
        r1, r2 = _rope_rows(kn[0:32], kn[32:64], ca, sa)
        kparts += [r1, r2]
    kg_ref[0] = jnp.concatenate(kparts, axis=0).T.astype(kg_ref.dtype)
    vg_ref[0] = feat(_S_GV, GQA_KV_HEADS * GQA_HEAD_DIM).astype(vg_ref.dtype)

    c_diff = DIFF_QK_DIM ** -0.5 * LOG2E
    cp = cp_ref[...]
    sp = sp_ref[...]
    row = lax.broadcasted_iota(jnp.int32, (64, 1), 0)
    in_c0 = (row < 4) | ((row >= 8) & (row < 12)) | ((row >= 16) & (row < 40))

    def diff_heads(lo):
        z = feat(lo, DIFF_HEADS * 2 * DIFF_QK_DIM)
        r1, r2 = _rope_rows(z[0:64], z[64:128], cp, sp)
        rest = z[128:512]
        return [jnp.concatenate([r1[8 * h:8 * h + 8], r2[8 * h:8 * h + 8], rest[48 * h:48 * h + 48]], axis=0)
                for h in range(DIFF_HEADS)]

    for h, q64 in enumerate(diff_heads(_S_DQ)):
        q64 = q64 * c_diff
        for c in range(2):
            qc = jnp.where(in_c0 if c == 0 else jnp.logical_not(in_c0), q64, 0.0)
            parts = [qc, zeros64] if h % 2 == 0 else [zeros64, qc]
            qd_ref[0, 2 * h + c] = jnp.concatenate(parts, axis=0).astype(qd_ref.dtype)
    k_heads = diff_heads(_S_DK)
    for p in range(DIFF_HEADS // 2):
        kt = jnp.concatenate([k_heads[2 * p], k_heads[2 * p + 1]], axis=0)
        kd_ref[0, :, 128 * p:128 * p + 128] = kt.T.astype(kd_ref.dtype)
    vd_ref[0] = feat(_S_DV, DIFF_HEADS * DIFF_V_DIM).astype(vd_ref.dtype)


def _mixer_prep(x, g, wt, wg, wqu, wkvu, mqg, mkvg, gqg, gkg, tables):
    B, S, D = x.shape
    tm = min(PREP_TM, S)
    cm, sm, ca, sa, cp, sp = tables
    grid = (B, S // tm)

    def tok(width):
        return pl.BlockSpec((1, tm, width), lambda b, i: (b, i, 0))

    def featm(rows):
        return pl.BlockSpec((1, rows, tm), lambda b, i: (b, 0, i))

    def heads(n):
        return pl.BlockSpec((1, n, QK_PAD, tm), lambda b, i: (b, 0, 0, i))

    def table(rows):
        return pl.BlockSpec((rows, tm), lambda b, i: (0, i))

    in_specs = [
        pl.BlockSpec((1, tm, D), lambda b, i: (b, i, 0)),
        _const_spec((1, D)),
        _const_spec(wt.shape), _const_spec(wg.shape), _const_spec(wqu.shape), _const_spec(wkvu.shape),
        _const_spec(mqg.shape), _const_spec(mkvg.shape), _const_spec(gqg.shape), _const_spec(gkg.shape),
        table(128), table(128), table(32), table(32), table(64), table(64),
    ]
    out_shape = [
        jax.ShapeDtypeStruct((B, MLA_HEADS, QK_PAD, S), BF16),
        jax.ShapeDtypeStruct((B, S, MLA_HEADS * QK_PAD), BF16),
        jax.ShapeDtypeStruct((B, MLA_HEADS * HEAD_V, S), BF16),
        jax.ShapeDtypeStruct((B, GQA_Q_HEADS, QK_PAD, S), BF16),
        jax.ShapeDtypeStruct((B, S, QK_PAD), BF16),
        jax.ShapeDtypeStruct((B, GQA_KV_HEADS * HEAD_V, S), BF16),
        jax.ShapeDtypeStruct((B, 2 * DIFF_HEADS, QK_PAD, S), BF16),
        jax.ShapeDtypeStruct((B, S, DIFF_HEADS // 2 * QK_PAD), BF16),
        jax.ShapeDtypeStruct((B, DIFF_HEADS * HEAD_V, S), BF16),
        jax.ShapeDtypeStruct((B, S, N_GATE), BF16),
    ]
    out_specs = [
        heads(MLA_HEADS), tok(MLA_HEADS * QK_PAD), featm(MLA_HEADS * HEAD_V),
        heads(GQA_Q_HEADS), tok(QK_PAD), featm(GQA_KV_HEADS * HEAD_V),
        heads(2 * DIFF_HEADS), tok(DIFF_HEADS // 2 * QK_PAD), featm(DIFF_HEADS * HEAD_V),
        tok(N_GATE),
    ]
    return pl.pallas_call(
        _mixer_prep_kernel, grid=grid, in_specs=in_specs, out_specs=out_specs, out_shape=out_shape,
        compiler_params=_cparams(("parallel", "parallel")), name="mixer_prep",
    )(x, g, wt, wg, wqu, wkvu, mqg, mkvg, gqg, gkg, cm, sm, ca, sa, cp, sp)


def _online_softmax_attention(q_maps, k_ref, v_ref, s_ref, *, tk):
    S = k_ref.shape[1]
    tq = q_maps[0].shape[1]
    n_chunks = S // tk
    unroll = min(ATTN_UNROLL, n_chunks)
    assert unroll % 2 == 0 and n_chunks % unroll == 0
    n_maps = len(q_maps)
    ones = jnp.ones((ONES_ROWS, tk), BF16)

    def scores(c, slot):
        kc = k_ref[0, pl.ds(pl.multiple_of(c * tk, tk), tk), :]
        col_max = []
        for j, q_t in enumerate(q_maps):
            s_t = _dot(kc, q_t)
            s_ref[slot, j] = s_t
            col_max.append(jnp.max(s_t, axis=0, keepdims=True))
        return col_max

    def update(c, slot, col_max, ms, accs):
        v_aug = jnp.concatenate([v_ref[0, :, pl.ds(pl.multiple_of(c * tk, tk), tk)], ones], axis=0)
        new_ms, new_accs = [], []
        for j in range(n_maps):
            m_new = jnp.maximum(ms[j], col_max[j])
            alpha = jnp.exp2(ms[j] - m_new)
            p = jnp.exp2(s_ref[slot, j] - m_new).astype(BF16)
            new_ms.append(m_new)
            new_accs.append(accs[j] * alpha + _dot(v_aug, p))
        return new_ms, new_accs

    def group(c, cmax, ms, accs, last):
        for u in range(unroll):
            is_final = last and u == unroll - 1
            cmax_next = None if is_final else scores(c + u + 1, (u + 1) % 2)
            ms, accs = update(c + u, u % 2, cmax, ms, accs)
            cmax = cmax_next
        return cmax, ms, accs

    def body(i, carry):
        return group(unroll * i, *carry, last=False)

    m0 = [jnp.full((1, tq), -jnp.inf, F32)] * n_maps
    acc0 = [jnp.zeros((HEAD_V + ONES_ROWS, tq), F32)] * n_maps
    carry = lax.fori_loop(0, n_chunks // unroll - 1, body, (scores(0, 0), m0, acc0))
    _, _, accs = group(n_chunks - unroll, *carry, last=True)
    return accs


def _attn_kernel(q_ref, k_ref, v_ref, o_ref, s_ref, *, tk):
    (acc,) = _online_softmax_attention([q_ref[0, 0]], k_ref, v_ref, s_ref, tk=tk)
    o_ref[0] = (acc[:HEAD_V] / acc[HEAD_V:HEAD_V + 1]).astype(o_ref.dtype)


def _attention(q_t, k, v_t, *, kv_of_head, kcol_of_head, name):
    B, H, _, S = q_t.shape
    tq = min(ATTN_TQ, S)
    tk = min(ATTN_TK, S // 2)
    return pl.pallas_call(
        functools.partial(_attn_kernel, tk=tk),
        grid=(B, H, S // tq),
        in_specs=[
            pl.BlockSpec((1, 1, QK_PAD, tq), lambda b, h, i: (b, h, 0, i)),
            pl.BlockSpec((1, S, QK_PAD), lambda b, h, i: (b, 0, kcol_of_head(h))),
            pl.BlockSpec((1, HEAD_V, S), lambda b, h, i: (b, kv_of_head(h), 0)),
        ],
        out_specs=pl.BlockSpec((1, HEAD_V, tq), lambda b, h, i: (b, h, i)),
        out_shape=jax.ShapeDtypeStruct((B, H * HEAD_V, S), F32),
        scratch_shapes=[pltpu.VMEM((2, 1, tk, tq), F32)],
        compiler_params=_cparams(("parallel", "parallel", "arbitrary")), name=name,
    )(q_t, k, v_t)


def _diff_attn_kernel(lq1_ref, lk1_ref, lq2_ref, lk2_ref, g_ref, q_ref, k_ref, v_ref, o_ref, s_ref, *,
                      tk, lambda_init):
    acc1, acc2 = _online_softmax_attention([q_ref[0, 0], q_ref[0, 1]], k_ref, v_ref, s_ref, tk=tk)
    lam =(jnp.exp(jnp.sum(lq1_ref[...] * lk1_ref[...], axis=-1, keepdims=True))
           - jnp.exp(jnp.sum(lq2_ref[...] * lk2_ref[...], axis=-1, keepdims=True)) + lambda_init)
    o = acc1[:HEAD_V] / acc1[HEAD_V:HEAD_V + 1] - lam * (acc2[:HEAD_V] / acc2[HEAD_V:HEAD_V + 1])
    o = _rms_cols(o, g_ref[...]) * (1.0 - lambda_init)
    o_ref[0] = o.astype(o_ref.dtype)


def _diff_attention(q_t, k, v_t, lq1, lk1, lq2, lk2, subln_g, lambda_init):
    B, H2, _, S = q_t.shape
    H = H2 // 2
    tq = min(ATTN_TQ, S)
    tk = min(ATTN_TK, S // 2)
    vec = _const_spec((1, DIFF_QK_DIM))
    return pl.pallas_call(
        functools.partial(_diff_attn_kernel, tk=tk, lambda_init=lambda_init),
        grid=(B, H, S // tq),
        in_specs=[
            vec, vec, vec, vec, _const_spec((DIFF_V_DIM, 1)),
            pl.BlockSpec((1, 2, QK_PAD, tq), lambda b, h, i: (b, h, 0, i)),
            pl.BlockSpec((1, S, QK_PAD), lambda b, h, i: (b, 0, h // 2)),
            pl.BlockSpec((1, HEAD_V, S), lambda b, h, i: (b, h, 0)),
        ],
        out_specs=pl.BlockSpec((1, HEAD_V, tq), lambda b, h, i: (b, h, i)),
        out_shape=jax.ShapeDtypeStruct((B, H * HEAD_V, S), F32),
        scratch_shapes=[pltpu.VMEM((2, 2, tk, tq), F32)],
        compiler_params=_cparams(("parallel", "parallel", "arbitrary")), name="diff_attn",
    )(lq1, lk1, lq2, lk2, subln_g, q_t, k, v_t)


def _mixer_merge_kernel(x_ref, a_ref, b_ref, c_ref, gate_ref, wb_ref, wo_ref, g_ref, o_ref):
    merged = None
    for n, br_ref in enumerate((a_ref, b_ref, c_ref)):
        br = br_ref[0].T.astype(BF16)
        proj = _dot(br, wb_ref[n])
        term = gate_ref[0, :, n * D_MODEL:(n + 1) * D_MODEL].astype(F32) * proj
        merged = term if merged is None else merged + term
    y = _dot(merged.astype(BF16), wo_ref[...])
    o_ref[0] = x_ref[0] + _rms_rows(y, g_ref[...])


def _mixer_merge(x, a_t, b_t, c_t, gates, wb, wo, g):
    B, S, D = x.shape
    tm = min(MERGE_TM, S)
    xspec = pl.BlockSpec((1, tm, D), lambda b, i: (b, i, 0))
    brspec = pl.BlockSpec((1, BRANCH_WIDTH, tm), lambda b, i: (b, 0, i))
    return pl.pallas_call(
        _mixer_merge_kernel, grid=(B, S // tm),
        in_specs=[xspec, brspec, brspec, brspec,
                  pl.BlockSpec((1, tm, N_GATE), lambda b, i: (b, i, 0)),
                  _const_spec(wb.shape), _const_spec(wo.shape), _const_spec((1, D))],
        out_specs=xspec, out_shape=jax.ShapeDtypeStruct(x.shape, x.dtype),
        compiler_params=_cparams(("parallel", "parallel")), name="mixer_merge",
    )(x, a_t, b_t, c_t, gates, wb, wo, g)


def _norm_matmul_kernel(x_ref, g_ref, w_ref, o_ref):
    hn = _rms_rows(x_ref[...], g_ref[...]).astype(BF16)
    o_ref[...] = _dot(hn, w_ref[...]).astype(o_ref.dtype)


def _norm_matmul(x2d, g, w, out_dtype):
    T, D = x2d.shape
    N = w.shape[1]
    tm = min(KV_TM, T)
    return pl.pallas_call(
        _norm_matmul_kernel, grid=(T // tm,),
        in_specs=[pl.BlockSpec((tm, D), lambda i: (i, 0)), _const_spec((1, D)), _const_spec(w.shape)],
        out_specs=pl.BlockSpec((tm, N), lambda i: (i, 0)),
        out_shape=jax.ShapeDtypeStruct((T, N), out_dtype),
        compiler_params=_cparams(("parallel",)), name="mem_kv",
    )(x2d, g, w)


def _mem_attn_kernel(x_ref, kv_ref, gpre_ref, wq_ref, wo_ref, gpost_ref, o_ref):
    x = x_ref[0]
    hn = _rms_rows(x, gpre_ref[...]).astype(BF16)
    q = (_dot(hn, wq_ref[...]) * (MEM_HEAD_DIM ** -0.5 * LOG2E)).astype(BF16)
    outs = []
    for h in range(MEM_HEADS):
        lo = h * MEM_HEAD_DIM
        k_h = kv_ref[0, :, lo:lo + MEM_HEAD_DIM]
        v_h = kv_ref[0, :, D_MODEL + lo:D_MODEL + lo + MEM_HEAD_DIM]
        s = _dot_nt(q[:, lo:lo + MEM_HEAD_DIM], k_h)
        e = jnp.exp2(s - jnp.max(s, axis=-1, keepdims=True))
        p = (e / jnp.sum(e, axis=-1, keepdims=True)).astype(BF16)
        outs.append(_dot(p, v_h))
    o = jnp.concatenate(outs, axis=-1).astype(BF16)
    y = _dot(o, wo_ref[...])
    o_ref[0] = x + _rms_rows(y, gpost_ref[...])


def _mem_attention(x, kv, gpre, wq, wo, gpost):
    B, S, D = x.shape
    M = kv.shape[1]
    tm = min(MEM_TM, S)
    xspec = pl.BlockSpec((1, tm, D), lambda b, i: (b, i, 0))
    return pl.pallas_call(
        _mem_attn_kernel, grid=(B, S // tm),
        in_specs=[xspec, pl.BlockSpec((1, M, 2 * D), lambda b, i: (b, 0, 0)),
                  _const_spec((1, D)), _const_spec(wq.shape), _const_spec(wo.shape), _const_spec((1, D))],
        out_specs=xspec, out_shape=jax.ShapeDtypeStruct(x.shape, x.dtype),
        compiler_params=_cparams(("parallel", "parallel")), name="mem_attn",
    )(x, kv, gpre, wq, wo, gpost)


def _swiglu_chunk(hn, wg, wu, wd):
    gate = _dot(hn, wg)
    act = (gate * jax.nn.sigmoid(gate) * _dot(hn, wu)).astype(BF16)
    return _dot(act, wd)


def _ffn_kernel(x_ref, gpre_ref, wg_ref, wu_ref, wd_ref, gpost_ref, o_ref, hn_ref, acc_ref):
    j = pl.program_id(1)

    @pl.when(j == 0)
    def _():
        hn_ref[...] = _rms_rows(x_ref[...], gpre_ref[...]).astype(BF16)

    y = _swiglu_chunk(hn_ref[...], wg_ref[...], wu_ref[...], wd_ref[...])

    @pl.when(j == 0)
    def _():
        acc_ref[...] = y

    @pl.when(j > 0)
    def _():
        acc_ref[...] += y

    @pl.when(j == pl.num_programs(1) - 1)
    def _():
        o_ref[...] = x_ref[...] + _rms_rows(acc_ref[...], gpost_ref[...])


def _ffn(x2d, gpre, wg, wu, wd, gpost):
    T, D = x2d.shape
    F = wg.shape[1]
    tm = min(FFN_TM, T)
    tf = FFN_TF
    xspec = pl.BlockSpec((tm, D), lambda i, j: (i, 0))
    return pl.pallas_call(
        _ffn_kernel, grid=(T // tm, F // tf),
        in_specs=[xspec, _const_spec((1, D)),
                  pl.BlockSpec((D, tf), lambda i, j: (0, j)),
                  pl.BlockSpec((D, tf), lambda i, j: (0, j)),
                  pl.BlockSpec((tf, D), lambda i, j: (j, 0)),
                  _const_spec((1, D))],
        out_specs=xspec, out_shape=jax.ShapeDtypeStruct(x2d.shape, x2d.dtype),
        scratch_shapes=[pltpu.VMEM((tm, D), BF16), pltpu.VMEM((tm, D), F32)],
        compiler_params=_cparams(("parallel", "arbitrary")), name="ffn",
    )(x2d, gpre, wg, wu, wd, gpost)


def _router_kernel(x_ref, gpre_ref, wr_ref, br_ref, o_ref):
    hn = _rms_rows(x_ref[...], gpre_ref[...])
    logits = lax.dot_general(wr_ref[...], hn, (((1,), (1,)), ((), ())),
                             precision=lax.Precision.HIGHEST,
                             preferred_element_type=F32) + br_ref[...]
    e_idx = lax.broadcasted_iota(jnp.int32, logits.shape, 0)
    m1 = jnp.max(logits, axis=0, keepdims=True)
    i1 = jnp.min(jnp.where(logits == m1, e_idx, N_EXPERTS), axis=0, keepdims=True)
    rest = jnp.where(e_idx == i1, -jnp.inf, logits)
    m2 = jnp.max(rest, axis=0, keepdims=True)
    i2 = jnp.min(jnp.where(rest == m2, e_idx, N_EXPERTS), axis=0, keepdims=True)
    e2 = jnp.exp(m2 - m1)
    w1 = 1.0 / (1.0 + e2)
    w2 = e2 / (1.0 + e2)
    o_ref[...] = jnp.where(e_idx == i1, w1, 0.0) + jnp.where(e_idx == i2, w2, 0.0)


def _router(x2d, gpre, wr_t, br_col):
    T, D = x2d.shape
    tm = min(FFN_TM, T)
    return pl.pallas_call(
        _router_kernel, grid=(T // tm,),
        in_specs=[pl.BlockSpec((tm, D), lambda i: (i, 0)), _const_spec((1, D)),
                  _const_spec(wr_t.shape), _const_spec(br_col.shape)],
        out_specs=pl.BlockSpec((N_EXPERTS, tm), lambda i: (0, i)),
        out_shape=jax.ShapeDtypeStruct((N_EXPERTS, T), F32),
        compiler_params=_cparams(("parallel",)), name="router",
    )(x2d, gpre, wr_t, br_col)


def _moe_ffn_kernel(x_ref, gate_ref, gpre_ref, wg_ref, wu_ref, wd_ref, gpost_ref, o_ref, hn_ref, acc_ref):
    e = pl.program_id(1)
    j = pl.program_id(2)
    first = (e == 0) & (j == 0)
    last = (e == pl.num_programs(1) - 1) & (j == pl.num_programs(2) - 1)

    @pl.when(first)
    def _():
        hn_ref[...] = _rms_rows(x_ref[...], gpre_ref[...]).astype(BF16)

    y = gate_ref[0] * _swiglu_chunk(hn_ref[...], wg_ref[0], wu_ref[0], wd_ref[0])

    @pl.when(first)
    def _():
        acc_ref[...] = y

    @pl.when(jnp.logical_not(first))
    def _():
        acc_ref[...] += y

    @pl.when(last)
    def _():
        o_ref[...] = x_ref[...] + _rms_rows(acc_ref[...], gpost_ref[...])


def _moe_ffn(x2d, gates_col, gpre, wg, wu, wd, gpost):
    T, D = x2d.shape
    E, _, F = wg.shape
    tm = min(MOE_TM, T)
    tf = FFN_TF
    xspec = pl.BlockSpec((tm, D), lambda i, e, j: (i, 0))
    return pl.pallas_call(
        _moe_ffn_kernel, grid=(T // tm, E, F // tf),
        in_specs=[xspec, pl.BlockSpec((1, tm, 1), lambda i, e, j: (e, i, 0)), _const_spec((1, D)),
                  pl.BlockSpec((1, D, tf), lambda i, e, j: (e, 0, j)),
                  pl.BlockSpec((1, D, tf), lambda i, e, j: (e, 0, j)),
                  pl.BlockSpec((1, tf, D), lambda i, e, j: (e, j, 0)),
                  _const_spec((1, D))],
        out_specs=xspec, out_shape=jax.ShapeDtypeStruct(x2d.shape, x2d.dtype),
        scratch_shapes=[pltpu.VMEM((tm, D), BF16), pltpu.VMEM((tm, D), F32)],
        compiler_params=_cparams(("parallel", "arbitrary", "arbitrary")), name="moe_ffn",
    )(x2d, gates_col, gpre, wg, wu, wd, gpost)


def _feature_row_order():
    def diff_block(base):
        x1, x2, rest = [], [], []
        half = DIFF_ROPE_DIM // 2
        for h in range(DIFF_HEADS):
            for c in range(2):
                lo = base + h * 2 * DIFF_QK_DIM + c * DIFF_QK_DIM
                x1 += range(lo, lo + half)
                x2 += range(lo + half, lo + DIFF_ROPE_DIM)
                rest += range(lo + DIFF_ROPE_DIM, lo + DIFF_QK_DIM)
        return x1 + x2 + rest

    order = list(range(0, _S_DQ)) + diff_block(_S_DQ) + diff_block(_S_DK) + list(range(_S_DV, N_FEAT))
    return np.asarray(order, np.int32)


def _mla_q_up_order():
    w = MLA_NOPE_DIM + MLA_ROPE_DIM
    half = MLA_ROPE_DIM // 2
    nope = [h * w + j for h in range(MLA_HEADS) for j in range(MLA_NOPE_DIM)]
    x1 = [h * w + MLA_NOPE_DIM + j for h in range(MLA_HEADS) for j in range(half)]
    x2 = [h * w + MLA_NOPE_DIM + half + j for h in range(MLA_HEADS) for j in range(half)]
    return np.asarray(nope + x1 + x2, np.int32)


def _mla_kv_up_order():
    w = MLA_NOPE_DIM + MLA_V_DIM
    k = [h * w + j for h in range(MLA_HEADS) for j in range(MLA_NOPE_DIM)]
    v = [h * w + MLA_NOPE_DIM + j for h in range(MLA_HEADS) for j in range(MLA_V_DIM)]
    return np.asarray(k + v, np.int32)


def _rope_tables(S):
    def cos_sin(pos, rot_dim):
        inv = ROPE_THETA ** (-jnp.arange(0, rot_dim, 2, dtype=F32) / rot_dim)
        ang = pos.astype(F32)[:, None] * inv[None, :]
        return jnp.cos(ang).T, jnp.sin(ang).T

    rows = S // GRID_W
    pos = jnp.arange(S)
    row = jnp.repeat(jnp.arange(rows), GRID_W)
    col = jnp.tile(jnp.arange(GRID_W), rows)
    cm, sm = cos_sin(pos, MLA_ROPE_DIM)
    cr, sr = cos_sin(row, GQA_HEAD_DIM // 2)
    cc, sc = cos_sin(col, GQA_HEAD_DIM // 2)
    cp, sp = cos_sin(pos, DIFF_ROPE_DIM)
    return (jnp.tile(cm, (MLA_HEADS, 1)), jnp.tile(sm, (MLA_HEADS, 1)),
            jnp.concatenate([cr, cc], axis=0), jnp.concatenate([sr, sc], axis=0),
            jnp.tile(cp, (2 * DIFF_HEADS, 1)), jnp.tile(sp, (2 * DIFF_HEADS, 1)))


def _row(v):
    return v.reshape(1, -1).astype(F32)


def _col(v):
    return v.reshape(-1, 1).astype(F32)


def kernel(x, mem, mix_pre_g, mix_post_g, w_in, mla_q_norm_g, mla_w_q_up, mla_kv_norm_g, mla_w_kv_up,
           gqa_q_norm_g, gqa_k_norm_g, diff_lambda_q1, diff_lambda_k1, diff_lambda_q2, diff_lambda_k2,
           diff_subln_g, w_branch, w_out, mem_pre_g, mem_post_g, mem_norm_g, mem_wq, mem_wkv, mem_wo,
           ffn_pre_g, ffn_post_g, dense_w_gate, dense_w_up, dense_w_down, moe_w_router, moe_b_router,
           moe_w_gate, moe_w_up, moe_w_down):
    B, S, D = x.shape
    M = mem.shape[1]
    depth = w_in.shape[0]
    tables = _rope_tables(S)
    feat_order = _feature_row_order()
    q_up_order = _mla_q_up_order()
    kv_up_order = _mla_kv_up_order()

    for layer in range(depth):
        lambda_init = 0.8 - 0.6 * math.exp(-0.3 * layer)

        w_l = w_in[layer]
        wt = w_l[:, feat_order].T.astype(BF16)
        wg = w_l[:, N_FEAT:].astype(BF16)
        wqu = mla_w_q_up[layer][:, q_up_order].T.astype(BF16)
        wkvu = mla_w_kv_up[layer][:, kv_up_order].T.astype(BF16)
        (qm, km, vm, qg, kg, vg, qd, kd, vd, gates) = _mixer_prep(
            x, _row(mix_pre_g[layer]), wt, wg, wqu, wkvu,
            _col(mla_q_norm_g[layer]), _col(mla_kv_norm_g[layer]),
            _col(gqa_q_norm_g[layer]), _col(gqa_k_norm_g[layer]), tables)
        a_t = _attention(qm, km, vm, kv_of_head=lambda h: h, kcol_of_head=lambda h: h, name="mla_attn")
        b_t = _attention(qg, kg, vg, kv_of_head=lambda h: h // GQA_GROUP, kcol_of_head=lambda h: 0,
                         name="gqa_attn")
        c_t = _diff_attention(qd, kd, vd, _row(diff_lambda_q1[layer]), _row(diff_lambda_k1[layer]),
                              _row(diff_lambda_q2[layer]), _row(diff_lambda_k2[layer]),
                              _col(diff_subln_g[layer]), lambda_init)
        x = _mixer_merge(x, a_t, b_t, c_t, gates, w_branch[layer].astype(BF16),
                         w_out[layer].astype(BF16), _row(mix_post_g[layer]))

        kv = _norm_matmul(mem.reshape(B * M, D), _row(mem_norm_g[layer]), mem_wkv[layer].astype(BF16), BF16)
        x = _mem_attention(x, kv.reshape(B, M, 2 * D), _row(mem_pre_g[layer]), mem_wq[layer].astype(BF16),
                           mem_wo[layer].astype(BF16), _row(mem_post_g[layer]))

        x2d = x.reshape(B * S, D)
        i = layer // 2
        if layer % 2 == 0:
            x2d = _ffn(x2d, _row(ffn_pre_g[layer]), dense_w_gate[i].astype(BF16), dense_w_up[i].astype(BF16),
                       dense_w_down[i].astype(BF16), _row(ffn_post_g[layer]))
        else:
            gates_e = _router(x2d, _row(ffn_pre_g[layer]), moe_w_router[i].T.astype(F32),
                              _col(moe_b_router[i]))
            x2d = _moe_ffn(x2d, gates_e.reshape(N_EXPERTS, B * S, 1), _row(ffn_pre_g[layer]),
                           moe_w_gate[i].astype(BF16), moe_w_up[i].astype(BF16), moe_w_down[i].astype(BF16),
                           _row(ffn_post_g[layer]))
        x = x2d.reshape(B, S, D)
    return x
```

```python
import functools
import math

import numpy as np
import jax
import jax.numpy as jnp
from jax import lax
from jax.experimental import pallas as pl
from jax.experimental.pallas import tpu as pltpu

F32 = jnp.float32
BF16 = jnp.bfloat16

D_MODEL = 1024
GRID_W = 64
ROPE_THETA = 500000.0
NORM_EPS = 1e-6

MLA_HEADS = 8
MLA_Q_LORA = 384
MLA_KV_LORA = 256
MLA_NOPE_DIM = 64
MLA_ROPE_DIM = 32
MLA_V_DIM = 64

GQA_Q_HEADS = 8
GQA_KV_HEADS = 2
GQA_GROUP = GQA_Q_HEADS // GQA_KV_HEADS
GQA_HEAD_DIM = 64

DIFF_HEADS = 8
DIFF_QK_DIM = 32
DIFF_V_DIM = 2 * DIFF_QK_DIM
DIFF_ROPE_DIM = DIFF_QK_DIM // 4

N_BRANCHES = 3
BRANCH_WIDTH = 512
HEAD_V = 64

MEM_HEADS = 4
MEM_HEAD_DIM = D_MODEL // MEM_HEADS

D_FF = 2816
N_EXPERTS = 8
TOP_K = 2

IN_SPLITS = (MLA_Q_LORA, MLA_KV_LORA, MLA_ROPE_DIM,
             GQA_Q_HEADS * GQA_HEAD_DIM, GQA_KV_HEADS * GQA_HEAD_DIM, GQA_KV_HEADS * GQA_HEAD_DIM,
             DIFF_HEADS * 2 * DIFF_QK_DIM, DIFF_HEADS * 2 * DIFF_QK_DIM, DIFF_HEADS * DIFF_V_DIM,
             N_BRANCHES * D_MODEL)
IN_OFFS = tuple(int(v) for v in np.cumsum((0,) + IN_SPLITS))
N_FEAT = IN_OFFS[9]
N_GATE = IN_SPLITS[9]

LOG2E = 1.4426950408889634
QK_PAD = 128
ONES_ROWS = 16

V7X_VMEM_LIMIT_BYTES = 56 * 1024 * 1024

PREP_TM = 256
ATTN_TQ = 1024
ATTN_TK = 512
ATTN_UNROLL = 4
MERGE_TM = 256
MEM_TM = 512
FFN_TM = 512
FFN_TF = 1408
MOE_BLOCK = 512
MOE_TF = 1408
KV_TM = 256


def _cparams(sem):
    return pltpu.CompilerParams(dimension_semantics=sem, vmem_limit_bytes=V7X_VMEM_LIMIT_BYTES)


def _const_spec(shape):
    nd = len(shape)
    return pl.BlockSpec(shape, lambda *_: (0,) * nd, pipeline_mode=pl.Buffered(1))


def _rms_rows(x, g_row):
    ms = jnp.mean(x * x, axis=-1, keepdims=True)
    return x * lax.rsqrt(ms + NORM_EPS) * g_row


def _rms_cols(x, g_col):
    ms = jnp.mean(x * x, axis=0, keepdims=True)
    return x * lax.rsqrt(ms + NORM_EPS) * g_col


def _dot(a, b):
    return jnp.dot(a, b, preferred_element_type=F32)


def _dot_nt(a, b):
    return lax.dot_general(a, b, (((1,), (1,)), ((), ())), preferred_element_type=F32)


_S_CQ, _S_CKV, _S_KR, _S_GQ, _S_GK, _S_GV, _S_DQ, _S_DK, _S_DV = IN_OFFS[:9]


def _rope_rows(x1, x2, c, s):
    return x1 * c - x2 * s, x2 * c + x1 * s


def _mixer_prep_kernel(x_ref, g_ref, wt_ref, wg_ref, wqu_ref, wkvu_ref,
                       mqg_ref, mkvg_ref, gqg_ref, gkg_ref,
                       cm_ref, sm_ref, ca_ref, sa_ref, cp_ref, sp_ref,
                       qm_ref, km_ref, vm_ref, qg_ref, kg_ref, vg_ref,
                       qd_ref, kd_ref, vd_ref, gate_ref):
    tm = x_ref.shape[1]
    hn = _rms_rows(x_ref[0], g_ref[...]).astype(BF16)

    def feat(lo, n):
        return _dot_nt(wt_ref[lo:lo + n, :], hn)

    gchunk = 1024
    for j in range(N_GATE // gchunk):
        z = _dot(hn, wg_ref[:, j * gchunk:(j + 1) * gchunk])
        gate_ref[0, :, j * gchunk:(j + 1) * gchunk] = jax.nn.sigmoid(z).astype(gate_ref.dtype)

    zeros32 = jnp.zeros((32, tm), F32)
    zeros64 = jnp.zeros((64, tm), F32)

    c_mla = (MLA_NOPE_DIM + MLA_ROPE_DIM) ** -0.5 * LOG2E
    cm = cm_ref[...]
    sm = sm_ref[...]
    cqn = _rms_cols(feat(_S_CQ, MLA_Q_LORA), mqg_ref[...]).astype(BF16)
    q_all = _dot(wqu_ref[...], cqn)
    q_nope = q_all[0:512] * c_mla
    q_r1, q_r2 = _rope_rows(q_all[512:640], q_all[640:768], cm, sm)
    q_r1 = q_r1 * c_mla
    q_r2 = q_r2 * c_mla
    for h in range(MLA_HEADS):
        qm_ref[0, h] = jnp.concatenate(
            [q_nope[64 * h:64 * h + 64], q_r1[16 * h:16 * h + 16], q_r2[16 * h:16 * h + 16], zeros32],
            axis=0).astype(qm_ref.dtype)

    ckvn = _rms_cols(feat(_S_CKV, MLA_KV_LORA), mkvg_ref[...]).astype(BF16)
    kv_all = _dot(wkvu_ref[...], ckvn)
    vm_ref[0] = kv_all[512:1024].astype(vm_ref.dtype)
    kr = feat(_S_KR, MLA_ROPE_DIM)
    k_r1, k_r2 = _rope_rows(kr[0:16], kr[16:32], cm[0:16], sm[0:16])
    for h in range(MLA_HEADS):
        kt = jnp.concatenate([kv_all[64 * h:64 * h + 64], k_r1, k_r2, zeros32], axis=0)
        km_ref[0, :, 128 * h:128 * h + 128] = kt.T.astype(km_ref.dtype)

    c_gqa = GQA_HEAD_DIM ** -0.5 * LOG2E
    ca = ca_ref[...]
    sa = sa_ref[...]
    gq = feat(_S_GQ, GQA_Q_HEADS * GQA_HEAD_DIM)
    for h in range(GQA_Q_HEADS):
        qn = _rms_cols(gq[64 * h:64 * h + 64], gqg_ref[...])
        r1, r2 = _rope_rows(qn[0:32], qn[32:64], ca, sa)
        q64 = jnp.concatenate([r1, r2], axis=0) * c_gqa
        parts = [q64, zeros64] if h // GQA_GROUP == 0 else [zeros64, q64]
        qg_ref[0, h] = jnp.concatenate(parts, axis=0).astype(qg_ref.dtype)
    gk = feat(_S_GK, GQA_KV_HEADS * GQA_HEAD_DIM)
    kparts = []
    for g in range(GQA_KV_HEADS):
        kn = _rms_cols(gk[64 * g:64 * g + 64], gkg_ref[...])
        r1, r2 = _rope_rows(kn[0:32], kn[32:64], ca, sa)
        kparts += [r1, r2]
    kg_ref[0] = jnp.concatenate(kparts, axis=0).T.astype(kg_ref.dtype)
    vg_ref[0] = feat(_S_GV, GQA_KV_HEADS * GQA_HEAD_DIM).astype(vg_ref.dtype)

    c_diff = DIFF_QK_DIM ** -0.5 * LOG2E
    cp = cp_ref[...]
    sp = sp_ref[...]
    row = lax.broadcasted_iota(jnp.int32, (64, 1), 0)
    in_c0 = (row < 4) | ((row >= 8) & (row < 12)) | ((row >= 16) & (row < 40))

    def diff_heads(lo):
        z = feat(lo, DIFF_HEADS * 2 * DIFF_QK_DIM)
        r1, r2 = _rope_rows(z[0:64], z[64:128], cp, sp)
        rest = z[128:512]
        return [jnp.concatenate([r1[8 * h:8 * h + 8], r2[8 * h:8 * h + 8], rest[48 * h:48 * h + 48]], axis=0)
                for h in range(DIFF_HEADS)]

    for h, q64 in enumerate(diff_heads(_S_DQ)):
        q64 = q64 * c_diff
        for c in range(2):
            qc = jnp.where(in_c0 if c == 0 else jnp.logical_not(in_c0), q64, 0.0)
            parts = [qc, zeros64] if h % 2 == 0 else [zeros64, qc]
            qd_ref[0, 2 * h + c] = jnp.concatenate(parts, axis=0).astype(qd_ref.dtype)
    k_heads = diff_heads(_S_DK)
    for p in range(DIFF_HEADS // 2):
        kt = jnp.concatenate([k_heads[2 * p], k_heads[2 * p + 1]], axis=0)
        kd_ref[0, :, 128 * p:128 * p + 128] = kt.T.astype(kd_ref.dtype)
    vd_ref[0] = feat(_S_DV, DIFF_HEADS * DIFF_V_DIM).astype(vd_ref.dtype)


def _mixer_prep(x, g, wt, wg, wqu, wkvu, mqg, mkvg, gqg, gkg, tables):
    B, S, D = x.shape
    tm = min(PREP_TM, S)
    cm, sm, ca, sa, cp, sp = tables
    grid = (B, S // tm)

    def tok(width):
        return pl.BlockSpec((1, tm, width), lambda b, i: (b, i, 0))

    def featm(rows):
        return pl.BlockSpec((1, rows, tm), lambda b, i: (b, 0, i))

    def heads(n):
        return pl.BlockSpec((1, n, QK_PAD, tm), lambda b, i: (b, 0, 0, i))

    def table(rows):
        return pl.BlockSpec((rows, tm), lambda b, i: (0, i))

    in_specs = [
        pl.BlockSpec((1, tm, D), lambda b, i: (b, i, 0)),
        _const_spec((1, D)),
        _const_spec(wt.shape), _const_spec(wg.shape), _const_spec(wqu.shape), _const_spec(wkvu.shape),
        _const_spec(mqg.shape), _const_spec(mkvg.shape), _const_spec(gqg.shape), _const_spec(gkg.shape),
        table(128), table(128), table(32), table(32), table(64), table(64),
    ]
    out_shape = [
        jax.ShapeDtypeStruct((B, MLA_HEADS, QK_PAD, S), BF16),
        jax.ShapeDtypeStruct((B, S, MLA_HEADS * QK_PAD), BF16),
        jax.ShapeDtypeStruct((B, MLA_HEADS * HEAD_V, S), BF16),
        jax.ShapeDtypeStruct((B, GQA_Q_HEADS, QK_PAD, S), BF16),
        jax.ShapeDtypeStruct((B, S, QK_PAD), BF16),
        jax.ShapeDtypeStruct((B, GQA_KV_HEADS * HEAD_V, S), BF16),
        jax.ShapeDtypeStruct((B, 2 * DIFF_HEADS, QK_PAD, S), BF16),
        jax.ShapeDtypeStruct((B, S, DIFF_HEADS // 2 * QK_PAD), BF16),
        jax.ShapeDtypeStruct((B, DIFF_HEADS * HEAD_V, S), BF16),
        jax.ShapeDtypeStruct((B, S, N_GATE), BF16),
    ]
    out_specs = [
        heads(MLA_HEADS), tok(MLA_HEADS * QK_PAD), featm(MLA_HEADS * HEAD_V),
        heads(GQA_Q_HEADS), tok(QK_PAD), featm(GQA_KV_HEADS * HEAD_V),
        heads(2 * DIFF_HEADS), tok(DIFF_HEADS // 2 * QK_PAD), featm(DIFF_HEADS * HEAD_V),
        tok(N_GATE),
    ]
    return pl.pallas_call(
        _mixer_prep_kernel, grid=grid, in_specs=in_specs, out_specs=out_specs, out_shape=out_shape,
        compiler_params=_cparams(("parallel", "parallel")), name="mixer_prep",
    )(x, g, wt, wg, wqu, wkvu, mqg, mkvg, gqg, gkg, cm, sm, ca, sa, cp, sp)


def _online_softmax_attention(q_maps, k_ref, v_ref, s_ref, *, tk):
    S = k_ref.shape[1]
    tq = q_maps[0].shape[1]
    n_chunks = S // tk
    unroll = min(ATTN_UNROLL, n_chunks)
    assert unroll % 2 == 0 and n_chunks % unroll == 0
    n_maps = len(q_maps)
    ones = jnp.ones((ONES_ROWS, tk), BF16)

    def scores(c, slot):
        kc = k_ref[0, pl.ds(pl.multiple_of(c * tk, tk), tk), :]
        col_max = []
        for j, q_t in enumerate(q_maps):
            s_t = _dot(kc, q_t)
            s_ref[slot, j] = s_t
            col_max.append(jnp.max(s_t, axis=0, keepdims=True))
        return col_max

    def update(c, slot, col_max, ms, accs):
        v_aug = jnp.concatenate([v_ref[0, :, pl.ds(pl.multiple_of(c * tk, tk), tk)], ones], axis=0)
        new_ms, new_accs = [], []
        for j in range(n_maps):
            m_new = jnp.maximum(ms[j], col_max[j])
            alpha = jnp.exp2(ms[j] - m_new)
            p = jnp.exp2(s_ref[slot, j] - m_new).astype(BF16)
            new_ms.append(m_new)
            new_accs.append(accs[j] * alpha + _dot(v_aug, p))
        return new_ms, new_accs

    def group(c, cmax, ms, accs, last):
        for u in range(unroll):
            is_final = last and u == unroll - 1
            cmax_next = None if is_final else scores(c + u + 1, (u + 1) % 2)
            ms, accs = update(c + u, u % 2, cmax, ms, accs)
            cmax = cmax_next
        return cmax, ms, accs

    def body(i, carry):
        return group(unroll * i, *carry, last=False)

    m0 = [jnp.full((1, tq), -jnp.inf, F32)] * n_maps
    acc0 = [jnp.zeros((HEAD_V + ONES_ROWS, tq), F32)] * n_maps
    carry = lax.fori_loop(0, n_chunks // unroll - 1, body, (scores(0, 0), m0, acc0))
    _, _, accs = group(n_chunks - unroll, *carry, last=True)
    return accs


def _attn_kernel(q_ref, k_ref, v_ref, o_ref, s_ref, *, tk):
    (acc,) = _online_softmax_attention([q_ref[0, 0]], k_ref, v_ref, s_ref, tk=tk)
    o_ref[0] = (acc[:HEAD_V] / acc[HEAD_V:HEAD_V + 1]).astype(o_ref.dtype)


def _attention(q_t, k, v_t, *, kv_of_head, kcol_of_head, name):
    B, H, _, S = q_t.shape
    tq = min(ATTN_TQ, S)
    tk = min(ATTN_TK, S // 2)
    return pl.pallas_call(
        functools.partial(_attn_kernel, tk=tk),
        grid=(B, H, S // tq),
        in_specs=[
            pl.BlockSpec((1, 1, QK_PAD, tq), lambda b, h, i: (b, h, 0, i)),
            pl.BlockSpec((1, S, QK_PAD), lambda b, h, i: (b, 0, kcol_of_head(h))),
            pl.BlockSpec((1, HEAD_V, S), lambda b, h, i: (b, kv_of_head(h), 0)),
        ],
        out_specs=pl.BlockSpec((1, HEAD_V, tq), lambda b, h, i: (b, h, i)),
        out_shape=jax.ShapeDtypeStruct((B, H * HEAD_V, S), F32),
        scratch_shapes=[pltpu.VMEM((2, 1, tk, tq), F32)],
        compiler_params=_cparams(("parallel", "parallel", "arbitrary")), name=name,
    )(q_t, k, v_t)


def _diff_attn_kernel(lq1_ref, lk1_ref, lq2_ref, lk2_ref, g_ref, q_ref, k_ref, v_ref, o_ref, s_ref, *,
                      tk, lambda_init):
    acc1, acc2 = _online_softmax_attention([q_ref[0, 0], q_ref[0, 1]], k_ref, v_ref, s_ref, tk=tk)
    lam =(jnp.exp(jnp.sum(lq1_ref[...] * lk1_ref[...], axis=-1, keepdims=True))
           - jnp.exp(jnp.sum(lq2_ref[...] * lk2_ref[...], axis=-1, keepdims=True)) + lambda_init)
    o = acc1[:HEAD_V] / acc1[HEAD_V:HEAD_V + 1] - lam * (acc2[:HEAD_V] / acc2[HEAD_V:HEAD_V + 1])
    o = _rms_cols(o, g_ref[...]) * (1.0 - lambda_init)
    o_ref[0] = o.astype(o_ref.dtype)


def _diff_attention(q_t, k, v_t, lq1, lk1, lq2, lk2, subln_g, lambda_init):
    B, H2, _, S = q_t.shape
    H = H2 // 2
    tq = min(ATTN_TQ, S)
    tk = min(ATTN_TK, S // 2)
    vec = _const_spec((1, DIFF_QK_DIM))
    return pl.pallas_call(
        functools.partial(_diff_attn_kernel, tk=tk, lambda_init=lambda_init),
        grid=(B, H, S // tq),
        in_specs=[
            vec, vec, vec, vec, _const_spec((DIFF_V_DIM, 1)),
            pl.BlockSpec((1, 2, QK_PAD, tq), lambda b, h, i: (b, h, 0, i)),
            pl.BlockSpec((1, S, QK_PAD), lambda b, h, i: (b, 0, h // 2)),
            pl.BlockSpec((1, HEAD_V, S), lambda b, h, i: (b, h, 0)),
        ],
        out_specs=pl.BlockSpec((1, HEAD_V, tq), lambda b, h, i: (b, h, i)),
        out_shape=jax.ShapeDtypeStruct((B, H * HEAD_V, S), F32),
        scratch_shapes=[pltpu.VMEM((2, 2, tk, tq), F32)],
        compiler_params=_cparams(("parallel", "parallel", "arbitrary")), name="diff_attn",
    )(lq1, lk1, lq2, lk2, subln_g, q_t, k, v_t)


def _mixer_merge_kernel(x_ref, a_ref, b_ref, c_ref, gate_ref, wb_ref, wo_ref, g_ref, o_ref):
    merged = None
    for n, br_ref in enumerate((a_ref, b_ref, c_ref)):
        br = br_ref[0].T.astype(BF16)
        proj = _dot(br, wb_ref[n])
        term = gate_ref[0, :, n * D_MODEL:(n + 1) * D_MODEL].astype(F32) * proj
        merged = term if merged is None else merged + term
    y = _dot(merged.astype(BF16), wo_ref[...])
    o_ref[0] = x_ref[0] + _rms_rows(y, g_ref[...])


def _mixer_merge(x, a_t, b_t, c_t, gates, wb, wo, g):
    B, S, D = x.shape
    tm = min(MERGE_TM, S)
    xspec = pl.BlockSpec((1, tm, D), lambda b, i: (b, i, 0))
    brspec = pl.BlockSpec((1, BRANCH_WIDTH, tm), lambda b, i: (b, 0, i))
    return pl.pallas_call(
        _mixer_merge_kernel, grid=(B, S // tm),
        in_specs=[xspec, brspec, brspec, brspec,
                  pl.BlockSpec((1, tm, N_GATE), lambda b, i: (b, i, 0)),
                  _const_spec(wb.shape), _const_spec(wo.shape), _const_spec((1, D))],
        out_specs=xspec, out_shape=jax.ShapeDtypeStruct(x.shape, x.dtype),
        compiler_params=_cparams(("parallel", "parallel")), name="mixer_merge",
    )(x, a_t, b_t, c_t, gates, wb, wo, g)


def _norm_matmul_kernel(x_ref, g_ref, w_ref, o_ref):
    hn = _rms_rows(x_ref[...], g_ref[...]).astype(BF16)
    o_ref[...] = _dot(hn, w_ref[...]).astype(o_ref.dtype)


def _norm_matmul(x2d, g, w, out_dtype):
    T, D = x2d.shape
    N = w.shape[1]
    tm = min(KV_TM, T)
    return pl.pallas_call(
        _norm_matmul_kernel, grid=(T // tm,),
        in_specs=[pl.BlockSpec((tm, D), lambda i: (i, 0)), _const_spec((1, D)), _const_spec(w.shape)],
        out_specs=pl.BlockSpec((tm, N), lambda i: (i, 0)),
        out_shape=jax.ShapeDtypeStruct((T, N), out_dtype),
        compiler_params=_cparams(("parallel",)), name="mem_kv",
    )(x2d, g, w)


def _mem_attn_kernel(x_ref, kv_ref, gpre_ref, wq_ref, wo_ref, gpost_ref, o_ref):
    x = x_ref[0]
    hn = _rms_rows(x, gpre_ref[...]).astype(BF16)
    q = (_dot(hn, wq_ref[...]) * (MEM_HEAD_DIM ** -0.5 * LOG2E)).astype(BF16)
    outs = []
    for h in range(MEM_HEADS):
        lo = h * MEM_HEAD_DIM
        k_h = kv_ref[0, :, lo:lo + MEM_HEAD_DIM]
        v_h = kv_ref[0, :, D_MODEL + lo:D_MODEL + lo + MEM_HEAD_DIM]
        s = _dot_nt(q[:, lo:lo + MEM_HEAD_DIM], k_h)
        e = jnp.exp2(s - jnp.max(s, axis=-1, keepdims=True))
        p = (e / jnp.sum(e, axis=-1, keepdims=True)).astype(BF16)
        outs.append(_dot(p, v_h))
    o = jnp.concatenate(outs, axis=-1).astype(BF16)
    y = _dot(o, wo_ref[...])
    o_ref[0] = x + _rms_rows(y, gpost_ref[...])


def _mem_attention(x, kv, gpre, wq, wo, gpost):
    B, S, D = x.shape
    M = kv.shape[1]
    tm = min(MEM_TM, S)
    xspec = pl.BlockSpec((1, tm, D), lambda b, i: (b, i, 0))
    return pl.pallas_call(
        _mem_attn_kernel, grid=(B, S // tm),
        in_specs=[xspec, pl.BlockSpec((1, M, 2 * D), lambda b, i: (b, 0, 0)),
                  _const_spec((1, D)), _const_spec(wq.shape), _const_spec(wo.shape), _const_spec((1, D))],
        out_specs=xspec, out_shape=jax.ShapeDtypeStruct(x.shape, x.dtype),
        compiler_params=_cparams(("parallel", "parallel")), name="mem_attn",
    )(x, kv, gpre, wq, wo, gpost)


def _swiglu_chunk(hn, wg, wu, wd):
    gate = _dot(hn, wg)
    act = (gate * jax.nn.sigmoid(gate) * _dot(hn, wu)).astype(BF16)
    return _dot(act, wd)


def _ffn_kernel(x_ref, gpre_ref, wg_ref, wu_ref, wd_ref, gpost_ref, o_ref, hn_ref, acc_ref):
    j = pl.program_id(1)

    @pl.when(j == 0)
    def _():
        hn_ref[...] = _rms_rows(x_ref[...], gpre_ref[...]).astype(BF16)

    y = _swiglu_chunk(hn_ref[...], wg_ref[...], wu_ref[...], wd_ref[...])

    @pl.when(j == 0)
    def _():
        acc_ref[...] = y

    @pl.when(j > 0)
    def _():
        acc_ref[...] += y

    @pl.when(j == pl.num_programs(1) - 1)
    def _():
        o_ref[...] = x_ref[...] + _rms_rows(acc_ref[...], gpost_ref[...])


def _ffn(x2d, gpre, wg, wu, wd, gpost):
    T, D = x2d.shape
    F = wg.shape[1]
    tm = min(FFN_TM, T)
    tf = FFN_TF
    xspec = pl.BlockSpec((tm, D), lambda i, j: (i, 0))
    return pl.pallas_call(
        _ffn_kernel, grid=(T // tm, F // tf),
        in_specs=[xspec, _const_spec((1, D)),
                  pl.BlockSpec((D, tf), lambda i, j: (0, j)),
                  pl.BlockSpec((D, tf), lambda i, j: (0, j)),
                  pl.BlockSpec((tf, D), lambda i, j: (j, 0)),
                  _const_spec((1, D))],
        out_specs=xspec, out_shape=jax.ShapeDtypeStruct(x2d.shape, x2d.dtype),
        scratch_shapes=[pltpu.VMEM((tm, D), BF16), pltpu.VMEM((tm, D), F32)],
        compiler_params=_cparams(("parallel", "arbitrary")), name="ffn",
    )(x2d, gpre, wg, wu, wd, gpost)


def _router_kernel(x_ref, gpre_ref, wr_ref, br_ref, tri_ref, hn_ref, gate_ref, pos_ref, cnt_ref, run_ref):
    @pl.when(pl.program_id(0) == 0)
    def _():
        run_ref[...] = jnp.zeros_like(run_ref)

    hn = _rms_rows(x_ref[...], gpre_ref[...])
    hn_ref[...] = hn.astype(hn_ref.dtype)
    logits = lax.dot_general(wr_ref[...], hn, (((1,), (1,)), ((), ())),
                             precision=lax.Precision.HIGHEST,
                             preferred_element_type=F32) + br_ref[...]
    e_idx = lax.broadcasted_iota(jnp.int32, logits.shape, 0)
    m1 = jnp.max(logits, axis=0, keepdims=True)
    i1 = jnp.min(jnp.where(logits == m1, e_idx, N_EXPERTS), axis=0, keepdims=True)
    rest = jnp.where(e_idx == i1, -jnp.inf, logits)
    m2 = jnp.max(rest, axis=0, keepdims=True)
    i2 = jnp.min(jnp.where(rest == m2, e_idx, N_EXPERTS), axis=0, keepdims=True)
    e2 = jnp.exp(m2 - m1)
    w1 = 1.0 / (1.0 + e2)
    w2 = e2 / (1.0 + e2)
    gate_ref[...] = jnp.where(e_idx == i1, w1, 0.0) + jnp.where(e_idx == i2, w2, 0.0)

    sel = (e_idx == i1) | (e_idx == i2)
    sel_f = jnp.where(sel, 1.0, 0.0)
    before = _dot(sel_f.astype(BF16), tri_ref[...])
    run = run_ref[:, 0:1]
    pos_ref[...] = jnp.where(sel, run + before, -1.0)
    cnt = jnp.sum(sel_f, axis=1, keepdims=True)
    cnt_ref[0] = cnt
    run_ref[...] = run_ref[...] + cnt


def _router(x2d, gpre, wr_t, br_col):
    T, D = x2d.shape
    tm = MOE_BLOCK
    nb = T // tm
    tri = (np.arange(tm)[:, None] < np.arange(tm)[None, :]).astype(np.float32)
    eb = pl.BlockSpec((N_EXPERTS, tm), lambda i: (0, i))
    return pl.pallas_call(
        _router_kernel, grid=(nb,),
        in_specs=[pl.BlockSpec((tm, D), lambda i: (i, 0)), _const_spec((1, D)),
                  _const_spec(wr_t.shape), _const_spec(br_col.shape), _const_spec((tm, tm))],
        out_specs=[pl.BlockSpec((tm, D), lambda i: (i, 0)), eb, eb,
                   pl.BlockSpec((1, N_EXPERTS, 1), lambda i: (i, 0, 0))],
        out_shape=[jax.ShapeDtypeStruct((T, D), BF16),
                   jax.ShapeDtypeStruct((N_EXPERTS, T), F32),
                   jax.ShapeDtypeStruct((N_EXPERTS, T), F32),
                   jax.ShapeDtypeStruct((nb, N_EXPERTS, 1), F32)],
        scratch_shapes=[pltpu.VMEM((N_EXPERTS, 128), F32)],
        compiler_params=_cparams(("arbitrary",)), name="router",
    )(x2d, gpre, wr_t, br_col, jnp.asarray(tri, BF16))


_VALID, _FIRST, _LAST = 1, 2, 4


def _moe_schedule(cnt_be, g_max):
    nb, ne = cnt_be.shape
    tr = MOE_BLOCK
    cnt_eb = cnt_be.T
    start_eb = jnp.cumsum(cnt_eb, axis=1) - cnt_eb
    n_tile_e = (jnp.sum(cnt_eb, axis=1) + tr - 1) // tr
    tile0_e = jnp.cumsum(n_tile_e) - n_tile_e
    first_eb = start_eb // tr
    n_eb = jnp.where(cnt_eb > 0, (start_eb + cnt_eb - 1) // tr - first_eb + 1, 0)
    g = jnp.arange(g_max, dtype=jnp.int32)

    def visit_list(pair_e, pair_b, group_of):
        n = n_eb[pair_e, pair_b]
        ends = jnp.cumsum(n)
        total = ends[-1]
        gi = jnp.minimum(g, total - 1)
        pair = jnp.searchsorted(ends, gi, side="right").astype(jnp.int32)
        e = pair_e[pair]
        b = pair_b[pair]
        local = first_eb[e, b] + gi - (ends[pair] - n[pair])
        tile = tile0_e[e] + local
        grp = group_of(tile, b)
        valid = g < total
        first = valid & ((g == 0) | (grp != jnp.roll(grp, 1)))
        last = valid & ((g == total - 1) | (grp != jnp.roll(grp, -1)))
        flags = valid * _VALID + first * _FIRST + last * _LAST
        return [v.astype(jnp.int32) for v in (e, b, tile, local * tr, flags)]

    ee, bb = np.meshgrid(np.arange(ne), np.arange(nb), indexing="ij")
    list_a = visit_list(jnp.asarray(ee.reshape(-1)), jnp.asarray(bb.reshape(-1)), lambda tile, b: tile)
    list_b = visit_list(jnp.asarray(ee.T.reshape(-1)), jnp.asarray(bb.T.reshape(-1)), lambda tile, b: b)
    return list_a, list_b


def _moe_expert_kernel(e_ref, b_ref, t_ref, r_ref, f_ref, hn_ref, pos_ref, wg_ref, wu_ref, wd_ref,
                       ys_ref, xs_ref):
    g = pl.program_id(0)
    flags = f_ref[g]
    tr = xs_ref.shape[0]

    @pl.when((flags & _VALID) != 0)
    def _():
        rows = (lax.broadcasted_iota(jnp.int32, (tr, 1), 0) + r_ref[g]).astype(F32)
        onehot = jnp.where(rows == pos_ref[0, 0], 1.0, 0.0).astype(BF16)
        picked = _dot(onehot, hn_ref[...])

        @pl.when((flags & _FIRST) != 0)
        def _():
            xs_ref[...] = picked

        @pl.when((flags & _FIRST) == 0)
        def _():
            xs_ref[...] += picked

    @pl.when((flags & _LAST) != 0)
    def _():
        xs = xs_ref[...].astype(BF16)
        f_total = wg_ref.shape[2]
        y = None
        for lo in range(0, f_total, MOE_TF):
            part = _swiglu_chunk(xs, wg_ref[0, :, lo:lo + MOE_TF], wu_ref[0, :, lo:lo + MOE_TF],
                                 wd_ref[0, lo:lo + MOE_TF, :])
            y = part if y is None else y + part
        ys_ref[...] = y.astype(ys_ref.dtype)


def _moe_expert(list_a, hn, pos, wg, wu, wd, n_tiles):
    T, D = hn.shape
    E, _, F = wg.shape
    tr = MOE_BLOCK
    g_max = list_a[0].shape[0]
    pos4 = pos.reshape(E, T // tr, 1, tr)
    wspec_in = pl.BlockSpec((1, D, F), lambda g, e, b, t, r, f: (e[g], 0, 0), pipeline_mode=pl.Buffered(1))
    wspec_out = pl.BlockSpec((1, F, D), lambda g, e, b, t, r, f: (e[g], 0, 0), pipeline_mode=pl.Buffered(1))
    grid_spec = pltpu.PrefetchScalarGridSpec(
        num_scalar_prefetch=5, grid=(g_max,),
        in_specs=[pl.BlockSpec((tr, D), lambda g, e, b, t, r, f: (b[g], 0)),
                  pl.BlockSpec((1, 1, 1, tr), lambda g, e, b, t, r, f: (e[g], b[g], 0, 0)),
                  wspec_in, wspec_in, wspec_out],
        out_specs=pl.BlockSpec((tr, D), lambda g, e, b, t, r, f: (t[g], 0)),
        scratch_shapes=[pltpu.VMEM((tr, D), F32)])
    return pl.pallas_call(
        _moe_expert_kernel, grid_spec=grid_spec,
        out_shape=jax.ShapeDtypeStruct((n_tiles * tr, D), BF16),
        compiler_params=_cparams(("arbitrary",)), name="moe_expert",
    )(*list_a, hn, pos4, wg, wu, wd)


def _moe_combine_kernel(e_ref, b_ref, t_ref, r_ref, f_ref, x_ref, ys_ref, pos_ref, gate_ref, gpost_ref,
                        o_ref, acc_ref):
    g = pl.program_id(0)
    flags = f_ref[g]
    tr = ys_ref.shape[0]

    @pl.when((flags & _VALID) != 0)
    def _():
        rows = (lax.broadcasted_iota(jnp.int32, (1, tr), 1) + r_ref[g]).astype(F32)
        onehot = jnp.where(pos_ref[0] == rows, 1.0, 0.0).astype(BF16)
        contrib = gate_ref[0] * _dot(onehot, ys_ref[...])

        @pl.when((flags & _FIRST) != 0)
        def _():
            acc_ref[...] = contrib

        @pl.when((flags & _FIRST) == 0)
        def _():
            acc_ref[...] += contrib

    @pl.when((flags & _LAST) != 0)
    def _():
        o_ref[...] = x_ref[...] + _rms_rows(acc_ref[...], gpost_ref[...])


def _moe_combine(list_b, x2d, ys, pos, gates, gpost):
    T, D = x2d.shape
    E = gates.shape[0]
    tr = MOE_BLOCK
    g_max = list_b[0].shape[0]
    xspec = pl.BlockSpec((tr, D), lambda g, e, b, t, r, f: (b[g], 0))
    colspec = pl.BlockSpec((1, tr, 1), lambda g, e, b, t, r, f: (e[g], b[g], 0))
    grid_spec = pltpu.PrefetchScalarGridSpec(
        num_scalar_prefetch=5, grid=(g_max,),
        in_specs=[xspec, pl.BlockSpec((tr, D), lambda g, e, b, t, r, f: (t[g], 0)), colspec, colspec,
                  pl.BlockSpec((1, D), lambda g, e, b, t, r, f: (0, 0))],
        out_specs=xspec,
        scratch_shapes=[pltpu.VMEM((tr, D), F32)])
    return pl.pallas_call(
        _moe_combine_kernel, grid_spec=grid_spec,
        out_shape=jax.ShapeDtypeStruct(x2d.shape, x2d.dtype),
        compiler_params=_cparams(("arbitrary",)), name="moe_combine",
    )(*list_b, x2d, ys, pos.reshape(E, T, 1), gates.reshape(E, T, 1), gpost)


def _moe(x2d, gpre, wr_t, br_col, wg, wu, wd, gpost):
    T, D = x2d.shape
    E = wg.shape[0]
    nb = T // MOE_BLOCK
    n_tiles = TOP_K * nb + E
    g_max = E * nb + n_tiles
    hn, gates, pos, cnt = _router(x2d, gpre, wr_t, br_col)
    list_a, list_b = _moe_schedule(cnt.reshape(nb, E).astype(jnp.int32), g_max)
    ys = _moe_expert(list_a, hn, pos, wg, wu, wd, n_tiles)
    return _moe_combine(list_b, x2d, ys, pos, gates, gpost)


def _feature_row_order():
    def diff_block(base):
        x1, x2, rest = [], [], []
        half = DIFF_ROPE_DIM // 2
        for h in range(DIFF_HEADS):
            for c in range(2):
                lo = base + h * 2 * DIFF_QK_DIM + c * DIFF_QK_DIM
                x1 += range(lo, lo + half)
                x2 += range(lo + half, lo + DIFF_ROPE_DIM)
                rest += range(lo + DIFF_ROPE_DIM, lo + DIFF_QK_DIM)
        return x1 + x2 + rest

    order = list(range(0, _S_DQ)) + diff_block(_S_DQ) + diff_block(_S_DK) + list(range(_S_DV, N_FEAT))
    return np.asarray(order, np.int32)


def _mla_q_up_order():
    w = MLA_NOPE_DIM + MLA_ROPE_DIM
    half = MLA_ROPE_DIM // 2
    nope = [h * w + j for h in range(MLA_HEADS) for j in range(MLA_NOPE_DIM)]
    x1 = [h * w + MLA_NOPE_DIM + j for h in range(MLA_HEADS) for j in range(half)]
    x2 = [h * w + MLA_NOPE_DIM + half + j for h in range(MLA_HEADS) for j in range(half)]
    return np.asarray(nope + x1 + x2, np.int32)


def _mla_kv_up_order():
    w = MLA_NOPE_DIM + MLA_V_DIM
    k = [h * w + j for h in range(MLA_HEADS) for j in range(MLA_NOPE_DIM)]
    v = [h * w + MLA_NOPE_DIM + j for h in range(MLA_HEADS) for j in range(MLA_V_DIM)]
    return np.asarray(k + v, np.int32)


def _rope_tables(S):
    def cos_sin(pos, rot_dim):
        inv = ROPE_THETA ** (-jnp.arange(0, rot_dim, 2, dtype=F32) / rot_dim)
        ang = pos.astype(F32)[:, None] * inv[None, :]
        return jnp.cos(ang).T, jnp.sin(ang).T

    rows = S // GRID_W
    pos = jnp.arange(S)
    row = jnp.repeat(jnp.arange(rows), GRID_W)
    col = jnp.tile(jnp.arange(GRID_W), rows)
    cm, sm = cos_sin(pos, MLA_ROPE_DIM)
    cr, sr = cos_sin(row, GQA_HEAD_DIM // 2)
    cc, sc = cos_sin(col, GQA_HEAD_DIM // 2)
    cp, sp = cos_sin(pos, DIFF_ROPE_DIM)
    return (jnp.tile(cm, (MLA_HEADS, 1)), jnp.tile(sm, (MLA_HEADS, 1)),
            jnp.concatenate([cr, cc], axis=0), jnp.concatenate([sr, sc], axis=0),
            jnp.tile(cp, (2 * DIFF_HEADS, 1)), jnp.tile(sp, (2 * DIFF_HEADS, 1)))


def _row(v):
    return v.reshape(1, -1).astype(F32)


def _col(v):
    return v.reshape(-1, 1).astype(F32)


def kernel(x, mem, mix_pre_g, mix_post_g, w_in, mla_q_norm_g, mla_w_q_up, mla_kv_norm_g, mla_w_kv_up,
           gqa_q_norm_g, gqa_k_norm_g, diff_lambda_q1, diff_lambda_k1, diff_lambda_q2, diff_lambda_k2,
           diff_subln_g, w_branch, w_out, mem_pre_g, mem_post_g, mem_norm_g, mem_wq, mem_wkv, mem_wo,
           ffn_pre_g, ffn_post_g, dense_w_gate, dense_w_up, dense_w_down, moe_w_router, moe_b_router,
           moe_w_gate, moe_w_up, moe_w_down):
    B, S, D = x.shape
    M = mem.shape[1]
    depth = w_in.shape[0]
    tables = _rope_tables(S)
    feat_order = _feature_row_order()
    q_up_order = _mla_q_up_order()
    kv_up_order = _mla_kv_up_order()

    for layer in range(depth):
        lambda_init = 0.8 - 0.6 * math.exp(-0.3 * layer)

        w_l = w_in[layer]
        wt = w_l[:, feat_order].T.astype(BF16)
        wg = w_l[:, N_FEAT:].astype(BF16)
        wqu = mla_w_q_up[layer][:, q_up_order].T.astype(BF16)
        wkvu = mla_w_kv_up[layer][:, kv_up_order].T.astype(BF16)
        (qm, km, vm, qg, kg, vg, qd, kd, vd, gates) = _mixer_prep(
            x, _row(mix_pre_g[layer]), wt, wg, wqu, wkvu,
            _col(mla_q_norm_g[layer]), _col(mla_kv_norm_g[layer]),
            _col(gqa_q_norm_g[layer]), _col(gqa_k_norm_g[layer]), tables)
        a_t = _attention(qm, km, vm, kv_of_head=lambda h: h, kcol_of_head=lambda h: h, name="mla_attn")
        b_t = _attention(qg, kg, vg, kv_of_head=lambda h: h // GQA_GROUP, kcol_of_head=lambda h: 0,
                         name="gqa_attn")
        c_t = _diff_attention(qd, kd, vd, _row(diff_lambda_q1[layer]), _row(diff_lambda_k1[layer]),
                              _row(diff_lambda_q2[layer]), _row(diff_lambda_k2[layer]),
                              _col(diff_subln_g[layer]), lambda_init)
        x = _mixer_merge(x, a_t, b_t, c_t, gates, w_branch[layer].astype(BF16),
                         w_out[layer].astype(BF16), _row(mix_post_g[layer]))

        kv = _norm_matmul(mem.reshape(B * M, D), _row(mem_norm_g[layer]), mem_wkv[layer].astype(BF16), BF16)
        x = _mem_attention(x, kv.reshape(B, M, 2 * D), _row(mem_pre_g[layer]), mem_wq[layer].astype(BF16),
                           mem_wo[layer].astype(BF16), _row(mem_post_g[layer]))

        x2d = x.reshape(B * S, D)
        i = layer // 2
        if layer % 2 == 0:
            x2d = _ffn(x2d, _row(ffn_pre_g[layer]), dense_w_gate[i].astype(BF16), dense_w_up[i].astype(BF16),
                       dense_w_down[i].astype(BF16), _row(ffn_post_g[layer]))
        else:
            x2d = _moe(x2d, _row(ffn_pre_g[layer]), moe_w_router[i].T.astype(F32), _col(moe_b_router[i]),
                       moe_w_gate[i].astype(BF16), moe_w_up[i].astype(BF16), moe_w_down[i].astype(BF16),
                       _row(ffn_post_g[layer]))
        x = x2d.reshape(B, S, D)
    return x
```

```python
import functools
import math

import numpy as np
import jax
import jax.numpy as jnp
from jax import lax
from jax.experimental import pallas as pl
from jax.experimental.pallas import tpu as pltpu

F32 = jnp.float32
BF16 = jnp.bfloat16

D_MODEL = 1024
GRID_W = 64
ROPE_THETA = 500000.0
NORM_EPS = 1e-6

MLA_HEADS = 8
MLA_Q_LORA = 384
MLA_KV_LORA = 256
MLA_NOPE_DIM = 64
MLA_ROPE_DIM = 32
MLA_V_DIM = 64

GQA_Q_HEADS = 8
GQA_KV_HEADS = 2
GQA_GROUP = GQA_Q_HEADS // GQA_KV_HEADS
GQA_HEAD_DIM = 64

DIFF_HEADS = 8
DIFF_QK_DIM = 32
DIFF_V_DIM = 2 * DIFF_QK_DIM
DIFF_ROPE_DIM = DIFF_QK_DIM // 4

N_BRANCHES = 3
BRANCH_WIDTH = 512
HEAD_V = 64

MEM_HEADS = 4
MEM_HEAD_DIM = D_MODEL // MEM_HEADS

D_FF = 2816
N_EXPERTS = 8
TOP_K = 2

IN_SPLITS = (MLA_Q_LORA, MLA_KV_LORA, MLA_ROPE_DIM,
             GQA_Q_HEADS * GQA_HEAD_DIM, GQA_KV_HEADS * GQA_HEAD_DIM, GQA_KV_HEADS * GQA_HEAD_DIM,
             DIFF_HEADS * 2 * DIFF_QK_DIM, DIFF_HEADS * 2 * DIFF_QK_DIM, DIFF_HEADS * DIFF_V_DIM,
             N_BRANCHES * D_MODEL)
IN_OFFS = tuple(int(v) for v in np.cumsum((0,) + IN_SPLITS))
N_FEAT = IN_OFFS[9]
N_GATE = IN_SPLITS[9]

LOG2E = 1.4426950408889634
QK_PAD = 128
ONES_ROWS = 16

V7X_VMEM_LIMIT_BYTES = 56 * 1024 * 1024

PREP_TM = 512
ATTN_TQ = 1024
ATTN_TK = 256
ATTN_UNROLL = 8
MERGE_TM = 256
MEM_TM = 512
FFN_TM = 512
FFN_TF = 1408
MOE_BLOCK = 512
MOE_TF = 1408
KV_TM = 256


def _cparams(sem):
    return pltpu.CompilerParams(dimension_semantics=sem, vmem_limit_bytes=V7X_VMEM_LIMIT_BYTES)


def _const_spec(shape):
    nd = len(shape)
    return pl.BlockSpec(shape, lambda *_: (0,) * nd, pipeline_mode=pl.Buffered(1))


def _rms_rows(x, g_row):
    ms = jnp.mean(x * x, axis=-1, keepdims=True)
    return x * lax.rsqrt(ms + NORM_EPS) * g_row


def _rms_cols(x, g_col):
    ms = jnp.mean(x * x, axis=0, keepdims=True)
    return x * lax.rsqrt(ms + NORM_EPS) * g_col


def _dot(a, b):
    return jnp.dot(a, b, preferred_element_type=F32)


def _dot_nt(a, b):
    return lax.dot_general(a, b, (((1,), (1,)), ((), ())), preferred_element_type=F32)


_S_CQ, _S_CKV, _S_KR, _S_GQ, _S_GK, _S_GV, _S_DQ, _S_DK, _S_DV = IN_OFFS[:9]


def _rope_rows(x1, x2, c, s):
    return x1 * c - x2 * s, x2 * c + x1 * s


def _mixer_prep_kernel(x_ref, g_ref, wt_ref, wg_ref, wqu_ref, wkvu_ref,
                       mqg_ref, mkvg_ref, gqg_ref, gkg_ref,
                       cm_ref, sm_ref, ca_ref, sa_ref, cp_ref, sp_ref,
                       qm_ref, km_ref, vm_ref, qg_ref, kg_ref, vg_ref,
                       qd_ref, kd_ref, vd_ref, gate_ref):
    tm = x_ref.shape[1]
    hn = _rms_rows(x_ref[0], g_ref[...]).astype(BF16)

    def feat(lo, n):
        return _dot_nt(wt_ref[lo:lo + n, :], hn)

    gchunk = 1024
    for j in range(N_GATE // gchunk):
        z = _dot(hn, wg_ref[:, j * gchunk:(j + 1) * gchunk])
        gate_ref[0, :, j * gchunk:(j + 1) * gchunk] = jax.nn.sigmoid(z).astype(gate_ref.dtype)

    zeros32 = jnp.zeros((32, tm), F32)
    zeros64 = jnp.zeros((64, tm), F32)

    c_mla = (MLA_NOPE_DIM + MLA_ROPE_DIM) ** -0.5 * LOG2E
    cm = cm_ref[...]
    sm = sm_ref[...]
    cqn = _rms_cols(feat(_S_CQ, MLA_Q_LORA), mqg_ref[...]).astype(BF16)
    q_all = _dot(wqu_ref[...], cqn)
    q_nope = q_all[0:512] * c_mla
    q_r1, q_r2 = _rope_rows(q_all[512:640], q_all[640:768], cm, sm)
    q_r1 = q_r1 * c_mla
    q_r2 = q_r2 * c_mla
    for h in range(MLA_HEADS):
        qm_ref[0, h] = jnp.concatenate(
            [q_nope[64 * h:64 * h + 64], q_r1[16 * h:16 * h + 16], q_r2[16 * h:16 * h + 16], zeros32],
            axis=0).astype(qm_ref.dtype)

    ckvn = _rms_cols(feat(_S_CKV, MLA_KV_LORA), mkvg_ref[...]).astype(BF16)
    kv_all = _dot(wkvu_ref[...], ckvn)
    vm_ref[0] = kv_all[512:1024].astype(vm_ref.dtype)
    kr = feat(_S_KR, MLA_ROPE_DIM)
    k_r1, k_r2 = _rope_rows(kr[0:16], kr[16:32], cm[0:16], sm[0:16])
    for h in range(MLA_HEADS):
        kt = jnp.concatenate([kv_all[64 * h:64 * h + 64], k_r1, k_r2, zeros32], axis=0)
        km_ref[0, :, 128 * h:128 * h + 128] = kt.T.astype(km_ref.dtype)

    c_gqa = GQA_HEAD_DIM ** -0.5 * LOG2E
    ca = ca_ref[...]
    sa = sa_ref[...]
    gq = feat(_S_GQ, GQA_Q_HEADS * GQA_HEAD_DIM)
    for h in range(GQA_Q_HEADS):
        qn = _rms_cols(gq[64 * h:64 * h + 64], gqg_ref[...])
        r1, r2 = _rope_rows(qn[0:32], qn[32:64], ca, sa)
        q64 = jnp.concatenate([r1, r2], axis=0) * c_gqa
        parts = [q64, zeros64] if h // GQA_GROUP == 0 else [zeros64, q64]
        qg_ref[0, h] = jnp.concatenate(parts, axis=0).astype(qg_ref.dtype)
    gk = feat(_S_GK, GQA_KV_HEADS * GQA_HEAD_DIM)
    kparts = []
    for g in range(GQA_KV_HEADS):
        kn = _rms_cols(gk[64 * g:64 * g + 64], gkg_ref[...])
        r1, r2 = _rope_rows(kn[0:32], kn[32:64], ca, sa)
        kparts += [r1, r2]
    kg_ref[0] = jnp.concatenate(kparts, axis=0).T.astype(kg_ref.dtype)
    vg_ref[0] = feat(_S_GV, GQA_KV_HEADS * GQA_HEAD_DIM).astype(vg_ref.dtype)

    c_diff = DIFF_QK_DIM ** -0.5 * LOG2E
    cp = cp_ref[...]
    sp = sp_ref[...]
    row = lax.broadcasted_iota(jnp.int32, (64, 1), 0)
    in_c0 = (row < 4) | ((row >= 8) & (row < 12)) | ((row >= 16) & (row < 40))

    def diff_heads(lo):
        z = feat(lo, DIFF_HEADS * 2 * DIFF_QK_DIM)
        r1, r2 = _rope_rows(z[0:64], z[64:128], cp, sp)
        rest = z[128:512]
        return [jnp.concatenate([r1[8 * h:8 * h + 8], r2[8 * h:8 * h + 8], rest[48 * h:48 * h + 48]], axis=0)
                for h in range(DIFF_HEADS)]

    for h, q64 in enumerate(diff_heads(_S_DQ)):
        q64 = q64 * c_diff
        for c in range(2):
            qc = jnp.where(in_c0 if c == 0 else jnp.logical_not(in_c0), q64, 0.0)
            parts = [qc, zeros64] if h % 2 == 0 else [zeros64, qc]
            qd_ref[0, 2 * h + c] = jnp.concatenate(parts, axis=0).astype(qd_ref.dtype)
    k_heads = diff_heads(_S_DK)
    for p in range(DIFF_HEADS // 2):
        kt = jnp.concatenate([k_heads[2 * p], k_heads[2 * p + 1]], axis=0)
        kd_ref[0, :, 128 * p:128 * p + 128] = kt.T.astype(kd_ref.dtype)
    vd_ref[0] = feat(_S_DV, DIFF_HEADS * DIFF_V_DIM).astype(vd_ref.dtype)


def _mixer_prep(x, g, wt, wg, wqu, wkvu, mqg, mkvg, gqg, gkg, tables):
    B, S, D = x.shape
    tm = min(PREP_TM, S)
    cm, sm, ca, sa, cp, sp = tables
    grid = (B, S // tm)

    def tok(width):
        return pl.BlockSpec((1, tm, width), lambda b, i: (b, i, 0))

    def featm(rows):
        return pl.BlockSpec((1, rows, tm), lambda b, i: (b, 0, i))

    def heads(n):
        return pl.BlockSpec((1, n, QK_PAD, tm), lambda b, i: (b, 0, 0, i))

    def table(rows):
        return pl.BlockSpec((rows, tm), lambda b, i: (0, i))

    in_specs = [
        pl.BlockSpec((1, tm, D), lambda b, i: (b, i, 0)),
        _const_spec((1, D)),
        _const_spec(wt.shape), _const_spec(wg.shape), _const_spec(wqu.shape), _const_spec(wkvu.shape),
        _const_spec(mqg.shape), _const_spec(mkvg.shape), _const_spec(gqg.shape), _const_spec(gkg.shape),
        table(128), table(128), table(32), table(32), table(64), table(64),
    ]
    out_shape = [
        jax.ShapeDtypeStruct((B, MLA_HEADS, QK_PAD, S), BF16),
        jax.ShapeDtypeStruct((B, S, MLA_HEADS * QK_PAD), BF16),
        jax.ShapeDtypeStruct((B, MLA_HEADS * HEAD_V, S), BF16),
        jax.ShapeDtypeStruct((B, GQA_Q_HEADS, QK_PAD, S), BF16),
        jax.ShapeDtypeStruct((B, S, QK_PAD), BF16),
        jax.ShapeDtypeStruct((B, GQA_KV_HEADS * HEAD_V, S), BF16),
        jax.ShapeDtypeStruct((B, 2 * DIFF_HEADS, QK_PAD, S), BF16),
        jax.ShapeDtypeStruct((B, S, DIFF_HEADS // 2 * QK_PAD), BF16),
        jax.ShapeDtypeStruct((B, DIFF_HEADS * HEAD_V, S), BF16),
        jax.ShapeDtypeStruct((B, S, N_GATE), BF16),
    ]
    out_specs = [
        heads(MLA_HEADS), tok(MLA_HEADS * QK_PAD), featm(MLA_HEADS * HEAD_V),
        heads(GQA_Q_HEADS), tok(QK_PAD), featm(GQA_KV_HEADS * HEAD_V),
        heads(2 * DIFF_HEADS), tok(DIFF_HEADS // 2 * QK_PAD), featm(DIFF_HEADS * HEAD_V),
        tok(N_GATE),
    ]
    return pl.pallas_call(
        _mixer_prep_kernel, grid=grid, in_specs=in_specs, out_specs=out_specs, out_shape=out_shape,
        compiler_params=_cparams(("parallel", "parallel")), name="mixer_prep",
    )(x, g, wt, wg, wqu, wkvu, mqg, mkvg, gqg, gkg, cm, sm, ca, sa, cp, sp)


def _online_softmax_attention(q_maps, k_ref, v_ref, s_ref, *, tk):
    S = k_ref.shape[1]
    tq = q_maps[0].shape[1]
    n_chunks = S // tk
    unroll = min(ATTN_UNROLL, n_chunks)
    assert unroll % 2 == 0 and n_chunks % unroll == 0
    n_maps = len(q_maps)
    ones = jnp.ones((ONES_ROWS, tk), BF16)

    def scores(c, slot):
        kc = k_ref[0, pl.ds(pl.multiple_of(c * tk, tk), tk), :]
        col_max = []
        for j, q_t in enumerate(q_maps):
            s_t = _dot(kc, q_t)
            s_ref[slot, j] = s_t
            col_max.append(jnp.max(s_t, axis=0, keepdims=True))
        return col_max

    def update(c, slot, col_max, ms, accs):
        v_aug = jnp.concatenate([v_ref[0, :, pl.ds(pl.multiple_of(c * tk, tk), tk)], ones], axis=0)
        new_ms, new_accs = [], []
        for j in range(n_maps):
            m_new = jnp.maximum(ms[j], col_max[j])
            alpha = jnp.exp2(ms[j] - m_new)
            p = jnp.exp2(s_ref[slot, j] - m_new).astype(BF16)
            new_ms.append(m_new)
            new_accs.append(accs[j] * alpha + _dot(v_aug, p))
        return new_ms, new_accs

    def group(c, cmax, ms, accs, last):
        for u in range(unroll):
            is_final = last and u == unroll - 1
            cmax_next = None if is_final else scores(c + u + 1, (u + 1) % 2)
            ms, accs = update(c + u, u % 2, cmax, ms, accs)
            cmax = cmax_next
        return cmax, ms, accs

    def body(i, carry):
        return group(unroll * i, *carry, last=False)

    m0 = [jnp.full((1, tq), -jnp.inf, F32)] * n_maps
    acc0 = [jnp.zeros((HEAD_V + ONES_ROWS, tq), F32)] * n_maps
    carry = lax.fori_loop(0, n_chunks // unroll - 1, body, (scores(0, 0), m0, acc0))
    _, _, accs = group(n_chunks - unroll, *carry, last=True)
    return accs


def _attn_kernel(q_ref, k_ref, v_ref, o_ref, s_ref, *, tk):
    (acc,) = _online_softmax_attention([q_ref[0, 0]], k_ref, v_ref, s_ref, tk=tk)
    o_ref[0] = (acc[:HEAD_V] / acc[HEAD_V:HEAD_V + 1]).astype(o_ref.dtype)


def _attention(q_t, k, v_t, *, kv_of_head, kcol_of_head, name):
    B, H, _, S = q_t.shape
    tq = min(ATTN_TQ, S)
    tk = min(ATTN_TK, S // 2)
    return pl.pallas_call(
        functools.partial(_attn_kernel, tk=tk),
        grid=(B, H, S // tq),
        in_specs=[
            pl.BlockSpec((1, 1, QK_PAD, tq), lambda b, h, i: (b, h, 0, i)),
            pl.BlockSpec((1, S, QK_PAD), lambda b, h, i: (b, 0, kcol_of_head(h))),
            pl.BlockSpec((1, HEAD_V, S), lambda b, h, i: (b, kv_of_head(h), 0)),
        ],
        out_specs=pl.BlockSpec((1, HEAD_V, tq), lambda b, h, i: (b, h, i)),
        out_shape=jax.ShapeDtypeStruct((B, H * HEAD_V, S), F32),
        scratch_shapes=[pltpu.VMEM((2, 1, tk, tq), F32)],
        compiler_params=_cparams(("parallel", "parallel", "arbitrary")), name=name,
    )(q_t, k, v_t)


def _diff_attn_kernel(lq1_ref, lk1_ref, lq2_ref, lk2_ref, g_ref, q_ref, k_ref, v_ref, o_ref, s_ref, *,
                      tk, lambda_init):
    acc1, acc2 = _online_softmax_attention([q_ref[0, 0], q_ref[0, 1]], k_ref, v_ref, s_ref, tk=tk)
    lam =(jnp.exp(jnp.sum(lq1_ref[...] * lk1_ref[...], axis=-1, keepdims=True))
           - jnp.exp(jnp.sum(lq2_ref[...] * lk2_ref[...], axis=-1, keepdims=True)) + lambda_init)
    o = acc1[:HEAD_V] / acc1[HEAD_V:HEAD_V + 1] - lam * (acc2[:HEAD_V] / acc2[HEAD_V:HEAD_V + 1])
    o = _rms_cols(o, g_ref[...]) * (1.0 - lambda_init)
    o_ref[0] = o.astype(o_ref.dtype)


def _diff_attention(q_t, k, v_t, lq1, lk1, lq2, lk2, subln_g, lambda_init):
    B, H2, _, S = q_t.shape
    H = H2 // 2
    tq = min(ATTN_TQ, S)
    tk = min(ATTN_TK, S // 2)
    vec = _const_spec((1, DIFF_QK_DIM))
    return pl.pallas_call(
        functools.partial(_diff_attn_kernel, tk=tk, lambda_init=lambda_init),
        grid=(B, H, S // tq),
        in_specs=[
            vec, vec, vec, vec, _const_spec((DIFF_V_DIM, 1)),
            pl.BlockSpec((1, 2, QK_PAD, tq), lambda b, h, i: (b, h, 0, i)),
            pl.BlockSpec((1, S, QK_PAD), lambda b, h, i: (b, 0, h // 2)),
            pl.BlockSpec((1, HEAD_V, S), lambda b, h, i: (b, h, 0)),
        ],
        out_specs=pl.BlockSpec((1, HEAD_V, tq), lambda b, h, i: (b, h, i)),
        out_shape=jax.ShapeDtypeStruct((B, H * HEAD_V, S), F32),
        scratch_shapes=[pltpu.VMEM((2, 2, tk, tq), F32)],
        compiler_params=_cparams(("parallel", "parallel", "arbitrary")), name="diff_attn",
    )(lq1, lk1, lq2, lk2, subln_g, q_t, k, v_t)


def _mixer_merge_kernel(x_ref, a_ref, b_ref, c_ref, gate_ref, wb_ref, wo_ref, g_ref, o_ref):
    merged = None
    for n, br_ref in enumerate((a_ref, b_ref, c_ref)):
        br = br_ref[0].T.astype(BF16)
        proj = _dot(br, wb_ref[n])
        term = gate_ref[0, :, n * D_MODEL:(n + 1) * D_MODEL].astype(F32) * proj
        merged = term if merged is None else merged + term
    y = _dot(merged.astype(BF16), wo_ref[...])
    o_ref[0] = x_ref[0] + _rms_rows(y, g_ref[...])


def _mixer_merge(x, a_t, b_t, c_t, gates, wb, wo, g):
    B, S, D = x.shape
    tm = min(MERGE_TM, S)
    xspec = pl.BlockSpec((1, tm, D), lambda b, i: (b, i, 0))
    brspec = pl.BlockSpec((1, BRANCH_WIDTH, tm), lambda b, i: (b, 0, i))
    return pl.pallas_call(
        _mixer_merge_kernel, grid=(B, S // tm),
        in_specs=[xspec, brspec, brspec, brspec,
                  pl.BlockSpec((1, tm, N_GATE), lambda b, i: (b, i, 0)),
                  _const_spec(wb.shape), _const_spec(wo.shape), _const_spec((1, D))],
        out_specs=xspec, out_shape=jax.ShapeDtypeStruct(x.shape, x.dtype),
        compiler_params=_cparams(("parallel", "parallel")), name="mixer_merge",
    )(x, a_t, b_t, c_t, gates, wb, wo, g)


def _norm_matmul_kernel(x_ref, g_ref, w_ref, o_ref):
    hn = _rms_rows(x_ref[...], g_ref[...]).astype(BF16)
    o_ref[...] = _dot(hn, w_ref[...]).astype(o_ref.dtype)


def _norm_matmul(x2d, g, w, out_dtype):
    T, D = x2d.shape
    N = w.shape[1]
    tm = min(KV_TM, T)
    return pl.pallas_call(
        _norm_matmul_kernel, grid=(T // tm,),
        in_specs=[pl.BlockSpec((tm, D), lambda i: (i, 0)), _const_spec((1, D)), _const_spec(w.shape)],
        out_specs=pl.BlockSpec((tm, N), lambda i: (i, 0)),
        out_shape=jax.ShapeDtypeStruct((T, N), out_dtype),
        compiler_params=_cparams(("parallel",)), name="mem_kv",
    )(x2d, g, w)


def _mem_attn_kernel(x_ref, kv_ref, gpre_ref, wq_ref, wo_ref, gpost_ref, o_ref):
    x = x_ref[0]
    hn = _rms_rows(x, gpre_ref[...]).astype(BF16)
    q = (_dot(hn, wq_ref[...]) * (MEM_HEAD_DIM ** -0.5 * LOG2E)).astype(BF16)
    outs = []
    for h in range(MEM_HEADS):
        lo = h * MEM_HEAD_DIM
        k_h = kv_ref[0, :, lo:lo + MEM_HEAD_DIM]
        v_h = kv_ref[0, :, D_MODEL + lo:D_MODEL + lo + MEM_HEAD_DIM]
        s = _dot_nt(q[:, lo:lo + MEM_HEAD_DIM], k_h)
        e = jnp.exp2(s - jnp.max(s, axis=-1, keepdims=True))
        p = (e / jnp.sum(e, axis=-1, keepdims=True)).astype(BF16)
        outs.append(_dot(p, v_h))
    o = jnp.concatenate(outs, axis=-1).astype(BF16)
    y = _dot(o, wo_ref[...])
    o_ref[0] = x + _rms_rows(y, gpost_ref[...])


def _mem_attention(x, kv, gpre, wq, wo, gpost):
    B, S, D = x.shape
    M = kv.shape[1]
    tm = min(MEM_TM, S)
    xspec = pl.BlockSpec((1, tm, D), lambda b, i: (b, i, 0))
    return pl.pallas_call(
        _mem_attn_kernel, grid=(B, S // tm),
        in_specs=[xspec, pl.BlockSpec((1, M, 2 * D), lambda b, i: (b, 0, 0)),
                  _const_spec((1, D)), _const_spec(wq.shape), _const_spec(wo.shape), _const_spec((1, D))],
        out_specs=xspec, out_shape=jax.ShapeDtypeStruct(x.shape, x.dtype),
        compiler_params=_cparams(("parallel", "parallel")), name="mem_attn",
    )(x, kv, gpre, wq, wo, gpost)


def _swiglu_chunk(hn, wg, wu, wd):
    gate = _dot(hn, wg)
    act = (gate * jax.nn.sigmoid(gate) * _dot(hn, wu)).astype(BF16)
    return _dot(act, wd)


def _ffn_kernel(x_ref, gpre_ref, wg_ref, wu_ref, wd_ref, gpost_ref, o_ref, hn_ref, acc_ref):
    j = pl.program_id(1)

    @pl.when(j == 0)
    def _():
        hn_ref[...] = _rms_rows(x_ref[...], gpre_ref[...]).astype(BF16)

    y = _swiglu_chunk(hn_ref[...], wg_ref[...], wu_ref[...], wd_ref[...])

    @pl.when(j == 0)
    def _():
        acc_ref[...] = y

    @pl.when(j > 0)
    def _():
        acc_ref[...] += y

    @pl.when(j == pl.num_programs(1) - 1)
    def _():
        o_ref[...] = x_ref[...] + _rms_rows(acc_ref[...], gpost_ref[...])


def _ffn(x2d, gpre, wg, wu, wd, gpost):
    T, D = x2d.shape
    F = wg.shape[1]
    tm = min(FFN_TM, T)
    tf = FFN_TF
    xspec = pl.BlockSpec((tm, D), lambda i, j: (i, 0))
    return pl.pallas_call(
        _ffn_kernel, grid=(T // tm, F // tf),
        in_specs=[xspec, _const_spec((1, D)),
                  pl.BlockSpec((D, tf), lambda i, j: (0, j)),
                  pl.BlockSpec((D, tf), lambda i, j: (0, j)),
                  pl.BlockSpec((tf, D), lambda i, j: (j, 0)),
                  _const_spec((1, D))],
        out_specs=xspec, out_shape=jax.ShapeDtypeStruct(x2d.shape, x2d.dtype),
        scratch_shapes=[pltpu.VMEM((tm, D), BF16), pltpu.VMEM((tm, D), F32)],
        compiler_params=_cparams(("parallel", "arbitrary")), name="ffn",
    )(x2d, gpre, wg, wu, wd, gpost)


def _router_kernel(x_ref, gpre_ref, wr_ref, br_ref, tri_ref, hn_ref, gate_ref, pos_ref, cnt_ref, run_ref):
    @pl.when(pl.program_id(0) == 0)
    def _():
        run_ref[...] = jnp.zeros_like(run_ref)

    hn = _rms_rows(x_ref[...], gpre_ref[...])
    hn_ref[...] = hn.astype(hn_ref.dtype)
    logits = lax.dot_general(wr_ref[...], hn, (((1,), (1,)), ((), ())),
                             precision=lax.Precision.HIGHEST,
                             preferred_element_type=F32) + br_ref[...]
    e_idx = lax.broadcasted_iota(jnp.int32, logits.shape, 0)
    m1 = jnp.max(logits, axis=0, keepdims=True)
    i1 = jnp.min(jnp.where(logits == m1, e_idx, N_EXPERTS), axis=0, keepdims=True)
    rest = jnp.where(e_idx == i1, -jnp.inf, logits)
    m2 = jnp.max(rest, axis=0, keepdims=True)
    i2 = jnp.min(jnp.where(rest == m2, e_idx, N_EXPERTS), axis=0, keepdims=True)
    e2 = jnp.exp(m2 - m1)
    w1 = 1.0 / (1.0 + e2)
    w2 = e2 / (1.0 + e2)
    gate_ref[...] = jnp.where(e_idx == i1, w1, 0.0) + jnp.where(e_idx == i2, w2, 0.0)

    sel = (e_idx == i1) | (e_idx == i2)
    sel_f = jnp.where(sel, 1.0, 0.0)
    before = _dot(sel_f.astype(BF16), tri_ref[...])
    run = run_ref[:, 0:1]
    pos_ref[...] = jnp.where(sel, run + before, -1.0)
    cnt = jnp.sum(sel_f, axis=1, keepdims=True)
    cnt_ref[0] = cnt
    run_ref[...] = run_ref[...] + cnt


def _router(x2d, gpre, wr_t, br_col):
    T, D = x2d.shape
    tm = MOE_BLOCK
    nb = T // tm
    tri = (np.arange(tm)[:, None] < np.arange(tm)[None, :]).astype(np.float32)
    eb = pl.BlockSpec((N_EXPERTS, tm), lambda i: (0, i))
    return pl.pallas_call(
        _router_kernel, grid=(nb,),
        in_specs=[pl.BlockSpec((tm, D), lambda i: (i, 0)), _const_spec((1, D)),
                  _const_spec(wr_t.shape), _const_spec(br_col.shape), _const_spec((tm, tm))],
        out_specs=[pl.BlockSpec((tm, D), lambda i: (i, 0)), eb, eb,
                   pl.BlockSpec((1, N_EXPERTS, 1), lambda i: (i, 0, 0))],
        out_shape=[jax.ShapeDtypeStruct((T, D), BF16),
                   jax.ShapeDtypeStruct((N_EXPERTS, T), F32),
                   jax.ShapeDtypeStruct((N_EXPERTS, T), F32),
                   jax.ShapeDtypeStruct((nb, N_EXPERTS, 1), F32)],
        scratch_shapes=[pltpu.VMEM((N_EXPERTS, 128), F32)],
        compiler_params=_cparams(("arbitrary",)), name="router",
    )(x2d, gpre, wr_t, br_col, jnp.asarray(tri, BF16))


_FIRST, _LAST = 1, 2
_HALF = (4, 8)


def _moe_schedule(cnt_be, g_max):
    nb, ne = cnt_be.shape
    tr = MOE_BLOCK
    cnt_eb = cnt_be.T
    start_eb = jnp.cumsum(cnt_eb, axis=1) - cnt_eb
    n_tile_e = (jnp.sum(cnt_eb, axis=1) + tr - 1) // tr
    tile0_e = jnp.cumsum(n_tile_e) - n_tile_e
    first_eb = start_eb // tr
    n_eb = jnp.where(cnt_eb > 0, (start_eb + cnt_eb - 1) // tr - first_eb + 1, 0)
    g = jnp.arange(g_max, dtype=jnp.int32)

    def visit_list(pair_e, pair_b, group_of):
        n = n_eb[pair_e, pair_b]
        ends = jnp.cumsum(n)
        total = ends[-1]
        gi = jnp.minimum(g, total - 1)
        pair = jnp.sum(ends[None, :] <= gi[:, None], axis=1).astype(jnp.int32)
        e = pair_e[pair]
        b = pair_b[pair]
        local = first_eb[e, b] + gi - (ends[pair] - n[pair])
        tile = tile0_e[e] + local
        grp = group_of(tile, b)
        valid = g < total
        first = valid & ((g == 0) | (grp != jnp.roll(grp, 1)))
        last = valid & ((g == total - 1) | (grp != jnp.roll(grp, -1)))
        lo = start_eb[e, b] - local * tr
        hi = lo + cnt_eb[e, b]
        flags = (first * _FIRST + last * _LAST
                 + (valid & (lo < tr // 2)) * _HALF[0] + (valid & (hi > tr // 2)) * _HALF[1])
        return [v.astype(jnp.int32) for v in (e, b, tile, local * tr, flags)]

    ee, bb = np.meshgrid(np.arange(ne), np.arange(nb), indexing="ij")
    list_a = visit_list(jnp.asarray(ee.reshape(-1)), jnp.asarray(bb.reshape(-1)), lambda tile, b: tile)
    list_b = visit_list(jnp.asarray(ee.T.reshape(-1)), jnp.asarray(bb.T.reshape(-1)), lambda tile, b: b)
    return list_a, list_b


def _moe_expert_kernel(e_ref, b_ref, t_ref, r_ref, f_ref, hn_ref, pos_ref, wg_ref, wu_ref, wd_ref,
                       ys_ref, xs_ref):
    g = pl.program_id(0)
    flags = f_ref[g]
    tr = xs_ref.shape[0]

    @pl.when((flags & _FIRST) != 0)
    def _():
        xs_ref[...] = jnp.zeros_like(xs_ref)

    half = tr // 2
    for h in range(2):
        @pl.when((flags & _HALF[h]) != 0)
        def _():
            rows = (lax.broadcasted_iota(jnp.int32, (half, 1), 0) + (r_ref[g] + h * half)).astype(F32)
            onehot = jnp.where(rows == pos_ref[0, 0], 1.0, 0.0).astype(BF16)
            xs_ref[h * half:(h + 1) * half, :] += _dot(onehot, hn_ref[...])

    @pl.when((flags & _LAST) != 0)
    def _():
        xs = xs_ref[...].astype(BF16)
        f_total = wg_ref.shape[2]
        y = None
        for lo in range(0, f_total, MOE_TF):
            part = _swiglu_chunk(xs, wg_ref[0, :, lo:lo + MOE_TF], wu_ref[0, :, lo:lo + MOE_TF],
                                 wd_ref[0, lo:lo + MOE_TF, :])
            y = part if y is None else y + part
        ys_ref[...] = y.astype(ys_ref.dtype)


def _moe_expert(list_a, hn, pos, wg, wu, wd, n_tiles):
    T, D = hn.shape
    E, _, F = wg.shape
    tr = MOE_BLOCK
    g_max = list_a[0].shape[0]
    pos4 = pos.reshape(E, T // tr, 1, tr)
    wspec_in = pl.BlockSpec((1, D, F), lambda g, e, b, t, r, f: (e[g], 0, 0), pipeline_mode=pl.Buffered(1))
    wspec_out = pl.BlockSpec((1, F, D), lambda g, e, b, t, r, f: (e[g], 0, 0), pipeline_mode=pl.Buffered(1))
    grid_spec = pltpu.PrefetchScalarGridSpec(
        num_scalar_prefetch=5, grid=(g_max,),
        in_specs=[pl.BlockSpec((tr, D), lambda g, e, b, t, r, f: (b[g], 0)),
                  pl.BlockSpec((1, 1, 1, tr), lambda g, e, b, t, r, f: (e[g], b[g], 0, 0)),
                  wspec_in, wspec_in, wspec_out],
        out_specs=pl.BlockSpec((tr, D), lambda g, e, b, t, r, f: (t[g], 0)),
        scratch_shapes=[pltpu.VMEM((tr, D), F32)])
    return pl.pallas_call(
        _moe_expert_kernel, grid_spec=grid_spec,
        out_shape=jax.ShapeDtypeStruct((n_tiles * tr, D), BF16),
        compiler_params=_cparams(("arbitrary",)), name="moe_expert",
    )(*list_a, hn, pos4, wg, wu, wd)


def _moe_combine_kernel(e_ref, b_ref, t_ref, r_ref, f_ref, x_ref, ys_ref, pos_ref, gate_ref, gpost_ref,
                        o_ref, acc_ref):
    g = pl.program_id(0)
    flags = f_ref[g]
    tr = ys_ref.shape[0]

    @pl.when((flags & _FIRST) != 0)
    def _():
        acc_ref[...] = jnp.zeros_like(acc_ref)

    half = tr // 2
    for h in range(2):
        @pl.when((flags & _HALF[h]) != 0)
        def _():
            rows = (lax.broadcasted_iota(jnp.int32, (1, half), 1) + (r_ref[g] + h * half)).astype(F32)
            onehot = jnp.where(pos_ref[0] == rows, 1.0, 0.0).astype(BF16)
            acc_ref[...] += gate_ref[0] * _dot(onehot, ys_ref[h * half:(h + 1) * half, :])

    @pl.when((flags & _LAST) != 0)
    def _():
        o_ref[...] = x_ref[...] + _rms_rows(acc_ref[...], gpost_ref[...])


def _moe_combine(list_b, x2d, ys, pos, gates, gpost):
    T, D = x2d.shape
    E = gates.shape[0]
    tr = MOE_BLOCK
    g_max = list_b[0].shape[0]
    xspec = pl.BlockSpec((tr, D), lambda g, e, b, t, r, f: (b[g], 0))
    colspec = pl.BlockSpec((1, tr, 1), lambda g, e, b, t, r, f: (e[g], b[g], 0))
    grid_spec = pltpu.PrefetchScalarGridSpec(
        num_scalar_prefetch=5, grid=(g_max,),
        in_specs=[xspec, pl.BlockSpec((tr, D), lambda g, e, b, t, r, f: (t[g], 0)), colspec, colspec,
                  pl.BlockSpec((1, D), lambda g, e, b, t, r, f: (0, 0))],
        out_specs=xspec,
        scratch_shapes=[pltpu.VMEM((tr, D), F32)])
    return pl.pallas_call(
        _moe_combine_kernel, grid_spec=grid_spec,
        out_shape=jax.ShapeDtypeStruct(x2d.shape, x2d.dtype),
        compiler_params=_cparams(("arbitrary",)), name="moe_combine",
    )(*list_b, x2d, ys, pos.reshape(E, T, 1), gates.reshape(E, T, 1), gpost)


def _moe(x2d, gpre, wr_t, br_col, wg, wu, wd, gpost):
    T, D = x2d.shape
    E = wg.shape[0]
    nb = T // MOE_BLOCK
    n_tiles = TOP_K * nb + E
    g_max = E * nb + n_tiles
    hn, gates, pos, cnt = _router(x2d, gpre, wr_t, br_col)
    list_a, list_b = _moe_schedule(cnt.reshape(nb, E).astype(jnp.int32), g_max)
    ys = _moe_expert(list_a, hn, pos, wg, wu, wd, n_tiles)
    return _moe_combine(list_b, x2d, ys, pos, gates, gpost)


def _feature_row_order():
    def diff_block(base):
        x1, x2, rest = [], [], []
        half = DIFF_ROPE_DIM // 2
        for h in range(DIFF_HEADS):
            for c in range(2):
                lo = base + h * 2 * DIFF_QK_DIM + c * DIFF_QK_DIM
                x1 += range(lo, lo + half)
                x2 += range(lo + half, lo + DIFF_ROPE_DIM)
                rest += range(lo + DIFF_ROPE_DIM, lo + DIFF_QK_DIM)
        return x1 + x2 + rest

    order = list(range(0, _S_DQ)) + diff_block(_S_DQ) + diff_block(_S_DK) + list(range(_S_DV, N_FEAT))
    return np.asarray(order, np.int32)


def _mla_q_up_order():
    w = MLA_NOPE_DIM + MLA_ROPE_DIM
    half = MLA_ROPE_DIM // 2
    nope = [h * w + j for h in range(MLA_HEADS) for j in range(MLA_NOPE_DIM)]
    x1 = [h * w + MLA_NOPE_DIM + j for h in range(MLA_HEADS) for j in range(half)]
    x2 = [h * w + MLA_NOPE_DIM + half + j for h in range(MLA_HEADS) for j in range(half)]
    return np.asarray(nope + x1 + x2, np.int32)


def _mla_kv_up_order():
    w = MLA_NOPE_DIM + MLA_V_DIM
    k = [h * w + j for h in range(MLA_HEADS) for j in range(MLA_NOPE_DIM)]
    v = [h * w + MLA_NOPE_DIM + j for h in range(MLA_HEADS) for j in range(MLA_V_DIM)]
    return np.asarray(k + v, np.int32)


def _rope_tables(S):
    def cos_sin(pos, rot_dim):
        inv = ROPE_THETA ** (-jnp.arange(0, rot_dim, 2, dtype=F32) / rot_dim)
        ang = pos.astype(F32)[:, None] * inv[None, :]
        return jnp.cos(ang).T, jnp.sin(ang).T

    rows = S // GRID_W
    pos = jnp.arange(S)
    row = jnp.repeat(jnp.arange(rows), GRID_W)
    col = jnp.tile(jnp.arange(GRID_W), rows)
    cm, sm = cos_sin(pos, MLA_ROPE_DIM)
    cr, sr = cos_sin(row, GQA_HEAD_DIM // 2)
    cc, sc = cos_sin(col, GQA_HEAD_DIM // 2)
    cp, sp = cos_sin(pos, DIFF_ROPE_DIM)
    return (jnp.tile(cm, (MLA_HEADS, 1)), jnp.tile(sm, (MLA_HEADS, 1)),
            jnp.concatenate([cr, cc], axis=0), jnp.concatenate([sr, sc], axis=0),
            jnp.tile(cp, (2 * DIFF_HEADS, 1)), jnp.tile(sp, (2 * DIFF_HEADS, 1)))


def _row(v):
    return v.reshape(1, -1).astype(F32)


def _col(v):
    return v.reshape(-1, 1).astype(F32)


def kernel(x, mem, mix_pre_g, mix_post_g, w_in, mla_q_norm_g, mla_w_q_up, mla_kv_norm_g, mla_w_kv_up,
           gqa_q_norm_g, gqa_k_norm_g, diff_lambda_q1, diff_lambda_k1, diff_lambda_q2, diff_lambda_k2,
           diff_subln_g, w_branch, w_out, mem_pre_g, mem_post_g, mem_norm_g, mem_wq, mem_wkv, mem_wo,
           ffn_pre_g, ffn_post_g, dense_w_gate, dense_w_up, dense_w_down, moe_w_router, moe_b_router,
           moe_w_gate, moe_w_up, moe_w_down):
    B, S, D = x.shape
    M = mem.shape[1]
    depth = w_in.shape[0]
    tables = _rope_tables(S)
    feat_order = _feature_row_order()
    q_up_order = _mla_q_up_order()
    kv_up_order = _mla_kv_up_order()

    for layer in range(depth):
        lambda_init = 0.8 - 0.6 * math.exp(-0.3 * layer)

        w_l = w_in[layer]
        wt = w_l[:, feat_order].T.astype(BF16)
        wg = w_l[:, N_FEAT:].astype(BF16)
        wqu = mla_w_q_up[layer][:, q_up_order].T.astype(BF16)
        wkvu = mla_w_kv_up[layer][:, kv_up_order].T.astype(BF16)
        (qm, km, vm, qg, kg, vg, qd, kd, vd, gates) = _mixer_prep(
            x, _row(mix_pre_g[layer]), wt, wg, wqu, wkvu,
            _col(mla_q_norm_g[layer]), _col(mla_kv_norm_g[layer]),
            _col(gqa_q_norm_g[layer]), _col(gqa_k_norm_g[layer]), tables)
        a_t = _attention(qm, km, vm, kv_of_head=lambda h: h, kcol_of_head=lambda h: h, name="mla_attn")
        b_t = _attention(qg, kg, vg, kv_of_head=lambda h: h // GQA_GROUP, kcol_of_head=lambda h: 0,
                         name="gqa_attn")
        c_t = _diff_attention(qd, kd, vd, _row(diff_lambda_q1[layer]), _row(diff_lambda_k1[layer]),
                              _row(diff_lambda_q2[layer]), _row(diff_lambda_k2[layer]),
                              _col(diff_subln_g[layer]), lambda_init)
        x = _mixer_merge(x, a_t, b_t, c_t, gates, w_branch[layer].astype(BF16),
                         w_out[layer].astype(BF16), _row(mix_post_g[layer]))

        kv = _norm_matmul(mem.reshape(B * M, D), _row(mem_norm_g[layer]), mem_wkv[layer].astype(BF16), BF16)
        x = _mem_attention(x, kv.reshape(B, M, 2 * D), _row(mem_pre_g[layer]), mem_wq[layer].astype(BF16),
                           mem_wo[layer].astype(BF16), _row(mem_post_g[layer]))

        x2d = x.reshape(B * S, D)
        i = layer // 2
        if layer % 2 == 0:
            x2d = _ffn(x2d, _row(ffn_pre_g[layer]), dense_w_gate[i].astype(BF16), dense_w_up[i].astype(BF16),
                       dense_w_down[i].astype(BF16), _row(ffn_post_g[layer]))
        else:
            x2d = _moe(x2d, _row(ffn_pre_g[layer]), moe_w_router[i].T.astype(F32), _col(moe_b_router[i]),
                       moe_w_gate[i].astype(BF16), moe_w_up[i].astype(BF16), moe_w_down[i].astype(BF16),
                       _row(ffn_post_g[layer]))
        x = x2d.reshape(B, S, D)
    return x
```

```python
import functools
import math

import numpy as np
import jax
import jax.numpy as jnp
from jax import lax
from jax.experimental import pallas as pl
from jax.experimental.pallas import tpu as pltpu

F32 = jnp.float32
BF16 = jnp.bfloat16

D_MODEL = 1024
GRID_W = 64
ROPE_THETA = 500000.0
NORM_EPS = 1e-6

MLA_HEADS = 8
MLA_Q_LORA = 384
MLA_KV_LORA = 256
MLA_NOPE_DIM = 64
MLA_ROPE_DIM = 32
MLA_V_DIM = 64

GQA_Q_HEADS = 8
GQA_KV_HEADS = 2
GQA_GROUP = GQA_Q_HEADS // GQA_KV_HEADS
GQA_HEAD_DIM = 64

DIFF_HEADS = 8
DIFF_QK_DIM = 32
DIFF_V_DIM = 2 * DIFF_QK_DIM
DIFF_ROPE_DIM = DIFF_QK_DIM // 4

N_BRANCHES = 3
BRANCH_WIDTH = 512
HEAD_V = 64

MEM_HEADS = 4
MEM_HEAD_DIM = D_MODEL // MEM_HEADS

D_FF = 2816
N_EXPERTS = 8
TOP_K = 2

IN_SPLITS = (MLA_Q_LORA, MLA_KV_LORA, MLA_ROPE_DIM,
             GQA_Q_HEADS * GQA_HEAD_DIM, GQA_KV_HEADS * GQA_HEAD_DIM, GQA_KV_HEADS * GQA_HEAD_DIM,
             DIFF_HEADS * 2 * DIFF_QK_DIM, DIFF_HEADS * 2 * DIFF_QK_DIM, DIFF_HEADS * DIFF_V_DIM,
             N_BRANCHES * D_MODEL)
IN_OFFS = tuple(int(v) for v in np.cumsum((0,) + IN_SPLITS))
N_FEAT = IN_OFFS[9]
N_GATE = IN_SPLITS[9]

LOG2E = 1.4426950408889634
QK_PAD = 128
ONES_ROWS = 16

V7X_VMEM_LIMIT_BYTES = 56 * 1024 * 1024

PREP_TM = 512
ATTN_TQ = 1024
ATTN_TK = 256
ATTN_UNROLL = 8
MERGE_TM = 256
MEM_TM = 512
FFN_TM = 512
FFN_TF = 1408
MOE_BLOCK = 512
MOE_TF = 1408
KV_TM = 256


def _cparams(sem):
    return pltpu.CompilerParams(dimension_semantics=sem, vmem_limit_bytes=V7X_VMEM_LIMIT_BYTES)


def _const_spec(shape):
    nd = len(shape)
    return pl.BlockSpec(shape, lambda *_: (0,) * nd, pipeline_mode=pl.Buffered(1))


def _rms_rows(x, g_row):
    ms = jnp.mean(x * x, axis=-1, keepdims=True)
    return x * lax.rsqrt(ms + NORM_EPS) * g_row


def _rms_cols(x, g_col):
    ms = jnp.mean(x * x, axis=0, keepdims=True)
    return x * lax.rsqrt(ms + NORM_EPS) * g_col


def _dot(a, b):
    return jnp.dot(a, b, preferred_element_type=F32)


def _dot_nt(a, b):
    return lax.dot_general(a, b, (((1,), (1,)), ((), ())), preferred_element_type=F32)


_S_CQ, _S_CKV, _S_KR, _S_GQ, _S_GK, _S_GV, _S_DQ, _S_DK, _S_DV = IN_OFFS[:9]


def _rope_rows(x1, x2, c, s):
    return x1 * c - x2 * s, x2 * c + x1 * s


def _mixer_prep_kernel(x_ref, g_ref, wt_ref, wg_ref, wqu_ref, wkvu_ref,
                       mqg_ref, mkvg_ref, gqg_ref, gkg_ref,
                       cm_ref, sm_ref, ca_ref, sa_ref, cp_ref, sp_ref,
                       qm_ref, km_ref, vm_ref, qg_ref, kg_ref, vg_ref,
                       qd_ref, kd_ref, vd_ref, gate_ref):
    tm = x_ref.shape[1]
    hn = _rms_rows(x_ref[0], g_ref[...]).astype(BF16)

    def feat(lo, n):
        return _dot_nt(wt_ref[lo:lo + n, :], hn)

    gchunk = 1024
    for j in range(N_GATE // gchunk):
        z = _dot(hn, wg_ref[:, j * gchunk:(j + 1) * gchunk])
        gate_ref[0, :, j * gchunk:(j + 1) * gchunk] = jax.nn.sigmoid(z).astype(gate_ref.dtype)

    zeros32 = jnp.zeros((32, tm), F32)
    zeros64 = jnp.zeros((64, tm), F32)

    c_mla = (MLA_NOPE_DIM + MLA_ROPE_DIM) ** -0.5 * LOG2E
    cm = cm_ref[...]
    sm = sm_ref[...]
    cqn = _rms_cols(feat(_S_CQ, MLA_Q_LORA), mqg_ref[...]).astype(BF16)
    q_all = _dot(wqu_ref[...], cqn)
    q_nope = q_all[0:512] * c_mla
    q_r1, q_r2 = _rope_rows(q_all[512:640], q_all[640:768], cm, sm)
    q_r1 = q_r1 * c_mla
    q_r2 = q_r2 * c_mla
    for h in range(MLA_HEADS):
        qm_ref[0, h] = jnp.concatenate(
            [q_nope[64 * h:64 * h + 64], q_r1[16 * h:16 * h + 16], q_r2[16 * h:16 * h + 16], zeros32],
            axis=0).astype(qm_ref.dtype)

    ckvn = _rms_cols(feat(_S_CKV, MLA_KV_LORA), mkvg_ref[...]).astype(BF16)
    kv_all = _dot(wkvu_ref[...], ckvn)
    vm_ref[0] = kv_all[512:1024].astype(vm_ref.dtype)
    kr = feat(_S_KR, MLA_ROPE_DIM)
    k_r1, k_r2 = _rope_rows(kr[0:16], kr[16:32], cm[0:16], sm[0:16])
    for h in range(MLA_HEADS):
        kt = jnp.concatenate([kv_all[64 * h:64 * h + 64], k_r1, k_r2, zeros32], axis=0)
        km_ref[0, :, 128 * h:128 * h + 128] = kt.T.astype(km_ref.dtype)

    c_gqa = GQA_HEAD_DIM ** -0.5 * LOG2E
    ca = ca_ref[...]
    sa = sa_ref[...]
    gq = feat(_S_GQ, GQA_Q_HEADS * GQA_HEAD_DIM)
    for h in range(GQA_Q_HEADS):
        qn = _rms_cols(gq[64 * h:64 * h + 64], gqg_ref[...])
        r1, r2 = _rope_rows(qn[0:32], qn[32:64], ca, sa)
        q64 = jnp.concatenate([r1, r2], axis=0) * c_gqa
        parts = [q64, zeros64] if h // GQA_GROUP == 0 else [zeros64, q64]
        qg_ref[0, h] = jnp.concatenate(parts, axis=0).astype(qg_ref.dtype)
    gk = feat(_S_GK, GQA_KV_HEADS * GQA_HEAD_DIM)
    kparts = []
    for g in range(GQA_KV_HEADS):
        kn = _rms_cols(gk[64 * g:64 * g + 64], gkg_ref[...])
        r1, r2 = _rope_rows(kn[0:32], kn[32:64], ca, sa)
        kparts += [r1, r2]
    kg_ref[0] = jnp.concatenate(kparts, axis=0).T.astype(kg_ref.dtype)
    vg_ref[0] = feat(_S_GV, GQA_KV_HEADS * GQA_HEAD_DIM).astype(vg_ref.dtype)

    c_diff = DIFF_QK_DIM ** -0.5 * LOG2E
    cp = cp_ref[...]
    sp = sp_ref[...]
    row = lax.broadcasted_iota(jnp.int32, (64, 1), 0)
    in_c0 = (row < 4) | ((row >= 8) & (row < 12)) | ((row >= 16) & (row < 40))

    def diff_heads(lo):
        z = feat(lo, DIFF_HEADS * 2 * DIFF_QK_DIM)
        r1, r2 = _rope_rows(z[0:64], z[64:128], cp, sp)
        rest = z[128:512]
        return [jnp.concatenate([r1[8 * h:8 * h + 8], r2[8 * h:8 * h + 8], rest[48 * h:48 * h + 48]], axis=0)
                for h in range(DIFF_HEADS)]

    for h, q64 in enumerate(diff_heads(_S_DQ)):
        q64 = q64 * c_diff
        for c in range(2):
            qc = jnp.where(in_c0 if c == 0 else jnp.logical_not(in_c0), q64, 0.0)
            parts = [qc, zeros64] if h % 2 == 0 else [zeros64, qc]
            qd_ref[0, 2 * h + c] = jnp.concatenate(parts, axis=0).astype(qd_ref.dtype)
    k_heads = diff_heads(_S_DK)
    for p in range(DIFF_HEADS // 2):
        kt = jnp.concatenate([k_heads[2 * p], k_heads[2 * p + 1]], axis=0)
        kd_ref[0, :, 128 * p:128 * p + 128] = kt.T.astype(kd_ref.dtype)
    vd_ref[0] = feat(_S_DV, DIFF_HEADS * DIFF_V_DIM).astype(vd_ref.dtype)


def _mixer_prep(x, g, wt, wg, wqu, wkvu, mqg, mkvg, gqg, gkg, tables):
    B, S, D = x.shape
    tm = min(PREP_TM, S)
    cm, sm, ca, sa, cp, sp = tables
    grid = (B, S // tm)

    def tok(width):
        return pl.BlockSpec((1, tm, width), lambda b, i: (b, i, 0))

    def featm(rows):
        return pl.BlockSpec((1, rows, tm), lambda b, i: (b, 0, i))

    def heads(n):
        return pl.BlockSpec((1, n, QK_PAD, tm), lambda b, i: (b, 0, 0, i))

    def table(rows):
        return pl.BlockSpec((rows, tm), lambda b, i: (0, i))

    in_specs = [
        pl.BlockSpec((1, tm, D), lambda b, i: (b, i, 0)),
        _const_spec((1, D)),
        _const_spec(wt.shape), _const_spec(wg.shape), _const_spec(wqu.shape), _const_spec(wkvu.shape),
        _const_spec(mqg.shape), _const_spec(mkvg.shape), _const_spec(gqg.shape), _const_spec(gkg.shape),
        table(128), table(128), table(32), table(32), table(64), table(64),
    ]
    out_shape = [
        jax.ShapeDtypeStruct((B, MLA_HEADS, QK_PAD, S), BF16),
        jax.ShapeDtypeStruct((B, S, MLA_HEADS * QK_PAD), BF16),
        jax.ShapeDtypeStruct((B, MLA_HEADS * HEAD_V, S), BF16),
        jax.ShapeDtypeStruct((B, GQA_Q_HEADS, QK_PAD, S), BF16),
        jax.ShapeDtypeStruct((B, S, QK_PAD), BF16),
        jax.ShapeDtypeStruct((B, GQA_KV_HEADS * HEAD_V, S), BF16),
        jax.ShapeDtypeStruct((B, 2 * DIFF_HEADS, QK_PAD, S), BF16),
        jax.ShapeDtypeStruct((B, S, DIFF_HEADS // 2 * QK_PAD), BF16),
        jax.ShapeDtypeStruct((B, DIFF_HEADS * HEAD_V, S), BF16),
        jax.ShapeDtypeStruct((B, S, N_GATE), BF16),
    ]
    out_specs = [
        heads(MLA_HEADS), tok(MLA_HEADS * QK_PAD), featm(MLA_HEADS * HEAD_V),
        heads(GQA_Q_HEADS), tok(QK_PAD), featm(GQA_KV_HEADS * HEAD_V),
        heads(2 * DIFF_HEADS), tok(DIFF_HEADS // 2 * QK_PAD), featm(DIFF_HEADS * HEAD_V),
        tok(N_GATE),
    ]
    return pl.pallas_call(
        _mixer_prep_kernel, grid=grid, in_specs=in_specs, out_specs=out_specs, out_shape=out_shape,
        compiler_params=_cparams(("parallel", "parallel")), name="mixer_prep",
    )(x, g, wt, wg, wqu, wkvu, mqg, mkvg, gqg, gkg, cm, sm, ca, sa, cp, sp)


def _online_softmax_attention(q_maps, k_ref, v_ref, s_ref, *, tk):
    S = k_ref.shape[1]
    tq = q_maps[0].shape[1]
    n_chunks = S // tk
    unroll = min(ATTN_UNROLL, n_chunks)
    assert unroll % 2 == 0 and n_chunks % unroll == 0
    n_maps = len(q_maps)
    ones = jnp.ones((ONES_ROWS, tk), BF16)

    def scores(c, slot):
        kc = k_ref[0, pl.ds(pl.multiple_of(c * tk, tk), tk), :]
        col_max = []
        for j, q_t in enumerate(q_maps):
            s_t = _dot(kc, q_t)
            s_ref[slot, j] = s_t
            col_max.append(jnp.max(s_t, axis=0, keepdims=True))
        return col_max

    def update(c, slot, col_max, ms, accs):
        v_aug = jnp.concatenate([v_ref[0, :, pl.ds(pl.multiple_of(c * tk, tk), tk)], ones], axis=0)
        new_ms, new_accs = [], []
        for j in range(n_maps):
            m_new = jnp.maximum(ms[j], col_max[j])
            alpha = jnp.exp2(ms[j] - m_new)
            p = jnp.exp2(s_ref[slot, j] - m_new).astype(BF16)
            new_ms.append(m_new)
            new_accs.append(accs[j] * alpha + _dot(v_aug, p))
        return new_ms, new_accs

    def group(c, cmax, ms, accs, last):
        for u in range(unroll):
            is_final = last and u == unroll - 1
            cmax_next = None if is_final else scores(c + u + 1, (u + 1) % 2)
            ms, accs = update(c + u, u % 2, cmax, ms, accs)
            cmax = cmax_next
        return cmax, ms, accs

    def body(i, carry):
        return group(unroll * i, *carry, last=False)

    m0 = [jnp.full((1, tq), -jnp.inf, F32)] * n_maps
    acc0 = [jnp.zeros((HEAD_V + ONES_ROWS, tq), F32)] * n_maps
    carry = lax.fori_loop(0, n_chunks // unroll - 1, body, (scores(0, 0), m0, acc0))
    _, _, accs = group(n_chunks - unroll, *carry, last=True)
    return accs


def _attn_kernel(q_ref, k_ref, v_ref, o_ref, s_ref, *, tk):
    (acc,) = _online_softmax_attention([q_ref[0, 0]], k_ref, v_ref, s_ref, tk=tk)
    o_ref[0] = (acc[:HEAD_V] / acc[HEAD_V:HEAD_V + 1]).astype(o_ref.dtype)


def _attention(q_t, k, v_t, *, kv_of_head, kcol_of_head, name):
    B, H, _, S = q_t.shape
    tq = min(ATTN_TQ, S)
    tk = min(ATTN_TK, S // 2)
    return pl.pallas_call(
        functools.partial(_attn_kernel, tk=tk),
        grid=(B, H, S // tq),
        in_specs=[
            pl.BlockSpec((1, 1, QK_PAD, tq), lambda b, h, i: (b, h, 0, i)),
            pl.BlockSpec((1, S, QK_PAD), lambda b, h, i: (b, 0, kcol_of_head(h))),
            pl.BlockSpec((1, HEAD_V, S), lambda b, h, i: (b, kv_of_head(h), 0)),
        ],
        out_specs=pl.BlockSpec((1, HEAD_V, tq), lambda b, h, i: (b, h, i)),
        out_shape=jax.ShapeDtypeStruct((B, H * HEAD_V, S), F32),
        scratch_shapes=[pltpu.VMEM((2, 1, tk, tq), F32)],
        compiler_params=_cparams(("parallel", "parallel", "arbitrary")), name=name,
    )(q_t, k, v_t)


def _diff_attn_kernel(lq1_ref, lk1_ref, lq2_ref, lk2_ref, g_ref, q_ref, k_ref, v_ref, o_ref, s_ref, *,
                      tk, lambda_init):
    tq = q_ref.shape[3]
    q_both = jnp.concatenate([q_ref[0, 0], q_ref[0, 1]], axis=1)
    (acc,) = _online_softmax_attention([q_both], k_ref, v_ref, s_ref, tk=tk)
    o_maps = acc[:HEAD_V] / acc[HEAD_V:HEAD_V + 1]
    lam = (jnp.exp(jnp.sum(lq1_ref[...] * lk1_ref[...], axis=-1, keepdims=True))
           - jnp.exp(jnp.sum(lq2_ref[...] * lk2_ref[...], axis=-1, keepdims=True)) + lambda_init)
    o = o_maps[:, :tq] - lam * o_maps[:, tq:]
    o = _rms_cols(o, g_ref[...]) * (1.0 - lambda_init)
    o_ref[0] = o.astype(o_ref.dtype)


def _diff_attention(q_t, k, v_t, lq1, lk1, lq2, lk2, subln_g, lambda_init):
    B, H2, _, S = q_t.shape
    H = H2 // 2
    tq = min(ATTN_TQ // 2, S)
    tk = min(ATTN_TK, S // 2)
    vec = _const_spec((1, DIFF_QK_DIM))
    return pl.pallas_call(
        functools.partial(_diff_attn_kernel, tk=tk, lambda_init=lambda_init),
        grid=(B, H, S // tq),
        in_specs=[
            vec, vec, vec, vec, _const_spec((DIFF_V_DIM, 1)),
            pl.BlockSpec((1, 2, QK_PAD, tq), lambda b, h, i: (b, h, 0, i)),
            pl.BlockSpec((1, S, QK_PAD), lambda b, h, i: (b, 0, h // 2)),
            pl.BlockSpec((1, HEAD_V, S), lambda b, h, i: (b, h, 0)),
        ],
        out_specs=pl.BlockSpec((1, HEAD_V, tq), lambda b, h, i: (b, h, i)),
        out_shape=jax.ShapeDtypeStruct((B, H * HEAD_V, S), F32),
        scratch_shapes=[pltpu.VMEM((2, 1, tk, 2 * tq), F32)],
        compiler_params=_cparams(("parallel", "parallel", "arbitrary")), name="diff_attn",
    )(lq1, lk1, lq2, lk2, subln_g, q_t, k, v_t)


def _mixer_merge_kernel(x_ref, a_ref, b_ref, c_ref, gate_ref, wb_ref, wo_ref, g_ref, o_ref):
    merged = None
    for n, br_ref in enumerate((a_ref, b_ref, c_ref)):
        br = br_ref[0].T.astype(BF16)
        proj = _dot(br, wb_ref[n])
        term = gate_ref[0, :, n * D_MODEL:(n + 1) * D_MODEL].astype(F32) * proj
        merged = term if merged is None else merged + term
    y = _dot(merged.astype(BF16), wo_ref[...])
    o_ref[0] = x_ref[0] + _rms_rows(y, g_ref[...])


def _mixer_merge(x, a_t, b_t, c_t, gates, wb, wo, g):
    B, S, D = x.shape
    tm = min(MERGE_TM, S)
    xspec = pl.BlockSpec((1, tm, D), lambda b, i: (b, i, 0))
    brspec = pl.BlockSpec((1, BRANCH_WIDTH, tm), lambda b, i: (b, 0, i))
    return pl.pallas_call(
        _mixer_merge_kernel, grid=(B, S // tm),
        in_specs=[xspec, brspec, brspec, brspec,
                  pl.BlockSpec((1, tm, N_GATE), lambda b, i: (b, i, 0)),
                  _const_spec(wb.shape), _const_spec(wo.shape), _const_spec((1, D))],
        out_specs=xspec, out_shape=jax.ShapeDtypeStruct(x.shape, x.dtype),
        compiler_params=_cparams(("parallel", "parallel")), name="mixer_merge",
    )(x, a_t, b_t, c_t, gates, wb, wo, g)


def _norm_matmul_kernel(x_ref, g_ref, w_ref, o_ref):
    hn = _rms_rows(x_ref[...], g_ref[...]).astype(BF16)
    o_ref[...] = _dot(hn, w_ref[...]).astype(o_ref.dtype)


def _norm_matmul(x2d, g, w, out_dtype):
    T, D = x2d.shape
    N = w.shape[1]
    tm = min(KV_TM, T)
    return pl.pallas_call(
        _norm_matmul_kernel, grid=(T // tm,),
        in_specs=[pl.BlockSpec((tm, D), lambda i: (i, 0)), _const_spec((1, D)), _const_spec(w.shape)],
        out_specs=pl.BlockSpec((tm, N), lambda i: (i, 0)),
        out_shape=jax.ShapeDtypeStruct((T, N), out_dtype),
        compiler_params=_cparams(("parallel",)), name="mem_kv",
    )(x2d, g, w)


def _mem_attn_kernel(x_ref, kv_ref, gpre_ref, wq_ref, wo_ref, gpost_ref, o_ref):
    x = x_ref[0]
    hn = _rms_rows(x, gpre_ref[...]).astype(BF16)
    q = (_dot(hn, wq_ref[...]) * (MEM_HEAD_DIM ** -0.5 * LOG2E)).astype(BF16)
    outs = []
    for h in range(MEM_HEADS):
        lo = h * MEM_HEAD_DIM
        k_h = kv_ref[0, :, lo:lo + MEM_HEAD_DIM]
        v_h = kv_ref[0, :, D_MODEL + lo:D_MODEL + lo + MEM_HEAD_DIM]
        s = _dot_nt(q[:, lo:lo + MEM_HEAD_DIM], k_h)
        e = jnp.exp2(s - jnp.max(s, axis=-1, keepdims=True))
        p = (e / jnp.sum(e, axis=-1, keepdims=True)).astype(BF16)
        outs.append(_dot(p, v_h))
    o = jnp.concatenate(outs, axis=-1).astype(BF16)
    y = _dot(o, wo_ref[...])
    o_ref[0] = x + _rms_rows(y, gpost_ref[...])


def _mem_attention(x, kv, gpre, wq, wo, gpost):
    B, S, D = x.shape
    M = kv.shape[1]
    tm = min(MEM_TM, S)
    xspec = pl.BlockSpec((1, tm, D), lambda b, i: (b, i, 0))
    return pl.pallas_call(
        _mem_attn_kernel, grid=(B, S // tm),
        in_specs=[xspec, pl.BlockSpec((1, M, 2 * D), lambda b, i: (b, 0, 0)),
                  _const_spec((1, D)), _const_spec(wq.shape), _const_spec(wo.shape), _const_spec((1, D))],
        out_specs=xspec, out_shape=jax.ShapeDtypeStruct(x.shape, x.dtype),
        compiler_params=_cparams(("parallel", "parallel")), name="mem_attn",
    )(x, kv, gpre, wq, wo, gpost)


def _swiglu_chunk(hn, wg, wu, wd):
    gate = _dot(hn, wg)
    act = (gate * jax.nn.sigmoid(gate) * _dot(hn, wu)).astype(BF16)
    return _dot(act, wd)


def _ffn_kernel(x_ref, gpre_ref, wg_ref, wu_ref, wd_ref, gpost_ref, o_ref, hn_ref, acc_ref):
    j = pl.program_id(1)

    @pl.when(j == 0)
    def _():
        hn_ref[...] = _rms_rows(x_ref[...], gpre_ref[...]).astype(BF16)

    y = _swiglu_chunk(hn_ref[...], wg_ref[...], wu_ref[...], wd_ref[...])

    @pl.when(j == 0)
    def _():
        acc_ref[...] = y

    @pl.when(j > 0)
    def _():
        acc_ref[...] += y

    @pl.when(j == pl.num_programs(1) - 1)
    def _():
        o_ref[...] = x_ref[...] + _rms_rows(acc_ref[...], gpost_ref[...])


def _ffn(x2d, gpre, wg, wu, wd, gpost):
    T, D = x2d.shape
    F = wg.shape[1]
    tm = min(FFN_TM, T)
    tf = FFN_TF
    xspec = pl.BlockSpec((tm, D), lambda i, j: (i, 0))
    return pl.pallas_call(
        _ffn_kernel, grid=(T // tm, F // tf),
        in_specs=[xspec, _const_spec((1, D)),
                  pl.BlockSpec((D, tf), lambda i, j: (0, j)),
                  pl.BlockSpec((D, tf), lambda i, j: (0, j)),
                  pl.BlockSpec((tf, D), lambda i, j: (j, 0)),
                  _const_spec((1, D))],
        out_specs=xspec, out_shape=jax.ShapeDtypeStruct(x2d.shape, x2d.dtype),
        scratch_shapes=[pltpu.VMEM((tm, D), BF16), pltpu.VMEM((tm, D), F32)],
        compiler_params=_cparams(("parallel", "arbitrary")), name="ffn",
    )(x2d, gpre, wg, wu, wd, gpost)


def _router_kernel(x_ref, gpre_ref, wr_ref, br_ref, tri_ref,
                   hn_ref, pos_ref, gate_col_ref, pos_col_ref, cnt_ref, run_ref):
    @pl.when(pl.program_id(0) == 0)
    def _():
        run_ref[...] = jnp.zeros_like(run_ref)

    hn = _rms_rows(x_ref[...], gpre_ref[...])
    hn_ref[...] = hn.astype(hn_ref.dtype)
    logits = lax.dot_general(wr_ref[...], hn, (((1,), (1,)), ((), ())),
                             precision=lax.Precision.HIGHEST,
                             preferred_element_type=F32) + br_ref[...]
    e_idx = lax.broadcasted_iota(jnp.int32, logits.shape, 0)
    m1 = jnp.max(logits, axis=0, keepdims=True)
    i1 = jnp.min(jnp.where(logits == m1, e_idx, N_EXPERTS), axis=0, keepdims=True)
    rest = jnp.where(e_idx == i1, -jnp.inf, logits)
    m2 = jnp.max(rest, axis=0, keepdims=True)
    i2 = jnp.min(jnp.where(rest == m2, e_idx, N_EXPERTS), axis=0, keepdims=True)
    e2 = jnp.exp(m2 - m1)
    w1 = 1.0 / (1.0 + e2)
    w2 = e2 / (1.0 + e2)
    gates = jnp.where(e_idx == i1, w1, 0.0) + jnp.where(e_idx == i2, w2, 0.0)

    sel = (e_idx == i1) | (e_idx == i2)
    sel_f = jnp.where(sel, 1.0, 0.0)
    before = _dot(sel_f.astype(BF16), tri_ref[...])
    run = run_ref[:, 0:1]
    pos = jnp.where(sel, run + before, -1.0)
    pos_ref[...] = pos
    tm = pos.shape[1]
    cols = jnp.concatenate([gates, pos, jnp.zeros((128 - 2 * N_EXPERTS, tm), F32)], axis=0).T
    for e in range(N_EXPERTS):
        gate_col_ref[e] = cols[:, e:e + 1]
        pos_col_ref[e] = cols[:, N_EXPERTS + e:N_EXPERTS + e + 1]
    cnt = jnp.sum(sel_f, axis=1, keepdims=True)
    cnt_ref[0] = cnt
    run_ref[...] = run_ref[...] + cnt


def _router(x2d, gpre, wr_t, br_col):
    T, D = x2d.shape
    tm = MOE_BLOCK
    nb = T // tm
    tri = (np.arange(tm)[:, None] < np.arange(tm)[None, :]).astype(np.float32)
    col = pl.BlockSpec((N_EXPERTS, tm, 1), lambda i: (0, i, 0))
    return pl.pallas_call(
        _router_kernel, grid=(nb,),
        in_specs=[pl.BlockSpec((tm, D), lambda i: (i, 0)), _const_spec((1, D)),
                  _const_spec(wr_t.shape), _const_spec(br_col.shape), _const_spec((tm, tm))],
        out_specs=[pl.BlockSpec((tm, D), lambda i: (i, 0)),
                   pl.BlockSpec((N_EXPERTS, tm), lambda i: (0, i)), col, col,
                   pl.BlockSpec((1, N_EXPERTS, 1), lambda i: (i, 0, 0))],
        out_shape=[jax.ShapeDtypeStruct((T, D), BF16),
                   jax.ShapeDtypeStruct((N_EXPERTS, T), F32),
                   jax.ShapeDtypeStruct((N_EXPERTS, T, 1), F32),
                   jax.ShapeDtypeStruct((N_EXPERTS, T, 1), F32),
                   jax.ShapeDtypeStruct((nb, N_EXPERTS, 1), F32)],
        scratch_shapes=[pltpu.VMEM((N_EXPERTS, 128), F32)],
        compiler_params=_cparams(("arbitrary",)), name="router",
    )(x2d, gpre, wr_t, br_col, jnp.asarray(tri, BF16))


_FIRST, _LAST = 1, 2
_HALF = (4, 8)


def _moe_schedule(cnt_be, g_max):
    nb, ne = cnt_be.shape
    tr = MOE_BLOCK
    cnt_eb = cnt_be.T
    start_eb = jnp.cumsum(cnt_eb, axis=1) - cnt_eb
    n_tile_e = (jnp.sum(cnt_eb, axis=1) + tr - 1) // tr
    tile0_e = jnp.cumsum(n_tile_e) - n_tile_e
    first_eb = start_eb // tr
    n_eb = jnp.where(cnt_eb > 0, (start_eb + cnt_eb - 1) // tr - first_eb + 1, 0)
    g = jnp.arange(g_max, dtype=jnp.int32)

    def visit_list(pair_e, pair_b, group_of):
        n = n_eb[pair_e, pair_b]
        ends = jnp.cumsum(n)
        total = ends[-1]
        gi = jnp.minimum(g, total - 1)
        pair = jnp.sum(ends[None, :] <= gi[:, None], axis=1).astype(jnp.int32)
        e = pair_e[pair]
        b = pair_b[pair]
        local = first_eb[e, b] + gi - (ends[pair] - n[pair])
        tile = tile0_e[e] + local
        grp = group_of(tile, b)
        valid = g < total
        first = valid & ((g == 0) | (grp != jnp.roll(grp, 1)))
        last = valid & ((g == total - 1) | (grp != jnp.roll(grp, -1)))
        lo = start_eb[e, b] - local * tr
        hi = lo + cnt_eb[e, b]
        flags = (first * _FIRST + last * _LAST
                 + (valid & (lo < tr // 2)) * _HALF[0] + (valid & (hi > tr // 2)) * _HALF[1])
        return [v.astype(jnp.int32) for v in (e, b, tile, local * tr, flags)]

    ee, bb = np.meshgrid(np.arange(ne), np.arange(nb), indexing="ij")
    list_a = visit_list(jnp.asarray(ee.reshape(-1)), jnp.asarray(bb.reshape(-1)), lambda tile, b: tile)
    list_b = visit_list(jnp.asarray(ee.T.reshape(-1)), jnp.asarray(bb.T.reshape(-1)), lambda tile, b: b)
    return list_a, list_b


def _moe_expert_kernel(e_ref, b_ref, t_ref, r_ref, f_ref, hn_ref, pos_ref, wg_ref, wu_ref, wd_ref,
                       ys_ref, xs_ref):
    g = pl.program_id(0)
    flags = f_ref[g]
    tr = xs_ref.shape[0]

    @pl.when((flags & _FIRST) != 0)
    def _():
        xs_ref[...] = jnp.zeros_like(xs_ref)

    half = tr // 2
    for h in range(2):
        @pl.when((flags & _HALF[h]) != 0)
        def _():
            rows = (lax.broadcasted_iota(jnp.int32, (half, 1), 0) + (r_ref[g] + h * half)).astype(F32)
            onehot = jnp.where(rows == pos_ref[0, 0], 1.0, 0.0).astype(BF16)
            xs_ref[h * half:(h + 1) * half, :] += _dot(onehot, hn_ref[...])

    @pl.when((flags & _LAST) != 0)
    def _():
        xs = xs_ref[...].astype(BF16)
        f_total = wg_ref.shape[2]
        y = None
        for lo in range(0, f_total, MOE_TF):
            part = _swiglu_chunk(xs, wg_ref[0, :, lo:lo + MOE_TF], wu_ref[0, :, lo:lo + MOE_TF],
                                 wd_ref[0, lo:lo + MOE_TF, :])
            y = part if y is None else y + part
        ys_ref[...] = y.astype(ys_ref.dtype)


def _moe_expert(list_a, hn, pos, wg, wu, wd, n_tiles):
    T, D = hn.shape
    E, _, F = wg.shape
    tr = MOE_BLOCK
    g_max = list_a[0].shape[0]
    pos4 = pos.reshape(E, T // tr, 1, tr)
    wspec_in = pl.BlockSpec((1, D, F), lambda g, e, b, t, r, f: (e[g], 0, 0), pipeline_mode=pl.Buffered(1))
    wspec_out = pl.BlockSpec((1, F, D), lambda g, e, b, t, r, f: (e[g], 0, 0), pipeline_mode=pl.Buffered(1))
    grid_spec = pltpu.PrefetchScalarGridSpec(
        num_scalar_prefetch=5, grid=(g_max,),
        in_specs=[pl.BlockSpec((tr, D), lambda g, e, b, t, r, f: (b[g], 0)),
                  pl.BlockSpec((1, 1, 1, tr), lambda g, e, b, t, r, f: (e[g], b[g], 0, 0)),
                  wspec_in, wspec_in, wspec_out],
        out_specs=pl.BlockSpec((tr, D), lambda g, e, b, t, r, f: (t[g], 0)),
        scratch_shapes=[pltpu.VMEM((tr, D), F32)])
    return pl.pallas_call(
        _moe_expert_kernel, grid_spec=grid_spec,
        out_shape=jax.ShapeDtypeStruct((n_tiles * tr, D), BF16),
        compiler_params=_cparams(("arbitrary",)), name="moe_expert",
    )(*list_a, hn, pos4, wg, wu, wd)


def _moe_combine_kernel(e_ref, b_ref, t_ref, r_ref, f_ref, x_ref, ys_ref, pos_ref, gate_ref, gpost_ref,
                        o_ref, acc_ref):
    g = pl.program_id(0)
    flags = f_ref[g]
    tr = ys_ref.shape[0]

    @pl.when((flags & _FIRST) != 0)
    def _():
        acc_ref[...] = jnp.zeros_like(acc_ref)

    half = tr // 2
    for h in range(2):
        @pl.when((flags & _HALF[h]) != 0)
        def _():
            rows = (lax.broadcasted_iota(jnp.int32, (1, half), 1) + (r_ref[g] + h * half)).astype(F32)
            onehot = jnp.where(pos_ref[0] == rows, 1.0, 0.0).astype(BF16)
            acc_ref[...] += gate_ref[0] * _dot(onehot, ys_ref[h * half:(h + 1) * half, :])

    @pl.when((flags & _LAST) != 0)
    def _():
        o_ref[...] = x_ref[...] + _rms_rows(acc_ref[...], gpost_ref[...])


def _moe_combine(list_b, x2d, ys, pos_col, gate_col, gpost):
    T, D = x2d.shape
    tr = MOE_BLOCK
    g_max = list_b[0].shape[0]
    xspec = pl.BlockSpec((tr, D), lambda g, e, b, t, r, f: (b[g], 0))
    colspec = pl.BlockSpec((1, tr, 1), lambda g, e, b, t, r, f: (e[g], b[g], 0))
    grid_spec = pltpu.PrefetchScalarGridSpec(
        num_scalar_prefetch=5, grid=(g_max,),
        in_specs=[xspec, pl.BlockSpec((tr, D), lambda g, e, b, t, r, f: (t[g], 0)), colspec, colspec,
                  pl.BlockSpec((1, D), lambda g, e, b, t, r, f: (0, 0))],
        out_specs=xspec,
        scratch_shapes=[pltpu.VMEM((tr, D), F32)])
    return pl.pallas_call(
        _moe_combine_kernel, grid_spec=grid_spec,
        out_shape=jax.ShapeDtypeStruct(x2d.shape, x2d.dtype),
        compiler_params=_cparams(("arbitrary",)), name="moe_combine",
    )(*list_b, x2d, ys, pos_col, gate_col, gpost)


def _moe(x2d, gpre, wr_t, br_col, wg, wu, wd, gpost):
    T, D = x2d.shape
    E = wg.shape[0]
    nb = T // MOE_BLOCK
    n_tiles = TOP_K * nb + E
    g_max = E * nb + n_tiles
    hn, pos, gate_col, pos_col, cnt = _router(x2d, gpre, wr_t, br_col)
    list_a, list_b = _moe_schedule(cnt.reshape(nb, E).astype(jnp.int32), g_max)
    ys = _moe_expert(list_a, hn, pos, wg, wu, wd, n_tiles)
    return _moe_combine(list_b, x2d, ys, pos_col, gate_col, gpost)


def _feature_row_order():
    def diff_block(base):
        x1, x2, rest = [], [], []
        half = DIFF_ROPE_DIM // 2
        for h in range(DIFF_HEADS):
            for c in range(2):
                lo = base + h * 2 * DIFF_QK_DIM + c * DIFF_QK_DIM
                x1 += range(lo, lo + half)
                x2 += range(lo + half, lo + DIFF_ROPE_DIM)
                rest += range(lo + DIFF_ROPE_DIM, lo + DIFF_QK_DIM)
        return x1 + x2 + rest

    order = list(range(0, _S_DQ)) + diff_block(_S_DQ) + diff_block(_S_DK) + list(range(_S_DV, N_FEAT))
    return np.asarray(order, np.int32)


def _mla_q_up_order():
    w = MLA_NOPE_DIM + MLA_ROPE_DIM
    half = MLA_ROPE_DIM // 2
    nope = [h * w + j for h in range(MLA_HEADS) for j in range(MLA_NOPE_DIM)]
    x1 = [h * w + MLA_NOPE_DIM + j for h in range(MLA_HEADS) for j in range(half)]
    x2 = [h * w + MLA_NOPE_DIM + half + j for h in range(MLA_HEADS) for j in range(half)]
    return np.asarray(nope + x1 + x2, np.int32)


def _mla_kv_up_order():
    w = MLA_NOPE_DIM + MLA_V_DIM
    k = [h * w + j for h in range(MLA_HEADS) for j in range(MLA_NOPE_DIM)]
    v = [h * w + MLA_NOPE_DIM + j for h in range(MLA_HEADS) for j in range(MLA_V_DIM)]
    return np.asarray(k + v, np.int32)


def _rope_tables(S):
    def cos_sin(pos, rot_dim):
        inv = ROPE_THETA ** (-jnp.arange(0, rot_dim, 2, dtype=F32) / rot_dim)
        ang = pos.astype(F32)[:, None] * inv[None, :]
        return jnp.cos(ang).T, jnp.sin(ang).T

    rows = S // GRID_W
    pos = jnp.arange(S)
    row = jnp.repeat(jnp.arange(rows), GRID_W)
    col = jnp.tile(jnp.arange(GRID_W), rows)
    cm, sm = cos_sin(pos, MLA_ROPE_DIM)
    cr, sr = cos_sin(row, GQA_HEAD_DIM // 2)
    cc, sc = cos_sin(col, GQA_HEAD_DIM // 2)
    cp, sp = cos_sin(pos, DIFF_ROPE_DIM)
    return (jnp.tile(cm, (MLA_HEADS, 1)), jnp.tile(sm, (MLA_HEADS, 1)),
            jnp.concatenate([cr, cc], axis=0), jnp.concatenate([sr, sc], axis=0),
            jnp.tile(cp, (2 * DIFF_HEADS, 1)), jnp.tile(sp, (2 * DIFF_HEADS, 1)))


def _row(v):
    return v.reshape(1, -1).astype(F32)


def _col(v):
    return v.reshape(-1, 1).astype(F32)


def kernel(x, mem, mix_pre_g, mix_post_g, w_in, mla_q_norm_g, mla_w_q_up, mla_kv_norm_g, mla_w_kv_up,
           gqa_q_norm_g, gqa_k_norm_g, diff_lambda_q1, diff_lambda_k1, diff_lambda_q2, diff_lambda_k2,
           diff_subln_g, w_branch, w_out, mem_pre_g, mem_post_g, mem_norm_g, mem_wq, mem_wkv, mem_wo,
           ffn_pre_g, ffn_post_g, dense_w_gate, dense_w_up, dense_w_down, moe_w_router, moe_b_router,
           moe_w_gate, moe_w_up, moe_w_down):
    B, S, D = x.shape
    M = mem.shape[1]
    depth = w_in.shape[0]
    tables = _rope_tables(S)
    feat_order = _feature_row_order()
    q_up_order = _mla_q_up_order()
    kv_up_order = _mla_kv_up_order()

    for layer in range(depth):
        lambda_init = 0.8 - 0.6 * math.exp(-0.3 * layer)

        w_l = w_in[layer]
        wt = w_l[:, feat_order].T.astype(BF16)
        wg = w_l[:, N_FEAT:].astype(BF16)
        wqu = mla_w_q_up[layer][:, q_up_order].T.astype(BF16)
        wkvu = mla_w_kv_up[layer][:, kv_up_order].T.astype(BF16)
        (qm, km, vm, qg, kg, vg, qd, kd, vd, gates) = _mixer_prep(
            x, _row(mix_pre_g[layer]), wt, wg, wqu, wkvu,
            _col(mla_q_norm_g[layer]), _col(mla_kv_norm_g[layer]),
            _col(gqa_q_norm_g[layer]), _col(gqa_k_norm_g[layer]), tables)
        a_t = _attention(qm, km, vm, kv_of_head=lambda h: h, kcol_of_head=lambda h: h, name="mla_attn")
        b_t = _attention(qg, kg, vg, kv_of_head=lambda h: h // GQA_GROUP, kcol_of_head=lambda h: 0,
                         name="gqa_attn")
        c_t = _diff_attention(qd, kd, vd, _row(diff_lambda_q1[layer]), _row(diff_lambda_k1[layer]),
                              _row(diff_lambda_q2[layer]), _row(diff_lambda_k2[layer]),
                              _col(diff_subln_g[layer]), lambda_init)
        x = _mixer_merge(x, a_t, b_t, c_t, gates, w_branch[layer].astype(BF16),
                         w_out[layer].astype(BF16), _row(mix_post_g[layer]))

        kv = _norm_matmul(mem.reshape(B * M, D), _row(mem_norm_g[layer]), mem_wkv[layer].astype(BF16), BF16)
        x = _mem_attention(x, kv.reshape(B, M, 2 * D), _row(mem_pre_g[layer]), mem_wq[layer].astype(BF16),
                           mem_wo[layer].astype(BF16), _row(mem_post_g[layer]))

        x2d = x.reshape(B * S, D)
        i = layer // 2
        if layer % 2 == 0:
            x2d = _ffn(x2d, _row(ffn_pre_g[layer]), dense_w_gate[i].astype(BF16), dense_w_up[i].astype(BF16),
                       dense_w_down[i].astype(BF16), _row(ffn_post_g[layer]))
        else:
            x2d = _moe(x2d, _row(ffn_pre_g[layer]), moe_w_router[i].T.astype(F32), _col(moe_b_router[i]),
                       moe_w_gate[i].astype(BF16), moe_w_up[i].astype(BF16), moe_w_down[i].astype(BF16),
                       _row(ffn_post_g[layer]))
        x = x2d.reshape(B, S, D)
    return x
```

```python
import functools
import math

import numpy as np
import jax
import jax.numpy as jnp
from jax import lax
from jax.experimental import pallas as pl
from jax.experimental.pallas import tpu as pltpu

F32 = jnp.float32
BF16 = jnp.bfloat16

D_MODEL = 1024
GRID_W = 64
ROPE_THETA = 500000.0
NORM_EPS = 1e-6

MLA_HEADS = 8
MLA_Q_LORA = 384
MLA_KV_LORA = 256
MLA_NOPE_DIM = 64
MLA_ROPE_DIM = 32
MLA_V_DIM = 64

GQA_Q_HEADS = 8
GQA_KV_HEADS = 2
GQA_GROUP = GQA_Q_HEADS // GQA_KV_HEADS
GQA_HEAD_DIM = 64

DIFF_HEADS = 8
DIFF_QK_DIM = 32
DIFF_V_DIM = 2 * DIFF_QK_DIM
DIFF_ROPE_DIM = DIFF_QK_DIM // 4

N_BRANCHES = 3
BRANCH_WIDTH = 512
HEAD_V = 64

MEM_HEADS = 4
MEM_HEAD_DIM = D_MODEL // MEM_HEADS

D_FF = 2816
N_EXPERTS = 8
TOP_K = 2

IN_SPLITS = (MLA_Q_LORA, MLA_KV_LORA, MLA_ROPE_DIM,
             GQA_Q_HEADS * GQA_HEAD_DIM, GQA_KV_HEADS * GQA_HEAD_DIM, GQA_KV_HEADS * GQA_HEAD_DIM,
             DIFF_HEADS * 2 * DIFF_QK_DIM, DIFF_HEADS * 2 * DIFF_QK_DIM, DIFF_HEADS * DIFF_V_DIM,
             N_BRANCHES * D_MODEL)
IN_OFFS = tuple(int(v) for v in np.cumsum((0,) + IN_SPLITS))
N_FEAT = IN_OFFS[9]
N_GATE = IN_SPLITS[9]

LOG2E = 1.4426950408889634
QK_PAD = 128
ONES_ROWS = 16

V7X_VMEM_LIMIT_BYTES = 56 * 1024 * 1024

PREP_TM = 512
ATTN_TQ = 1024
ATTN_TK = 256
ATTN_UNROLL = 8
MERGE_TM = 512
MEM_TM = 1024
FFN_TM = 512
FFN_TF = 1408
MOE_BLOCK = 512
KV_TM = 256


def _cparams(sem):
    return pltpu.CompilerParams(dimension_semantics=sem, vmem_limit_bytes=V7X_VMEM_LIMIT_BYTES)


def _const_spec(shape):
    nd = len(shape)
    return pl.BlockSpec(shape, lambda *_: (0,) * nd, pipeline_mode=pl.Buffered(1))


def _rms_rows(x, g_row):
    ms = jnp.mean(x * x, axis=-1, keepdims=True)
    return x * lax.rsqrt(ms + NORM_EPS) * g_row


def _rms_cols(x, g_col):
    ms = jnp.mean(x * x, axis=0, keepdims=True)
    return x * lax.rsqrt(ms + NORM_EPS) * g_col


def _dot(a, b):
    return jnp.dot(a, b, preferred_element_type=F32)


def _dot_nt(a, b):
    return lax.dot_general(a, b, (((1,), (1,)), ((), ())), preferred_element_type=F32)


_S_CQ, _S_CKV, _S_KR, _S_GQ, _S_GK, _S_GV, _S_DQ, _S_DK, _S_DV = IN_OFFS[:9]


def _rope_rows(x1, x2, c, s):
    return x1 * c - x2 * s, x2 * c + x1 * s


def _mixer_prep_kernel(x_ref, g_ref, wt_ref, wg_ref, wqu_ref, wkvu_ref,
                       mqg_ref, mkvg_ref, gqg_ref, gkg_ref,
                       cm_ref, sm_ref, ca_ref, sa_ref, cp_ref, sp_ref,
                       qm_ref, km_ref, vm_ref, qg_ref, kg_ref, vg_ref,
                       qd_ref, kd_ref, vd_ref, gate_ref):
    tm = x_ref.shape[1]
    hn = _rms_rows(x_ref[0], g_ref[...]).astype(BF16)

    def feat(lo, n):
        return _dot_nt(wt_ref[lo:lo + n, :], hn)

    gchunk = 1024
    for j in range(N_GATE // gchunk):
        z = _dot(hn, wg_ref[:, j * gchunk:(j + 1) * gchunk])
        gate_ref[0, :, j * gchunk:(j + 1) * gchunk] = jax.nn.sigmoid(z).astype(gate_ref.dtype)

    zeros32 = jnp.zeros((32, tm), F32)
    zeros64 = jnp.zeros((64, tm), F32)

    c_mla = (MLA_NOPE_DIM + MLA_ROPE_DIM) ** -0.5 * LOG2E
    cm = cm_ref[...]
    sm = sm_ref[...]
    cqn = _rms_cols(feat(_S_CQ, MLA_Q_LORA), mqg_ref[...]).astype(BF16)
    q_all = _dot(wqu_ref[...], cqn)
    q_nope = q_all[0:512] * c_mla
    q_r1, q_r2 = _rope_rows(q_all[512:640], q_all[640:768], cm, sm)
    q_r1 = q_r1 * c_mla
    q_r2 = q_r2 * c_mla
    for h in range(MLA_HEADS):
        qm_ref[0, h] = jnp.concatenate(
            [q_nope[64 * h:64 * h + 64], q_r1[16 * h:16 * h + 16], q_r2[16 * h:16 * h + 16], zeros32],
            axis=0).astype(qm_ref.dtype)

    ckvn = _rms_cols(feat(_S_CKV, MLA_KV_LORA), mkvg_ref[...]).astype(BF16)
    kv_all = _dot(wkvu_ref[...], ckvn)
    vm_ref[0] = kv_all[512:1024].astype(vm_ref.dtype)
    kr = feat(_S_KR, MLA_ROPE_DIM)
    k_r1, k_r2 = _rope_rows(kr[0:16], kr[16:32], cm[0:16], sm[0:16])
    for h in range(MLA_HEADS):
        kt = jnp.concatenate([kv_all[64 * h:64 * h + 64], k_r1, k_r2, zeros32], axis=0)
        km_ref[0, :, 128 * h:128 * h + 128] = kt.T.astype(km_ref.dtype)

    c_gqa = GQA_HEAD_DIM ** -0.5 * LOG2E
    ca = ca_ref[...]
    sa = sa_ref[...]
    gq = feat(_S_GQ, GQA_Q_HEADS * GQA_HEAD_DIM)
    for h in range(GQA_Q_HEADS):
        qn = _rms_cols(gq[64 * h:64 * h + 64], gqg_ref[...])
        r1, r2 = _rope_rows(qn[0:32], qn[32:64], ca, sa)
        q64 = jnp.concatenate([r1, r2], axis=0) * c_gqa
        parts = [q64, zeros64] if h // GQA_GROUP == 0 else [zeros64, q64]
        qg_ref[0, h] = jnp.concatenate(parts, axis=0).astype(qg_ref.dtype)
    gk = feat(_S_GK, GQA_KV_HEADS * GQA_HEAD_DIM)
    kparts = []
    for g in range(GQA_KV_HEADS):
        kn = _rms_cols(gk[64 * g:64 * g + 64], gkg_ref[...])
        r1, r2 = _rope_rows(kn[0:32], kn[32:64], ca, sa)
        kparts += [r1, r2]
    kg_ref[0] = jnp.concatenate(kparts, axis=0).T.astype(kg_ref.dtype)
    vg_ref[0] = feat(_S_GV, GQA_KV_HEADS * GQA_HEAD_DIM).astype(vg_ref.dtype)

    c_diff = DIFF_QK_DIM ** -0.5 * LOG2E
    cp = cp_ref[...]
    sp = sp_ref[...]
    row = lax.broadcasted_iota(jnp.int32, (64, 1), 0)
    in_c0 = (row < 4) | ((row >= 8) & (row < 12)) | ((row >= 16) & (row < 40))

    def diff_heads(lo):
        z = feat(lo, DIFF_HEADS * 2 * DIFF_QK_DIM)
        r1, r2 = _rope_rows(z[0:64], z[64:128], cp, sp)
        rest = z[128:512]
        return [jnp.concatenate([r1[8 * h:8 * h + 8], r2[8 * h:8 * h + 8], rest[48 * h:48 * h + 48]], axis=0)
                for h in range(DIFF_HEADS)]

    for h, q64 in enumerate(diff_heads(_S_DQ)):
        q64 = q64 * c_diff
        for c in range(2):
            qc = jnp.where(in_c0 if c == 0 else jnp.logical_not(in_c0), q64, 0.0)
            parts = [qc, zeros64] if h % 2 == 0 else [zeros64, qc]
            qd_ref[0, 2 * h + c] = jnp.concatenate(parts, axis=0).astype(qd_ref.dtype)
    k_heads = diff_heads(_S_DK)
    for p in range(DIFF_HEADS // 2):
        kt = jnp.concatenate([k_heads[2 * p], k_heads[2 * p + 1]], axis=0)
        kd_ref[0, :, 128 * p:128 * p + 128] = kt.T.astype(kd_ref.dtype)
    vd_ref[0] = feat(_S_DV, DIFF_HEADS * DIFF_V_DIM).astype(vd_ref.dtype)


def _mixer_prep(x, g, wt, wg, wqu, wkvu, mqg, mkvg, gqg, gkg, tables):
    B, S, D = x.shape
    tm = min(PREP_TM, S)
    cm, sm, ca, sa, cp, sp = tables
    grid = (B, S // tm)

    def tok(width):
        return pl.BlockSpec((1, tm, width), lambda b, i: (b, i, 0))

    def featm(rows):
        return pl.BlockSpec((1, rows, tm), lambda b, i: (b, 0, i))

    def heads(n):
        return pl.BlockSpec((1, n, QK_PAD, tm), lambda b, i: (b, 0, 0, i))

    def table(rows):
        return pl.BlockSpec((rows, tm), lambda b, i: (0, i))

    in_specs = [
        pl.BlockSpec((1, tm, D), lambda b, i: (b, i, 0)),
        _const_spec((1, D)),
        _const_spec(wt.shape), _const_spec(wg.shape), _const_spec(wqu.shape), _const_spec(wkvu.shape),
        _const_spec(mqg.shape), _const_spec(mkvg.shape), _const_spec(gqg.shape), _const_spec(gkg.shape),
        table(128), table(128), table(32), table(32), table(64), table(64),
    ]
    out_shape = [
        jax.ShapeDtypeStruct((B, MLA_HEADS, QK_PAD, S), BF16),
        jax.ShapeDtypeStruct((B, S, MLA_HEADS * QK_PAD), BF16),
        jax.ShapeDtypeStruct((B, MLA_HEADS * HEAD_V, S), BF16),
        jax.ShapeDtypeStruct((B, GQA_Q_HEADS, QK_PAD, S), BF16),
        jax.ShapeDtypeStruct((B, S, QK_PAD), BF16),
        jax.ShapeDtypeStruct((B, GQA_KV_HEADS * HEAD_V, S), BF16),
        jax.ShapeDtypeStruct((B, 2 * DIFF_HEADS, QK_PAD, S), BF16),
        jax.ShapeDtypeStruct((B, S, DIFF_HEADS // 2 * QK_PAD), BF16),
        jax.ShapeDtypeStruct((B, DIFF_HEADS * HEAD_V, S), BF16),
        jax.ShapeDtypeStruct((B, S, N_GATE), BF16),
    ]
    out_specs = [
        heads(MLA_HEADS), tok(MLA_HEADS * QK_PAD), featm(MLA_HEADS * HEAD_V),
        heads(GQA_Q_HEADS), tok(QK_PAD), featm(GQA_KV_HEADS * HEAD_V),
        heads(2 * DIFF_HEADS), tok(DIFF_HEADS // 2 * QK_PAD), featm(DIFF_HEADS * HEAD_V),
        tok(N_GATE),
    ]
    return pl.pallas_call(
        _mixer_prep_kernel, grid=grid, in_specs=in_specs, out_specs=out_specs, out_shape=out_shape,
        compiler_params=_cparams(("parallel", "parallel")), name="mixer_prep",
    )(x, g, wt, wg, wqu, wkvu, mqg, mkvg, gqg, gkg, cm, sm, ca, sa, cp, sp)


def _online_softmax_attention(q_t, k_ref, v_ref, s_ref, *, tk):
    S = k_ref.shape[1]
    lanes = q_t.shape[1]
    n_chunks = S // tk
    unroll = min(ATTN_UNROLL, n_chunks)
    assert unroll % 2 == 0 and n_chunks % unroll == 0
    ones = jnp.ones((ONES_ROWS, tk), BF16)

    def scores(c, slot):
        s_t = _dot(k_ref[0, pl.ds(pl.multiple_of(c * tk, tk), tk), :], q_t)
        s_ref[slot] = s_t
        return jnp.max(s_t, axis=0, keepdims=True)

    def update(c, slot, col_max, m, acc):
        v_aug = jnp.concatenate([v_ref[0, :, pl.ds(pl.multiple_of(c * tk, tk), tk)], ones], axis=0)
        m_new = jnp.maximum(m, col_max)
        p = jnp.exp2(s_ref[slot] - m_new).astype(BF16)
        return m_new, acc * jnp.exp2(m - m_new) + _dot(v_aug, p)

    def group(c, cmax, m, acc, last):
        for u in range(unroll):
            is_final = last and u == unroll - 1
            cmax_next = None if is_final else scores(c + u + 1, (u + 1) % 2)
            m, acc = update(c + u, u % 2, cmax, m, acc)
            cmax = cmax_next
        return cmax, m, acc

    def body(i, carry):
        return group(unroll * i, *carry, last=False)

    m0 = jnp.full((1, lanes), -jnp.inf, F32)
    acc0 = jnp.zeros((HEAD_V + ONES_ROWS, lanes), F32)
    carry = lax.fori_loop(0, n_chunks // unroll - 1, body, (scores(0, 0), m0, acc0))
    _, _, acc = group(n_chunks - unroll, *carry, last=True)
    return acc[:HEAD_V] / acc[HEAD_V:HEAD_V + 1]


def _attn_kernel(q_ref, k_ref, v_ref, o_ref, s_ref, *, tk):
    o_ref[0] = _online_softmax_attention(q_ref[0, 0], k_ref, v_ref, s_ref, tk=tk).astype(o_ref.dtype)


def _attention(q_t, k, v_t, *, kv_of_head, kcol_of_head, name):
    B, H, _, S = q_t.shape
    tq = min(ATTN_TQ, S)
    tk = min(ATTN_TK, S // 2)
    return pl.pallas_call(
        functools.partial(_attn_kernel, tk=tk),
        grid=(B, H, S // tq),
        in_specs=[
            pl.BlockSpec((1, 1, QK_PAD, tq), lambda b, h, i: (b, h, 0, i)),
            pl.BlockSpec((1, S, QK_PAD), lambda b, h, i: (b, 0, kcol_of_head(h))),
            pl.BlockSpec((1, HEAD_V, S), lambda b, h, i: (b, kv_of_head(h), 0)),
        ],
        out_specs=pl.BlockSpec((1, HEAD_V, tq), lambda b, h, i: (b, h, i)),
        out_shape=jax.ShapeDtypeStruct((B, H * HEAD_V, S), F32),
        scratch_shapes=[pltpu.VMEM((2, tk, tq), F32)],
        compiler_params=_cparams(("parallel", "parallel", "arbitrary")), name=name,
    )(q_t, k, v_t)


def _diff_attn_kernel(lq1_ref, lk1_ref, lq2_ref, lk2_ref, g_ref, q_ref, k_ref, v_ref, o_ref, s_ref, *,
                      tk, lambda_init):
    tq = q_ref.shape[3]
    q_both = jnp.concatenate([q_ref[0, 0], q_ref[0, 1]], axis=1)
    o_maps = _online_softmax_attention(q_both, k_ref, v_ref, s_ref, tk=tk)
    lam = (jnp.exp(jnp.sum(lq1_ref[...] * lk1_ref[...], axis=-1, keepdims=True))
           - jnp.exp(jnp.sum(lq2_ref[...] * lk2_ref[...], axis=-1, keepdims=True)) + lambda_init)
    o = o_maps[:, :tq] - lam * o_maps[:, tq:]
    o = _rms_cols(o, g_ref[...]) * (1.0 - lambda_init)
    o_ref[0] = o.astype(o_ref.dtype)


def _diff_attention(q_t, k, v_t, lq1, lk1, lq2, lk2, subln_g, lambda_init):
    B, H2, _, S = q_t.shape
    H = H2 // 2
    tq = min(ATTN_TQ // 2, S)
    tk = min(ATTN_TK, S // 2)
    vec = _const_spec((1, DIFF_QK_DIM))
    return pl.pallas_call(
        functools.partial(_diff_attn_kernel, tk=tk, lambda_init=lambda_init),
        grid=(B, H, S // tq),
        in_specs=[
            vec, vec, vec, vec, _const_spec((DIFF_V_DIM, 1)),
            pl.BlockSpec((1, 2, QK_PAD, tq), lambda b, h, i: (b, h, 0, i)),
            pl.BlockSpec((1, S, QK_PAD), lambda b, h, i: (b, 0, h // 2)),
            pl.BlockSpec((1, HEAD_V, S), lambda b, h, i: (b, h, 0)),
        ],
        out_specs=pl.BlockSpec((1, HEAD_V, tq), lambda b, h, i: (b, h, i)),
        out_shape=jax.ShapeDtypeStruct((B, H * HEAD_V, S), F32),
        scratch_shapes=[pltpu.VMEM((2, tk, 2 * tq), F32)],
        compiler_params=_cparams(("parallel", "parallel", "arbitrary")), name="diff_attn",
    )(lq1, lk1, lq2, lk2, subln_g, q_t, k, v_t)


def _mixer_merge_kernel(x_ref, a_ref, b_ref, c_ref, gate_ref, wb_ref, wo_ref, g_ref, o_ref):
    merged = None
    for n, br_ref in enumerate((a_ref, b_ref, c_ref)):
        br = br_ref[0].T.astype(BF16)
        proj = _dot(br, wb_ref[n])
        term = gate_ref[0, :, n * D_MODEL:(n + 1) * D_MODEL].astype(F32) * proj
        merged = term if merged is None else merged + term
    y = _dot(merged.astype(BF16), wo_ref[...])
    o_ref[0] = x_ref[0] + _rms_rows(y, g_ref[...])


def _mixer_merge(x, a_t, b_t, c_t, gates, wb, wo, g):
    B, S, D = x.shape
    tm = min(MERGE_TM, S)
    xspec = pl.BlockSpec((1, tm, D), lambda b, i: (b, i, 0))
    brspec = pl.BlockSpec((1, BRANCH_WIDTH, tm), lambda b, i: (b, 0, i))
    return pl.pallas_call(
        _mixer_merge_kernel, grid=(B, S // tm),
        in_specs=[xspec, brspec, brspec, brspec,
                  pl.BlockSpec((1, tm, N_GATE), lambda b, i: (b, i, 0)),
                  _const_spec(wb.shape), _const_spec(wo.shape), _const_spec((1, D))],
        out_specs=xspec, out_shape=jax.ShapeDtypeStruct(x.shape, x.dtype),
        compiler_params=_cparams(("parallel", "parallel")), name="mixer_merge",
    )(x, a_t, b_t, c_t, gates, wb, wo, g)


def _norm_matmul_kernel(x_ref, g_ref, w_ref, o_ref):
    hn = _rms_rows(x_ref[...], g_ref[...]).astype(BF16)
    o_ref[...] = _dot(hn, w_ref[...]).astype(o_ref.dtype)


def _norm_matmul(x2d, g, w, out_dtype):
    T, D = x2d.shape
    N = w.shape[1]
    tm = min(KV_TM, T)
    return pl.pallas_call(
        _norm_matmul_kernel, grid=(T // tm,),
        in_specs=[pl.BlockSpec((tm, D), lambda i: (i, 0)), _const_spec((1, D)), _const_spec(w.shape)],
        out_specs=pl.BlockSpec((tm, N), lambda i: (i, 0)),
        out_shape=jax.ShapeDtypeStruct((T, N), out_dtype),
        compiler_params=_cparams(("parallel",)), name="mem_kv",
    )(x2d, g, w)


def _mem_attn_kernel(x_ref, kv_ref, gpre_ref, wq_ref, wo_ref, gpost_ref, o_ref):
    x = x_ref[0]
    hn = _rms_rows(x, gpre_ref[...]).astype(BF16)
    q = (_dot(hn, wq_ref[...]) * (MEM_HEAD_DIM ** -0.5 * LOG2E)).astype(BF16)
    outs = []
    for h in range(MEM_HEADS):
        lo = h * MEM_HEAD_DIM
        k_h = kv_ref[0, :, lo:lo + MEM_HEAD_DIM]
        v_h = kv_ref[0, :, D_MODEL + lo:D_MODEL + lo + MEM_HEAD_DIM]
        s = _dot_nt(q[:, lo:lo + MEM_HEAD_DIM], k_h)
        e = jnp.exp2(s - jnp.max(s, axis=-1, keepdims=True))
        p = (e / jnp.sum(e, axis=-1, keepdims=True)).astype(BF16)
        outs.append(_dot(p, v_h))
    o = jnp.concatenate(outs, axis=-1).astype(BF16)
    y = _dot(o, wo_ref[...])
    o_ref[0] = x + _rms_rows(y, gpost_ref[...])


def _mem_attention(x, kv, gpre, wq, wo, gpost):
    B, S, D = x.shape
    M = kv.shape[1]
    tm = min(MEM_TM, S)
    xspec = pl.BlockSpec((1, tm, D), lambda b, i: (b, i, 0))
    return pl.pallas_call(
        _mem_attn_kernel, grid=(B, S // tm),
        in_specs=[xspec, pl.BlockSpec((1, M, 2 * D), lambda b, i: (b, 0, 0)),
                  _const_spec((1, D)), _const_spec(wq.shape), _const_spec(wo.shape), _const_spec((1, D))],
        out_specs=xspec, out_shape=jax.ShapeDtypeStruct(x.shape, x.dtype),
        compiler_params=_cparams(("parallel", "parallel")), name="mem_attn",
    )(x, kv, gpre, wq, wo, gpost)


def _swiglu_chunk(hn, wg, wu, wd):
    gate = _dot(hn, wg)
    act = (gate * jax.nn.sigmoid(gate) * _dot(hn, wu)).astype(BF16)
    return _dot(act, wd)


def _swiglu(hn, wg_ref, wu_ref, wd_ref, tf):
    y = None
    for lo in range(0, wg_ref.shape[-1], tf):
        part = _swiglu_chunk(hn, wg_ref[:, lo:lo + tf], wu_ref[:, lo:lo + tf], wd_ref[lo:lo + tf, :])
        y = part if y is None else y + part
    return y


def _ffn_kernel(x_ref, gpre_ref, wg_ref, wu_ref, wd_ref, gpost_ref, o_ref):
    x = x_ref[...]
    hn = _rms_rows(x, gpre_ref[...]).astype(BF16)
    o_ref[...] = x + _rms_rows(_swiglu(hn, wg_ref, wu_ref, wd_ref, FFN_TF), gpost_ref[...])


def _ffn(x2d, gpre, wg, wu, wd, gpost):
    T, D = x2d.shape
    tm = min(FFN_TM, T)
    xspec = pl.BlockSpec((tm, D), lambda i: (i, 0))
    return pl.pallas_call(
        _ffn_kernel, grid=(T // tm,),
        in_specs=[xspec, _const_spec((1, D)), _const_spec(wg.shape), _const_spec(wu.shape),
                  _const_spec(wd.shape), _const_spec((1, D))],
        out_specs=xspec, out_shape=jax.ShapeDtypeStruct(x2d.shape, x2d.dtype),
        compiler_params=_cparams(("parallel",)), name="ffn",
    )(x2d, gpre, wg, wu, wd, gpost)


def _router_kernel(x_ref, gpre_ref, wr_ref, br_ref, tri_ref,
                   hn_ref, pos_ref, gate_col_ref, pos_col_ref, cnt_ref, run_ref):
    @pl.when(pl.program_id(0) == 0)
    def _():
        run_ref[...] = jnp.zeros_like(run_ref)

    hn = _rms_rows(x_ref[...], gpre_ref[...])
    hn_ref[...] = hn.astype(hn_ref.dtype)
    logits = lax.dot_general(wr_ref[...], hn, (((1,), (1,)), ((), ())),
                             precision=lax.Precision.HIGHEST,
                             preferred_element_type=F32) + br_ref[...]
    e_idx = lax.broadcasted_iota(jnp.int32, logits.shape, 0)
    m1 = jnp.max(logits, axis=0, keepdims=True)
    i1 = jnp.min(jnp.where(logits == m1, e_idx, N_EXPERTS), axis=0, keepdims=True)
    rest = jnp.where(e_idx == i1, -jnp.inf, logits)
    m2 = jnp.max(rest, axis=0, keepdims=True)
    i2 = jnp.min(jnp.where(rest == m2, e_idx, N_EXPERTS), axis=0, keepdims=True)
    e2 = jnp.exp(m2 - m1)
    w1 = 1.0 / (1.0 + e2)
    w2 = e2 / (1.0 + e2)
    gates = jnp.where(e_idx == i1, w1, 0.0) + jnp.where(e_idx == i2, w2, 0.0)

    sel = (e_idx == i1) | (e_idx == i2)
    sel_f = jnp.where(sel, 1.0, 0.0)
    before = _dot(sel_f.astype(BF16), tri_ref[...])
    run = run_ref[:, 0:1]
    pos = jnp.where(sel, run + before, -1.0)
    pos_ref[...] = pos
    tm = pos.shape[1]
    cols = jnp.concatenate([gates, pos, jnp.zeros((128 - 2 * N_EXPERTS, tm), F32)], axis=0).T
    for e in range(N_EXPERTS):
        gate_col_ref[e] = cols[:, e:e + 1]
        pos_col_ref[e] = cols[:, N_EXPERTS + e:N_EXPERTS + e + 1]
    cnt = jnp.sum(sel_f, axis=1, keepdims=True)
    cnt_ref[0] = cnt
    run_ref[...] = run_ref[...] + cnt


def _router(x2d, gpre, wr_t, br_col):
    T, D = x2d.shape
    tm = MOE_BLOCK
    nb = T // tm
    tri = (np.arange(tm)[:, None] < np.arange(tm)[None, :]).astype(np.float32)
    col = pl.BlockSpec((N_EXPERTS, tm, 1), lambda i: (0, i, 0))
    return pl.pallas_call(
        _router_kernel, grid=(nb,),
        in_specs=[pl.BlockSpec((tm, D), lambda i: (i, 0)), _const_spec((1, D)),
                  _const_spec(wr_t.shape), _const_spec(br_col.shape), _const_spec((tm, tm))],
        out_specs=[pl.BlockSpec((tm, D), lambda i: (i, 0)),
                   pl.BlockSpec((N_EXPERTS, tm), lambda i: (0, i)), col, col,
                   pl.BlockSpec((1, N_EXPERTS, 1), lambda i: (i, 0, 0))],
        out_shape=[jax.ShapeDtypeStruct((T, D), BF16),
                   jax.ShapeDtypeStruct((N_EXPERTS, T), F32),
                   jax.ShapeDtypeStruct((N_EXPERTS, T, 1), F32),
                   jax.ShapeDtypeStruct((N_EXPERTS, T, 1), F32),
                   jax.ShapeDtypeStruct((nb, N_EXPERTS, 1), F32)],
        scratch_shapes=[pltpu.VMEM((N_EXPERTS, 128), F32)],
        compiler_params=_cparams(("arbitrary",)), name="router",
    )(x2d, gpre, wr_t, br_col, jnp.asarray(tri, BF16))


_FIRST, _LAST = 1, 2
_HALF = (4, 8)


def _moe_schedule(cnt_be, g_max):
    nb, ne = cnt_be.shape
    tr = MOE_BLOCK
    cnt_eb = cnt_be.T
    start_eb = jnp.cumsum(cnt_eb, axis=1) - cnt_eb
    n_tile_e = (jnp.sum(cnt_eb, axis=1) + tr - 1) // tr
    tile0_e = jnp.cumsum(n_tile_e) - n_tile_e
    first_eb = start_eb // tr
    n_eb = jnp.where(cnt_eb > 0, (start_eb + cnt_eb - 1) // tr - first_eb + 1, 0)
    g = jnp.arange(g_max, dtype=jnp.int32)

    def visit_list(pair_e, pair_b, group_of):
        n = n_eb[pair_e, pair_b]
        ends = jnp.cumsum(n)
        total = ends[-1]
        gi = jnp.minimum(g, total - 1)
        pair = jnp.sum(ends[None, :] <= gi[:, None], axis=1).astype(jnp.int32)
        e = pair_e[pair]
        b = pair_b[pair]
        local = first_eb[e, b] + gi - (ends[pair] - n[pair])
        tile = tile0_e[e] + local
        grp = group_of(tile, b)
        valid = g < total
        first = valid & ((g == 0) | (grp != jnp.roll(grp, 1)))
        last = valid & ((g == total - 1) | (grp != jnp.roll(grp, -1)))
        lo = start_eb[e, b] - local * tr
        hi = lo + cnt_eb[e, b]
        flags = (first * _FIRST + last * _LAST
                 + (valid & (lo < tr // 2)) * _HALF[0] + (valid & (hi > tr // 2)) * _HALF[1])
        return [v.astype(jnp.int32) for v in (e, b, tile, local * tr, flags)]

    ee, bb = np.meshgrid(np.arange(ne), np.arange(nb), indexing="ij")
    list_a = visit_list(jnp.asarray(ee.reshape(-1)), jnp.asarray(bb.reshape(-1)), lambda tile, b: tile)
    list_b = visit_list(jnp.asarray(ee.T.reshape(-1)), jnp.asarray(bb.T.reshape(-1)), lambda tile, b: b)
    return list_a, list_b


def _moe_expert_kernel(e_ref, b_ref, t_ref, r_ref, f_ref, hn_ref, pos_ref, wg_ref, wu_ref, wd_ref,
                       ys_ref, xs_ref):
    g = pl.program_id(0)
    flags = f_ref[g]
    tr = xs_ref.shape[0]

    @pl.when((flags & _FIRST) != 0)
    def _():
        xs_ref[...] = jnp.zeros_like(xs_ref)

    half = tr // 2
    for h in range(2):
        @pl.when((flags & _HALF[h]) != 0)
        def _():
            rows = (lax.broadcasted_iota(jnp.int32, (half, 1), 0) + (r_ref[g] + h * half)).astype(F32)
            onehot = jnp.where(rows == pos_ref[0, 0], 1.0, 0.0).astype(BF16)
            xs_ref[h * half:(h + 1) * half, :] += _dot(onehot, hn_ref[...])

    @pl.when((flags & _LAST) != 0)
    def _():
        y = _swiglu(xs_ref[...].astype(BF16), wg_ref.at[0], wu_ref.at[0], wd_ref.at[0], FFN_TF)
        ys_ref[...] = y.astype(ys_ref.dtype)


def _moe_expert(list_a, hn, pos, wg, wu, wd, n_tiles):
    T, D = hn.shape
    E, _, F = wg.shape
    tr = MOE_BLOCK
    g_max = list_a[0].shape[0]
    pos4 = pos.reshape(E, T // tr, 1, tr)
    wspec_in = pl.BlockSpec((1, D, F), lambda g, e, b, t, r, f: (e[g], 0, 0), pipeline_mode=pl.Buffered(1))
    wspec_out = pl.BlockSpec((1, F, D), lambda g, e, b, t, r, f: (e[g], 0, 0), pipeline_mode=pl.Buffered(1))
    grid_spec = pltpu.PrefetchScalarGridSpec(
        num_scalar_prefetch=5, grid=(g_max,),
        in_specs=[pl.BlockSpec((tr, D), lambda g, e, b, t, r, f: (b[g], 0)),
                  pl.BlockSpec((1, 1, 1, tr), lambda g, e, b, t, r, f: (e[g], b[g], 0, 0)),
                  wspec_in, wspec_in, wspec_out],
        out_specs=pl.BlockSpec((tr, D), lambda g, e, b, t, r, f: (t[g], 0)),
        scratch_shapes=[pltpu.VMEM((tr, D), F32)])
    return pl.pallas_call(
        _moe_expert_kernel, grid_spec=grid_spec,
        out_shape=jax.ShapeDtypeStruct((n_tiles * tr, D), BF16),
        compiler_params=_cparams(("arbitrary",)), name="moe_expert",
    )(*list_a, hn, pos4, wg, wu, wd)


def _moe_combine_kernel(e_ref, b_ref, t_ref, r_ref, f_ref, x_ref, ys_ref, pos_ref, gate_ref, gpost_ref,
                        o_ref, acc_ref):
    g = pl.program_id(0)
    flags = f_ref[g]
    tr = ys_ref.shape[0]

    @pl.when((flags & _FIRST) != 0)
    def _():
        acc_ref[...] = jnp.zeros_like(acc_ref)

    half = tr // 2
    for h in range(2):
        @pl.when((flags & _HALF[h]) != 0)
        def _():
            rows = (lax.broadcasted_iota(jnp.int32, (1, half), 1) + (r_ref[g] + h * half)).astype(F32)
            onehot = jnp.where(pos_ref[0] == rows, 1.0, 0.0).astype(BF16)
            acc_ref[...] += gate_ref[0] * _dot(onehot, ys_ref[h * half:(h + 1) * half, :])

    @pl.when((flags & _LAST) != 0)
    def _():
        o_ref[...] = x_ref[...] + _rms_rows(acc_ref[...], gpost_ref[...])


def _moe_combine(list_b, x2d, ys, pos_col, gate_col, gpost):
    T, D = x2d.shape
    tr = MOE_BLOCK
    g_max = list_b[0].shape[0]
    xspec = pl.BlockSpec((tr, D), lambda g, e, b, t, r, f: (b[g], 0))
    colspec = pl.BlockSpec((1, tr, 1), lambda g, e, b, t, r, f: (e[g], b[g], 0))
    grid_spec = pltpu.PrefetchScalarGridSpec(
        num_scalar_prefetch=5, grid=(g_max,),
        in_specs=[xspec, pl.BlockSpec((tr, D), lambda g, e, b, t, r, f: (t[g], 0)), colspec, colspec,
                  pl.BlockSpec((1, D), lambda g, e, b, t, r, f: (0, 0))],
        out_specs=xspec,
        scratch_shapes=[pltpu.VMEM((tr, D), F32)])
    return pl.pallas_call(
        _moe_combine_kernel, grid_spec=grid_spec,
        out_shape=jax.ShapeDtypeStruct(x2d.shape, x2d.dtype),
        compiler_params=_cparams(("arbitrary",)), name="moe_combine",
    )(*list_b, x2d, ys, pos_col, gate_col, gpost)


def _moe(x2d, gpre, wr_t, br_col, wg, wu, wd, gpost):
    T, D = x2d.shape
    E = wg.shape[0]
    nb = T // MOE_BLOCK
    n_tiles = TOP_K * nb + E
    g_max = E * nb + n_tiles
    hn, pos, gate_col, pos_col, cnt = _router(x2d, gpre, wr_t, br_col)
    list_a, list_b = _moe_schedule(cnt.reshape(nb, E).astype(jnp.int32), g_max)
    ys = _moe_expert(list_a, hn, pos, wg, wu, wd, n_tiles)
    return _moe_combine(list_b, x2d, ys, pos_col, gate_col, gpost)


def _feature_row_order():
    def diff_block(base):
        x1, x2, rest = [], [], []
        half = DIFF_ROPE_DIM // 2
        for h in range(DIFF_HEADS):
            for c in range(2):
                lo = base + h * 2 * DIFF_QK_DIM + c * DIFF_QK_DIM
                x1 += range(lo, lo + half)
                x2 += range(lo + half, lo + DIFF_ROPE_DIM)
                rest += range(lo + DIFF_ROPE_DIM, lo + DIFF_QK_DIM)
        return x1 + x2 + rest

    order = list(range(0, _S_DQ)) + diff_block(_S_DQ) + diff_block(_S_DK) + list(range(_S_DV, N_FEAT))
    return np.asarray(order, np.int32)


def _mla_q_up_order():
    w = MLA_NOPE_DIM + MLA_ROPE_DIM
    half = MLA_ROPE_DIM // 2
    nope = [h * w + j for h in range(MLA_HEADS) for j in range(MLA_NOPE_DIM)]
    x1 = [h * w + MLA_NOPE_DIM + j for h in range(MLA_HEADS) for j in range(half)]
    x2 = [h * w + MLA_NOPE_DIM + half + j for h in range(MLA_HEADS) for j in range(half)]
    return np.asarray(nope + x1 + x2, np.int32)


def _mla_kv_up_order():
    w = MLA_NOPE_DIM + MLA_V_DIM
    k = [h * w + j for h in range(MLA_HEADS) for j in range(MLA_NOPE_DIM)]
    v = [h * w + MLA_NOPE_DIM + j for h in range(MLA_HEADS) for j in range(MLA_V_DIM)]
    return np.asarray(k + v, np.int32)


def _rope_tables(S):
    def cos_sin(pos, rot_dim):
        inv = ROPE_THETA ** (-jnp.arange(0, rot_dim, 2, dtype=F32) / rot_dim)
        ang = pos.astype(F32)[:, None] * inv[None, :]
        return jnp.cos(ang).T, jnp.sin(ang).T

    rows = S // GRID_W
    pos = jnp.arange(S)
    row = jnp.repeat(jnp.arange(rows), GRID_W)
    col = jnp.tile(jnp.arange(GRID_W), rows)
    cm, sm = cos_sin(pos, MLA_ROPE_DIM)
    cr, sr = cos_sin(row, GQA_HEAD_DIM // 2)
    cc, sc = cos_sin(col, GQA_HEAD_DIM // 2)
    cp, sp = cos_sin(pos, DIFF_ROPE_DIM)
    return (jnp.tile(cm, (MLA_HEADS, 1)), jnp.tile(sm, (MLA_HEADS, 1)),
            jnp.concatenate([cr, cc], axis=0), jnp.concatenate([sr, sc], axis=0),
            jnp.tile(cp, (2 * DIFF_HEADS, 1)), jnp.tile(sp, (2 * DIFF_HEADS, 1)))


def _row(v):
    return v.reshape(1, -1).astype(F32)


def _col(v):
    return v.reshape(-1, 1).astype(F32)


def kernel(x, mem, mix_pre_g, mix_post_g, w_in, mla_q_norm_g, mla_w_q_up, mla_kv_norm_g, mla_w_kv_up,
           gqa_q_norm_g, gqa_k_norm_g, diff_lambda_q1, diff_lambda_k1, diff_lambda_q2, diff_lambda_k2,
           diff_subln_g, w_branch, w_out, mem_pre_g, mem_post_g, mem_norm_g, mem_wq, mem_wkv, mem_wo,
           ffn_pre_g, ffn_post_g, dense_w_gate, dense_w_up, dense_w_down, moe_w_router, moe_b_router,
           moe_w_gate, moe_w_up, moe_w_down):
    B, S, D = x.shape
    M = mem.shape[1]
    depth = w_in.shape[0]
    tables = _rope_tables(S)
    feat_order = _feature_row_order()
    q_up_order = _mla_q_up_order()
    kv_up_order = _mla_kv_up_order()

    for layer in range(depth):
        lambda_init = 0.8 - 0.6 * math.exp(-0.3 * layer)

        w_l = w_in[layer]
        wt = w_l[:, feat_order].T.astype(BF16)
        wg = w_l[:, N_FEAT:].astype(BF16)
        wqu = mla_w_q_up[layer][:, q_up_order].T.astype(BF16)
        wkvu = mla_w_kv_up[layer][:, kv_up_order].T.astype(BF16)
        (qm, km, vm, qg, kg, vg, qd, kd, vd, gates) = _mixer_prep(
            x, _row(mix_pre_g[layer]), wt, wg, wqu, wkvu,
            _col(mla_q_norm_g[layer]), _col(mla_kv_norm_g[layer]),
            _col(gqa_q_norm_g[layer]), _col(gqa_k_norm_g[layer]), tables)
        a_t = _attention(qm, km, vm, kv_of_head=lambda h: h, kcol_of_head=lambda h: h, name="mla_attn")
        b_t = _attention(qg, kg, vg, kv_of_head=lambda h: h // GQA_GROUP, kcol_of_head=lambda h: 0,
                         name="gqa_attn")
        c_t = _diff_attention(qd, kd, vd, _row(diff_lambda_q1[layer]), _row(diff_lambda_k1[layer]),
                              _row(diff_lambda_q2[layer]), _row(diff_lambda_k2[layer]),
                              _col(diff_subln_g[layer]), lambda_init)
        x = _mixer_merge(x, a_t, b_t, c_t, gates, w_branch[layer].astype(BF16),
                         w_out[layer].astype(BF16), _row(mix_post_g[layer]))

        kv = _norm_matmul(mem.reshape(B * M, D), _row(mem_norm_g[layer]), mem_wkv[layer].astype(BF16), BF16)
        x = _mem_attention(x, kv.reshape(B, M, 2 * D), _row(mem_pre_g[layer]), mem_wq[layer].astype(BF16),
                           mem_wo[layer].astype(BF16), _row(mem_post_g[layer]))

        x2d = x.reshape(B * S, D)
        i = layer // 2
        if layer % 2 == 0:
            x2d = _ffn(x2d, _row(ffn_pre_g[layer]), dense_w_gate[i].astype(BF16), dense_w_up[i].astype(BF16),
                       dense_w_down[i].astype(BF16), _row(ffn_post_g[layer]))
        else:
            x2d = _moe(x2d, _row(ffn_pre_g[layer]), moe_w_router[i].T.astype(F32), _col(moe_b_router[i]),
                       moe_w_gate[i].astype(BF16), moe_w_up[i].astype(BF16), moe_w_down[i].astype(BF16),
                       _row(ffn_post_g[layer]))
        x = x2d.reshape(B, S, D)
    return x
```

```python
import functools
import math

import numpy as np
import jax
import jax.numpy as jnp
from jax import lax
from jax.experimental import pallas as pl
from jax.experimental.pallas import tpu as pltpu

F32 = jnp.float32
BF16 = jnp.bfloat16

D_MODEL = 1024
GRID_W = 64
ROPE_THETA = 500000.0
NORM_EPS = 1e-6

MLA_HEADS = 8
MLA_Q_LORA = 384
MLA_KV_LORA = 256
MLA_NOPE_DIM = 64
MLA_ROPE_DIM = 32
MLA_V_DIM = 64

GQA_Q_HEADS = 8
GQA_KV_HEADS = 2
GQA_GROUP = GQA_Q_HEADS // GQA_KV_HEADS
GQA_HEAD_DIM = 64

DIFF_HEADS = 8
DIFF_QK_DIM = 32
DIFF_V_DIM = 2 * DIFF_QK_DIM
DIFF_ROPE_DIM = DIFF_QK_DIM // 4

N_BRANCHES = 3
BRANCH_WIDTH = 512
HEAD_V = 64

MEM_HEADS = 4
MEM_HEAD_DIM = D_MODEL // MEM_HEADS

D_FF = 2816
N_EXPERTS = 8
TOP_K = 2

IN_SPLITS = (MLA_Q_LORA, MLA_KV_LORA, MLA_ROPE_DIM,
             GQA_Q_HEADS * GQA_HEAD_DIM, GQA_KV_HEADS * GQA_HEAD_DIM, GQA_KV_HEADS * GQA_HEAD_DIM,
             DIFF_HEADS * 2 * DIFF_QK_DIM, DIFF_HEADS * 2 * DIFF_QK_DIM, DIFF_HEADS * DIFF_V_DIM,
             N_BRANCHES * D_MODEL)
IN_OFFS = tuple(int(v) for v in np.cumsum((0,) + IN_SPLITS))
N_FEAT = IN_OFFS[9]
N_GATE = IN_SPLITS[9]

LOG2E = 1.4426950408889634
QK_PAD = 128
ONES_ROWS = 16

V7X_VMEM_LIMIT_BYTES = 56 * 1024 * 1024

PREP_TM = 512
ATTN_TQ = 1024
ATTN_TK = 256
ATTN_UNROLL = 8
MERGE_TM = 512
MEM_TM = 1024
FFN_TM = 512
FFN_TF = 1408
MOE_BLOCK = 512
KV_TM = 256


def _cparams(sem):
    return pltpu.CompilerParams(dimension_semantics=sem, vmem_limit_bytes=V7X_VMEM_LIMIT_BYTES)


def _const_spec(shape):
    nd = len(shape)
    return pl.BlockSpec(shape, lambda *_: (0,) * nd, pipeline_mode=pl.Buffered(1))


def _rms_rows(x, g_row):
    ms = jnp.mean(x * x, axis=-1, keepdims=True)
    return x * lax.rsqrt(ms + NORM_EPS) * g_row


def _rms_cols(x, g_col):
    ms = jnp.mean(x * x, axis=0, keepdims=True)
    return x * lax.rsqrt(ms + NORM_EPS) * g_col


def _dot(a, b):
    return jnp.dot(a, b, preferred_element_type=F32)


def _dot_nt(a, b):
    return lax.dot_general(a, b, (((1,), (1,)), ((), ())), preferred_element_type=F32)


_S_CQ, _S_CKV, _S_KR, _S_GQ, _S_GK, _S_GV, _S_DQ, _S_DK, _S_DV = IN_OFFS[:9]


def _rope_rows(x1, x2, c, s):
    return x1 * c - x2 * s, x2 * c + x1 * s


def _mixer_prep_kernel(x_ref, g_ref, wt_ref, wg_ref, wqu_ref, wkvu_ref,
                       mqg_ref, mkvg_ref, gqg_ref, gkg_ref,
                       cm_ref, sm_ref, ca_ref, sa_ref, cp_ref, sp_ref,
                       qm_ref, km_ref, vm_ref, qg_ref, kg_ref, vg_ref,
                       qd_ref, kd_ref, vd_ref, gate_ref):
    tm = x_ref.shape[1]
    hn = _rms_rows(x_ref[0], g_ref[...]).astype(BF16)

    def feat(lo, n):
        return _dot_nt(wt_ref[lo:lo + n, :], hn)

    gchunk = 1024
    for j in range(N_GATE // gchunk):
        z = _dot(hn, wg_ref[:, j * gchunk:(j + 1) * gchunk])
        gate_ref[0, :, j * gchunk:(j + 1) * gchunk] = jax.nn.sigmoid(z).astype(gate_ref.dtype)

    zeros32 = jnp.zeros((32, tm), F32)
    zeros64 = jnp.zeros((64, tm), F32)

    c_mla = (MLA_NOPE_DIM + MLA_ROPE_DIM) ** -0.5 * LOG2E
    cm = cm_ref[...]
    sm = sm_ref[...]
    cqn = _rms_cols(feat(_S_CQ, MLA_Q_LORA), mqg_ref[...]).astype(BF16)
    q_all = _dot(wqu_ref[...], cqn)
    q_nope = q_all[0:512] * c_mla
    q_r1, q_r2 = _rope_rows(q_all[512:640], q_all[640:768], cm, sm)
    q_r1 = q_r1 * c_mla
    q_r2 = q_r2 * c_mla
    for h in range(MLA_HEADS):
        qm_ref[0, h] = jnp.concatenate(
            [q_nope[64 * h:64 * h + 64], q_r1[16 * h:16 * h + 16], q_r2[16 * h:16 * h + 16], zeros32],
            axis=0).astype(qm_ref.dtype)

    ckvn = _rms_cols(feat(_S_CKV, MLA_KV_LORA), mkvg_ref[...]).astype(BF16)
    kv_all = _dot(wkvu_ref[...], ckvn)
    vm_ref[0] = kv_all[512:1024].astype(vm_ref.dtype)
    kr = feat(_S_KR, MLA_ROPE_DIM)
    k_r1, k_r2 = _rope_rows(kr[0:16], kr[16:32], cm[0:16], sm[0:16])
    for h in range(MLA_HEADS):
        kt = jnp.concatenate([kv_all[64 * h:64 * h + 64], k_r1, k_r2, zeros32], axis=0)
        km_ref[0, :, 128 * h:128 * h + 128] = kt.T.astype(km_ref.dtype)

    c_gqa = GQA_HEAD_DIM ** -0.5 * LOG2E
    ca = ca_ref[...]
    sa = sa_ref[...]
    gq = feat(_S_GQ, GQA_Q_HEADS * GQA_HEAD_DIM)
    for h in range(GQA_Q_HEADS):
        qn = _rms_cols(gq[64 * h:64 * h + 64], gqg_ref[...])
        r1, r2 = _rope_rows(qn[0:32], qn[32:64], ca, sa)
        q64 = jnp.concatenate([r1, r2], axis=0) * c_gqa
        parts = [q64, zeros64] if h // GQA_GROUP == 0 else [zeros64, q64]
        qg_ref[0, h] = jnp.concatenate(parts, axis=0).astype(qg_ref.dtype)
    gk = feat(_S_GK, GQA_KV_HEADS * GQA_HEAD_DIM)
    kparts = []
    for g in range(GQA_KV_HEADS):
        kn = _rms_cols(gk[64 * g:64 * g + 64], gkg_ref[...])
        r1, r2 = _rope_rows(kn[0:32], kn[32:64], ca, sa)
        kparts += [r1, r2]
    kg_ref[0] = jnp.concatenate(kparts, axis=0).T.astype(kg_ref.dtype)
    vg_ref[0] = feat(_S_GV, GQA_KV_HEADS * GQA_HEAD_DIM).astype(vg_ref.dtype)

    c_diff = DIFF_QK_DIM ** -0.5 * LOG2E
    cp = cp_ref[...]
    sp = sp_ref[...]
    row = lax.broadcasted_iota(jnp.int32, (64, 1), 0)
    in_c0 = (row < 4) | ((row >= 8) & (row < 12)) | ((row >= 16) & (row < 40))

    def diff_heads(lo):
        z = feat(lo, DIFF_HEADS * 2 * DIFF_QK_DIM)
        r1, r2 = _rope_rows(z[0:64], z[64:128], cp, sp)
        rest = z[128:512]
        return [jnp.concatenate([r1[8 * h:8 * h + 8], r2[8 * h:8 * h + 8], rest[48 * h:48 * h + 48]], axis=0)
                for h in range(DIFF_HEADS)]

    for h, q64 in enumerate(diff_heads(_S_DQ)):
        q64 = q64 * c_diff
        for c in range(2):
            qc = jnp.where(in_c0 if c == 0 else jnp.logical_not(in_c0), q64, 0.0)
            parts = [qc, zeros64] if h % 2 == 0 else [zeros64, qc]
            qd_ref[0, 2 * h + c] = jnp.concatenate(parts, axis=0).astype(qd_ref.dtype)
    k_heads = diff_heads(_S_DK)
    for p in range(DIFF_HEADS // 2):
        kt = jnp.concatenate([k_heads[2 * p], k_heads[2 * p + 1]], axis=0)
        kd_ref[0, :, 128 * p:128 * p + 128] = kt.T.astype(kd_ref.dtype)
    vd_ref[0] = feat(_S_DV, DIFF_HEADS * DIFF_V_DIM).astype(vd_ref.dtype)


def _mixer_prep(x, g, wt, wg, wqu, wkvu, mqg, mkvg, gqg, gkg, tables):
    B, S, D = x.shape
    tm = min(PREP_TM, S)
    cm, sm, ca, sa, cp, sp = tables
    grid = (B, S // tm)

    def tok(width):
        return pl.BlockSpec((1, tm, width), lambda b, i: (b, i, 0))

    def featm(rows):
        return pl.BlockSpec((1, rows, tm), lambda b, i: (b, 0, i))

    def heads(n):
        return pl.BlockSpec((1, n, QK_PAD, tm), lambda b, i: (b, 0, 0, i))

    def table(rows):
        return pl.BlockSpec((rows, tm), lambda b, i: (0, i))

    in_specs = [
        pl.BlockSpec((1, tm, D), lambda b, i: (b, i, 0)),
        _const_spec((1, D)),
        _const_spec(wt.shape), _const_spec(wg.shape), _const_spec(wqu.shape), _const_spec(wkvu.shape),
        _const_spec(mqg.shape), _const_spec(mkvg.shape), _const_spec(gqg.shape), _const_spec(gkg.shape),
        table(128), table(128), table(32), table(32), table(64), table(64),
    ]
    out_shape = [
        jax.ShapeDtypeStruct((B, MLA_HEADS, QK_PAD, S), BF16),
        jax.ShapeDtypeStruct((B, S, MLA_HEADS * QK_PAD), BF16),
        jax.ShapeDtypeStruct((B, MLA_HEADS * HEAD_V, S), BF16),
        jax.ShapeDtypeStruct((B, GQA_Q_HEADS, QK_PAD, S), BF16),
        jax.ShapeDtypeStruct((B, S, QK_PAD), BF16),
        jax.ShapeDtypeStruct((B, GQA_KV_HEADS * HEAD_V, S), BF16),
        jax.ShapeDtypeStruct((B, 2 * DIFF_HEADS, QK_PAD, S), BF16),
        jax.ShapeDtypeStruct((B, S, DIFF_HEADS // 2 * QK_PAD), BF16),
        jax.ShapeDtypeStruct((B, DIFF_HEADS * HEAD_V, S), BF16),
        jax.ShapeDtypeStruct((B, S, N_GATE), BF16),
    ]
    out_specs = [
        heads(MLA_HEADS), tok(MLA_HEADS * QK_PAD), featm(MLA_HEADS * HEAD_V),
        heads(GQA_Q_HEADS), tok(QK_PAD), featm(GQA_KV_HEADS * HEAD_V),
        heads(2 * DIFF_HEADS), tok(DIFF_HEADS // 2 * QK_PAD), featm(DIFF_HEADS * HEAD_V),
        tok(N_GATE),
    ]
    return pl.pallas_call(
        _mixer_prep_kernel, grid=grid, in_specs=in_specs, out_specs=out_specs, out_shape=out_shape,
        compiler_params=_cparams(("parallel", "parallel")), name="mixer_prep",
    )(x, g, wt, wg, wqu, wkvu, mqg, mkvg, gqg, gkg, cm, sm, ca, sa, cp, sp)


def _online_softmax_attention(q_t, load_q_next, k_ref, v_ref, s_ref, cmax_ref, *, tk):
    S = k_ref.shape[1]
    lanes = q_t.shape[1]
    n_chunks = S // tk
    unroll = min(ATTN_UNROLL, n_chunks)
    assert unroll % 2 == 0 and n_chunks % unroll == 0
    ones = jnp.ones((ONES_ROWS, tk), BF16)

    def scores(c, slot, q=q_t):
        s_t = _dot(k_ref[0, pl.ds(pl.multiple_of(c * tk, tk), tk), :], q)
        s_ref[slot] = s_t
        return jnp.max(s_t, axis=0, keepdims=True)

    def update(c, slot, col_max, m, acc):
        v_aug = jnp.concatenate([v_ref[0, :, pl.ds(pl.multiple_of(c * tk, tk), tk)], ones], axis=0)
        m_new = jnp.maximum(m, col_max)
        p = jnp.exp2(s_ref[slot] - m_new).astype(BF16)
        return m_new, acc * jnp.exp2(m - m_new) + _dot(v_aug, p)

    def group(c, cmax, m, acc, last):
        for u in range(unroll):
            is_final = last and u == unroll - 1
            cmax_next = scores(0, 0, load_q_next()) if is_final else scores(c + u + 1, (u + 1) % 2)
            m, acc = update(c + u, u % 2, cmax, m, acc)
            cmax = cmax_next
        return cmax, m, acc

    def body(i, carry):
        return group(unroll * i, *carry, last=False)

    @pl.when(pl.program_id(2) == 0)
    def _():
        cmax_ref[...] = jnp.broadcast_to(scores(0, 0), cmax_ref.shape)

    cmax0 = jnp.max(cmax_ref[...], axis=0, keepdims=True)
    m0 = jnp.full((1, lanes), -jnp.inf, F32)
    acc0 = jnp.zeros((HEAD_V + ONES_ROWS, lanes), F32)
    carry = lax.fori_loop(0, n_chunks // unroll - 1, body, (cmax0, m0, acc0))
    cmax_next, _, acc = group(n_chunks - unroll, *carry, last=True)
    cmax_ref[...] = jnp.broadcast_to(cmax_next, cmax_ref.shape)
    return acc[:HEAD_V] / acc[HEAD_V:HEAD_V + 1]


def _attn_kernel(q_ref, qn_ref, k_ref, v_ref, o_ref, s_ref, cmax_ref, *, tk):
    o = _online_softmax_attention(q_ref[0, 0], lambda: qn_ref[0, 0], k_ref, v_ref, s_ref, cmax_ref, tk=tk)
    o_ref[0] = o.astype(o_ref.dtype)


def _next_tile(n_tiles):
    return lambda i: jnp.minimum(i + 1, n_tiles - 1)


def _attention(q_t, k, v_t, *, kv_of_head, kcol_of_head, name):
    B, H, _, S = q_t.shape
    tq = min(ATTN_TQ, S)
    tk = min(ATTN_TK, S // 2)
    nxt = _next_tile(S // tq)
    return pl.pallas_call(
        functools.partial(_attn_kernel, tk=tk),
        grid=(B, H, S // tq),
        in_specs=[
            pl.BlockSpec((1, 1, QK_PAD, tq), lambda b, h, i: (b, h, 0, i)),
            pl.BlockSpec((1, 1, QK_PAD, tq), lambda b, h, i: (b, h, 0, nxt(i))),
            pl.BlockSpec((1, S, QK_PAD), lambda b, h, i: (b, 0, kcol_of_head(h))),
            pl.BlockSpec((1, HEAD_V, S), lambda b, h, i: (b, kv_of_head(h), 0)),
        ],
        out_specs=pl.BlockSpec((1, HEAD_V, tq), lambda b, h, i: (b, h, i)),
        out_shape=jax.ShapeDtypeStruct((B, H * HEAD_V, S), F32),
        scratch_shapes=[pltpu.VMEM((2, tk, tq), F32), pltpu.VMEM((8, tq), F32)],
        compiler_params=_cparams(("parallel", "parallel", "arbitrary")), name=name,
    )(q_t, q_t, k, v_t)


def _diff_attn_kernel(lq1_ref, lk1_ref, lq2_ref, lk2_ref, g_ref, q_ref, qn_ref, k_ref, v_ref, o_ref,
                      s_ref, cmax_ref, *, tk, lambda_init):
    tq = q_ref.shape[3]
    q_both = jnp.concatenate([q_ref[0, 0], q_ref[0, 1]], axis=1)
    o_maps = _online_softmax_attention(
        q_both, lambda: jnp.concatenate([qn_ref[0, 0], qn_ref[0, 1]], axis=1), k_ref, v_ref, s_ref, cmax_ref,
        tk=tk)
    lam = (jnp.exp(jnp.sum(lq1_ref[...] * lk1_ref[...], axis=-1, keepdims=True))
           - jnp.exp(jnp.sum(lq2_ref[...] * lk2_ref[...], axis=-1, keepdims=True)) + lambda_init)
    o = o_maps[:, :tq] - lam * o_maps[:, tq:]
    o = _rms_cols(o, g_ref[...]) * (1.0 - lambda_init)
    o_ref[0] = o.astype(o_ref.dtype)


def _diff_attention(q_t, k, v_t, lq1, lk1, lq2, lk2, subln_g, lambda_init):
    B, H2, _, S = q_t.shape
    H = H2 // 2
    tq = min(ATTN_TQ // 2, S)
    tk = min(ATTN_TK, S // 2)
    nxt = _next_tile(S // tq)
    vec = _const_spec((1, DIFF_QK_DIM))
    return pl.pallas_call(
        functools.partial(_diff_attn_kernel, tk=tk, lambda_init=lambda_init),
        grid=(B, H, S // tq),
        in_specs=[
            vec, vec, vec, vec, _const_spec((DIFF_V_DIM, 1)),
            pl.BlockSpec((1, 2, QK_PAD, tq), lambda b, h, i: (b, h, 0, i)),
            pl.BlockSpec((1, 2, QK_PAD, tq), lambda b, h, i: (b, h, 0, nxt(i))),
            pl.BlockSpec((1, S, QK_PAD), lambda b, h, i: (b, 0, h // 2)),
            pl.BlockSpec((1, HEAD_V, S), lambda b, h, i: (b, h, 0)),
        ],
        out_specs=pl.BlockSpec((1, HEAD_V, tq), lambda b, h, i: (b, h, i)),
        out_shape=jax.ShapeDtypeStruct((B, H * HEAD_V, S), F32),
        scratch_shapes=[pltpu.VMEM((2, tk, 2 * tq), F32), pltpu.VMEM((8, 2 * tq), F32)],
        compiler_params=_cparams(("parallel", "parallel", "arbitrary")), name="diff_attn",
    )(lq1, lk1, lq2, lk2, subln_g, q_t, q_t, k, v_t)


def _mixer_merge_kernel(x_ref, a_ref, b_ref, c_ref, gate_ref, wb_ref, wo_ref, g_ref, o_ref):
    merged = None
    for n, br_ref in enumerate((a_ref, b_ref, c_ref)):
        br = br_ref[0].T.astype(BF16)
        proj = _dot(br, wb_ref[n])
        term = gate_ref[0, :, n * D_MODEL:(n + 1) * D_MODEL].astype(F32) * proj
        merged = term if merged is None else merged + term
    y = _dot(merged.astype(BF16), wo_ref[...])
    o_ref[0] = x_ref[0] + _rms_rows(y, g_ref[...])


def _mixer_merge(x, a_t, b_t, c_t, gates, wb, wo, g):
    B, S, D = x.shape
    tm = min(MERGE_TM, S)
    xspec = pl.BlockSpec((1, tm, D), lambda b, i: (b, i, 0))
    brspec = pl.BlockSpec((1, BRANCH_WIDTH, tm), lambda b, i: (b, 0, i))
    return pl.pallas_call(
        _mixer_merge_kernel, grid=(B, S // tm),
        in_specs=[xspec, brspec, brspec, brspec,
                  pl.BlockSpec((1, tm, N_GATE), lambda b, i: (b, i, 0)),
                  _const_spec(wb.shape), _const_spec(wo.shape), _const_spec((1, D))],
        out_specs=xspec, out_shape=jax.ShapeDtypeStruct(x.shape, x.dtype),
        compiler_params=_cparams(("parallel", "parallel")), name="mixer_merge",
    )(x, a_t, b_t, c_t, gates, wb, wo, g)


def _norm_matmul_kernel(x_ref, g_ref, w_ref, o_ref):
    hn = _rms_rows(x_ref[...], g_ref[...]).astype(BF16)
    o_ref[...] = _dot(hn, w_ref[...]).astype(o_ref.dtype)


def _norm_matmul(x2d, g, w, out_dtype):
    T, D = x2d.shape
    N = w.shape[1]
    tm = min(KV_TM, T)
    return pl.pallas_call(
        _norm_matmul_kernel, grid=(T // tm,),
        in_specs=[pl.BlockSpec((tm, D), lambda i: (i, 0)), _const_spec((1, D)), _const_spec(w.shape)],
        out_specs=pl.BlockSpec((tm, N), lambda i: (i, 0)),
        out_shape=jax.ShapeDtypeStruct((T, N), out_dtype),
        compiler_params=_cparams(("parallel",)), name="mem_kv",
    )(x2d, g, w)


def _mem_attn_kernel(x_ref, kv_ref, gpre_ref, wq_ref, wo_ref, gpost_ref, o_ref):
    x = x_ref[0]
    hn = _rms_rows(x, gpre_ref[...]).astype(BF16)
    q = (_dot(hn, wq_ref[...]) * (MEM_HEAD_DIM ** -0.5 * LOG2E)).astype(BF16)
    outs = []
    for h in range(MEM_HEADS):
        lo = h * MEM_HEAD_DIM
        k_h = kv_ref[0, :, lo:lo + MEM_HEAD_DIM]
        v_h = kv_ref[0, :, D_MODEL + lo:D_MODEL + lo + MEM_HEAD_DIM]
        s = _dot_nt(q[:, lo:lo + MEM_HEAD_DIM], k_h)
        e = jnp.exp2(s - jnp.max(s, axis=-1, keepdims=True))
        p = (e / jnp.sum(e, axis=-1, keepdims=True)).astype(BF16)
        outs.append(_dot(p, v_h))
    o = jnp.concatenate(outs, axis=-1).astype(BF16)
    y = _dot(o, wo_ref[...])
    o_ref[0] = x + _rms_rows(y, gpost_ref[...])


def _mem_attention(x, kv, gpre, wq, wo, gpost):
    B, S, D = x.shape
    M = kv.shape[1]
    tm = min(MEM_TM, S)
    xspec = pl.BlockSpec((1, tm, D), lambda b, i: (b, i, 0))
    return pl.pallas_call(
        _mem_attn_kernel, grid=(B, S // tm),
        in_specs=[xspec, pl.BlockSpec((1, M, 2 * D), lambda b, i: (b, 0, 0)),
                  _const_spec((1, D)), _const_spec(wq.shape), _const_spec(wo.shape), _const_spec((1, D))],
        out_specs=xspec, out_shape=jax.ShapeDtypeStruct(x.shape, x.dtype),
        compiler_params=_cparams(("parallel", "parallel")), name="mem_attn",
    )(x, kv, gpre, wq, wo, gpost)


def _swiglu_chunk(hn, wg, wu, wd):
    gate = _dot(hn, wg)
    act = (gate * jax.nn.sigmoid(gate) * _dot(hn, wu)).astype(BF16)
    return _dot(act, wd)


def _swiglu(hn, wg_ref, wu_ref, wd_ref, tf):
    y = None
    for lo in range(0, wg_ref.shape[-1], tf):
        part = _swiglu_chunk(hn, wg_ref[:, lo:lo + tf], wu_ref[:, lo:lo + tf], wd_ref[lo:lo + tf, :])
        y = part if y is None else y + part
    return y


def _ffn_kernel(x_ref, gpre_ref, wg_ref, wu_ref, wd_ref, gpost_ref, o_ref):
    x = x_ref[...]
    hn = _rms_rows(x, gpre_ref[...]).astype(BF16)
    o_ref[...] = x + _rms_rows(_swiglu(hn, wg_ref, wu_ref, wd_ref, FFN_TF), gpost_ref[...])


def _ffn(x2d, gpre, wg, wu, wd, gpost):
    T, D = x2d.shape
    tm = min(FFN_TM, T)
    xspec = pl.BlockSpec((tm, D), lambda i: (i, 0))
    return pl.pallas_call(
        _ffn_kernel, grid=(T // tm,),
        in_specs=[xspec, _const_spec((1, D)), _const_spec(wg.shape), _const_spec(wu.shape),
                  _const_spec(wd.shape), _const_spec((1, D))],
        out_specs=xspec, out_shape=jax.ShapeDtypeStruct(x2d.shape, x2d.dtype),
        compiler_params=_cparams(("parallel",)), name="ffn",
    )(x2d, gpre, wg, wu, wd, gpost)


def _router_kernel(x_ref, gpre_ref, wr_ref, br_ref, tri_ref,
                   hn_ref, pos_ref, gate_col_ref, pos_col_ref, cnt_ref, run_ref):
    @pl.when(pl.program_id(0) == 0)
    def _():
        run_ref[...] = jnp.zeros_like(run_ref)

    hn = _rms_rows(x_ref[...], gpre_ref[...])
    hn_ref[...] = hn.astype(hn_ref.dtype)
    logits = lax.dot_general(wr_ref[...], hn, (((1,), (1,)), ((), ())),
                             precision=lax.Precision.HIGHEST,
                             preferred_element_type=F32) + br_ref[...]
    e_idx = lax.broadcasted_iota(jnp.int32, logits.shape, 0)
    m1 = jnp.max(logits, axis=0, keepdims=True)
    i1 = jnp.min(jnp.where(logits == m1, e_idx, N_EXPERTS), axis=0, keepdims=True)
    rest = jnp.where(e_idx == i1, -jnp.inf, logits)
    m2 = jnp.max(rest, axis=0, keepdims=True)
    i2 = jnp.min(jnp.where(rest == m2, e_idx, N_EXPERTS), axis=0, keepdims=True)
    e2 = jnp.exp(m2 - m1)
    w1 = 1.0 / (1.0 + e2)
    w2 = e2 / (1.0 + e2)
    gates = jnp.where(e_idx == i1, w1, 0.0) + jnp.where(e_idx == i2, w2, 0.0)

    sel = (e_idx == i1) | (e_idx == i2)
    sel_f = jnp.where(sel, 1.0, 0.0)
    before = _dot(sel_f.astype(BF16), tri_ref[...])
    run = run_ref[:, 0:1]
    pos = jnp.where(sel, run + before, -1.0)
    pos_ref[...] = pos
    tm = pos.shape[1]
    cols = jnp.concatenate([gates, pos, jnp.zeros((128 - 2 * N_EXPERTS, tm), F32)], axis=0).T
    for e in range(N_EXPERTS):
        gate_col_ref[e] = cols[:, e:e + 1]
        pos_col_ref[e] = cols[:, N_EXPERTS + e:N_EXPERTS + e + 1]
    cnt = jnp.sum(sel_f, axis=1, keepdims=True)
    cnt_ref[0] = cnt
    run_ref[...] = run_ref[...] + cnt


def _router(x2d, gpre, wr_t, br_col):
    T, D = x2d.shape
    tm = MOE_BLOCK
    nb = T // tm
    tri = (np.arange(tm)[:, None] < np.arange(tm)[None, :]).astype(np.float32)
    col = pl.BlockSpec((N_EXPERTS, tm, 1), lambda i: (0, i, 0))
    return pl.pallas_call(
        _router_kernel, grid=(nb,),
        in_specs=[pl.BlockSpec((tm, D), lambda i: (i, 0)), _const_spec((1, D)),
                  _const_spec(wr_t.shape), _const_spec(br_col.shape), _const_spec((tm, tm))],
        out_specs=[pl.BlockSpec((tm, D), lambda i: (i, 0)),
                   pl.BlockSpec((N_EXPERTS, tm), lambda i: (0, i)), col, col,
                   pl.BlockSpec((1, N_EXPERTS, 1), lambda i: (i, 0, 0))],
        out_shape=[jax.ShapeDtypeStruct((T, D), BF16),
                   jax.ShapeDtypeStruct((N_EXPERTS, T), F32),
                   jax.ShapeDtypeStruct((N_EXPERTS, T, 1), F32),
                   jax.ShapeDtypeStruct((N_EXPERTS, T, 1), F32),
                   jax.ShapeDtypeStruct((nb, N_EXPERTS, 1), F32)],
        scratch_shapes=[pltpu.VMEM((N_EXPERTS, 128), F32)],
        compiler_params=_cparams(("arbitrary",)), name="router",
    )(x2d, gpre, wr_t, br_col, jnp.asarray(tri, BF16))


_FIRST, _LAST = 1, 2
_HALF = (4, 8)


def _moe_schedule(cnt_be, g_max):
    nb, ne = cnt_be.shape
    tr = MOE_BLOCK
    cnt_eb = cnt_be.T
    start_eb = jnp.cumsum(cnt_eb, axis=1) - cnt_eb
    n_tile_e = (jnp.sum(cnt_eb, axis=1) + tr - 1) // tr
    tile0_e = jnp.cumsum(n_tile_e) - n_tile_e
    first_eb = start_eb // tr
    n_eb = jnp.where(cnt_eb > 0, (start_eb + cnt_eb - 1) // tr - first_eb + 1, 0)
    g = jnp.arange(g_max, dtype=jnp.int32)

    def visit_list(pair_e, pair_b, group_of):
        n = n_eb[pair_e, pair_b]
        ends = jnp.cumsum(n)
        total = ends[-1]
        gi = jnp.minimum(g, total - 1)
        pair = jnp.sum(ends[None, :] <= gi[:, None], axis=1).astype(jnp.int32)
        e = pair_e[pair]
        b = pair_b[pair]
        local = first_eb[e, b] + gi - (ends[pair] - n[pair])
        tile = tile0_e[e] + local
        grp = group_of(tile, b)
        valid = g < total
        first = valid & ((g == 0) | (grp != jnp.roll(grp, 1)))
        last = valid & ((g == total - 1) | (grp != jnp.roll(grp, -1)))
        lo = start_eb[e, b] - local * tr
        hi = lo + cnt_eb[e, b]
        flags = (first * _FIRST + last * _LAST
                 + (valid & (lo < tr // 2)) * _HALF[0] + (valid & (hi > tr // 2)) * _HALF[1])
        return [v.astype(jnp.int32) for v in (e, b, tile, local * tr, flags)]

    ee, bb = np.meshgrid(np.arange(ne), np.arange(nb), indexing="ij")
    list_a = visit_list(jnp.asarray(ee.reshape(-1)), jnp.asarray(bb.reshape(-1)), lambda tile, b: tile)
    list_b = visit_list(jnp.asarray(ee.T.reshape(-1)), jnp.asarray(bb.T.reshape(-1)), lambda tile, b: b)
    return list_a, list_b


def _moe_expert_kernel(e_ref, b_ref, t_ref, r_ref, f_ref, hn_ref, pos_ref, wg_ref, wu_ref, wd_ref,
                       ys_ref, xs_ref):
    g = pl.program_id(0)
    flags = f_ref[g]
    tr = xs_ref.shape[0]

    @pl.when((flags & _FIRST) != 0)
    def _():
        xs_ref[...] = jnp.zeros_like(xs_ref)

    half = tr // 2
    for h in range(2):
        @pl.when((flags & _HALF[h]) != 0)
        def _():
            rows = (lax.broadcasted_iota(jnp.int32, (half, 1), 0) + (r_ref[g] + h * half)).astype(F32)
            onehot = jnp.where(rows == pos_ref[0, 0], 1.0, 0.0).astype(BF16)
            xs_ref[h * half:(h + 1) * half, :] += _dot(onehot, hn_ref[...])

    @pl.when((flags & _LAST) != 0)
    def _():
        y = _swiglu(xs_ref[...].astype(BF16), wg_ref.at[0], wu_ref.at[0], wd_ref.at[0], FFN_TF)
        ys_ref[...] = y.astype(ys_ref.dtype)


def _moe_expert(list_a, hn, pos, wg, wu, wd, n_tiles):
    T, D = hn.shape
    E, _, F = wg.shape
    tr = MOE_BLOCK
    g_max = list_a[0].shape[0]
    pos4 = pos.reshape(E, T // tr, 1, tr)
    wspec_in = pl.BlockSpec((1, D, F), lambda g, e, b, t, r, f: (e[g], 0, 0), pipeline_mode=pl.Buffered(1))
    wspec_out = pl.BlockSpec((1, F, D), lambda g, e, b, t, r, f: (e[g], 0, 0), pipeline_mode=pl.Buffered(1))
    grid_spec = pltpu.PrefetchScalarGridSpec(
        num_scalar_prefetch=5, grid=(g_max,),
        in_specs=[pl.BlockSpec((tr, D), lambda g, e, b, t, r, f: (b[g], 0)),
                  pl.BlockSpec((1, 1, 1, tr), lambda g, e, b, t, r, f: (e[g], b[g], 0, 0)),
                  wspec_in, wspec_in, wspec_out],
        out_specs=pl.BlockSpec((tr, D), lambda g, e, b, t, r, f: (t[g], 0)),
        scratch_shapes=[pltpu.VMEM((tr, D), F32)])
    return pl.pallas_call(
        _moe_expert_kernel, grid_spec=grid_spec,
        out_shape=jax.ShapeDtypeStruct((n_tiles * tr, D), BF16),
        compiler_params=_cparams(("arbitrary",)), name="moe_expert",
    )(*list_a, hn, pos4, wg, wu, wd)


def _moe_combine_kernel(e_ref, b_ref, t_ref, r_ref, f_ref, x_ref, ys_ref, pos_ref, gate_ref, gpost_ref,
                        o_ref, acc_ref):
    g = pl.program_id(0)
    flags = f_ref[g]
    tr = ys_ref.shape[0]

    @pl.when((flags & _FIRST) != 0)
    def _():
        acc_ref[...] = jnp.zeros_like(acc_ref)

    half = tr // 2
    for h in range(2):
        @pl.when((flags & _HALF[h]) != 0)
        def _():
            rows = (lax.broadcasted_iota(jnp.int32, (1, half), 1) + (r_ref[g] + h * half)).astype(F32)
            onehot = jnp.where(pos_ref[0] == rows, 1.0, 0.0).astype(BF16)
            acc_ref[...] += gate_ref[0] * _dot(onehot, ys_ref[h * half:(h + 1) * half, :])

    @pl.when((flags & _LAST) != 0)
    def _():
        o_ref[...] = x_ref[...] + _rms_rows(acc_ref[...], gpost_ref[...])


def _moe_combine(list_b, x2d, ys, pos_col, gate_col, gpost):
    T, D = x2d.shape
    tr = MOE_BLOCK
    g_max = list_b[0].shape[0]
    xspec = pl.BlockSpec((tr, D), lambda g, e, b, t, r, f: (b[g], 0))
    colspec = pl.BlockSpec((1, tr, 1), lambda g, e, b, t, r, f: (e[g], b[g], 0))
    grid_spec = pltpu.PrefetchScalarGridSpec(
        num_scalar_prefetch=5, grid=(g_max,),
        in_specs=[xspec, pl.BlockSpec((tr, D), lambda g, e, b, t, r, f: (t[g], 0)), colspec, colspec,
                  pl.BlockSpec((1, D), lambda g, e, b, t, r, f: (0, 0))],
        out_specs=xspec,
        scratch_shapes=[pltpu.VMEM((tr, D), F32)])
    return pl.pallas_call(
        _moe_combine_kernel, grid_spec=grid_spec,
        out_shape=jax.ShapeDtypeStruct(x2d.shape, x2d.dtype),
        compiler_params=_cparams(("arbitrary",)), name="moe_combine",
    )(*list_b, x2d, ys, pos_col, gate_col, gpost)


def _moe(x2d, gpre, wr_t, br_col, wg, wu, wd, gpost):
    T, D = x2d.shape
    E = wg.shape[0]
    nb = T // MOE_BLOCK
    n_tiles = TOP_K * nb + E
    g_max = E * nb + n_tiles
    hn, pos, gate_col, pos_col, cnt = _router(x2d, gpre, wr_t, br_col)
    list_a, list_b = _moe_schedule(cnt.reshape(nb, E).astype(jnp.int32), g_max)
    ys = _moe_expert(list_a, hn, pos, wg, wu, wd, n_tiles)
    return _moe_combine(list_b, x2d, ys, pos_col, gate_col, gpost)


def _feature_row_order():
    def diff_block(base):
        x1, x2, rest = [], [], []
        half = DIFF_ROPE_DIM // 2
        for h in range(DIFF_HEADS):
            for c in range(2):
                lo = base + h * 2 * DIFF_QK_DIM + c * DIFF_QK_DIM
                x1 += range(lo, lo + half)
                x2 += range(lo + half, lo + DIFF_ROPE_DIM)
                rest += range(lo + DIFF_ROPE_DIM, lo + DIFF_QK_DIM)
        return x1 + x2 + rest

    order = list(range(0, _S_DQ)) + diff_block(_S_DQ) + diff_block(_S_DK) + list(range(_S_DV, N_FEAT))
    return np.asarray(order, np.int32)


def _mla_q_up_order():
    w = MLA_NOPE_DIM + MLA_ROPE_DIM
    half = MLA_ROPE_DIM // 2
    nope = [h * w + j for h in range(MLA_HEADS) for j in range(MLA_NOPE_DIM)]
    x1 = [h * w + MLA_NOPE_DIM + j for h in range(MLA_HEADS) for j in range(half)]
    x2 = [h * w + MLA_NOPE_DIM + half + j for h in range(MLA_HEADS) for j in range(half)]
    return np.asarray(nope + x1 + x2, np.int32)


def _mla_kv_up_order():
    w = MLA_NOPE_DIM + MLA_V_DIM
    k = [h * w + j for h in range(MLA_HEADS) for j in range(MLA_NOPE_DIM)]
    v = [h * w + MLA_NOPE_DIM + j for h in range(MLA_HEADS) for j in range(MLA_V_DIM)]
    return np.asarray(k + v, np.int32)


def _rope_tables(S):
    def cos_sin(pos, rot_dim):
        inv = ROPE_THETA ** (-jnp.arange(0, rot_dim, 2, dtype=F32) / rot_dim)
        ang = pos.astype(F32)[:, None] * inv[None, :]
        return jnp.cos(ang).T, jnp.sin(ang).T

    rows = S // GRID_W
    pos = jnp.arange(S)
    row = jnp.repeat(jnp.arange(rows), GRID_W)
    col = jnp.tile(jnp.arange(GRID_W), rows)
    cm, sm = cos_sin(pos, MLA_ROPE_DIM)
    cr, sr = cos_sin(row, GQA_HEAD_DIM // 2)
    cc, sc = cos_sin(col, GQA_HEAD_DIM // 2)
    cp, sp = cos_sin(pos, DIFF_ROPE_DIM)
    return (jnp.tile(cm, (MLA_HEADS, 1)), jnp.tile(sm, (MLA_HEADS, 1)),
            jnp.concatenate([cr, cc], axis=0), jnp.concatenate([sr, sc], axis=0),
            jnp.tile(cp, (2 * DIFF_HEADS, 1)), jnp.tile(sp, (2 * DIFF_HEADS, 1)))


def _row(v):
    return v.reshape(1, -1).astype(F32)


def _col(v):
    return v.reshape(-1, 1).astype(F32)


def kernel(x, mem, mix_pre_g, mix_post_g, w_in, mla_q_norm_g, mla_w_q_up, mla_kv_norm_g, mla_w_kv_up,
           gqa_q_norm_g, gqa_k_norm_g, diff_lambda_q1, diff_lambda_k1, diff_lambda_q2, diff_lambda_k2,
           diff_subln_g, w_branch, w_out, mem_pre_g, mem_post_g, mem_norm_g, mem_wq, mem_wkv, mem_wo,
           ffn_pre_g, ffn_post_g, dense_w_gate, dense_w_up, dense_w_down, moe_w_router, moe_b_router,
           moe_w_gate, moe_w_up, moe_w_down):
    B, S, D = x.shape
    M = mem.shape[1]
    depth = w_in.shape[0]
    tables = _rope_tables(S)
    feat_order = _feature_row_order()
    q_up_order = _mla_q_up_order()
    kv_up_order = _mla_kv_up_order()

    for layer in range(depth):
        lambda_init = 0.8 - 0.6 * math.exp(-0.3 * layer)

        w_l = w_in[layer]
        wt = w_l[:, feat_order].T.astype(BF16)
        wg = w_l[:, N_FEAT:].astype(BF16)
        wqu = mla_w_q_up[layer][:, q_up_order].T.astype(BF16)
        wkvu = mla_w_kv_up[layer][:, kv_up_order].T.astype(BF16)
        (qm, km, vm, qg, kg, vg, qd, kd, vd, gates) = _mixer_prep(
            x, _row(mix_pre_g[layer]), wt, wg, wqu, wkvu,
            _col(mla_q_norm_g[layer]), _col(mla_kv_norm_g[layer]),
            _col(gqa_q_norm_g[layer]), _col(gqa_k_norm_g[layer]), tables)
        a_t = _attention(qm, km, vm, kv_of_head=lambda h: h, kcol_of_head=lambda h: h, name="mla_attn")
        b_t = _attention(qg, kg, vg, kv_of_head=lambda h: h // GQA_GROUP, kcol_of_head=lambda h: 0,
                         name="gqa_attn")
        c_t = _diff_attention(qd, kd, vd, _row(diff_lambda_q1[layer]), _row(diff_lambda_k1[layer]),
                              _row(diff_lambda_q2[layer]), _row(diff_lambda_k2[layer]),
                              _col(diff_subln_g[layer]), lambda_init)
        x = _mixer_merge(x, a_t, b_t, c_t, gates, w_branch[layer].astype(BF16),
                         w_out[layer].astype(BF16), _row(mix_post_g[layer]))

        kv = _norm_matmul(mem.reshape(B * M, D), _row(mem_norm_g[layer]), mem_wkv[layer].astype(BF16), BF16)
        x = _mem_attention(x, kv.reshape(B, M, 2 * D), _row(mem_pre_g[layer]), mem_wq[layer].astype(BF16),
                           mem_wo[layer].astype(BF16), _row(mem_post_g[layer]))

        x2d = x.reshape(B * S, D)
        i = layer // 2
        if layer % 2 == 0:
            x2d = _ffn(x2d, _row(ffn_pre_g[layer]), dense_w_gate[i].astype(BF16), dense_w_up[i].astype(BF16),
                       dense_w_down[i].astype(BF16), _row(ffn_post_g[layer]))
        else:
            x2d = _moe(x2d, _row(ffn_pre_g[layer]), moe_w_router[i].T.astype(F32), _col(moe_b_router[i]),
                       moe_w_gate[i].astype(BF16), moe_w_up[i].astype(BF16), moe_w_down[i].astype(BF16),
                       _row(ffn_post_g[layer]))
        x = x2d.reshape(B, S, D)
    return x
```

```python
import functools
import math

import numpy as np
import jax
import jax.numpy as jnp
from jax import lax
from jax.experimental import pallas as pl
from jax.experimental.pallas import tpu as pltpu

F32 = jnp.float32
BF16 = jnp.bfloat16

D_MODEL = 1024
GRID_W = 64
ROPE_THETA = 500000.0
NORM_EPS = 1e-6

MLA_HEADS = 8
MLA_Q_LORA = 384
MLA_KV_LORA = 256
MLA_NOPE_DIM = 64
MLA_ROPE_DIM = 32
MLA_V_DIM = 64

GQA_Q_HEADS = 8
GQA_KV_HEADS = 2
GQA_GROUP = GQA_Q_HEADS // GQA_KV_HEADS
GQA_HEAD_DIM = 64

DIFF_HEADS = 8
DIFF_QK_DIM = 32
DIFF_V_DIM = 2 * DIFF_QK_DIM
DIFF_ROPE_DIM = DIFF_QK_DIM // 4

N_BRANCHES = 3
BRANCH_WIDTH = 512
HEAD_V = 64

MEM_HEADS = 4
MEM_HEAD_DIM = D_MODEL // MEM_HEADS

D_FF = 2816
N_EXPERTS = 8
TOP_K = 2

IN_SPLITS = (MLA_Q_LORA, MLA_KV_LORA, MLA_ROPE_DIM,
             GQA_Q_HEADS * GQA_HEAD_DIM, GQA_KV_HEADS * GQA_HEAD_DIM, GQA_KV_HEADS * GQA_HEAD_DIM,
             DIFF_HEADS * 2 * DIFF_QK_DIM, DIFF_HEADS * 2 * DIFF_QK_DIM, DIFF_HEADS * DIFF_V_DIM,
             N_BRANCHES * D_MODEL)
IN_OFFS = tuple(int(v) for v in np.cumsum((0,) + IN_SPLITS))
N_FEAT = IN_OFFS[9]
N_GATE = IN_SPLITS[9]

LOG2E = 1.4426950408889634
QK_PAD = 128
ONES_ROWS = 16

V7X_VMEM_LIMIT_BYTES = 56 * 1024 * 1024

PREP_TM = 512
ATTN_TQ = 1024
ATTN_TK = 256
ATTN_UNROLL = 8
MERGE_TM = 512
MEM_TM = 1024
FFN_TM = 512
FFN_TF = 2816
MOE_BLOCK = 512
KV_TM = 256


def _cparams(sem):
    return pltpu.CompilerParams(dimension_semantics=sem, vmem_limit_bytes=V7X_VMEM_LIMIT_BYTES)


def _const_spec(shape):
    nd = len(shape)
    return pl.BlockSpec(shape, lambda *_: (0,) * nd, pipeline_mode=pl.Buffered(1))


def _rms_rows(x, g_row):
    ms = jnp.mean(x * x, axis=-1, keepdims=True)
    return x * lax.rsqrt(ms + NORM_EPS) * g_row


def _rms_cols(x, g_col):
    ms = jnp.mean(x * x, axis=0, keepdims=True)
    return x * lax.rsqrt(ms + NORM_EPS) * g_col


def _dot(a, b):
    return jnp.dot(a, b, preferred_element_type=F32)


def _dot_nt(a, b):
    return lax.dot_general(a, b, (((1,), (1,)), ((), ())), preferred_element_type=F32)


_S_CQ, _S_CKV, _S_KR, _S_GQ, _S_GK, _S_GV, _S_DQ, _S_DK, _S_DV = IN_OFFS[:9]


def _rope_rows(x1, x2, c, s):
    return x1 * c - x2 * s, x2 * c + x1 * s


def _mixer_prep_kernel(x_ref, g_ref, wt_ref, wg_ref, wqu_ref, wkvu_ref,
                       mqg_ref, mkvg_ref, gqg_ref, gkg_ref,
                       cm_ref, sm_ref, ca_ref, sa_ref, cp_ref, sp_ref,
                       qm_ref, km_ref, vm_ref, qg_ref, kg_ref, vg_ref,
                       qd_ref, kd_ref, vd_ref, gate_ref):
    tm = x_ref.shape[1]
    hn = _rms_rows(x_ref[0], g_ref[...]).astype(BF16)

    def feat(lo, n):
        return _dot_nt(wt_ref[lo:lo + n, :], hn)

    gate_ref[0] = jax.nn.sigmoid(_dot(hn, wg_ref[...])).astype(gate_ref.dtype)

    zeros32 = jnp.zeros((32, tm), F32)
    zeros64 = jnp.zeros((64, tm), F32)

    c_mla = (MLA_NOPE_DIM + MLA_ROPE_DIM) ** -0.5 * LOG2E
    cm = cm_ref[...]
    sm = sm_ref[...]
    cqn = _rms_cols(feat(_S_CQ, MLA_Q_LORA), mqg_ref[...]).astype(BF16)
    q_all = _dot(wqu_ref[...], cqn)
    q_nope = q_all[0:512] * c_mla
    q_r1, q_r2 = _rope_rows(q_all[512:640], q_all[640:768], cm, sm)
    q_r1 = q_r1 * c_mla
    q_r2 = q_r2 * c_mla
    for h in range(MLA_HEADS):
        qm_ref[0, h] = jnp.concatenate(
            [q_nope[64 * h:64 * h + 64], q_r1[16 * h:16 * h + 16], q_r2[16 * h:16 * h + 16], zeros32],
            axis=0).astype(qm_ref.dtype)

    ckvn = _rms_cols(feat(_S_CKV, MLA_KV_LORA), mkvg_ref[...]).astype(BF16)
    kv_all = _dot(wkvu_ref[...], ckvn)
    vm_ref[0] = kv_all[512:1024].astype(vm_ref.dtype)
    kr = feat(_S_KR, MLA_ROPE_DIM)
    k_r1, k_r2 = _rope_rows(kr[0:16], kr[16:32], cm[0:16], sm[0:16])
    for h in range(MLA_HEADS):
        kt = jnp.concatenate([kv_all[64 * h:64 * h + 64], k_r1, k_r2, zeros32], axis=0)
        km_ref[0, :, 128 * h:128 * h + 128] = kt.T.astype(km_ref.dtype)

    c_gqa = GQA_HEAD_DIM ** -0.5 * LOG2E
    ca = ca_ref[...]
    sa = sa_ref[...]
    gq = feat(_S_GQ, GQA_Q_HEADS * GQA_HEAD_DIM)
    for h in range(GQA_Q_HEADS):
        qn = _rms_cols(gq[64 * h:64 * h + 64], gqg_ref[...])
        r1, r2 = _rope_rows(qn[0:32], qn[32:64], ca, sa)
        q64 = jnp.concatenate([r1, r2], axis=0) * c_gqa
        parts = [q64, zeros64] if h // GQA_GROUP == 0 else [zeros64, q64]
        qg_ref[0, h] = jnp.concatenate(parts, axis=0).astype(qg_ref.dtype)
    gk = feat(_S_GK, GQA_KV_HEADS * GQA_HEAD_DIM)
    kparts = []
    for g in range(GQA_KV_HEADS):
        kn = _rms_cols(gk[64 * g:64 * g + 64], gkg_ref[...])
        r1, r2 = _rope_rows(kn[0:32], kn[32:64], ca, sa)
        kparts += [r1, r2]
    kg_ref[0] = jnp.concatenate(kparts, axis=0).T.astype(kg_ref.dtype)
    vg_ref[0] = feat(_S_GV, GQA_KV_HEADS * GQA_HEAD_DIM).astype(vg_ref.dtype)

    c_diff = DIFF_QK_DIM ** -0.5 * LOG2E
    cp = cp_ref[...]
    sp = sp_ref[...]
    row = lax.broadcasted_iota(jnp.int32, (64, 1), 0)
    in_c0 = (row < 4) | ((row >= 8) & (row < 12)) | ((row >= 16) & (row < 40))

    def diff_heads(lo):
        z = feat(lo, DIFF_HEADS * 2 * DIFF_QK_DIM)
        r1, r2 = _rope_rows(z[0:64], z[64:128], cp, sp)
        rest = z[128:512]
        return [jnp.concatenate([r1[8 * h:8 * h + 8], r2[8 * h:8 * h + 8], rest[48 * h:48 * h + 48]], axis=0)
                for h in range(DIFF_HEADS)]

    for h, q64 in enumerate(diff_heads(_S_DQ)):
        q64 = q64 * c_diff
        for c in range(2):
            qc = jnp.where(in_c0 if c == 0 else jnp.logical_not(in_c0), q64, 0.0)
            parts = [qc, zeros64] if h % 2 == 0 else [zeros64, qc]
            qd_ref[0, 2 * h + c] = jnp.concatenate(parts, axis=0).astype(qd_ref.dtype)
    k_heads = diff_heads(_S_DK)
    for p in range(DIFF_HEADS // 2):
        kt = jnp.concatenate([k_heads[2 * p], k_heads[2 * p + 1]], axis=0)
        kd_ref[0, :, 128 * p:128 * p + 128] = kt.T.astype(kd_ref.dtype)
    vd_ref[0] = feat(_S_DV, DIFF_HEADS * DIFF_V_DIM).astype(vd_ref.dtype)


def _mixer_prep(x, g, wt, wg, wqu, wkvu, mqg, mkvg, gqg, gkg, tables):
    B, S, D = x.shape
    tm = min(PREP_TM, S)
    cm, sm, ca, sa, cp, sp = tables
    grid = (B, S // tm)

    def tok(width):
        return pl.BlockSpec((1, tm, width), lambda b, i: (b, i, 0))

    def featm(rows):
        return pl.BlockSpec((1, rows, tm), lambda b, i: (b, 0, i))

    def heads(n):
        return pl.BlockSpec((1, n, QK_PAD, tm), lambda b, i: (b, 0, 0, i))

    def table(rows):
        return pl.BlockSpec((rows, tm), lambda b, i: (0, i))

    in_specs = [
        pl.BlockSpec((1, tm, D), lambda b, i: (b, i, 0)),
        _const_spec((1, D)),
        _const_spec(wt.shape), _const_spec(wg.shape), _const_spec(wqu.shape), _const_spec(wkvu.shape),
        _const_spec(mqg.shape), _const_spec(mkvg.shape), _const_spec(gqg.shape), _const_spec(gkg.shape),
        table(128), table(128), table(32), table(32), table(64), table(64),
    ]
    out_shape = [
        jax.ShapeDtypeStruct((B, MLA_HEADS, QK_PAD, S), BF16),
        jax.ShapeDtypeStruct((B, S, MLA_HEADS * QK_PAD), BF16),
        jax.ShapeDtypeStruct((B, MLA_HEADS * HEAD_V, S), BF16),
        jax.ShapeDtypeStruct((B, GQA_Q_HEADS, QK_PAD, S), BF16),
        jax.ShapeDtypeStruct((B, S, QK_PAD), BF16),
        jax.ShapeDtypeStruct((B, GQA_KV_HEADS * HEAD_V, S), BF16),
        jax.ShapeDtypeStruct((B, 2 * DIFF_HEADS, QK_PAD, S), BF16),
        jax.ShapeDtypeStruct((B, S, DIFF_HEADS // 2 * QK_PAD), BF16),
        jax.ShapeDtypeStruct((B, DIFF_HEADS * HEAD_V, S), BF16),
        jax.ShapeDtypeStruct((B, S, N_GATE), BF16),
    ]
    out_specs = [
        heads(MLA_HEADS), tok(MLA_HEADS * QK_PAD), featm(MLA_HEADS * HEAD_V),
        heads(GQA_Q_HEADS), tok(QK_PAD), featm(GQA_KV_HEADS * HEAD_V),
        heads(2 * DIFF_HEADS), tok(DIFF_HEADS // 2 * QK_PAD), featm(DIFF_HEADS * HEAD_V),
        tok(N_GATE),
    ]
    return pl.pallas_call(
        _mixer_prep_kernel, grid=grid, in_specs=in_specs, out_specs=out_specs, out_shape=out_shape,
        compiler_params=_cparams(("parallel", "parallel")), name="mixer_prep",
    )(x, g, wt, wg, wqu, wkvu, mqg, mkvg, gqg, gkg, cm, sm, ca, sa, cp, sp)


def _online_softmax_attention(q_t, load_q_next, k_ref, v_ref, s_ref, cmax_ref, *, tk):
    S = k_ref.shape[1]
    lanes = q_t.shape[1]
    n_chunks = S // tk
    unroll = min(ATTN_UNROLL, n_chunks)
    assert unroll % 2 == 0 and n_chunks % unroll == 0
    ones = jnp.ones((ONES_ROWS, tk), BF16)

    def scores(c, slot, q=q_t):
        s_t = _dot(k_ref[0, pl.ds(pl.multiple_of(c * tk, tk), tk), :], q)
        s_ref[slot] = s_t
        return jnp.max(s_t, axis=0, keepdims=True)

    def update(c, slot, col_max, m, acc):
        v_aug = jnp.concatenate([v_ref[0, :, pl.ds(pl.multiple_of(c * tk, tk), tk)], ones], axis=0)
        m_new = jnp.maximum(m, col_max)
        p = jnp.exp2(s_ref[slot] - m_new).astype(BF16)
        return m_new, acc * jnp.exp2(m - m_new) + _dot(v_aug, p)

    def group(c, cmax, m, acc, last):
        for u in range(unroll):
            is_final = last and u == unroll - 1
            cmax_next = scores(0, 0, load_q_next()) if is_final else scores(c + u + 1, (u + 1) % 2)
            m, acc = update(c + u, u % 2, cmax, m, acc)
            cmax = cmax_next
        return cmax, m, acc

    def body(i, carry):
        return group(unroll * i, *carry, last=False)

    @pl.when(pl.program_id(2) == 0)
    def _():
        cmax_ref[...] = jnp.broadcast_to(scores(0, 0), cmax_ref.shape)

    cmax0 = jnp.max(cmax_ref[...], axis=0, keepdims=True)
    m0 = jnp.full((1, lanes), -jnp.inf, F32)
    acc0 = jnp.zeros((HEAD_V + ONES_ROWS, lanes), F32)
    carry = lax.fori_loop(0, n_chunks // unroll - 1, body, (cmax0, m0, acc0))
    cmax_next, _, acc = group(n_chunks - unroll, *carry, last=True)
    cmax_ref[...] = jnp.broadcast_to(cmax_next, cmax_ref.shape)
    return acc[:HEAD_V] / acc[HEAD_V:HEAD_V + 1]


def _attn_kernel(q_ref, qn_ref, k_ref, v_ref, o_ref, s_ref, cmax_ref, *, tk):
    o = _online_softmax_attention(q_ref[0, 0], lambda: qn_ref[0, 0], k_ref, v_ref, s_ref, cmax_ref, tk=tk)
    o_ref[0] = o.astype(o_ref.dtype)


def _next_tile(n_tiles):
    return lambda i: jnp.minimum(i + 1, n_tiles - 1)


def _attention(q_t, k, v_t, *, kv_of_head, kcol_of_head, name):
    B, H, _, S = q_t.shape
    tq = min(ATTN_TQ, S)
    tk = min(ATTN_TK, S // 2)
    nxt = _next_tile(S // tq)
    return pl.pallas_call(
        functools.partial(_attn_kernel, tk=tk),
        grid=(B, H, S // tq),
        in_specs=[
            pl.BlockSpec((1, 1, QK_PAD, tq), lambda b, h, i: (b, h, 0, i)),
            pl.BlockSpec((1, 1, QK_PAD, tq), lambda b, h, i: (b, h, 0, nxt(i))),
            pl.BlockSpec((1, S, QK_PAD), lambda b, h, i: (b, 0, kcol_of_head(h))),
            pl.BlockSpec((1, HEAD_V, S), lambda b, h, i: (b, kv_of_head(h), 0)),
        ],
        out_specs=pl.BlockSpec((1, HEAD_V, tq), lambda b, h, i: (b, h, i)),
        out_shape=jax.ShapeDtypeStruct((B, H * HEAD_V, S), F32),
        scratch_shapes=[pltpu.VMEM((2, tk, tq), F32), pltpu.VMEM((8, tq), F32)],
        compiler_params=_cparams(("parallel", "parallel", "arbitrary")), name=name,
    )(q_t, q_t, k, v_t)


def _diff_attn_kernel(lq1_ref, lk1_ref, lq2_ref, lk2_ref, g_ref, q_ref, qn_ref, k_ref, v_ref, o_ref,
                      s_ref, cmax_ref, *, tk, lambda_init):
    tq = q_ref.shape[3]
    q_both = jnp.concatenate([q_ref[0, 0], q_ref[0, 1]], axis=1)
    o_maps = _online_softmax_attention(
        q_both, lambda: jnp.concatenate([qn_ref[0, 0], qn_ref[0, 1]], axis=1), k_ref, v_ref, s_ref, cmax_ref,
        tk=tk)
    lam = (jnp.exp(jnp.sum(lq1_ref[...] * lk1_ref[...], axis=-1, keepdims=True))
           - jnp.exp(jnp.sum(lq2_ref[...] * lk2_ref[...], axis=-1, keepdims=True)) + lambda_init)
    o = o_maps[:, :tq] - lam * o_maps[:, tq:]
    o = _rms_cols(o, g_ref[...]) * (1.0 - lambda_init)
    o_ref[0] = o.astype(o_ref.dtype)


def _diff_attention(q_t, k, v_t, lq1, lk1, lq2, lk2, subln_g, lambda_init):
    B, H2, _, S = q_t.shape
    H = H2 // 2
    tq = min(ATTN_TQ // 2, S)
    tk = min(ATTN_TK, S // 2)
    nxt = _next_tile(S // tq)
    vec = _const_spec((1, DIFF_QK_DIM))
    return pl.pallas_call(
        functools.partial(_diff_attn_kernel, tk=tk, lambda_init=lambda_init),
        grid=(B, H, S // tq),
        in_specs=[
            vec, vec, vec, vec, _const_spec((DIFF_V_DIM, 1)),
            pl.BlockSpec((1, 2, QK_PAD, tq), lambda b, h, i: (b, h, 0, i)),
            pl.BlockSpec((1, 2, QK_PAD, tq), lambda b, h, i: (b, h, 0, nxt(i))),
            pl.BlockSpec((1, S, QK_PAD), lambda b, h, i: (b, 0, h // 2)),
            pl.BlockSpec((1, HEAD_V, S), lambda b, h, i: (b, h, 0)),
        ],
        out_specs=pl.BlockSpec((1, HEAD_V, tq), lambda b, h, i: (b, h, i)),
        out_shape=jax.ShapeDtypeStruct((B, H * HEAD_V, S), F32),
        scratch_shapes=[pltpu.VMEM((2, tk, 2 * tq), F32), pltpu.VMEM((8, 2 * tq), F32)],
        compiler_params=_cparams(("parallel", "parallel", "arbitrary")), name="diff_attn",
    )(lq1, lk1, lq2, lk2, subln_g, q_t, q_t, k, v_t)


def _mixer_merge_kernel(x_ref, a_ref, b_ref, c_ref, gate_ref, wb_ref, wo_ref, g_ref, o_ref):
    merged = None
    for n, br_ref in enumerate((a_ref, b_ref, c_ref)):
        br = br_ref[0].T.astype(BF16)
        proj = _dot(br, wb_ref[n])
        term = gate_ref[0, :, n * D_MODEL:(n + 1) * D_MODEL].astype(F32) * proj
        merged = term if merged is None else merged + term
    y = _dot(merged.astype(BF16), wo_ref[...])
    o_ref[0] = x_ref[0] + _rms_rows(y, g_ref[...])


def _mixer_merge(x, a_t, b_t, c_t, gates, wb, wo, g):
    B, S, D = x.shape
    tm = min(MERGE_TM, S)
    xspec = pl.BlockSpec((1, tm, D), lambda b, i: (b, i, 0))
    brspec = pl.BlockSpec((1, BRANCH_WIDTH, tm), lambda b, i: (b, 0, i))
    return pl.pallas_call(
        _mixer_merge_kernel, grid=(B, S // tm),
        in_specs=[xspec, brspec, brspec, brspec,
                  pl.BlockSpec((1, tm, N_GATE), lambda b, i: (b, i, 0)),
                  _const_spec(wb.shape), _const_spec(wo.shape), _const_spec((1, D))],
        out_specs=xspec, out_shape=jax.ShapeDtypeStruct(x.shape, x.dtype),
        compiler_params=_cparams(("parallel", "parallel")), name="mixer_merge",
    )(x, a_t, b_t, c_t, gates, wb, wo, g)


def _norm_matmul_kernel(x_ref, g_ref, w_ref, o_ref):
    hn = _rms_rows(x_ref[...], g_ref[...]).astype(BF16)
    o_ref[...] = _dot(hn, w_ref[...]).astype(o_ref.dtype)


def _norm_matmul(x2d, g, w, out_dtype):
    T, D = x2d.shape
    N = w.shape[1]
    tm = min(KV_TM, T)
    return pl.pallas_call(
        _norm_matmul_kernel, grid=(T // tm,),
        in_specs=[pl.BlockSpec((tm, D), lambda i: (i, 0)), _const_spec((1, D)), _const_spec(w.shape)],
        out_specs=pl.BlockSpec((tm, N), lambda i: (i, 0)),
        out_shape=jax.ShapeDtypeStruct((T, N), out_dtype),
        compiler_params=_cparams(("parallel",)), name="mem_kv",
    )(x2d, g, w)


def _mem_attn_kernel(x_ref, kv_ref, gpre_ref, wq_ref, wo_ref, gpost_ref, o_ref):
    x = x_ref[0]
    hn = _rms_rows(x, gpre_ref[...]).astype(BF16)
    q = (_dot(hn, wq_ref[...]) * (MEM_HEAD_DIM ** -0.5 * LOG2E)).astype(BF16)
    outs = []
    for h in range(MEM_HEADS):
        lo = h * MEM_HEAD_DIM
        k_h = kv_ref[0, :, lo:lo + MEM_HEAD_DIM]
        v_h = kv_ref[0, :, D_MODEL + lo:D_MODEL + lo + MEM_HEAD_DIM]
        s = _dot_nt(q[:, lo:lo + MEM_HEAD_DIM], k_h)
        e = jnp.exp2(s - jnp.max(s, axis=-1, keepdims=True))
        p = (e / jnp.sum(e, axis=-1, keepdims=True)).astype(BF16)
        outs.append(_dot(p, v_h))
    o = jnp.concatenate(outs, axis=-1).astype(BF16)
    y = _dot(o, wo_ref[...])
    o_ref[0] = x + _rms_rows(y, gpost_ref[...])


def _mem_attention(x, kv, gpre, wq, wo, gpost):
    B, S, D = x.shape
    M = kv.shape[1]
    tm = min(MEM_TM, S)
    xspec = pl.BlockSpec((1, tm, D), lambda b, i: (b, i, 0))
    return pl.pallas_call(
        _mem_attn_kernel, grid=(B, S // tm),
        in_specs=[xspec, pl.BlockSpec((1, M, 2 * D), lambda b, i: (b, 0, 0)),
                  _const_spec((1, D)), _const_spec(wq.shape), _const_spec(wo.shape), _const_spec((1, D))],
        out_specs=xspec, out_shape=jax.ShapeDtypeStruct(x.shape, x.dtype),
        compiler_params=_cparams(("parallel", "parallel")), name="mem_attn",
    )(x, kv, gpre, wq, wo, gpost)


def _swiglu_chunk(hn, wg, wu, wd):
    gate = _dot(hn, wg)
    act = (gate * jax.nn.sigmoid(gate) * _dot(hn, wu)).astype(BF16)
    return _dot(act, wd)


def _swiglu(hn, wg_ref, wu_ref, wd_ref, tf):
    y = None
    for lo in range(0, wg_ref.shape[-1], tf):
        part = _swiglu_chunk(hn, wg_ref[:, lo:lo + tf], wu_ref[:, lo:lo + tf], wd_ref[lo:lo + tf, :])
        y = part if y is None else y + part
    return y


def _ffn_kernel(x_ref, gpre_ref, wg_ref, wu_ref, wd_ref, gpost_ref, o_ref):
    x = x_ref[...]
    hn = _rms_rows(x, gpre_ref[...]).astype(BF16)
    o_ref[...] = x + _rms_rows(_swiglu(hn, wg_ref, wu_ref, wd_ref, FFN_TF), gpost_ref[...])


def _ffn(x2d, gpre, wg, wu, wd, gpost):
    T, D = x2d.shape
    tm = min(FFN_TM, T)
    xspec = pl.BlockSpec((tm, D), lambda i: (i, 0))
    return pl.pallas_call(
        _ffn_kernel, grid=(T // tm,),
        in_specs=[xspec, _const_spec((1, D)), _const_spec(wg.shape), _const_spec(wu.shape),
                  _const_spec(wd.shape), _const_spec((1, D))],
        out_specs=xspec, out_shape=jax.ShapeDtypeStruct(x2d.shape, x2d.dtype),
        compiler_params=_cparams(("parallel",)), name="ffn",
    )(x2d, gpre, wg, wu, wd, gpost)


def _router_kernel(x_ref, gpre_ref, wr_ref, br_ref, tri_ref,
                   hn_ref, pos_ref, gate_col_ref, pos_col_ref, cnt_ref, run_ref):
    @pl.when(pl.program_id(0) == 0)
    def _():
        run_ref[...] = jnp.zeros_like(run_ref)

    hn = _rms_rows(x_ref[...], gpre_ref[...])
    hn_ref[...] = hn.astype(hn_ref.dtype)
    logits = lax.dot_general(wr_ref[...], hn, (((1,), (1,)), ((), ())),
                             precision=lax.Precision.HIGHEST,
                             preferred_element_type=F32) + br_ref[...]
    e_idx = lax.broadcasted_iota(jnp.int32, logits.shape, 0)
    m1 = jnp.max(logits, axis=0, keepdims=True)
    i1 = jnp.min(jnp.where(logits == m1, e_idx, N_EXPERTS), axis=0, keepdims=True)
    rest = jnp.where(e_idx == i1, -jnp.inf, logits)
    m2 = jnp.max(rest, axis=0, keepdims=True)
    i2 = jnp.min(jnp.where(rest == m2, e_idx, N_EXPERTS), axis=0, keepdims=True)
    e2 = jnp.exp(m2 - m1)
    w1 = 1.0 / (1.0 + e2)
    w2 = e2 / (1.0 + e2)
    gates = jnp.where(e_idx == i1, w1, 0.0) + jnp.where(e_idx == i2, w2, 0.0)

    sel = (e_idx == i1) | (e_idx == i2)
    sel_f = jnp.where(sel, 1.0, 0.0)
    before = _dot(sel_f.astype(BF16), tri_ref[...])
    run = run_ref[:, 0:1]
    pos = jnp.where(sel, run + before, -1.0)
    pos_ref[...] = pos
    tm = pos.shape[1]
    cols = jnp.concatenate([gates, pos, jnp.zeros((128 - 2 * N_EXPERTS, tm), F32)], axis=0).T
    for e in range(N_EXPERTS):
        gate_col_ref[e] = cols[:, e:e + 1]
        pos_col_ref[e] = cols[:, N_EXPERTS + e:N_EXPERTS + e + 1]
    cnt = jnp.sum(sel_f, axis=1, keepdims=True)
    cnt_ref[0] = cnt.astype(jnp.int32)
    run_ref[...] = run_ref[...] + cnt


def _router(x2d, gpre, wr_t, br_col):
    T, D = x2d.shape
    tm = MOE_BLOCK
    nb = T // tm
    tri = (np.arange(tm)[:, None] < np.arange(tm)[None, :]).astype(np.float32)
    col = pl.BlockSpec((N_EXPERTS, tm, 1), lambda i: (0, i, 0))
    return pl.pallas_call(
        _router_kernel, grid=(nb,),
        in_specs=[pl.BlockSpec((tm, D), lambda i: (i, 0)), _const_spec((1, D)),
                  _const_spec(wr_t.shape), _const_spec(br_col.shape), _const_spec((tm, tm))],
        out_specs=[pl.BlockSpec((tm, D), lambda i: (i, 0)),
                   pl.BlockSpec((N_EXPERTS, tm), lambda i: (0, i)), col, col,
                   pl.BlockSpec((1, N_EXPERTS, 1), lambda i: (i, 0, 0))],
        out_shape=[jax.ShapeDtypeStruct((T, D), BF16),
                   jax.ShapeDtypeStruct((N_EXPERTS, T), F32),
                   jax.ShapeDtypeStruct((N_EXPERTS, T, 1), F32),
                   jax.ShapeDtypeStruct((N_EXPERTS, T, 1), F32),
                   jax.ShapeDtypeStruct((nb, N_EXPERTS, 1), jnp.int32)],
        scratch_shapes=[pltpu.VMEM((N_EXPERTS, 128), F32)],
        compiler_params=_cparams(("arbitrary",)), name="router",
    )(x2d, gpre, wr_t, br_col, jnp.asarray(tri, BF16))


_FIRST, _LAST = 1, 2
_HALF = (4, 8)


def _moe_schedule_kernel(cnt_ref, ea_ref, ba_ref, ta_ref, ra_ref, fa_ref, eb_ref, bb_ref, tb_ref, rb_ref, fb_ref,
                         tile0_ref, start_ref, *, nb, ne):
    tr = MOE_BLOCK
    shift = tr.bit_length() - 1
    assert tr == 1 << shift
    g_max = ea_ref.shape[0]

    def first_tiles(e, tile):
        rows = lax.fori_loop(0, nb, lambda b, acc: acc + cnt_ref[b * ne + e], 0)
        tile0_ref[e] = tile
        start_ref[e] = 0
        return tile + ((rows + tr - 1) >> shift)

    lax.fori_loop(0, ne, first_tiles, 0)

    def append(lists, by_tile, e, b, start, carry):
        e_ref, b_ref, t_ref, r_ref, f_ref = lists
        c = cnt_ref[b * ne + e]
        first = start >> shift
        n = jnp.where(c > 0, ((start + c - 1) >> shift) - first + 1, 0)

        def visit(j, carry):
            g, prev_group = carry
            local = first + j
            tile = tile0_ref[e] + local
            group = tile if by_tile else b
            lo = start - (local << shift)
            opens = group != prev_group

            @pl.when(opens & (g > 0))
            def _():
                f_ref[g - 1] = f_ref[g - 1] | _LAST

            e_ref[g] = e
            b_ref[g] = b
            t_ref[g] = tile
            r_ref[g] = local << shift
            f_ref[g] = (jnp.where(opens, _FIRST, 0) | jnp.where(lo < tr // 2, _HALF[0], 0)
                        | jnp.where(lo + c > tr // 2, _HALF[1], 0))
            return g + 1, group

        return lax.fori_loop(0, n, visit, carry)

    def finish(lists, g):
        e_ref, b_ref, t_ref, r_ref, f_ref = lists
        f_ref[g - 1] = f_ref[g - 1] | _LAST

        def pad(k, _):
            e_ref[k] = e_ref[g - 1]
            b_ref[k] = b_ref[g - 1]
            t_ref[k] = t_ref[g - 1]
            r_ref[k] = r_ref[g - 1]
            f_ref[k] = 0
            return 0

        lax.fori_loop(g, g_max, pad, 0)

    list_a = (ea_ref, ba_ref, ta_ref, ra_ref, fa_ref)
    list_b = (eb_ref, bb_ref, tb_ref, rb_ref, fb_ref)

    def expert_major(e, carry):
        def block(b, inner):
            start, visits = inner
            return start + cnt_ref[b * ne + e], append(list_a, True, e, b, start, visits)
        return lax.fori_loop(0, nb, block, (0, carry))[1]

    g_a, _ = lax.fori_loop(0, ne, expert_major, (0, -1))
    finish(list_a, g_a)

    def block_major(b, carry):
        def expert(e, visits):
            start = start_ref[e]
            start_ref[e] = start + cnt_ref[b * ne + e]
            return append(list_b, False, e, b, start, visits)
        return lax.fori_loop(0, ne, expert, carry)

    g_b, _ = lax.fori_loop(0, nb, block_major, (0, -1))
    finish(list_b, g_b)


def _moe_schedule(cnt, nb, ne, g_max):
    smem = pl.BlockSpec(memory_space=pltpu.SMEM)
    out = jax.ShapeDtypeStruct((g_max,), jnp.int32)
    lists = pl.pallas_call(
        functools.partial(_moe_schedule_kernel, nb=nb, ne=ne),
        in_specs=[smem], out_specs=[smem] * 10, out_shape=[out] * 10,
        scratch_shapes=[pltpu.SMEM((ne,), jnp.int32), pltpu.SMEM((ne,), jnp.int32)],
        name="moe_schedule",
    )(cnt)
    return lists[:5], lists[5:]


def _moe_expert_kernel(e_ref, b_ref, t_ref, r_ref, f_ref, hn_ref, pos_ref, wg_ref, wu_ref, wd_ref,
                       ys_ref, xs_ref):
    g = pl.program_id(0)
    flags = f_ref[g]
    tr = xs_ref.shape[0]

    @pl.when((flags & _FIRST) != 0)
    def _():
        xs_ref[...] = jnp.zeros_like(xs_ref)

    half = tr // 2
    for h in range(2):
        @pl.when((flags & _HALF[h]) != 0)
        def _():
            rows = (lax.broadcasted_iota(jnp.int32, (half, 1), 0) + (r_ref[g] + h * half)).astype(F32)
            onehot = jnp.where(rows == pos_ref[0, 0], 1.0, 0.0).astype(BF16)
            xs_ref[h * half:(h + 1) * half, :] += _dot(onehot, hn_ref[...])

    @pl.when((flags & _LAST) != 0)
    def _():
        y = _swiglu(xs_ref[...].astype(BF16), wg_ref.at[0], wu_ref.at[0], wd_ref.at[0], FFN_TF)
        ys_ref[...] = y.astype(ys_ref.dtype)


def _moe_expert(list_a, hn, pos, wg, wu, wd, n_tiles):
    T, D = hn.shape
    E, _, F = wg.shape
    tr = MOE_BLOCK
    g_max = list_a[0].shape[0]
    pos4 = pos.reshape(E, T // tr, 1, tr)
    wspec_in = pl.BlockSpec((1, D, F), lambda g, e, b, t, r, f: (e[g], 0, 0), pipeline_mode=pl.Buffered(1))
    wspec_out = pl.BlockSpec((1, F, D), lambda g, e, b, t, r, f: (e[g], 0, 0), pipeline_mode=pl.Buffered(1))
    grid_spec = pltpu.PrefetchScalarGridSpec(
        num_scalar_prefetch=5, grid=(g_max,),
        in_specs=[pl.BlockSpec((tr, D), lambda g, e, b, t, r, f: (b[g], 0)),
                  pl.BlockSpec((1, 1, 1, tr), lambda g, e, b, t, r, f: (e[g], b[g], 0, 0)),
                  wspec_in, wspec_in, wspec_out],
        out_specs=pl.BlockSpec((tr, D), lambda g, e, b, t, r, f: (t[g], 0)),
        scratch_shapes=[pltpu.VMEM((tr, D), F32)])
    return pl.pallas_call(
        _moe_expert_kernel, grid_spec=grid_spec,
        out_shape=jax.ShapeDtypeStruct((n_tiles * tr, D), BF16),
        compiler_params=_cparams(("arbitrary",)), name="moe_expert",
    )(*list_a, hn, pos4, wg, wu, wd)


def _moe_combine_kernel(e_ref, b_ref, t_ref, r_ref, f_ref, x_ref, ys_ref, pos_ref, gate_ref, gpost_ref,
                        o_ref, acc_ref):
    g = pl.program_id(0)
    flags = f_ref[g]
    tr = ys_ref.shape[0]

    @pl.when((flags & _FIRST) != 0)
    def _():
        acc_ref[...] = jnp.zeros_like(acc_ref)

    half = tr // 2
    for h in range(2):
        @pl.when((flags & _HALF[h]) != 0)
        def _():
            rows = (lax.broadcasted_iota(jnp.int32, (1, half), 1) + (r_ref[g] + h * half)).astype(F32)
            onehot = jnp.where(pos_ref[0] == rows, 1.0, 0.0).astype(BF16)
            acc_ref[...] += gate_ref[0] * _dot(onehot, ys_ref[h * half:(h + 1) * half, :])

    @pl.when((flags & _LAST) != 0)
    def _():
        o_ref[...] = x_ref[...] + _rms_rows(acc_ref[...], gpost_ref[...])


def _moe_combine(list_b, x2d, ys, pos_col, gate_col, gpost):
    T, D = x2d.shape
    tr = MOE_BLOCK
    g_max = list_b[0].shape[0]
    xspec = pl.BlockSpec((tr, D), lambda g, e, b, t, r, f: (b[g], 0))
    colspec = pl.BlockSpec((1, tr, 1), lambda g, e, b, t, r, f: (e[g], b[g], 0))
    grid_spec = pltpu.PrefetchScalarGridSpec(
        num_scalar_prefetch=5, grid=(g_max,),
        in_specs=[xspec, pl.BlockSpec((tr, D), lambda g, e, b, t, r, f: (t[g], 0)), colspec, colspec,
                  pl.BlockSpec((1, D), lambda g, e, b, t, r, f: (0, 0))],
        out_specs=xspec,
        scratch_shapes=[pltpu.VMEM((tr, D), F32)])
    return pl.pallas_call(
        _moe_combine_kernel, grid_spec=grid_spec,
        out_shape=jax.ShapeDtypeStruct(x2d.shape, x2d.dtype),
        compiler_params=_cparams(("arbitrary",)), name="moe_combine",
    )(*list_b, x2d, ys, pos_col, gate_col, gpost)


def _moe(x2d, gpre, wr_t, br_col, wg, wu, wd, gpost):
    T, D = x2d.shape
    E = wg.shape[0]
    nb = T // MOE_BLOCK
    n_tiles = TOP_K * nb + E
    g_max = E * nb + n_tiles
    hn, pos, gate_col, pos_col, cnt = _router(x2d, gpre, wr_t, br_col)
    list_a, list_b = _moe_schedule(cnt.reshape(nb * E), nb, E, g_max)
    ys = _moe_expert(list_a, hn, pos, wg, wu, wd, n_tiles)
    return _moe_combine(list_b, x2d, ys, pos_col, gate_col, gpost)


def _feature_row_order():
    def diff_block(base):
        x1, x2, rest = [], [], []
        half = DIFF_ROPE_DIM // 2
        for h in range(DIFF_HEADS):
            for c in range(2):
                lo = base + h * 2 * DIFF_QK_DIM + c * DIFF_QK_DIM
                x1 += range(lo, lo + half)
                x2 += range(lo + half, lo + DIFF_ROPE_DIM)
                rest += range(lo + DIFF_ROPE_DIM, lo + DIFF_QK_DIM)
        return x1 + x2 + rest

    order = list(range(0, _S_DQ)) + diff_block(_S_DQ) + diff_block(_S_DK) + list(range(_S_DV, N_FEAT))
    return np.asarray(order, np.int32)


def _mla_q_up_order():
    w = MLA_NOPE_DIM + MLA_ROPE_DIM
    half = MLA_ROPE_DIM // 2
    nope = [h * w + j for h in range(MLA_HEADS) for j in range(MLA_NOPE_DIM)]
    x1 = [h * w + MLA_NOPE_DIM + j for h in range(MLA_HEADS) for j in range(half)]
    x2 = [h * w + MLA_NOPE_DIM + half + j for h in range(MLA_HEADS) for j in range(half)]
    return np.asarray(nope + x1 + x2, np.int32)


def _mla_kv_up_order():
    w = MLA_NOPE_DIM + MLA_V_DIM
    k = [h * w + j for h in range(MLA_HEADS) for j in range(MLA_NOPE_DIM)]
    v = [h * w + MLA_NOPE_DIM + j for h in range(MLA_HEADS) for j in range(MLA_V_DIM)]
    return np.asarray(k + v, np.int32)


def _rope_tables(S):
    def cos_sin(pos, rot_dim):
        inv = ROPE_THETA ** (-jnp.arange(0, rot_dim, 2, dtype=F32) / rot_dim)
        ang = pos.astype(F32)[:, None] * inv[None, :]
        return jnp.cos(ang).T, jnp.sin(ang).T

    rows = S // GRID_W
    pos = jnp.arange(S)
    row = jnp.repeat(jnp.arange(rows), GRID_W)
    col = jnp.tile(jnp.arange(GRID_W), rows)
    cm, sm = cos_sin(pos, MLA_ROPE_DIM)
    cr, sr = cos_sin(row, GQA_HEAD_DIM // 2)
    cc, sc = cos_sin(col, GQA_HEAD_DIM // 2)
    cp, sp = cos_sin(pos, DIFF_ROPE_DIM)
    return (jnp.tile(cm, (MLA_HEADS, 1)), jnp.tile(sm, (MLA_HEADS, 1)),
            jnp.concatenate([cr, cc], axis=0), jnp.concatenate([sr, sc], axis=0),
            jnp.tile(cp, (2 * DIFF_HEADS, 1)), jnp.tile(sp, (2 * DIFF_HEADS, 1)))


def _row(v):
    return v.reshape(1, -1).astype(F32)


def _col(v):
    return v.reshape(-1, 1).astype(F32)


def kernel(x, mem, mix_pre_g, mix_post_g, w_in, mla_q_norm_g, mla_w_q_up, mla_kv_norm_g, mla_w_kv_up,
           gqa_q_norm_g, gqa_k_norm_g, diff_lambda_q1, diff_lambda_k1, diff_lambda_q2, diff_lambda_k2,
           diff_subln_g, w_branch, w_out, mem_pre_g, mem_post_g, mem_norm_g, mem_wq, mem_wkv, mem_wo,
           ffn_pre_g, ffn_post_g, dense_w_gate, dense_w_up, dense_w_down, moe_w_router, moe_b_router,
           moe_w_gate, moe_w_up, moe_w_down):
    B, S, D = x.shape
    M = mem.shape[1]
    depth = w_in.shape[0]
    tables = _rope_tables(S)
    feat_order = _feature_row_order()
    q_up_order = _mla_q_up_order()
    kv_up_order = _mla_kv_up_order()

    for layer in range(depth):
        lambda_init = 0.8 - 0.6 * math.exp(-0.3 * layer)

        w_l = w_in[layer]
        wt = w_l[:, feat_order].T.astype(BF16)
        wg = w_l[:, N_FEAT:].astype(BF16)
        wqu = mla_w_q_up[layer][:, q_up_order].T.astype(BF16)
        wkvu = mla_w_kv_up[layer][:, kv_up_order].T.astype(BF16)
        (qm, km, vm, qg, kg, vg, qd, kd, vd, gates) = _mixer_prep(
            x, _row(mix_pre_g[layer]), wt, wg, wqu, wkvu,
            _col(mla_q_norm_g[layer]), _col(mla_kv_norm_g[layer]),
            _col(gqa_q_norm_g[layer]), _col(gqa_k_norm_g[layer]), tables)
        a_t = _attention(qm, km, vm, kv_of_head=lambda h: h, kcol_of_head=lambda h: h, name="mla_attn")
        b_t = _attention(qg, kg, vg, kv_of_head=lambda h: h // GQA_GROUP, kcol_of_head=lambda h: 0,
                         name="gqa_attn")
        c_t = _diff_attention(qd, kd, vd, _row(diff_lambda_q1[layer]), _row(diff_lambda_k1[layer]),
                              _row(diff_lambda_q2[layer]), _row(diff_lambda_k2[layer]),
                              _col(diff_subln_g[layer]), lambda_init)
        x = _mixer_merge(x, a_t, b_t, c_t, gates, w_branch[layer].astype(BF16),
                         w_out[layer].astype(BF16), _row(mix_post_g[layer]))

        kv = _norm_matmul(mem.reshape(B * M, D), _row(mem_norm_g[layer]), mem_wkv[layer].astype(BF16), BF16)
        x = _mem_attention(x, kv.reshape(B, M, 2 * D), _row(mem_pre_g[layer]), mem_wq[layer].astype(BF16),
                           mem_wo[layer].astype(BF16), _row(mem_post_g[layer]))

        x2d = x.reshape(B * S, D)
        i = layer // 2
        if layer % 2 == 0:
            x2d = _ffn(x2d, _row(ffn_pre_g[layer]), dense_w_gate[i].astype(BF16), dense_w_up[i].astype(BF16),
                       dense_w_down[i].astype(BF16), _row(ffn_post_g[layer]))
        else:
            x2d = _moe(x2d, _row(ffn_pre_g[layer]), moe_w_router[i].T.astype(F32), _col(moe_b_router[i]),
                       moe_w_gate[i].astype(BF16), moe_w_up[i].astype(BF16), moe_w_down[i].astype(BF16),
                       _row(ffn_post_g[layer]))
        x = x2d.reshape(B, S, D)
    return x
```

```python
import functools
import math

import numpy as np
import jax
import jax.numpy as jnp
from jax import lax
from jax.experimental import pallas as pl
from jax.experimental.pallas import tpu as pltpu

F32 = jnp.float32
BF16 = jnp.bfloat16

D_MODEL = 1024
GRID_W = 64
ROPE_THETA = 500000.0
NORM_EPS = 1e-6

MLA_HEADS = 8
MLA_Q_LORA = 384
MLA_KV_LORA = 256
MLA_NOPE_DIM = 64
MLA_ROPE_DIM = 32
MLA_V_DIM = 64

GQA_Q_HEADS = 8
GQA_KV_HEADS = 2
GQA_GROUP = GQA_Q_HEADS // GQA_KV_HEADS
GQA_HEAD_DIM = 64

DIFF_HEADS = 8
DIFF_QK_DIM = 32
DIFF_V_DIM = 2 * DIFF_QK_DIM
DIFF_ROPE_DIM = DIFF_QK_DIM // 4

N_BRANCHES = 3
BRANCH_WIDTH = 512
HEAD_V = 64

MEM_HEADS = 4
MEM_HEAD_DIM = D_MODEL // MEM_HEADS

D_FF = 2816
N_EXPERTS = 8
TOP_K = 2

IN_SPLITS = (MLA_Q_LORA, MLA_KV_LORA, MLA_ROPE_DIM,
             GQA_Q_HEADS * GQA_HEAD_DIM, GQA_KV_HEADS * GQA_HEAD_DIM, GQA_KV_HEADS * GQA_HEAD_DIM,
             DIFF_HEADS * 2 * DIFF_QK_DIM, DIFF_HEADS * 2 * DIFF_QK_DIM, DIFF_HEADS * DIFF_V_DIM,
             N_BRANCHES * D_MODEL)
IN_OFFS = tuple(int(v) for v in np.cumsum((0,) + IN_SPLITS))
N_FEAT = IN_OFFS[9]
N_GATE = IN_SPLITS[9]

LOG2E = 1.4426950408889634
QK_PAD = 128
ONES_ROWS = 16

V7X_VMEM_LIMIT_BYTES = 56 * 1024 * 1024

PREP_TM = 512
ATTN_TQ = 1024
ATTN_TK = 256
ATTN_UNROLL = 8
MERGE_TM = 512
MEM_TM = 1024
FFN_TM = 512
FFN_TF = 2816
MOE_BLOCK = 512
KV_TM = 256


def _cparams(sem):
    return pltpu.CompilerParams(dimension_semantics=sem, vmem_limit_bytes=V7X_VMEM_LIMIT_BYTES)


def _const_spec(shape):
    nd = len(shape)
    return pl.BlockSpec(shape, lambda *_: (0,) * nd, pipeline_mode=pl.Buffered(1))


def _rms_rows(x, g_row):
    ms = jnp.mean(x * x, axis=-1, keepdims=True)
    return x * lax.rsqrt(ms + NORM_EPS) * g_row


def _rms_cols(x, g_col):
    ms = jnp.mean(x * x, axis=0, keepdims=True)
    return x * lax.rsqrt(ms + NORM_EPS) * g_col


def _sigmoid(x):
    return 0.5 * jnp.tanh(0.5 * x) + 0.5


def _dot(a, b):
    return jnp.dot(a, b, preferred_element_type=F32)


def _dot_nt(a, b):
    return lax.dot_general(a, b, (((1,), (1,)), ((), ())), preferred_element_type=F32)


_S_CQ, _S_CKV, _S_KR, _S_GQ, _S_GK, _S_GV, _S_DQ, _S_DK, _S_DV = IN_OFFS[:9]


def _rope_rows(x1, x2, c, s):
    return x1 * c - x2 * s, x2 * c + x1 * s


def _mixer_prep_kernel(x_ref, g_ref, wt_ref, wg_ref, wqu_ref, wkvu_ref,
                       mqg_ref, mkvg_ref, gqg_ref, gkg_ref,
                       cm_ref, sm_ref, ca_ref, sa_ref, cp_ref, sp_ref,
                       qm_ref, km_ref, vm_ref, qg_ref, kg_ref, vg_ref,
                       qd_ref, kd_ref, vd_ref, gate_ref):
    tm = x_ref.shape[1]
    hn = _rms_rows(x_ref[0], g_ref[...]).astype(BF16)

    z_all = _dot_nt(wt_ref[...], hn)

    def feat(lo, n):
        return z_all[lo:lo + n]

    gate_ref[0] = _sigmoid(_dot(hn, wg_ref[...])).astype(gate_ref.dtype)

    zeros32 = jnp.zeros((32, tm), F32)
    zeros64 = jnp.zeros((64, tm), F32)

    c_mla = (MLA_NOPE_DIM + MLA_ROPE_DIM) ** -0.5 * LOG2E
    cm = cm_ref[...]
    sm = sm_ref[...]
    cqn = _rms_cols(feat(_S_CQ, MLA_Q_LORA), mqg_ref[...]).astype(BF16)
    q_all = _dot(wqu_ref[...], cqn)
    q_nope = q_all[0:512] * c_mla
    q_r1, q_r2 = _rope_rows(q_all[512:640], q_all[640:768], cm, sm)
    q_r1 = q_r1 * c_mla
    q_r2 = q_r2 * c_mla
    for h in range(MLA_HEADS):
        qm_ref[0, h] = jnp.concatenate(
            [q_nope[64 * h:64 * h + 64], q_r1[16 * h:16 * h + 16], q_r2[16 * h:16 * h + 16], zeros32],
            axis=0).astype(qm_ref.dtype)

    ckvn = _rms_cols(feat(_S_CKV, MLA_KV_LORA), mkvg_ref[...]).astype(BF16)
    kv_all = _dot(wkvu_ref[...], ckvn)
    vm_ref[0] = kv_all[512:1024].astype(vm_ref.dtype)
    kr = feat(_S_KR, MLA_ROPE_DIM)
    k_r1, k_r2 = _rope_rows(kr[0:16], kr[16:32], cm[0:16], sm[0:16])
    for h in range(MLA_HEADS):
        kt = jnp.concatenate([kv_all[64 * h:64 * h + 64], k_r1, k_r2, zeros32], axis=0)
        km_ref[0, :, 128 * h:128 * h + 128] = kt.T.astype(km_ref.dtype)

    c_gqa = GQA_HEAD_DIM ** -0.5 * LOG2E
    ca = ca_ref[...]
    sa = sa_ref[...]
    gq = feat(_S_GQ, GQA_Q_HEADS * GQA_HEAD_DIM)
    for h in range(GQA_Q_HEADS):
        qn = _rms_cols(gq[64 * h:64 * h + 64], gqg_ref[...])
        r1, r2 = _rope_rows(qn[0:32], qn[32:64], ca, sa)
        q64 = jnp.concatenate([r1, r2], axis=0) * c_gqa
        parts = [q64, zeros64] if h // GQA_GROUP == 0 else [zeros64, q64]
        qg_ref[0, h] = jnp.concatenate(parts, axis=0).astype(qg_ref.dtype)
    gk = feat(_S_GK, GQA_KV_HEADS * GQA_HEAD_DIM)
    kparts = []
    for g in range(GQA_KV_HEADS):
        kn = _rms_cols(gk[64 * g:64 * g + 64], gkg_ref[...])
        r1, r2 = _rope_rows(kn[0:32], kn[32:64], ca, sa)
        kparts += [r1, r2]
    kg_ref[0] = jnp.concatenate(kparts, axis=0).T.astype(kg_ref.dtype)
    vg_ref[0] = feat(_S_GV, GQA_KV_HEADS * GQA_HEAD_DIM).astype(vg_ref.dtype)

    c_diff = DIFF_QK_DIM ** -0.5 * LOG2E
    cp = cp_ref[...]
    sp = sp_ref[...]
    row = lax.broadcasted_iota(jnp.int32, (64, 1), 0)
    in_c0 = (row < 4) | ((row >= 8) & (row < 12)) | ((row >= 16) & (row < 40))

    def diff_heads(lo):
        z = feat(lo, DIFF_HEADS * 2 * DIFF_QK_DIM)
        r1, r2 = _rope_rows(z[0:64], z[64:128], cp, sp)
        rest = z[128:512]
        return [jnp.concatenate([r1[8 * h:8 * h + 8], r2[8 * h:8 * h + 8], rest[48 * h:48 * h + 48]], axis=0)
                for h in range(DIFF_HEADS)]

    for h, q64 in enumerate(diff_heads(_S_DQ)):
        q64 = q64 * c_diff
        for c in range(2):
            qc = jnp.where(in_c0 if c == 0 else jnp.logical_not(in_c0), q64, 0.0)
            parts = [qc, zeros64] if h % 2 == 0 else [zeros64, qc]
            qd_ref[0, 2 * h + c] = jnp.concatenate(parts, axis=0).astype(qd_ref.dtype)
    k_heads = diff_heads(_S_DK)
    for p in range(DIFF_HEADS // 2):
        kt = jnp.concatenate([k_heads[2 * p], k_heads[2 * p + 1]], axis=0)
        kd_ref[0, :, 128 * p:128 * p + 128] = kt.T.astype(kd_ref.dtype)
    vd_ref[0] = feat(_S_DV, DIFF_HEADS * DIFF_V_DIM).astype(vd_ref.dtype)


def _mixer_prep(x, g, wt, wg, wqu, wkvu, mqg, mkvg, gqg, gkg, tables):
    B, S, D = x.shape
    tm = min(PREP_TM, S)
    cm, sm, ca, sa, cp, sp = tables
    grid = (B, S // tm)

    def tok(width):
        return pl.BlockSpec((1, tm, width), lambda b, i: (b, i, 0))

    def featm(rows):
        return pl.BlockSpec((1, rows, tm), lambda b, i: (b, 0, i))

    def heads(n):
        return pl.BlockSpec((1, n, QK_PAD, tm), lambda b, i: (b, 0, 0, i))

    def table(rows):
        return pl.BlockSpec((rows, tm), lambda b, i: (0, i))

    in_specs = [
        pl.BlockSpec((1, tm, D), lambda b, i: (b, i, 0)),
        _const_spec((1, D)),
        _const_spec(wt.shape), _const_spec(wg.shape), _const_spec(wqu.shape), _const_spec(wkvu.shape),
        _const_spec(mqg.shape), _const_spec(mkvg.shape), _const_spec(gqg.shape), _const_spec(gkg.shape),
        table(128), table(128), table(32), table(32), table(64), table(64),
    ]
    out_shape = [
        jax.ShapeDtypeStruct((B, MLA_HEADS, QK_PAD, S), BF16),
        jax.ShapeDtypeStruct((B, S, MLA_HEADS * QK_PAD), BF16),
        jax.ShapeDtypeStruct((B, MLA_HEADS * HEAD_V, S), BF16),
        jax.ShapeDtypeStruct((B, GQA_Q_HEADS, QK_PAD, S), BF16),
        jax.ShapeDtypeStruct((B, S, QK_PAD), BF16),
        jax.ShapeDtypeStruct((B, GQA_KV_HEADS * HEAD_V, S), BF16),
        jax.ShapeDtypeStruct((B, 2 * DIFF_HEADS, QK_PAD, S), BF16),
        jax.ShapeDtypeStruct((B, S, DIFF_HEADS // 2 * QK_PAD), BF16),
        jax.ShapeDtypeStruct((B, DIFF_HEADS * HEAD_V, S), BF16),
        jax.ShapeDtypeStruct((B, S, N_GATE), BF16),
    ]
    out_specs = [
        heads(MLA_HEADS), tok(MLA_HEADS * QK_PAD), featm(MLA_HEADS * HEAD_V),
        heads(GQA_Q_HEADS), tok(QK_PAD), featm(GQA_KV_HEADS * HEAD_V),
        heads(2 * DIFF_HEADS), tok(DIFF_HEADS // 2 * QK_PAD), featm(DIFF_HEADS * HEAD_V),
        tok(N_GATE),
    ]
    return pl.pallas_call(
        _mixer_prep_kernel, grid=grid, in_specs=in_specs, out_specs=out_specs, out_shape=out_shape,
        compiler_params=_cparams(("parallel", "parallel")), name="mixer_prep",
    )(x, g, wt, wg, wqu, wkvu, mqg, mkvg, gqg, gkg, cm, sm, ca, sa, cp, sp)


def _online_softmax_attention(q_t, load_q_next, k_ref, v_ref, s_ref, cmax_ref, *, tk):
    S = k_ref.shape[1]
    lanes = q_t.shape[1]
    n_chunks = S // tk
    unroll = min(ATTN_UNROLL, n_chunks)
    assert unroll % 2 == 0 and n_chunks % unroll == 0
    ones = jnp.ones((ONES_ROWS, tk), BF16)

    def scores(c, slot, q=q_t):
        s_t = _dot(k_ref[0, pl.ds(pl.multiple_of(c * tk, tk), tk), :], q)
        s_ref[slot] = s_t
        return jnp.max(s_t, axis=0, keepdims=True)

    def update(c, slot, col_max, m, acc):
        v_aug = jnp.concatenate([v_ref[0, :, pl.ds(pl.multiple_of(c * tk, tk), tk)], ones], axis=0)
        m_new = jnp.maximum(m, col_max)
        p = jnp.exp2(s_ref[slot] - m_new).astype(BF16)
        return m_new, acc * jnp.exp2(m - m_new) + _dot(v_aug, p)

    def group(c, cmax, m, acc, last):
        for u in range(unroll):
            is_final = last and u == unroll - 1
            cmax_next = scores(0, 0, load_q_next()) if is_final else scores(c + u + 1, (u + 1) % 2)
            m, acc = update(c + u, u % 2, cmax, m, acc)
            cmax = cmax_next
        return cmax, m, acc

    def body(i, carry):
        return group(unroll * i, *carry, last=False)

    @pl.when(pl.program_id(2) == 0)
    def _():
        cmax_ref[...] = jnp.broadcast_to(scores(0, 0), cmax_ref.shape)

    cmax0 = jnp.max(cmax_ref[...], axis=0, keepdims=True)
    m0 = jnp.full((1, lanes), -jnp.inf, F32)
    acc0 = jnp.zeros((HEAD_V + ONES_ROWS, lanes), F32)
    carry = lax.fori_loop(0, n_chunks // unroll - 1, body, (cmax0, m0, acc0))
    cmax_next, _, acc = group(n_chunks - unroll, *carry, last=True)
    cmax_ref[...] = jnp.broadcast_to(cmax_next, cmax_ref.shape)
    return acc[:HEAD_V] / acc[HEAD_V:HEAD_V + 1]


def _attn_kernel(q_ref, qn_ref, k_ref, v_ref, o_ref, s_ref, cmax_ref, *, tk):
    o = _online_softmax_attention(q_ref[0, 0], lambda: qn_ref[0, 0], k_ref, v_ref, s_ref, cmax_ref, tk=tk)
    o_ref[0] = o.astype(o_ref.dtype)


def _next_tile(n_tiles):
    return lambda i: jnp.minimum(i + 1, n_tiles - 1)


def _attention(q_t, k, v_t, *, kv_of_head, kcol_of_head, name):
    B, H, _, S = q_t.shape
    tq = min(ATTN_TQ, S)
    tk = min(ATTN_TK, S // 2)
    nxt = _next_tile(S // tq)
    return pl.pallas_call(
        functools.partial(_attn_kernel, tk=tk),
        grid=(B, H, S // tq),
        in_specs=[
            pl.BlockSpec((1, 1, QK_PAD, tq), lambda b, h, i: (b, h, 0, i)),
            pl.BlockSpec((1, 1, QK_PAD, tq), lambda b, h, i: (b, h, 0, nxt(i))),
            pl.BlockSpec((1, S, QK_PAD), lambda b, h, i: (b, 0, kcol_of_head(h))),
            pl.BlockSpec((1, HEAD_V, S), lambda b, h, i: (b, kv_of_head(h), 0)),
        ],
        out_specs=pl.BlockSpec((1, HEAD_V, tq), lambda b, h, i: (b, h, i)),
        out_shape=jax.ShapeDtypeStruct((B, H * HEAD_V, S), F32),
        scratch_shapes=[pltpu.VMEM((2, tk, tq), F32), pltpu.VMEM((8, tq), F32)],
        compiler_params=_cparams(("parallel", "parallel", "arbitrary")), name=name,
    )(q_t, q_t, k, v_t)


def _diff_attn_kernel(lq1_ref, lk1_ref, lq2_ref, lk2_ref, g_ref, q_ref, qn_ref, k_ref, v_ref, o_ref,
                      s_ref, cmax_ref, *, tk, lambda_init):
    tq = q_ref.shape[3]
    q_both = jnp.concatenate([q_ref[0, 0], q_ref[0, 1]], axis=1)
    o_maps = _online_softmax_attention(
        q_both, lambda: jnp.concatenate([qn_ref[0, 0], qn_ref[0, 1]], axis=1), k_ref, v_ref, s_ref, cmax_ref,
        tk=tk)
    lam = (jnp.exp(jnp.sum(lq1_ref[...] * lk1_ref[...], axis=-1, keepdims=True))
           - jnp.exp(jnp.sum(lq2_ref[...] * lk2_ref[...], axis=-1, keepdims=True)) + lambda_init)
    o = o_maps[:, :tq] - lam * o_maps[:, tq:]
    o = _rms_cols(o, g_ref[...]) * (1.0 - lambda_init)
    o_ref[0] = o.astype(o_ref.dtype)


def _diff_attention(q_t, k, v_t, lq1, lk1, lq2, lk2, subln_g, lambda_init):
    B, H2, _, S = q_t.shape
    H = H2 // 2
    tq = min(ATTN_TQ // 2, S)
    tk = min(ATTN_TK, S // 2)
    nxt = _next_tile(S // tq)
    vec = _const_spec((1, DIFF_QK_DIM))
    return pl.pallas_call(
        functools.partial(_diff_attn_kernel, tk=tk, lambda_init=lambda_init),
        grid=(B, H, S // tq),
        in_specs=[
            vec, vec, vec, vec, _const_spec((DIFF_V_DIM, 1)),
            pl.BlockSpec((1, 2, QK_PAD, tq), lambda b, h, i: (b, h, 0, i)),
            pl.BlockSpec((1, 2, QK_PAD, tq), lambda b, h, i: (b, h, 0, nxt(i))),
            pl.BlockSpec((1, S, QK_PAD), lambda b, h, i: (b, 0, h // 2)),
            pl.BlockSpec((1, HEAD_V, S), lambda b, h, i: (b, h, 0)),
        ],
        out_specs=pl.BlockSpec((1, HEAD_V, tq), lambda b, h, i: (b, h, i)),
        out_shape=jax.ShapeDtypeStruct((B, H * HEAD_V, S), F32),
        scratch_shapes=[pltpu.VMEM((2, tk, 2 * tq), F32), pltpu.VMEM((8, 2 * tq), F32)],
        compiler_params=_cparams(("parallel", "parallel", "arbitrary")), name="diff_attn",
    )(lq1, lk1, lq2, lk2, subln_g, q_t, q_t, k, v_t)


def _mixer_merge_kernel(x_ref, a_ref, b_ref, c_ref, gate_ref, wb_ref, wo_ref, g_ref, o_ref):
    merged = None
    for n, br_ref in enumerate((a_ref, b_ref, c_ref)):
        br = br_ref[0].T.astype(BF16)
        proj = _dot(br, wb_ref[n])
        term = gate_ref[0, :, n * D_MODEL:(n + 1) * D_MODEL].astype(F32) * proj
        merged = term if merged is None else merged + term
    y = _dot(merged.astype(BF16), wo_ref[...])
    o_ref[0] = x_ref[0] + _rms_rows(y, g_ref[...])


def _mixer_merge(x, a_t, b_t, c_t, gates, wb, wo, g):
    B, S, D = x.shape
    tm = min(MERGE_TM, S)
    xspec = pl.BlockSpec((1, tm, D), lambda b, i: (b, i, 0))
    brspec = pl.BlockSpec((1, BRANCH_WIDTH, tm), lambda b, i: (b, 0, i))
    return pl.pallas_call(
        _mixer_merge_kernel, grid=(B, S // tm),
        in_specs=[xspec, brspec, brspec, brspec,
                  pl.BlockSpec((1, tm, N_GATE), lambda b, i: (b, i, 0)),
                  _const_spec(wb.shape), _const_spec(wo.shape), _const_spec((1, D))],
        out_specs=xspec, out_shape=jax.ShapeDtypeStruct(x.shape, x.dtype),
        compiler_params=_cparams(("parallel", "parallel")), name="mixer_merge",
    )(x, a_t, b_t, c_t, gates, wb, wo, g)


def _norm_matmul_kernel(x_ref, g_ref, w_ref, o_ref):
    hn = _rms_rows(x_ref[...], g_ref[...]).astype(BF16)
    o_ref[...] = _dot(hn, w_ref[...]).astype(o_ref.dtype)


def _norm_matmul(x2d, g, w, out_dtype):
    T, D = x2d.shape
    N = w.shape[1]
    tm = min(KV_TM, T)
    return pl.pallas_call(
        _norm_matmul_kernel, grid=(T // tm,),
        in_specs=[pl.BlockSpec((tm, D), lambda i: (i, 0)), _const_spec((1, D)), _const_spec(w.shape)],
        out_specs=pl.BlockSpec((tm, N), lambda i: (i, 0)),
        out_shape=jax.ShapeDtypeStruct((T, N), out_dtype),
        compiler_params=_cparams(("parallel",)), name="mem_kv",
    )(x2d, g, w)


def _mem_attn_kernel(x_ref, kv_ref, gpre_ref, wq_ref, wo_ref, gpost_ref, o_ref):
    x = x_ref[0]
    hn = _rms_rows(x, gpre_ref[...]).astype(BF16)
    q = (_dot(hn, wq_ref[...]) * (MEM_HEAD_DIM ** -0.5 * LOG2E)).astype(BF16)
    outs = []
    for h in range(MEM_HEADS):
        lo = h * MEM_HEAD_DIM
        k_h = kv_ref[0, :, lo:lo + MEM_HEAD_DIM]
        v_h = kv_ref[0, :, D_MODEL + lo:D_MODEL + lo + MEM_HEAD_DIM]
        s = _dot_nt(q[:, lo:lo + MEM_HEAD_DIM], k_h)
        e = jnp.exp2(s - jnp.max(s, axis=-1, keepdims=True))
        p = (e / jnp.sum(e, axis=-1, keepdims=True)).astype(BF16)
        outs.append(_dot(p, v_h))
    o = jnp.concatenate(outs, axis=-1).astype(BF16)
    y = _dot(o, wo_ref[...])
    o_ref[0] = x + _rms_rows(y, gpost_ref[...])


def _mem_attention(x, kv, gpre, wq, wo, gpost):
    B, S, D = x.shape
    M = kv.shape[1]
    tm = min(MEM_TM, S)
    xspec = pl.BlockSpec((1, tm, D), lambda b, i: (b, i, 0))
    return pl.pallas_call(
        _mem_attn_kernel, grid=(B, S // tm),
        in_specs=[xspec, pl.BlockSpec((1, M, 2 * D), lambda b, i: (b, 0, 0)),
                  _const_spec((1, D)), _const_spec(wq.shape), _const_spec(wo.shape), _const_spec((1, D))],
        out_specs=xspec, out_shape=jax.ShapeDtypeStruct(x.shape, x.dtype),
        compiler_params=_cparams(("parallel", "parallel")), name="mem_attn",
    )(x, kv, gpre, wq, wo, gpost)


def _swiglu_chunk(hn, wg, wu, wd):
    gate = _dot(hn, wg)
    act = (gate * _sigmoid(gate) * _dot(hn, wu)).astype(BF16)
    return _dot(act, wd)


def _swiglu(hn, wg_ref, wu_ref, wd_ref, tf):
    y = None
    for lo in range(0, wg_ref.shape[-1], tf):
        part = _swiglu_chunk(hn, wg_ref[:, lo:lo + tf], wu_ref[:, lo:lo + tf], wd_ref[lo:lo + tf, :])
        y = part if y is None else y + part
    return y


def _ffn_kernel(x_ref, gpre_ref, wg_ref, wu_ref, wd_ref, gpost_ref, o_ref):
    x = x_ref[...]
    hn = _rms_rows(x, gpre_ref[...]).astype(BF16)
    o_ref[...] = x + _rms_rows(_swiglu(hn, wg_ref, wu_ref, wd_ref, FFN_TF), gpost_ref[...])


def _ffn(x2d, gpre, wg, wu, wd, gpost):
    T, D = x2d.shape
    tm = min(FFN_TM, T)
    xspec = pl.BlockSpec((tm, D), lambda i: (i, 0))
    return pl.pallas_call(
        _ffn_kernel, grid=(T // tm,),
        in_specs=[xspec, _const_spec((1, D)), _const_spec(wg.shape), _const_spec(wu.shape),
                  _const_spec(wd.shape), _const_spec((1, D))],
        out_specs=xspec, out_shape=jax.ShapeDtypeStruct(x2d.shape, x2d.dtype),
        compiler_params=_cparams(("parallel",)), name="ffn",
    )(x2d, gpre, wg, wu, wd, gpost)


def _router_kernel(x_ref, gpre_ref, wr_ref, br_ref, tri_ref,
                   hn_ref, pos_ref, gate_col_ref, pos_col_ref, cnt_ref, run_ref):
    @pl.when(pl.program_id(0) == 0)
    def _():
        run_ref[...] = jnp.zeros_like(run_ref)

    hn = _rms_rows(x_ref[...], gpre_ref[...])
    hn_ref[...] = hn.astype(hn_ref.dtype)
    logits = lax.dot_general(wr_ref[...], hn, (((1,), (1,)), ((), ())),
                             precision=lax.Precision.HIGHEST,
                             preferred_element_type=F32) + br_ref[...]
    e_idx = lax.broadcasted_iota(jnp.int32, logits.shape, 0)
    m1 = jnp.max(logits, axis=0, keepdims=True)
    i1 = jnp.min(jnp.where(logits == m1, e_idx, N_EXPERTS), axis=0, keepdims=True)
    rest = jnp.where(e_idx == i1, -jnp.inf, logits)
    m2 = jnp.max(rest, axis=0, keepdims=True)
    i2 = jnp.min(jnp.where(rest == m2, e_idx, N_EXPERTS), axis=0, keepdims=True)
    e2 = jnp.exp(m2 - m1)
    w1 = 1.0 / (1.0 + e2)
    w2 = e2 / (1.0 + e2)
    gates = jnp.where(e_idx == i1, w1, 0.0) + jnp.where(e_idx == i2, w2, 0.0)

    sel = (e_idx == i1) | (e_idx == i2)
    sel_f = jnp.where(sel, 1.0, 0.0)
    before = _dot(sel_f.astype(BF16), tri_ref[...])
    run = run_ref[:, 0:1]
    pos = jnp.where(sel, run + before, -1.0)
    pos_ref[...] = pos
    tm = pos.shape[1]
    cols = jnp.concatenate([gates, pos, jnp.zeros((128 - 2 * N_EXPERTS, tm), F32)], axis=0).T
    for e in range(N_EXPERTS):
        gate_col_ref[e] = cols[:, e:e + 1]
        pos_col_ref[e] = cols[:, N_EXPERTS + e:N_EXPERTS + e + 1]
    cnt = jnp.sum(sel_f, axis=1, keepdims=True)
    cnt_ref[0] = cnt.astype(jnp.int32)
    run_ref[...] = run_ref[...] + cnt


def _router(x2d, gpre, wr_t, br_col):
    T, D = x2d.shape
    tm = MOE_BLOCK
    nb = T // tm
    tri = (np.arange(tm)[:, None] < np.arange(tm)[None, :]).astype(np.float32)
    col = pl.BlockSpec((N_EXPERTS, tm, 1), lambda i: (0, i, 0))
    return pl.pallas_call(
        _router_kernel, grid=(nb,),
        in_specs=[pl.BlockSpec((tm, D), lambda i: (i, 0)), _const_spec((1, D)),
                  _const_spec(wr_t.shape), _const_spec(br_col.shape), _const_spec((tm, tm))],
        out_specs=[pl.BlockSpec((tm, D), lambda i: (i, 0)),
                   pl.BlockSpec((N_EXPERTS, tm), lambda i: (0, i)), col, col,
                   pl.BlockSpec((1, N_EXPERTS, 1), lambda i: (i, 0, 0))],
        out_shape=[jax.ShapeDtypeStruct((T, D), BF16),
                   jax.ShapeDtypeStruct((N_EXPERTS, T), F32),
                   jax.ShapeDtypeStruct((N_EXPERTS, T, 1), F32),
                   jax.ShapeDtypeStruct((N_EXPERTS, T, 1), F32),
                   jax.ShapeDtypeStruct((nb, N_EXPERTS, 1), jnp.int32)],
        scratch_shapes=[pltpu.VMEM((N_EXPERTS, 128), F32)],
        compiler_params=_cparams(("arbitrary",)), name="router",
    )(x2d, gpre, wr_t, br_col, jnp.asarray(tri, BF16))


_FIRST, _LAST = 1, 2
_HALF = (4, 8)


def _moe_schedule_kernel(cnt_ref, ea_ref, ba_ref, ta_ref, ra_ref, fa_ref, eb_ref, bb_ref, tb_ref, rb_ref, fb_ref,
                         tile0_ref, start_ref, *, nb, ne):
    tr = MOE_BLOCK
    shift = tr.bit_length() - 1
    assert tr == 1 << shift
    g_max = ea_ref.shape[0]

    def first_tiles(e, tile):
        rows = lax.fori_loop(0, nb, lambda b, acc: acc + cnt_ref[b * ne + e], 0)
        tile0_ref[e] = tile
        start_ref[e] = 0
        return tile + ((rows + tr - 1) >> shift)

    lax.fori_loop(0, ne, first_tiles, 0)

    def append(lists, by_tile, e, b, start, carry):
        e_ref, b_ref, t_ref, r_ref, f_ref = lists
        c = cnt_ref[b * ne + e]
        first = start >> shift
        n = jnp.where(c > 0, ((start + c - 1) >> shift) - first + 1, 0)

        def visit(j, carry):
            g, prev_group = carry
            local = first + j
            tile = tile0_ref[e] + local
            group = tile if by_tile else b
            lo = start - (local << shift)
            opens = group != prev_group

            @pl.when(opens & (g > 0))
            def _():
                f_ref[g - 1] = f_ref[g - 1] | _LAST

            e_ref[g] = e
            b_ref[g] = b
            t_ref[g] = tile
            r_ref[g] = local << shift
            f_ref[g] = (jnp.where(opens, _FIRST, 0) | jnp.where(lo < tr // 2, _HALF[0], 0)
                        | jnp.where(lo + c > tr // 2, _HALF[1], 0))
            return g + 1, group

        return lax.fori_loop(0, n, visit, carry)

    def finish(lists, g):
        e_ref, b_ref, t_ref, r_ref, f_ref = lists
        f_ref[g - 1] = f_ref[g - 1] | _LAST

        def pad(k, _):
            e_ref[k] = e_ref[g - 1]
            b_ref[k] = b_ref[g - 1]
            t_ref[k] = t_ref[g - 1]
            r_ref[k] = r_ref[g - 1]
            f_ref[k] = 0
            return 0

        lax.fori_loop(g, g_max, pad, 0)

    list_a = (ea_ref, ba_ref, ta_ref, ra_ref, fa_ref)
    list_b = (eb_ref, bb_ref, tb_ref, rb_ref, fb_ref)

    def expert_major(e, carry):
        def block(b, inner):
            start, visits = inner
            return start + cnt_ref[b * ne + e], append(list_a, True, e, b, start, visits)
        return lax.fori_loop(0, nb, block, (0, carry))[1]

    g_a, _ = lax.fori_loop(0, ne, expert_major, (0, -1))
    finish(list_a, g_a)

    def block_major(b, carry):
        def expert(e, visits):
            start = start_ref[e]
            start_ref[e] = start + cnt_ref[b * ne + e]
            return append(list_b, False, e, b, start, visits)
        return lax.fori_loop(0, ne, expert, carry)

    g_b, _ = lax.fori_loop(0, nb, block_major, (0, -1))
    finish(list_b, g_b)


def _moe_schedule(cnt, nb, ne, g_max):
    smem = pl.BlockSpec(memory_space=pltpu.SMEM)
    out = jax.ShapeDtypeStruct((g_max,), jnp.int32)
    lists = pl.pallas_call(
        functools.partial(_moe_schedule_kernel, nb=nb, ne=ne),
        in_specs=[smem], out_specs=[smem] * 10, out_shape=[out] * 10,
        scratch_shapes=[pltpu.SMEM((ne,), jnp.int32), pltpu.SMEM((ne,), jnp.int32)],
        name="moe_schedule",
    )(cnt)
    return lists[:5], lists[5:]


def _moe_expert_kernel(e_ref, b_ref, t_ref, r_ref, f_ref, hn_ref, pos_ref, wg_ref, wu_ref, wd_ref,
                       ys_ref, xs_ref):
    g = pl.program_id(0)
    flags = f_ref[g]
    tr = xs_ref.shape[0]

    @pl.when((flags & _FIRST) != 0)
    def _():
        xs_ref[...] = jnp.zeros_like(xs_ref)

    half = tr // 2
    for h in range(2):
        @pl.when((flags & _HALF[h]) != 0)
        def _():
            rows = (lax.broadcasted_iota(jnp.int32, (half, 1), 0) + (r_ref[g] + h * half)).astype(F32)
            onehot = jnp.where(rows == pos_ref[0, 0], 1.0, 0.0).astype(BF16)
            xs_ref[h * half:(h + 1) * half, :] += _dot(onehot, hn_ref[...])

    @pl.when((flags & _LAST) != 0)
    def _():
        y = _swiglu(xs_ref[...].astype(BF16), wg_ref.at[0], wu_ref.at[0], wd_ref.at[0], FFN_TF)
        ys_ref[...] = y.astype(ys_ref.dtype)


def _moe_expert(list_a, hn, pos, wg, wu, wd, n_tiles):
    T, D = hn.shape
    E, _, F = wg.shape
    tr = MOE_BLOCK
    g_max = list_a[0].shape[0]
    pos4 = pos.reshape(E, T // tr, 1, tr)
    wspec_in = pl.BlockSpec((1, D, F), lambda g, e, b, t, r, f: (e[g], 0, 0), pipeline_mode=pl.Buffered(1))
    wspec_out = pl.BlockSpec((1, F, D), lambda g, e, b, t, r, f: (e[g], 0, 0), pipeline_mode=pl.Buffered(1))
    grid_spec = pltpu.PrefetchScalarGridSpec(
        num_scalar_prefetch=5, grid=(g_max,),
        in_specs=[pl.BlockSpec((tr, D), lambda g, e, b, t, r, f: (b[g], 0)),
                  pl.BlockSpec((1, 1, 1, tr), lambda g, e, b, t, r, f: (e[g], b[g], 0, 0)),
                  wspec_in, wspec_in, wspec_out],
        out_specs=pl.BlockSpec((tr, D), lambda g, e, b, t, r, f: (t[g], 0)),
        scratch_shapes=[pltpu.VMEM((tr, D), F32)])
    return pl.pallas_call(
        _moe_expert_kernel, grid_spec=grid_spec,
        out_shape=jax.ShapeDtypeStruct((n_tiles * tr, D), BF16),
        compiler_params=_cparams(("arbitrary",)), name="moe_expert",
    )(*list_a, hn, pos4, wg, wu, wd)


def _moe_combine_kernel(e_ref, b_ref, t_ref, r_ref, f_ref, x_ref, ys_ref, pos_ref, gate_ref, gpost_ref,
                        o_ref, acc_ref):
    g = pl.program_id(0)
    flags = f_ref[g]
    tr = ys_ref.shape[0]

    @pl.when((flags & _FIRST) != 0)
    def _():
        acc_ref[...] = jnp.zeros_like(acc_ref)

    half = tr // 2
    for h in range(2):
        @pl.when((flags & _HALF[h]) != 0)
        def _():
            rows = (lax.broadcasted_iota(jnp.int32, (1, half), 1) + (r_ref[g] + h * half)).astype(F32)
            onehot = jnp.where(pos_ref[0] == rows, 1.0, 0.0).astype(BF16)
            acc_ref[...] += gate_ref[0] * _dot(onehot, ys_ref[h * half:(h + 1) * half, :])

    @pl.when((flags & _LAST) != 0)
    def _():
        o_ref[...] = x_ref[...] + _rms_rows(acc_ref[...], gpost_ref[...])


def _moe_combine(list_b, x2d, ys, pos_col, gate_col, gpost):
    T, D = x2d.shape
    tr = MOE_BLOCK
    g_max = list_b[0].shape[0]
    xspec = pl.BlockSpec((tr, D), lambda g, e, b, t, r, f: (b[g], 0))
    colspec = pl.BlockSpec((1, tr, 1), lambda g, e, b, t, r, f: (e[g], b[g], 0))
    grid_spec = pltpu.PrefetchScalarGridSpec(
        num_scalar_prefetch=5, grid=(g_max,),
        in_specs=[xspec, pl.BlockSpec((tr, D), lambda g, e, b, t, r, f: (t[g], 0)), colspec, colspec,
                  pl.BlockSpec((1, D), lambda g, e, b, t, r, f: (0, 0))],
        out_specs=xspec,
        scratch_shapes=[pltpu.VMEM((tr, D), F32)])
    return pl.pallas_call(
        _moe_combine_kernel, grid_spec=grid_spec,
        out_shape=jax.ShapeDtypeStruct(x2d.shape, x2d.dtype),
        compiler_params=_cparams(("arbitrary",)), name="moe_combine",
    )(*list_b, x2d, ys, pos_col, gate_col, gpost)


def _moe(x2d, gpre, wr_t, br_col, wg, wu, wd, gpost):
    T, D = x2d.shape
    E = wg.shape[0]
    nb = T // MOE_BLOCK
    n_tiles = TOP_K * nb + E
    g_max = E * nb + n_tiles
    hn, pos, gate_col, pos_col, cnt = _router(x2d, gpre, wr_t, br_col)
    list_a, list_b = _moe_schedule(cnt.reshape(nb * E), nb, E, g_max)
    ys = _moe_expert(list_a, hn, pos, wg, wu, wd, n_tiles)
    return _moe_combine(list_b, x2d, ys, pos_col, gate_col, gpost)


def _feature_row_order():
    def diff_block(base):
        x1, x2, rest = [], [], []
        half = DIFF_ROPE_DIM // 2
        for h in range(DIFF_HEADS):
            for c in range(2):
                lo = base + h * 2 * DIFF_QK_DIM + c * DIFF_QK_DIM
                x1 += range(lo, lo + half)
                x2 += range(lo + half, lo + DIFF_ROPE_DIM)
                rest += range(lo + DIFF_ROPE_DIM, lo + DIFF_QK_DIM)
        return x1 + x2 + rest

    order = list(range(0, _S_DQ)) + diff_block(_S_DQ) + diff_block(_S_DK) + list(range(_S_DV, N_FEAT))
    return np.asarray(order, np.int32)


def _mla_q_up_order():
    w = MLA_NOPE_DIM + MLA_ROPE_DIM
    half = MLA_ROPE_DIM // 2
    nope = [h * w + j for h in range(MLA_HEADS) for j in range(MLA_NOPE_DIM)]
    x1 = [h * w + MLA_NOPE_DIM + j for h in range(MLA_HEADS) for j in range(half)]
    x2 = [h * w + MLA_NOPE_DIM + half + j for h in range(MLA_HEADS) for j in range(half)]
    return np.asarray(nope + x1 + x2, np.int32)


def _mla_kv_up_order():
    w = MLA_NOPE_DIM + MLA_V_DIM
    k = [h * w + j for h in range(MLA_HEADS) for j in range(MLA_NOPE_DIM)]
    v = [h * w + MLA_NOPE_DIM + j for h in range(MLA_HEADS) for j in range(MLA_V_DIM)]
    return np.asarray(k + v, np.int32)


def _rope_tables(S):
    def cos_sin(pos, rot_dim):
        inv = ROPE_THETA ** (-jnp.arange(0, rot_dim, 2, dtype=F32) / rot_dim)
        ang = pos.astype(F32)[:, None] * inv[None, :]
        return jnp.cos(ang).T, jnp.sin(ang).T

    rows = S // GRID_W
    pos = jnp.arange(S)
    row = jnp.repeat(jnp.arange(rows), GRID_W)
    col = jnp.tile(jnp.arange(GRID_W), rows)
    cm, sm = cos_sin(pos, MLA_ROPE_DIM)
    cr, sr = cos_sin(row, GQA_HEAD_DIM // 2)
    cc, sc = cos_sin(col, GQA_HEAD_DIM // 2)
    cp, sp = cos_sin(pos, DIFF_ROPE_DIM)
    return (jnp.tile(cm, (MLA_HEADS, 1)), jnp.tile(sm, (MLA_HEADS, 1)),
            jnp.concatenate([cr, cc], axis=0), jnp.concatenate([sr, sc], axis=0),
            jnp.tile(cp, (2 * DIFF_HEADS, 1)), jnp.tile(sp, (2 * DIFF_HEADS, 1)))


def _row(v):
    return v.reshape(1, -1).astype(F32)


def _col(v):
    return v.reshape(-1, 1).astype(F32)


def kernel(x, mem, mix_pre_g, mix_post_g, w_in, mla_q_norm_g, mla_w_q_up, mla_kv_norm_g, mla_w_kv_up,
           gqa_q_norm_g, gqa_k_norm_g, diff_lambda_q1, diff_lambda_k1, diff_lambda_q2, diff_lambda_k2,
           diff_subln_g, w_branch, w_out, mem_pre_g, mem_post_g, mem_norm_g, mem_wq, mem_wkv, mem_wo,
           ffn_pre_g, ffn_post_g, dense_w_gate, dense_w_up, dense_w_down, moe_w_router, moe_b_router,
           moe_w_gate, moe_w_up, moe_w_down):
    B, S, D = x.shape
    M = mem.shape[1]
    depth = w_in.shape[0]
    tables = _rope_tables(S)
    feat_order = _feature_row_order()
    q_up_order = _mla_q_up_order()
    kv_up_order = _mla_kv_up_order()

    for layer in range(depth):
        lambda_init = 0.8 - 0.6 * math.exp(-0.3 * layer)

        w_l = w_in[layer]
        wt = w_l[:, feat_order].T.astype(BF16)
        wg = w_l[:, N_FEAT:].astype(BF16)
        wqu = mla_w_q_up[layer][:, q_up_order].T.astype(BF16)
        wkvu = mla_w_kv_up[layer][:, kv_up_order].T.astype(BF16)
        (qm, km, vm, qg, kg, vg, qd, kd, vd, gates) = _mixer_prep(
            x, _row(mix_pre_g[layer]), wt, wg, wqu, wkvu,
            _col(mla_q_norm_g[layer]), _col(mla_kv_norm_g[layer]),
            _col(gqa_q_norm_g[layer]), _col(gqa_k_norm_g[layer]), tables)
        a_t = _attention(qm, km, vm, kv_of_head=lambda h: h, kcol_of_head=lambda h: h, name="mla_attn")
        b_t = _attention(qg, kg, vg, kv_of_head=lambda h: h // GQA_GROUP, kcol_of_head=lambda h: 0,
                         name="gqa_attn")
        c_t = _diff_attention(qd, kd, vd, _row(diff_lambda_q1[layer]), _row(diff_lambda_k1[layer]),
                              _row(diff_lambda_q2[layer]), _row(diff_lambda_k2[layer]),
                              _col(diff_subln_g[layer]), lambda_init)
        x = _mixer_merge(x, a_t, b_t, c_t, gates, w_branch[layer].astype(BF16),
                         w_out[layer].astype(BF16), _row(mix_post_g[layer]))

        kv = _norm_matmul(mem.reshape(B * M, D), _row(mem_norm_g[layer]), mem_wkv[layer].astype(BF16), BF16)
        x = _mem_attention(x, kv.reshape(B, M, 2 * D), _row(mem_pre_g[layer]), mem_wq[layer].astype(BF16),
                           mem_wo[layer].astype(BF16), _row(mem_post_g[layer]))

        x2d = x.reshape(B * S, D)
        i = layer // 2
        if layer % 2 == 0:
            x2d = _ffn(x2d, _row(ffn_pre_g[layer]), dense_w_gate[i].astype(BF16), dense_w_up[i].astype(BF16),
                       dense_w_down[i].astype(BF16), _row(ffn_post_g[layer]))
        else:
            x2d = _moe(x2d, _row(ffn_pre_g[layer]), moe_w_router[i].T.astype(F32), _col(moe_b_router[i]),
                       moe_w_gate[i].astype(BF16), moe_w_up[i].astype(BF16), moe_w_down[i].astype(BF16),
                       _row(ffn_post_g[layer]))
        x = x2d.reshape(B, S, D)
    return x
```

```python
import functools
import math

import numpy as np
import jax
import jax.numpy as jnp
from jax import lax
from jax.experimental import pallas as pl
from jax.experimental.pallas import tpu as pltpu

F32 = jnp.float32
BF16 = jnp.bfloat16

D_MODEL = 1024
GRID_W = 64
ROPE_THETA = 500000.0
NORM_EPS = 1e-6

MLA_HEADS = 8
MLA_Q_LORA = 384
MLA_KV_LORA = 256
MLA_NOPE_DIM = 64
MLA_ROPE_DIM = 32
MLA_V_DIM = 64

GQA_Q_HEADS = 8
GQA_KV_HEADS = 2
GQA_GROUP = GQA_Q_HEADS // GQA_KV_HEADS
GQA_HEAD_DIM = 64

DIFF_HEADS = 8
DIFF_QK_DIM = 32
DIFF_V_DIM = 2 * DIFF_QK_DIM
DIFF_ROPE_DIM = DIFF_QK_DIM // 4

N_BRANCHES = 3
BRANCH_WIDTH = 512
HEAD_V = 64

MEM_HEADS = 4
MEM_HEAD_DIM = D_MODEL // MEM_HEADS

D_FF = 2816
N_EXPERTS = 8
TOP_K = 2

IN_SPLITS = (MLA_Q_LORA, MLA_KV_LORA, MLA_ROPE_DIM,
             GQA_Q_HEADS * GQA_HEAD_DIM, GQA_KV_HEADS * GQA_HEAD_DIM, GQA_KV_HEADS * GQA_HEAD_DIM,
             DIFF_HEADS * 2 * DIFF_QK_DIM, DIFF_HEADS * 2 * DIFF_QK_DIM, DIFF_HEADS * DIFF_V_DIM,
             N_BRANCHES * D_MODEL)
IN_OFFS = tuple(int(v) for v in np.cumsum((0,) + IN_SPLITS))
N_FEAT = IN_OFFS[9]
N_GATE = IN_SPLITS[9]

LOG2E = 1.4426950408889634
V7X_LANES = 128
QK_PAD = V7X_LANES
ONES_ROWS = 16

V7X_VMEM_LIMIT_BYTES = 56 * 1024 * 1024

PREP_TM = 512
ATTN_TQ = 1024
ATTN_TK = 256
ATTN_UNROLL = 32
MERGE_TM = 512
MEM_TM = 1024
FFN_TM = 512
FFN_TF = 2816
MOE_BLOCK = 512
KV_TM = 256


def _cparams(sem):
    return pltpu.CompilerParams(dimension_semantics=sem, vmem_limit_bytes=V7X_VMEM_LIMIT_BYTES)


def _const_spec(shape):
    nd = len(shape)
    return pl.BlockSpec(shape, lambda *_: (0,) * nd, pipeline_mode=pl.Buffered(1))


def _rms_rows(x, g_row):
    ms = jnp.mean(x * x, axis=-1, keepdims=True)
    return x * lax.rsqrt(ms + NORM_EPS) * g_row


def _rms_cols(x, g_col):
    ms = jnp.mean(x * x, axis=0, keepdims=True)
    return x * lax.rsqrt(ms + NORM_EPS) * g_col


def _sigmoid(x):
    return 0.5 * jnp.tanh(0.5 * x) + 0.5


def _dot(a, b):
    return jnp.dot(a, b, preferred_element_type=F32)


def _dot_nt(a, b):
    return lax.dot_general(a, b, (((1,), (1,)), ((), ())), preferred_element_type=F32)


_S_CQ, _S_CKV, _S_KR, _S_GQ, _S_GK, _S_GV, _S_DQ, _S_DK, _S_DV = IN_OFFS[:9]


def _rope_rows(x1, x2, c, s):
    return x1 * c - x2 * s, x2 * c + x1 * s


def _mixer_prep_kernel(x_ref, g_ref, wt_ref, wg_ref, wqu_ref, wkvu_ref,
                       mqg_ref, mkvg_ref, gqg_ref, gkg_ref,
                       cm_ref, sm_ref, ca_ref, sa_ref, cp_ref, sp_ref,
                       qm_ref, km_ref, vm_ref, qg_ref, kg_ref, vg_ref,
                       qd_ref, kd_ref, vd_ref, gate_ref):
    tm = x_ref.shape[1]
    hn = _rms_rows(x_ref[0], g_ref[...]).astype(BF16)

    z_all = _dot_nt(wt_ref[...], hn)

    def feat(lo, n):
        return z_all[lo:lo + n]

    gate_ref[0] = _sigmoid(_dot(hn, wg_ref[...])).astype(gate_ref.dtype)

    zeros32 = jnp.zeros((32, tm), F32)
    zeros64 = jnp.zeros((64, tm), F32)

    c_mla = (MLA_NOPE_DIM + MLA_ROPE_DIM) ** -0.5 * LOG2E
    cm = cm_ref[...]
    sm = sm_ref[...]
    cqn = _rms_cols(feat(_S_CQ, MLA_Q_LORA), mqg_ref[...]).astype(BF16)
    q_all = _dot(wqu_ref[...], cqn)
    q_nope = q_all[0:512] * c_mla
    q_r1, q_r2 = _rope_rows(q_all[512:640], q_all[640:768], cm, sm)
    q_r1 = q_r1 * c_mla
    q_r2 = q_r2 * c_mla
    for h in range(MLA_HEADS):
        qm_ref[0, h] = jnp.concatenate(
            [q_nope[64 * h:64 * h + 64], q_r1[16 * h:16 * h + 16], q_r2[16 * h:16 * h + 16], zeros32],
            axis=0).astype(qm_ref.dtype)

    ckvn = _rms_cols(feat(_S_CKV, MLA_KV_LORA), mkvg_ref[...]).astype(BF16)
    kv_all = _dot(wkvu_ref[...], ckvn)
    vm_ref[0] = kv_all[512:1024].astype(vm_ref.dtype)
    kr = feat(_S_KR, MLA_ROPE_DIM)
    k_r1, k_r2 = _rope_rows(kr[0:16], kr[16:32], cm[0:16], sm[0:16])
    for h in range(MLA_HEADS):
        kt = jnp.concatenate([kv_all[64 * h:64 * h + 64], k_r1, k_r2, zeros32], axis=0)
        km_ref[0, :, QK_PAD * h:QK_PAD * (h + 1)] = kt.T.astype(km_ref.dtype)

    c_gqa = GQA_HEAD_DIM ** -0.5 * LOG2E
    ca = ca_ref[...]
    sa = sa_ref[...]
    gq = feat(_S_GQ, GQA_Q_HEADS * GQA_HEAD_DIM)
    for h in range(GQA_Q_HEADS):
        qn = _rms_cols(gq[64 * h:64 * h + 64], gqg_ref[...])
        r1, r2 = _rope_rows(qn[0:32], qn[32:64], ca, sa)
        q64 = jnp.concatenate([r1, r2], axis=0) * c_gqa
        parts = [q64, zeros64] if h // GQA_GROUP == 0 else [zeros64, q64]
        qg_ref[0, h] = jnp.concatenate(parts, axis=0).astype(qg_ref.dtype)
    gk = feat(_S_GK, GQA_KV_HEADS * GQA_HEAD_DIM)
    kparts = []
    for g in range(GQA_KV_HEADS):
        kn = _rms_cols(gk[64 * g:64 * g + 64], gkg_ref[...])
        r1, r2 = _rope_rows(kn[0:32], kn[32:64], ca, sa)
        kparts += [r1, r2]
    kg_ref[0] = jnp.concatenate(kparts, axis=0).T.astype(kg_ref.dtype)
    vg_ref[0] = feat(_S_GV, GQA_KV_HEADS * GQA_HEAD_DIM).astype(vg_ref.dtype)

    c_diff = DIFF_QK_DIM ** -0.5 * LOG2E
    cp = cp_ref[...]
    sp = sp_ref[...]
    row = lax.broadcasted_iota(jnp.int32, (64, 1), 0)
    in_c0 = (row < 4) | ((row >= 8) & (row < 12)) | ((row >= 16) & (row < 40))

    def diff_heads(lo):
        z = feat(lo, DIFF_HEADS * 2 * DIFF_QK_DIM)
        r1, r2 = _rope_rows(z[0:64], z[64:128], cp, sp)
        rest = z[128:512]
        return [jnp.concatenate([r1[8 * h:8 * h + 8], r2[8 * h:8 * h + 8], rest[48 * h:48 * h + 48]], axis=0)
                for h in range(DIFF_HEADS)]

    for h, q64 in enumerate(diff_heads(_S_DQ)):
        q64 = q64 * c_diff
        for c in range(2):
            qc = jnp.where(in_c0 if c == 0 else jnp.logical_not(in_c0), q64, 0.0)
            parts = [qc, zeros64] if h % 2 == 0 else [zeros64, qc]
            qd_ref[0, 2 * h + c] = jnp.concatenate(parts, axis=0).astype(qd_ref.dtype)
    k_heads = diff_heads(_S_DK)
    for p in range(DIFF_HEADS // 2):
        kt = jnp.concatenate([k_heads[2 * p], k_heads[2 * p + 1]], axis=0)
        kd_ref[0, :, QK_PAD * p:QK_PAD * (p + 1)] = kt.T.astype(kd_ref.dtype)
    vd_ref[0] = feat(_S_DV, DIFF_HEADS * DIFF_V_DIM).astype(vd_ref.dtype)


def _mixer_prep(x, g, wt, wg, wqu, wkvu, mqg, mkvg, gqg, gkg, tables):
    B, S, D = x.shape
    tm = min(PREP_TM, S)
    cm, sm, ca, sa, cp, sp = tables
    grid = (B, S // tm)

    def tok(width):
        return pl.BlockSpec((1, tm, width), lambda b, i: (b, i, 0))

    def featm(rows):
        return pl.BlockSpec((1, rows, tm), lambda b, i: (b, 0, i))

    def heads(n):
        return pl.BlockSpec((1, n, QK_PAD, tm), lambda b, i: (b, 0, 0, i))

    def table(rows):
        return pl.BlockSpec((rows, tm), lambda b, i: (0, i))

    in_specs = [
        pl.BlockSpec((1, tm, D), lambda b, i: (b, i, 0)),
        _const_spec((1, D)),
        _const_spec(wt.shape), _const_spec(wg.shape), _const_spec(wqu.shape), _const_spec(wkvu.shape),
        _const_spec(mqg.shape), _const_spec(mkvg.shape), _const_spec(gqg.shape), _const_spec(gkg.shape),
        table(128), table(128), table(32), table(32), table(64), table(64),
    ]
    out_shape = [
        jax.ShapeDtypeStruct((B, MLA_HEADS, QK_PAD, S), BF16),
        jax.ShapeDtypeStruct((B, S, MLA_HEADS * QK_PAD), BF16),
        jax.ShapeDtypeStruct((B, MLA_HEADS * HEAD_V, S), BF16),
        jax.ShapeDtypeStruct((B, GQA_Q_HEADS, QK_PAD, S), BF16),
        jax.ShapeDtypeStruct((B, S, QK_PAD), BF16),
        jax.ShapeDtypeStruct((B, GQA_KV_HEADS * HEAD_V, S), BF16),
        jax.ShapeDtypeStruct((B, 2 * DIFF_HEADS, QK_PAD, S), BF16),
        jax.ShapeDtypeStruct((B, S, DIFF_HEADS // 2 * QK_PAD), BF16),
        jax.ShapeDtypeStruct((B, DIFF_HEADS * HEAD_V, S), BF16),
        jax.ShapeDtypeStruct((B, S, N_GATE), BF16),
    ]
    out_specs = [
        heads(MLA_HEADS), tok(MLA_HEADS * QK_PAD), featm(MLA_HEADS * HEAD_V),
        heads(GQA_Q_HEADS), tok(QK_PAD), featm(GQA_KV_HEADS * HEAD_V),
        heads(2 * DIFF_HEADS), tok(DIFF_HEADS // 2 * QK_PAD), featm(DIFF_HEADS * HEAD_V),
        tok(N_GATE),
    ]
    return pl.pallas_call(
        _mixer_prep_kernel, grid=grid, in_specs=in_specs, out_specs=out_specs, out_shape=out_shape,
        compiler_params=_cparams(("parallel", "parallel")), name="mixer_prep",
    )(x, g, wt, wg, wqu, wkvu, mqg, mkvg, gqg, gkg, cm, sm, ca, sa, cp, sp)


def _online_softmax_attention(q_t, load_q_next, k_ref, v_ref, s_ref, cmax_ref, qq_ref, *, tk):
    S = k_ref.shape[1]
    lanes = q_t.shape[1]
    n_chunks = S // tk
    unroll = min(ATTN_UNROLL, n_chunks)
    assert unroll % 2 == 0 and n_chunks % unroll == 0
    ones = jnp.ones((ONES_ROWS, tk), BF16)

    def scores(c, slot, q=q_t):
        s_t = _dot(k_ref[0, pl.ds(pl.multiple_of(c * tk, tk), tk), :], q)
        s_ref[slot] = s_t
        return jnp.max(s_t, axis=0, keepdims=True)

    def update(c, slot, col_max, m, acc):
        v_aug = jnp.concatenate([v_ref[0, :, pl.ds(pl.multiple_of(c * tk, tk), tk)], ones], axis=0)
        m_new = jnp.maximum(m, col_max)
        p = jnp.exp2(s_ref[slot] - m_new).astype(BF16)
        return m_new, acc * jnp.exp2(m - m_new) + _dot(v_aug, p)

    n_trips = n_chunks // unroll
    qq_ref[0] = q_t
    qq_ref[1] = load_q_next()

    def body(t, carry):
        cmax, m, acc = carry
        c = unroll * t
        last = t == n_trips - 1
        for u in range(unroll):
            if u == unroll - 1:
                if isinstance(last, bool):
                    chunk, tile = (0, 1) if last else (c + unroll, 0)
                else:
                    chunk, tile = jnp.where(last, 0, c + unroll), jnp.where(last, 1, 0)
                cmax_next = scores(chunk, 0, qq_ref[tile])
            else:
                cmax_next = scores(c + u + 1, (u + 1) % 2)
            m, acc = update(c + u, u % 2, cmax, m, acc)
            cmax = cmax_next
        return cmax, m, acc

    @pl.when(pl.program_id(2) == 0)
    def _():
        cmax_ref[...] = jnp.broadcast_to(scores(0, 0), cmax_ref.shape)

    cmax0 = jnp.max(cmax_ref[...], axis=0, keepdims=True)
    m0 = jnp.full((1, lanes), -jnp.inf, F32)
    acc0 = jnp.zeros((HEAD_V + ONES_ROWS, lanes), F32)
    carry = (cmax0, m0, acc0)
    cmax_next, _, acc = body(0, carry) if n_trips == 1 else lax.fori_loop(0, n_trips, body, carry)
    cmax_ref[...] = jnp.broadcast_to(cmax_next, cmax_ref.shape)
    return acc[:HEAD_V] / acc[HEAD_V:HEAD_V + 1]


def _attn_kernel(q_ref, qn_ref, k_ref, v_ref, o_ref, s_ref, cmax_ref, qq_ref, *, tk):
    o = _online_softmax_attention(q_ref[0, 0], lambda: qn_ref[0, 0], k_ref, v_ref, s_ref, cmax_ref, qq_ref,
                                  tk=tk)
    o_ref[0] = o.astype(o_ref.dtype)


def _next_tile(n_tiles):
    return lambda i: jnp.minimum(i + 1, n_tiles - 1)


def _attention(q_t, k, v_t, *, kv_of_head, kcol_of_head, name):
    B, H, _, S = q_t.shape
    tq = min(ATTN_TQ, S)
    tk = min(ATTN_TK, S // 2)
    nxt = _next_tile(S // tq)
    return pl.pallas_call(
        functools.partial(_attn_kernel, tk=tk),
        grid=(B, H, S // tq),
        in_specs=[
            pl.BlockSpec((1, 1, QK_PAD, tq), lambda b, h, i: (b, h, 0, i)),
            pl.BlockSpec((1, 1, QK_PAD, tq), lambda b, h, i: (b, h, 0, nxt(i))),
            pl.BlockSpec((1, S, QK_PAD), lambda b, h, i: (b, 0, kcol_of_head(h))),
            pl.BlockSpec((1, HEAD_V, S), lambda b, h, i: (b, kv_of_head(h), 0)),
        ],
        out_specs=pl.BlockSpec((1, HEAD_V, tq), lambda b, h, i: (b, h, i)),
        out_shape=jax.ShapeDtypeStruct((B, H * HEAD_V, S), F32),
        scratch_shapes=[pltpu.VMEM((2, tk, tq), F32), pltpu.VMEM((8, tq), F32),
                        pltpu.VMEM((2, QK_PAD, tq), BF16)],
        compiler_params=_cparams(("parallel", "parallel", "arbitrary")), name=name,
    )(q_t, q_t, k, v_t)


def _diff_attn_kernel(lq1_ref, lk1_ref, lq2_ref, lk2_ref, g_ref, q_ref, qn_ref, k_ref, v_ref, o_ref,
                      s_ref, cmax_ref, qq_ref, *, tk, lambda_init):
    tq = q_ref.shape[3]
    q_both = jnp.concatenate([q_ref[0, 0], q_ref[0, 1]], axis=1)
    o_maps = _online_softmax_attention(
        q_both, lambda: jnp.concatenate([qn_ref[0, 0], qn_ref[0, 1]], axis=1), k_ref, v_ref, s_ref, cmax_ref,
        qq_ref, tk=tk)
    lam = (jnp.exp(jnp.sum(lq1_ref[...] * lk1_ref[...], axis=-1, keepdims=True))
           - jnp.exp(jnp.sum(lq2_ref[...] * lk2_ref[...], axis=-1, keepdims=True)) + lambda_init)
    o = o_maps[:, :tq] - lam * o_maps[:, tq:]
    o = _rms_cols(o, g_ref[...]) * (1.0 - lambda_init)
    o_ref[0] = o.astype(o_ref.dtype)


def _diff_attention(q_t, k, v_t, lq1, lk1, lq2, lk2, subln_g, lambda_init):
    B, H2, _, S = q_t.shape
    H = H2 // 2
    tq = min(ATTN_TQ // 2, S)
    tk = min(ATTN_TK, S // 2)
    nxt = _next_tile(S // tq)
    vec = _const_spec((1, DIFF_QK_DIM))
    return pl.pallas_call(
        functools.partial(_diff_attn_kernel, tk=tk, lambda_init=lambda_init),
        grid=(B, H, S // tq),
        in_specs=[
            vec, vec, vec, vec, _const_spec((DIFF_V_DIM, 1)),
            pl.BlockSpec((1, 2, QK_PAD, tq), lambda b, h, i: (b, h, 0, i)),
            pl.BlockSpec((1, 2, QK_PAD, tq), lambda b, h, i: (b, h, 0, nxt(i))),
            pl.BlockSpec((1, S, QK_PAD), lambda b, h, i: (b, 0, h // 2)),
            pl.BlockSpec((1, HEAD_V, S), lambda b, h, i: (b, h, 0)),
        ],
        out_specs=pl.BlockSpec((1, HEAD_V, tq), lambda b, h, i: (b, h, i)),
        out_shape=jax.ShapeDtypeStruct((B, H * HEAD_V, S), F32),
        scratch_shapes=[pltpu.VMEM((2, tk, 2 * tq), F32), pltpu.VMEM((8, 2 * tq), F32),
                        pltpu.VMEM((2, QK_PAD, 2 * tq), BF16)],
        compiler_params=_cparams(("parallel", "parallel", "arbitrary")), name="diff_attn",
    )(lq1, lk1, lq2, lk2, subln_g, q_t, q_t, k, v_t)


def _mixer_merge_kernel(x_ref, a_ref, b_ref, c_ref, gate_ref, wb_ref, wo_ref, g_ref, o_ref):
    merged = None
    for n, br_ref in enumerate((a_ref, b_ref, c_ref)):
        br = br_ref[0].T.astype(BF16)
        proj = _dot(br, wb_ref[n])
        term = gate_ref[0, :, n * D_MODEL:(n + 1) * D_MODEL].astype(F32) * proj
        merged = term if merged is None else merged + term
    y = _dot(merged.astype(BF16), wo_ref[...])
    o_ref[0] = x_ref[0] + _rms_rows(y, g_ref[...])


def _mixer_merge(x, a_t, b_t, c_t, gates, wb, wo, g):
    B, S, D = x.shape
    tm = min(MERGE_TM, S)
    xspec = pl.BlockSpec((1, tm, D), lambda b, i: (b, i, 0))
    brspec = pl.BlockSpec((1, BRANCH_WIDTH, tm), lambda b, i: (b, 0, i))
    return pl.pallas_call(
        _mixer_merge_kernel, grid=(B, S // tm),
        in_specs=[xspec, brspec, brspec, brspec,
                  pl.BlockSpec((1, tm, N_GATE), lambda b, i: (b, i, 0)),
                  _const_spec(wb.shape), _const_spec(wo.shape), _const_spec((1, D))],
        out_specs=xspec, out_shape=jax.ShapeDtypeStruct(x.shape, x.dtype),
        compiler_params=_cparams(("parallel", "parallel")), name="mixer_merge",
    )(x, a_t, b_t, c_t, gates, wb, wo, g)


def _norm_matmul_kernel(x_ref, g_ref, w_ref, o_ref):
    hn = _rms_rows(x_ref[...], g_ref[...]).astype(BF16)
    o_ref[...] = _dot(hn, w_ref[...]).astype(o_ref.dtype)


def _norm_matmul(x2d, g, w, out_dtype):
    T, D = x2d.shape
    N = w.shape[1]
    tm = min(KV_TM, T)
    return pl.pallas_call(
        _norm_matmul_kernel, grid=(T // tm,),
        in_specs=[pl.BlockSpec((tm, D), lambda i: (i, 0)), _const_spec((1, D)), _const_spec(w.shape)],
        out_specs=pl.BlockSpec((tm, N), lambda i: (i, 0)),
        out_shape=jax.ShapeDtypeStruct((T, N), out_dtype),
        compiler_params=_cparams(("parallel",)), name="mem_kv",
    )(x2d, g, w)


def _mem_attn_kernel(x_ref, kv_ref, gpre_ref, wq_ref, wo_ref, gpost_ref, o_ref):
    x = x_ref[0]
    hn = _rms_rows(x, gpre_ref[...]).astype(BF16)
    q = (_dot(hn, wq_ref[...]) * (MEM_HEAD_DIM ** -0.5 * LOG2E)).astype(BF16)
    outs = []
    for h in range(MEM_HEADS):
        lo = h * MEM_HEAD_DIM
        k_h = kv_ref[0, :, lo:lo + MEM_HEAD_DIM]
        v_h = kv_ref[0, :, D_MODEL + lo:D_MODEL + lo + MEM_HEAD_DIM]
        s = _dot_nt(q[:, lo:lo + MEM_HEAD_DIM], k_h)
        e = jnp.exp2(s - jnp.max(s, axis=-1, keepdims=True))
        p = (e / jnp.sum(e, axis=-1, keepdims=True)).astype(BF16)
        outs.append(_dot(p, v_h))
    o = jnp.concatenate(outs, axis=-1).astype(BF16)
    y = _dot(o, wo_ref[...])
    o_ref[0] = x + _rms_rows(y, gpost_ref[...])


def _mem_attention(x, kv, gpre, wq, wo, gpost):
    B, S, D = x.shape
    M = kv.shape[1]
    tm = min(MEM_TM, S)
    xspec = pl.BlockSpec((1, tm, D), lambda b, i: (b, i, 0))
    return pl.pallas_call(
        _mem_attn_kernel, grid=(B, S // tm),
        in_specs=[xspec, pl.BlockSpec((1, M, 2 * D), lambda b, i: (b, 0, 0)),
                  _const_spec((1, D)), _const_spec(wq.shape), _const_spec(wo.shape), _const_spec((1, D))],
        out_specs=xspec, out_shape=jax.ShapeDtypeStruct(x.shape, x.dtype),
        compiler_params=_cparams(("parallel", "parallel")), name="mem_attn",
    )(x, kv, gpre, wq, wo, gpost)


def _swiglu_chunk(hn, wg, wu, wd):
    gate = _dot(hn, wg)
    act = (gate * _sigmoid(gate) * _dot(hn, wu)).astype(BF16)
    return _dot(act, wd)


def _swiglu(hn, wg_ref, wu_ref, wd_ref, tf):
    y = None
    for lo in range(0, wg_ref.shape[-1], tf):
        part = _swiglu_chunk(hn, wg_ref[:, lo:lo + tf], wu_ref[:, lo:lo + tf], wd_ref[lo:lo + tf, :])
        y = part if y is None else y + part
    return y


def _ffn_kernel(x_ref, gpre_ref, wg_ref, wu_ref, wd_ref, gpost_ref, o_ref):
    x = x_ref[...]
    hn = _rms_rows(x, gpre_ref[...]).astype(BF16)
    o_ref[...] = x + _rms_rows(_swiglu(hn, wg_ref, wu_ref, wd_ref, FFN_TF), gpost_ref[...])


def _ffn(x2d, gpre, wg, wu, wd, gpost):
    T, D = x2d.shape
    tm = min(FFN_TM, T)
    xspec = pl.BlockSpec((tm, D), lambda i: (i, 0))
    return pl.pallas_call(
        _ffn_kernel, grid=(T // tm,),
        in_specs=[xspec, _const_spec((1, D)), _const_spec(wg.shape), _const_spec(wu.shape),
                  _const_spec(wd.shape), _const_spec((1, D))],
        out_specs=xspec, out_shape=jax.ShapeDtypeStruct(x2d.shape, x2d.dtype),
        compiler_params=_cparams(("parallel",)), name="ffn",
    )(x2d, gpre, wg, wu, wd, gpost)


def _router_kernel(x_ref, gpre_ref, wr_ref, br_ref, tri_ref,
                   hn_ref, pos_ref, gate_col_ref, pos_col_ref, cnt_ref, run_ref):
    @pl.when(pl.program_id(0) == 0)
    def _():
        run_ref[...] = jnp.zeros_like(run_ref)

    hn = _rms_rows(x_ref[...], gpre_ref[...])
    hn_ref[...] = hn.astype(hn_ref.dtype)
    logits = lax.dot_general(wr_ref[...], hn, (((1,), (1,)), ((), ())),
                             precision=lax.Precision.HIGHEST,
                             preferred_element_type=F32) + br_ref[...]
    e_idx = lax.broadcasted_iota(jnp.int32, logits.shape, 0)
    m1 = jnp.max(logits, axis=0, keepdims=True)
    i1 = jnp.min(jnp.where(logits == m1, e_idx, N_EXPERTS), axis=0, keepdims=True)
    rest = jnp.where(e_idx == i1, -jnp.inf, logits)
    m2 = jnp.max(rest, axis=0, keepdims=True)
    i2 = jnp.min(jnp.where(rest == m2, e_idx, N_EXPERTS), axis=0, keepdims=True)
    e2 = jnp.exp(m2 - m1)
    w1 = 1.0 / (1.0 + e2)
    w2 = e2 / (1.0 + e2)
    gates = jnp.where(e_idx == i1, w1, 0.0) + jnp.where(e_idx == i2, w2, 0.0)

    sel = (e_idx == i1) | (e_idx == i2)
    sel_f = jnp.where(sel, 1.0, 0.0)
    before = _dot(sel_f.astype(BF16), tri_ref[...])
    run = run_ref[:, 0:1]
    pos = jnp.where(sel, run + before, -1.0)
    pos_ref[...] = pos
    tm = pos.shape[1]
    cols = jnp.concatenate([gates, pos, jnp.zeros((V7X_LANES - 2 * N_EXPERTS, tm), F32)], axis=0).T
    for e in range(N_EXPERTS):
        gate_col_ref[e] = cols[:, e:e + 1]
        pos_col_ref[e] = cols[:, N_EXPERTS + e:N_EXPERTS + e + 1]
    cnt = jnp.sum(sel_f, axis=1, keepdims=True)
    cnt_ref[0] = cnt.astype(jnp.int32)
    run_ref[...] = run_ref[...] + cnt


def _router(x2d, gpre, wr_t, br_col):
    T, D = x2d.shape
    tm = MOE_BLOCK
    nb = T // tm
    tri = (np.arange(tm)[:, None] < np.arange(tm)[None, :]).astype(np.float32)
    col = pl.BlockSpec((N_EXPERTS, tm, 1), lambda i: (0, i, 0))
    return pl.pallas_call(
        _router_kernel, grid=(nb,),
        in_specs=[pl.BlockSpec((tm, D), lambda i: (i, 0)), _const_spec((1, D)),
                  _const_spec(wr_t.shape), _const_spec(br_col.shape), _const_spec((tm, tm))],
        out_specs=[pl.BlockSpec((tm, D), lambda i: (i, 0)),
                   pl.BlockSpec((N_EXPERTS, tm), lambda i: (0, i)), col, col,
                   pl.BlockSpec((1, N_EXPERTS, 1), lambda i: (i, 0, 0))],
        out_shape=[jax.ShapeDtypeStruct((T, D), BF16),
                   jax.ShapeDtypeStruct((N_EXPERTS, T), F32),
                   jax.ShapeDtypeStruct((N_EXPERTS, T, 1), F32),
                   jax.ShapeDtypeStruct((N_EXPERTS, T, 1), F32),
                   jax.ShapeDtypeStruct((nb, N_EXPERTS, 1), jnp.int32)],
        scratch_shapes=[pltpu.VMEM((N_EXPERTS, V7X_LANES), F32)],
        compiler_params=_cparams(("arbitrary",)), name="router",
    )(x2d, gpre, wr_t, br_col, jnp.asarray(tri, BF16))


_FIRST, _LAST = 1, 2
_HALF = (4, 8)


def _moe_schedule_kernel(cnt_ref, ea_ref, ba_ref, ta_ref, ra_ref, fa_ref, eb_ref, bb_ref, tb_ref, rb_ref, fb_ref,
                         tile0_ref, start_ref, *, nb, ne):
    tr = MOE_BLOCK
    shift = tr.bit_length() - 1
    assert tr == 1 << shift
    g_max = ea_ref.shape[0]

    def first_tiles(e, tile):
        rows = lax.fori_loop(0, nb, lambda b, acc: acc + cnt_ref[b * ne + e], 0)
        tile0_ref[e] = tile
        start_ref[e] = 0
        return tile + ((rows + tr - 1) >> shift)

    lax.fori_loop(0, ne, first_tiles, 0)

    def append(lists, by_tile, e, b, start, carry):
        e_ref, b_ref, t_ref, r_ref, f_ref = lists
        c = cnt_ref[b * ne + e]
        first = start >> shift
        n = jnp.where(c > 0, ((start + c - 1) >> shift) - first + 1, 0)

        def visit(j, carry):
            g, prev_group = carry
            local = first + j
            tile = tile0_ref[e] + local
            group = tile if by_tile else b
            lo = start - (local << shift)
            opens = group != prev_group

            @pl.when(opens & (g > 0))
            def _():
                f_ref[g - 1] = f_ref[g - 1] | _LAST

            e_ref[g] = e
            b_ref[g] = b
            t_ref[g] = tile
            r_ref[g] = local << shift
            f_ref[g] = (jnp.where(opens, _FIRST, 0) | jnp.where(lo < tr // 2, _HALF[0], 0)
                        | jnp.where(lo + c > tr // 2, _HALF[1], 0))
            return g + 1, group

        return lax.fori_loop(0, n, visit, carry)

    def finish(lists, g):
        e_ref, b_ref, t_ref, r_ref, f_ref = lists
        f_ref[g - 1] = f_ref[g - 1] | _LAST

        def pad(k, _):
            e_ref[k] = e_ref[g - 1]
            b_ref[k] = b_ref[g - 1]
            t_ref[k] = t_ref[g - 1]
            r_ref[k] = r_ref[g - 1]
            f_ref[k] = 0
            return 0

        lax.fori_loop(g, g_max, pad, 0)

    list_a = (ea_ref, ba_ref, ta_ref, ra_ref, fa_ref)
    list_b = (eb_ref, bb_ref, tb_ref, rb_ref, fb_ref)

    def expert_major(e, carry):
        def block(b, inner):
            start, visits = inner
            return start + cnt_ref[b * ne + e], append(list_a, True, e, b, start, visits)
        return lax.fori_loop(0, nb, block, (0, carry))[1]

    g_a, _ = lax.fori_loop(0, ne, expert_major, (0, -1))
    finish(list_a, g_a)

    def block_major(b, carry):
        def expert(e, visits):
            start = start_ref[e]
            start_ref[e] = start + cnt_ref[b * ne + e]
            return append(list_b, False, e, b, start, visits)
        return lax.fori_loop(0, ne, expert, carry)

    g_b, _ = lax.fori_loop(0, nb, block_major, (0, -1))
    finish(list_b, g_b)


def _moe_schedule(cnt, nb, ne, g_max):
    smem = pl.BlockSpec(memory_space=pltpu.SMEM)
    out = jax.ShapeDtypeStruct((g_max,), jnp.int32)
    lists = pl.pallas_call(
        functools.partial(_moe_schedule_kernel, nb=nb, ne=ne),
        in_specs=[smem], out_specs=[smem] * 10, out_shape=[out] * 10,
        scratch_shapes=[pltpu.SMEM((ne,), jnp.int32), pltpu.SMEM((ne,), jnp.int32)],
        name="moe_schedule",
    )(cnt)
    return lists[:5], lists[5:]


def _moe_expert_kernel(e_ref, b_ref, t_ref, r_ref, f_ref, hn_ref, pos_ref, wg_ref, wu_ref, wd_ref,
                       ys_ref, xs_ref):
    g = pl.program_id(0)
    flags = f_ref[g]
    tr = xs_ref.shape[0]

    @pl.when((flags & _FIRST) != 0)
    def _():
        xs_ref[...] = jnp.zeros_like(xs_ref)

    half = tr // 2
    for h in range(2):
        @pl.when((flags & _HALF[h]) != 0)
        def _():
            rows = (lax.broadcasted_iota(jnp.int32, (half, 1), 0) + (r_ref[g] + h * half)).astype(F32)
            onehot = jnp.where(rows == pos_ref[0, 0], 1.0, 0.0).astype(BF16)
            xs_ref[h * half:(h + 1) * half, :] += _dot(onehot, hn_ref[...])

    @pl.when((flags & _LAST) != 0)
    def _():
        y = _swiglu(xs_ref[...].astype(BF16), wg_ref.at[0], wu_ref.at[0], wd_ref.at[0], FFN_TF)
        ys_ref[...] = y.astype(ys_ref.dtype)


def _moe_expert(list_a, hn, pos, wg, wu, wd, n_tiles):
    T, D = hn.shape
    E, _, F = wg.shape
    tr = MOE_BLOCK
    g_max = list_a[0].shape[0]
    pos4 = pos.reshape(E, T // tr, 1, tr)
    wspec_in = pl.BlockSpec((1, D, F), lambda g, e, b, t, r, f: (e[g], 0, 0), pipeline_mode=pl.Buffered(1))
    wspec_out = pl.BlockSpec((1, F, D), lambda g, e, b, t, r, f: (e[g], 0, 0), pipeline_mode=pl.Buffered(1))
    grid_spec = pltpu.PrefetchScalarGridSpec(
        num_scalar_prefetch=5, grid=(g_max,),
        in_specs=[pl.BlockSpec((tr, D), lambda g, e, b, t, r, f: (b[g], 0)),
                  pl.BlockSpec((1, 1, 1, tr), lambda g, e, b, t, r, f: (e[g], b[g], 0, 0)),
                  wspec_in, wspec_in, wspec_out],
        out_specs=pl.BlockSpec((tr, D), lambda g, e, b, t, r, f: (t[g], 0)),
        scratch_shapes=[pltpu.VMEM((tr, D), F32)])
    return pl.pallas_call(
        _moe_expert_kernel, grid_spec=grid_spec,
        out_shape=jax.ShapeDtypeStruct((n_tiles * tr, D), BF16),
        compiler_params=_cparams(("arbitrary",)), name="moe_expert",
    )(*list_a, hn, pos4, wg, wu, wd)


def _moe_combine_kernel(e_ref, b_ref, t_ref, r_ref, f_ref, x_ref, ys_ref, pos_ref, gate_ref, gpost_ref,
                        o_ref, acc_ref):
    g = pl.program_id(0)
    flags = f_ref[g]
    tr = ys_ref.shape[0]

    @pl.when((flags & _FIRST) != 0)
    def _():
        acc_ref[...] = jnp.zeros_like(acc_ref)

    half = tr // 2
    for h in range(2):
        @pl.when((flags & _HALF[h]) != 0)
        def _():
            rows = (lax.broadcasted_iota(jnp.int32, (1, half), 1) + (r_ref[g] + h * half)).astype(F32)
            onehot = jnp.where(pos_ref[0] == rows, 1.0, 0.0).astype(BF16)
            acc_ref[...] += gate_ref[0] * _dot(onehot, ys_ref[h * half:(h + 1) * half, :])

    @pl.when((flags & _LAST) != 0)
    def _():
        o_ref[...] = x_ref[...] + _rms_rows(acc_ref[...], gpost_ref[...])


def _moe_combine(list_b, x2d, ys, pos_col, gate_col, gpost):
    T, D = x2d.shape
    tr = MOE_BLOCK
    g_max = list_b[0].shape[0]
    xspec = pl.BlockSpec((tr, D), lambda g, e, b, t, r, f: (b[g], 0))
    colspec = pl.BlockSpec((1, tr, 1), lambda g, e, b, t, r, f: (e[g], b[g], 0))
    grid_spec = pltpu.PrefetchScalarGridSpec(
        num_scalar_prefetch=5, grid=(g_max,),
        in_specs=[xspec, pl.BlockSpec((tr, D), lambda g, e, b, t, r, f: (t[g], 0)), colspec, colspec,
                  pl.BlockSpec((1, D), lambda g, e, b, t, r, f: (0, 0))],
        out_specs=xspec,
        scratch_shapes=[pltpu.VMEM((tr, D), F32)])
    return pl.pallas_call(
        _moe_combine_kernel, grid_spec=grid_spec,
        out_shape=jax.ShapeDtypeStruct(x2d.shape, x2d.dtype),
        compiler_params=_cparams(("arbitrary",)), name="moe_combine",
    )(*list_b, x2d, ys, pos_col, gate_col, gpost)


def _moe(x2d, gpre, wr_t, br_col, wg, wu, wd, gpost):
    T, D = x2d.shape
    E = wg.shape[0]
    nb = T // MOE_BLOCK
    n_tiles = TOP_K * nb + E
    g_max = E * nb + n_tiles
    hn, pos, gate_col, pos_col, cnt = _router(x2d, gpre, wr_t, br_col)
    list_a, list_b = _moe_schedule(cnt.reshape(nb * E), nb, E, g_max)
    ys = _moe_expert(list_a, hn, pos, wg, wu, wd, n_tiles)
    return _moe_combine(list_b, x2d, ys, pos_col, gate_col, gpost)


def _feature_row_order():
    def diff_block(base):
        x1, x2, rest = [], [], []
        half = DIFF_ROPE_DIM // 2
        for h in range(DIFF_HEADS):
            for c in range(2):
                lo = base + h * 2 * DIFF_QK_DIM + c * DIFF_QK_DIM
                x1 += range(lo, lo + half)
                x2 += range(lo + half, lo + DIFF_ROPE_DIM)
                rest += range(lo + DIFF_ROPE_DIM, lo + DIFF_QK_DIM)
        return x1 + x2 + rest

    order = list(range(0, _S_DQ)) + diff_block(_S_DQ) + diff_block(_S_DK) + list(range(_S_DV, N_FEAT))
    return np.asarray(order, np.int32)


def _mla_q_up_order():
    w = MLA_NOPE_DIM + MLA_ROPE_DIM
    half = MLA_ROPE_DIM // 2
    nope = [h * w + j for h in range(MLA_HEADS) for j in range(MLA_NOPE_DIM)]
    x1 = [h * w + MLA_NOPE_DIM + j for h in range(MLA_HEADS) for j in range(half)]
    x2 = [h * w + MLA_NOPE_DIM + half + j for h in range(MLA_HEADS) for j in range(half)]
    return np.asarray(nope + x1 + x2, np.int32)


def _mla_kv_up_order():
    w = MLA_NOPE_DIM + MLA_V_DIM
    k = [h * w + j for h in range(MLA_HEADS) for j in range(MLA_NOPE_DIM)]
    v = [h * w + MLA_NOPE_DIM + j for h in range(MLA_HEADS) for j in range(MLA_V_DIM)]
    return np.asarray(k + v, np.int32)


def _rope_tables(S):
    def cos_sin(pos, rot_dim):
        inv = ROPE_THETA ** (-jnp.arange(0, rot_dim, 2, dtype=F32) / rot_dim)
        ang = pos.astype(F32)[:, None] * inv[None, :]
        return jnp.cos(ang).T, jnp.sin(ang).T

    rows = S // GRID_W
    pos = jnp.arange(S)
    row = jnp.repeat(jnp.arange(rows), GRID_W)
    col = jnp.tile(jnp.arange(GRID_W), rows)
    cm, sm = cos_sin(pos, MLA_ROPE_DIM)
    cr, sr = cos_sin(row, GQA_HEAD_DIM // 2)
    cc, sc = cos_sin(col, GQA_HEAD_DIM // 2)
    cp, sp = cos_sin(pos, DIFF_ROPE_DIM)
    return (jnp.tile(cm, (MLA_HEADS, 1)), jnp.tile(sm, (MLA_HEADS, 1)),
            jnp.concatenate([cr, cc], axis=0), jnp.concatenate([sr, sc], axis=0),
            jnp.tile(cp, (2 * DIFF_HEADS, 1)), jnp.tile(sp, (2 * DIFF_HEADS, 1)))


def _row(v):
    return v.reshape(1, -1).astype(F32)


def _col(v):
    return v.reshape(-1, 1).astype(F32)


def kernel(x, mem, mix_pre_g, mix_post_g, w_in, mla_q_norm_g, mla_w_q_up, mla_kv_norm_g, mla_w_kv_up,
           gqa_q_norm_g, gqa_k_norm_g, diff_lambda_q1, diff_lambda_k1, diff_lambda_q2, diff_lambda_k2,
           diff_subln_g, w_branch, w_out, mem_pre_g, mem_post_g, mem_norm_g, mem_wq, mem_wkv, mem_wo,
           ffn_pre_g, ffn_post_g, dense_w_gate, dense_w_up, dense_w_down, moe_w_router, moe_b_router,
           moe_w_gate, moe_w_up, moe_w_down):
    B, S, D = x.shape
    M = mem.shape[1]
    depth = w_in.shape[0]
    tables = _rope_tables(S)
    feat_order = _feature_row_order()
    q_up_order = _mla_q_up_order()
    kv_up_order = _mla_kv_up_order()

    for layer in range(depth):
        lambda_init = 0.8 - 0.6 * math.exp(-0.3 * layer)

        w_l = w_in[layer]
        wt = w_l[:, feat_order].T.astype(BF16)
        wg = w_l[:, N_FEAT:].astype(BF16)
        wqu = mla_w_q_up[layer][:, q_up_order].T.astype(BF16)
        wkvu = mla_w_kv_up[layer][:, kv_up_order].T.astype(BF16)
        (qm, km, vm, qg, kg, vg, qd, kd, vd, gates) = _mixer_prep(
            x, _row(mix_pre_g[layer]), wt, wg, wqu, wkvu,
            _col(mla_q_norm_g[layer]), _col(mla_kv_norm_g[layer]),
            _col(gqa_q_norm_g[layer]), _col(gqa_k_norm_g[layer]), tables)
        a_t = _attention(qm, km, vm, kv_of_head=lambda h: h, kcol_of_head=lambda h: h, name="mla_attn")
        b_t = _attention(qg, kg, vg, kv_of_head=lambda h: h // GQA_GROUP, kcol_of_head=lambda h: 0,
                         name="gqa_attn")
        c_t = _diff_attention(qd, kd, vd, _row(diff_lambda_q1[layer]), _row(diff_lambda_k1[layer]),
                              _row(diff_lambda_q2[layer]), _row(diff_lambda_k2[layer]),
                              _col(diff_subln_g[layer]), lambda_init)
        x = _mixer_merge(x, a_t, b_t, c_t, gates, w_branch[layer].astype(BF16),
                         w_out[layer].astype(BF16), _row(mix_post_g[layer]))

        kv = _norm_matmul(mem.reshape(B * M, D), _row(mem_norm_g[layer]), mem_wkv[layer].astype(BF16), BF16)
        x = _mem_attention(x, kv.reshape(B, M, 2 * D), _row(mem_pre_g[layer]), mem_wq[layer].astype(BF16),
                           mem_wo[layer].astype(BF16), _row(mem_post_g[layer]))

        x2d = x.reshape(B * S, D)
        i = layer // 2
        if layer % 2 == 0:
            x2d = _ffn(x2d, _row(ffn_pre_g[layer]), dense_w_gate[i].astype(BF16), dense_w_up[i].astype(BF16),
                       dense_w_down[i].astype(BF16), _row(ffn_post_g[layer]))
        else:
            x2d = _moe(x2d, _row(ffn_pre_g[layer]), moe_w_router[i].T.astype(F32), _col(moe_b_router[i]),
                       moe_w_gate[i].astype(BF16), moe_w_up[i].astype(BF16), moe_w_down[i].astype(BF16),
                       _row(ffn_post_g[layer]))
        x = x2d.reshape(B, S, D)
    return x
```

```python
import functools
import math

import numpy as np
import jax
import jax.numpy as jnp
from jax import lax
from jax.experimental import pallas as pl
from jax.experimental.pallas import tpu as pltpu

F32 = jnp.float32
BF16 = jnp.bfloat16

D_MODEL = 1024
GRID_W = 64
ROPE_THETA = 500000.0
NORM_EPS = 1e-6

MLA_HEADS = 8
MLA_Q_LORA = 384
MLA_KV_LORA = 256
MLA_NOPE_DIM = 64
MLA_ROPE_DIM = 32
MLA_V_DIM = 64

GQA_Q_HEADS = 8
GQA_KV_HEADS = 2
GQA_GROUP = GQA_Q_HEADS // GQA_KV_HEADS
GQA_HEAD_DIM = 64

DIFF_HEADS = 8
DIFF_QK_DIM = 32
DIFF_V_DIM = 2 * DIFF_QK_DIM
DIFF_ROPE_DIM = DIFF_QK_DIM // 4

N_BRANCHES = 3
BRANCH_WIDTH = 512
HEAD_V = 64

MEM_HEADS = 4
MEM_HEAD_DIM = D_MODEL // MEM_HEADS

D_FF = 2816
N_EXPERTS = 8
TOP_K = 2

IN_SPLITS = (MLA_Q_LORA, MLA_KV_LORA, MLA_ROPE_DIM,
             GQA_Q_HEADS * GQA_HEAD_DIM, GQA_KV_HEADS * GQA_HEAD_DIM, GQA_KV_HEADS * GQA_HEAD_DIM,
             DIFF_HEADS * 2 * DIFF_QK_DIM, DIFF_HEADS * 2 * DIFF_QK_DIM, DIFF_HEADS * DIFF_V_DIM,
             N_BRANCHES * D_MODEL)
IN_OFFS = tuple(int(v) for v in np.cumsum((0,) + IN_SPLITS))
N_FEAT = IN_OFFS[9]
N_GATE = IN_SPLITS[9]

LOG2E = 1.4426950408889634
V7X_LANES = 128
QK_PAD = V7X_LANES
ONES_ROWS = 16

V7X_VMEM_LIMIT_BYTES = 56 * 1024 * 1024

PREP_TM = 512
ATTN_TQ = 1024
ATTN_TK = 256
ATTN_UNROLL = 32
MERGE_TM = 512
MEM_TM = 1024
FFN_TM = 512
FFN_TF = 2816
MOE_BLOCK = 512
KV_TM = 256


def _cparams(sem):
    return pltpu.CompilerParams(dimension_semantics=sem, vmem_limit_bytes=V7X_VMEM_LIMIT_BYTES)


def _const_spec(shape):
    nd = len(shape)
    return pl.BlockSpec(shape, lambda *_: (0,) * nd, pipeline_mode=pl.Buffered(1))


def _rms_rows(x, g_row):
    ms = jnp.mean(x * x, axis=-1, keepdims=True)
    return x * lax.rsqrt(ms + NORM_EPS) * g_row


def _rms_cols(x, g_col):
    ms = jnp.mean(x * x, axis=0, keepdims=True)
    return x * lax.rsqrt(ms + NORM_EPS) * g_col


def _sigmoid(x):
    return 0.5 * jnp.tanh(0.5 * x) + 0.5


def _dot(a, b):
    return jnp.dot(a, b, preferred_element_type=F32)


def _dot_nt(a, b):
    return lax.dot_general(a, b, (((1,), (1,)), ((), ())), preferred_element_type=F32)


_S_CQ, _S_CKV, _S_KR, _S_GQ, _S_GK, _S_GV, _S_DQ, _S_DK, _S_DV = IN_OFFS[:9]


def _rope_rows(x1, x2, c, s):
    return x1 * c - x2 * s, x2 * c + x1 * s


def _mixer_prep_kernel(x_ref, g_ref, wt_ref, wg_ref, wqu_ref, wkvu_ref,
                       mqg_ref, mkvg_ref, gqg_ref, gkg_ref,
                       cm_ref, sm_ref, ca_ref, sa_ref, cp_ref, sp_ref,
                       qm_ref, km_ref, vm_ref, qg_ref, kg_ref, vg_ref,
                       qd_ref, kd_ref, vd_ref, gate_ref):
    tm = x_ref.shape[1]
    hn = _rms_rows(x_ref[0], g_ref[...]).astype(BF16)

    z_all = _dot_nt(wt_ref[...], hn)

    def feat(lo, n):
        return z_all[lo:lo + n]

    gate_ref[0] = _sigmoid(_dot(hn, wg_ref[...])).astype(gate_ref.dtype)

    zeros32 = jnp.zeros((32, tm), F32)
    zeros64 = jnp.zeros((64, tm), F32)

    c_mla = (MLA_NOPE_DIM + MLA_ROPE_DIM) ** -0.5 * LOG2E
    cm = cm_ref[...]
    sm = sm_ref[...]
    cqn = _rms_cols(feat(_S_CQ, MLA_Q_LORA), mqg_ref[...]).astype(BF16)
    q_all = _dot(wqu_ref[...], cqn)
    q_nope = q_all[0:512] * c_mla
    q_r1, q_r2 = _rope_rows(q_all[512:640], q_all[640:768], cm, sm)
    q_r1 = q_r1 * c_mla
    q_r2 = q_r2 * c_mla
    for h in range(MLA_HEADS):
        qm_ref[0, h] = jnp.concatenate(
            [q_nope[64 * h:64 * h + 64], q_r1[16 * h:16 * h + 16], q_r2[16 * h:16 * h + 16], zeros32],
            axis=0).astype(qm_ref.dtype)

    ckvn = _rms_cols(feat(_S_CKV, MLA_KV_LORA), mkvg_ref[...]).astype(BF16)
    kv_all = _dot(wkvu_ref[...], ckvn)
    vm_ref[0] = kv_all[512:1024].astype(vm_ref.dtype)
    kr = feat(_S_KR, MLA_ROPE_DIM)
    k_r1, k_r2 = _rope_rows(kr[0:16], kr[16:32], cm[0:16], sm[0:16])
    for h in range(MLA_HEADS):
        kt = jnp.concatenate([kv_all[64 * h:64 * h + 64], k_r1, k_r2, zeros32], axis=0)
        km_ref[0, :, QK_PAD * h:QK_PAD * (h + 1)] = kt.T.astype(km_ref.dtype)

    c_gqa = GQA_HEAD_DIM ** -0.5 * LOG2E
    ca = ca_ref[...]
    sa = sa_ref[...]
    gq = feat(_S_GQ, GQA_Q_HEADS * GQA_HEAD_DIM)
    for h in range(GQA_Q_HEADS):
        qn = _rms_cols(gq[64 * h:64 * h + 64], gqg_ref[...])
        r1, r2 = _rope_rows(qn[0:32], qn[32:64], ca, sa)
        q64 = jnp.concatenate([r1, r2], axis=0) * c_gqa
        parts = [q64, zeros64] if h // GQA_GROUP == 0 else [zeros64, q64]
        qg_ref[0, h] = jnp.concatenate(parts, axis=0).astype(qg_ref.dtype)
    gk = feat(_S_GK, GQA_KV_HEADS * GQA_HEAD_DIM)
    kparts = []
    for g in range(GQA_KV_HEADS):
        kn = _rms_cols(gk[64 * g:64 * g + 64], gkg_ref[...])
        r1, r2 = _rope_rows(kn[0:32], kn[32:64], ca, sa)
        kparts += [r1, r2]
    kg_ref[0] = jnp.concatenate(kparts, axis=0).T.astype(kg_ref.dtype)
    vg_ref[0] = feat(_S_GV, GQA_KV_HEADS * GQA_HEAD_DIM).astype(vg_ref.dtype)

    c_diff = DIFF_QK_DIM ** -0.5 * LOG2E
    cp = cp_ref[...]
    sp = sp_ref[...]
    row = lax.broadcasted_iota(jnp.int32, (64, 1), 0)
    in_c0 = (row < 4) | ((row >= 8) & (row < 12)) | ((row >= 16) & (row < 40))

    def diff_heads(lo):
        z = feat(lo, DIFF_HEADS * 2 * DIFF_QK_DIM)
        r1, r2 = _rope_rows(z[0:64], z[64:128], cp, sp)
        rest = z[128:512]
        return [jnp.concatenate([r1[8 * h:8 * h + 8], r2[8 * h:8 * h + 8], rest[48 * h:48 * h + 48]], axis=0)
                for h in range(DIFF_HEADS)]

    for h, q64 in enumerate(diff_heads(_S_DQ)):
        q64 = q64 * c_diff
        for c in range(2):
            qc = jnp.where(in_c0 if c == 0 else jnp.logical_not(in_c0), q64, 0.0)
            parts = [qc, zeros64] if h % 2 == 0 else [zeros64, qc]
            qd_ref[0, 2 * h + c] = jnp.concatenate(parts, axis=0).astype(qd_ref.dtype)
    k_heads = diff_heads(_S_DK)
    for p in range(DIFF_HEADS // 2):
        kt = jnp.concatenate([k_heads[2 * p], k_heads[2 * p + 1]], axis=0)
        kd_ref[0, :, QK_PAD * p:QK_PAD * (p + 1)] = kt.T.astype(kd_ref.dtype)
    vd_ref[0] = feat(_S_DV, DIFF_HEADS * DIFF_V_DIM).astype(vd_ref.dtype)


def _mixer_prep(x, g, wt, wg, wqu, wkvu, mqg, mkvg, gqg, gkg, tables):
    B, S, D = x.shape
    tm = min(PREP_TM, S)
    cm, sm, ca, sa, cp, sp = tables
    grid = (B, S // tm)

    def tok(width):
        return pl.BlockSpec((1, tm, width), lambda b, i: (b, i, 0))

    def featm(rows):
        return pl.BlockSpec((1, rows, tm), lambda b, i: (b, 0, i))

    def heads(n):
        return pl.BlockSpec((1, n, QK_PAD, tm), lambda b, i: (b, 0, 0, i))

    def table(rows):
        return pl.BlockSpec((rows, tm), lambda b, i: (0, i))

    in_specs = [
        pl.BlockSpec((1, tm, D), lambda b, i: (b, i, 0)),
        _const_spec((1, D)),
        _const_spec(wt.shape), _const_spec(wg.shape), _const_spec(wqu.shape), _const_spec(wkvu.shape),
        _const_spec(mqg.shape), _const_spec(mkvg.shape), _const_spec(gqg.shape), _const_spec(gkg.shape),
        table(128), table(128), table(32), table(32), table(64), table(64),
    ]
    out_shape = [
        jax.ShapeDtypeStruct((B, MLA_HEADS, QK_PAD, S), BF16),
        jax.ShapeDtypeStruct((B, S, MLA_HEADS * QK_PAD), BF16),
        jax.ShapeDtypeStruct((B, MLA_HEADS * HEAD_V, S), BF16),
        jax.ShapeDtypeStruct((B, GQA_Q_HEADS, QK_PAD, S), BF16),
        jax.ShapeDtypeStruct((B, S, QK_PAD), BF16),
        jax.ShapeDtypeStruct((B, GQA_KV_HEADS * HEAD_V, S), BF16),
        jax.ShapeDtypeStruct((B, 2 * DIFF_HEADS, QK_PAD, S), BF16),
        jax.ShapeDtypeStruct((B, S, DIFF_HEADS // 2 * QK_PAD), BF16),
        jax.ShapeDtypeStruct((B, DIFF_HEADS * HEAD_V, S), BF16),
        jax.ShapeDtypeStruct((B, S, N_GATE), BF16),
    ]
    out_specs = [
        heads(MLA_HEADS), tok(MLA_HEADS * QK_PAD), featm(MLA_HEADS * HEAD_V),
        heads(GQA_Q_HEADS), tok(QK_PAD), featm(GQA_KV_HEADS * HEAD_V),
        heads(2 * DIFF_HEADS), tok(DIFF_HEADS // 2 * QK_PAD), featm(DIFF_HEADS * HEAD_V),
        tok(N_GATE),
    ]
    return pl.pallas_call(
        _mixer_prep_kernel, grid=grid, in_specs=in_specs, out_specs=out_specs, out_shape=out_shape,
        compiler_params=_cparams(("parallel", "parallel")), name="mixer_prep",
    )(x, g, wt, wg, wqu, wkvu, mqg, mkvg, gqg, gkg, cm, sm, ca, sa, cp, sp)


def _online_softmax_attention(q_t, load_q_next, k_ref, v_ref, s_ref, cmax_ref, qq_ref, *, tk):
    S = k_ref.shape[1]
    lanes = q_t.shape[1]
    n_chunks = S // tk
    unroll = min(ATTN_UNROLL, n_chunks)
    assert unroll % 2 == 0 and n_chunks % unroll == 0
    ones = jnp.ones((ONES_ROWS, tk), BF16)

    def scores(c, slot, q=q_t):
        s_t = _dot(k_ref[0, pl.ds(pl.multiple_of(c * tk, tk), tk), :], q)
        s_ref[slot, :, :lanes] = s_t
        return jnp.max(s_t, axis=0, keepdims=True)

    def update(c, slot, col_max, m, acc):
        v_aug = jnp.concatenate([v_ref[0, :, pl.ds(pl.multiple_of(c * tk, tk), tk)], ones], axis=0)
        m_new = jnp.maximum(m, col_max)
        p = jnp.exp2(s_ref[slot, :, :lanes] - m_new).astype(BF16)
        return m_new, acc * jnp.exp2(m - m_new) + _dot(v_aug, p)

    n_trips = n_chunks // unroll
    qq_ref[0] = q_t
    qq_ref[1] = load_q_next()

    def body(t, carry):
        cmax, m, acc = carry
        c = unroll * t
        last = t == n_trips - 1
        for u in range(unroll):
            if u == unroll - 1:
                if isinstance(last, bool):
                    chunk, tile = (0, 1) if last else (c + unroll, 0)
                else:
                    chunk, tile = jnp.where(last, 0, c + unroll), jnp.where(last, 1, 0)
                cmax_next = scores(chunk, 0, qq_ref[tile])
            else:
                cmax_next = scores(c + u + 1, (u + 1) % 2)
            m, acc = update(c + u, u % 2, cmax, m, acc)
            cmax = cmax_next
        return cmax, m, acc

    @pl.when(pl.program_id(2) == 0)
    def _():
        cmax_ref[...] = jnp.broadcast_to(scores(0, 0), cmax_ref.shape)

    cmax0 = jnp.max(cmax_ref[...], axis=0, keepdims=True)
    m0 = jnp.full((1, lanes), -jnp.inf, F32)
    acc0 = jnp.zeros((HEAD_V + ONES_ROWS, lanes), F32)
    carry = (cmax0, m0, acc0)
    cmax_next, _, acc = body(0, carry) if n_trips == 1 else lax.fori_loop(0, n_trips, body, carry)
    cmax_ref[...] = jnp.broadcast_to(cmax_next, cmax_ref.shape)
    return acc[:HEAD_V] / acc[HEAD_V:HEAD_V + 1]


def _attn_kernel(q_ref, qn_ref, k_ref, v_ref, o_ref, s_ref, cmax_ref, qq_ref, *, tk):
    o = _online_softmax_attention(q_ref[0, 0], lambda: qn_ref[0, 0], k_ref, v_ref, s_ref, cmax_ref, qq_ref,
                                  tk=tk)
    o_ref[0] = o.astype(o_ref.dtype)


def _next_tile(n_tiles):
    return lambda i: jnp.minimum(i + 1, n_tiles - 1)


def _attention(q_t, k, v_t, *, kv_of_head, kcol_of_head, name):
    B, H, _, S = q_t.shape
    tq = min(ATTN_TQ, S)
    tk = min(ATTN_TK, S // 2)
    nxt = _next_tile(S // tq)
    return pl.pallas_call(
        functools.partial(_attn_kernel, tk=tk),
        grid=(B, H, S // tq),
        in_specs=[
            pl.BlockSpec((1, 1, QK_PAD, tq), lambda b, h, i: (b, h, 0, i)),
            pl.BlockSpec((1, 1, QK_PAD, tq), lambda b, h, i: (b, h, 0, nxt(i))),
            pl.BlockSpec((1, S, QK_PAD), lambda b, h, i: (b, 0, kcol_of_head(h))),
            pl.BlockSpec((1, HEAD_V, S), lambda b, h, i: (b, kv_of_head(h), 0)),
        ],
        out_specs=pl.BlockSpec((1, HEAD_V, tq), lambda b, h, i: (b, h, i)),
        out_shape=jax.ShapeDtypeStruct((B, H * HEAD_V, S), F32),
        scratch_shapes=[pltpu.VMEM((2, tk, tq + (V7X_LANES if name == "gqa_attn" else 0)), F32),
                        pltpu.VMEM((8, tq), F32),
                        pltpu.VMEM((2, QK_PAD, tq), BF16)],
        compiler_params=_cparams(("parallel", "parallel", "arbitrary")), name=name,
    )(q_t, q_t, k, v_t)


def _diff_attn_kernel(lq1_ref, lk1_ref, lq2_ref, lk2_ref, g_ref, q_ref, qn_ref, k_ref, v_ref, o_ref,
                      s_ref, cmax_ref, qq_ref, *, tk, lambda_init):
    tq = q_ref.shape[3]
    q_both = jnp.concatenate([q_ref[0, 0], q_ref[0, 1]], axis=1)
    o_maps = _online_softmax_attention(
        q_both, lambda: jnp.concatenate([qn_ref[0, 0], qn_ref[0, 1]], axis=1), k_ref, v_ref, s_ref, cmax_ref,
        qq_ref, tk=tk)
    lam = (jnp.exp(jnp.sum(lq1_ref[...] * lk1_ref[...], axis=-1, keepdims=True))
           - jnp.exp(jnp.sum(lq2_ref[...] * lk2_ref[...], axis=-1, keepdims=True)) + lambda_init)
    o = o_maps[:, :tq] - lam * o_maps[:, tq:]
    o = _rms_cols(o, g_ref[...]) * (1.0 - lambda_init)
    o_ref[0] = o.astype(o_ref.dtype)


def _diff_attention(q_t, k, v_t, lq1, lk1, lq2, lk2, subln_g, lambda_init):
    B, H2, _, S = q_t.shape
    H = H2 // 2
    tq = min(ATTN_TQ // 2, S)
    tk = min(ATTN_TK, S // 2)
    nxt = _next_tile(S // tq)
    vec = _const_spec((1, DIFF_QK_DIM))
    return pl.pallas_call(
        functools.partial(_diff_attn_kernel, tk=tk, lambda_init=lambda_init),
        grid=(B, H, S // tq),
        in_specs=[
            vec, vec, vec, vec, _const_spec((DIFF_V_DIM, 1)),
            pl.BlockSpec((1, 2, QK_PAD, tq), lambda b, h, i: (b, h, 0, i)),
            pl.BlockSpec((1, 2, QK_PAD, tq), lambda b, h, i: (b, h, 0, nxt(i))),
            pl.BlockSpec((1, S, QK_PAD), lambda b, h, i: (b, 0, h // 2)),
            pl.BlockSpec((1, HEAD_V, S), lambda b, h, i: (b, h, 0)),
        ],
        out_specs=pl.BlockSpec((1, HEAD_V, tq), lambda b, h, i: (b, h, i)),
        out_shape=jax.ShapeDtypeStruct((B, H * HEAD_V, S), F32),
        scratch_shapes=[pltpu.VMEM((2, tk, 2 * tq), F32), pltpu.VMEM((8, 2 * tq), F32),
                        pltpu.VMEM((2, QK_PAD, 2 * tq), BF16)],
        compiler_params=_cparams(("parallel", "parallel", "arbitrary")), name="diff_attn",
    )(lq1, lk1, lq2, lk2, subln_g, q_t, q_t, k, v_t)


def _mixer_merge_kernel(x_ref, a_ref, b_ref, c_ref, gate_ref, wb_ref, wo_ref, g_ref, o_ref):
    merged = None
    for n, br_ref in enumerate((a_ref, b_ref, c_ref)):
        br = br_ref[0].T.astype(BF16)
        proj = _dot(br, wb_ref[n])
        term = gate_ref[0, :, n * D_MODEL:(n + 1) * D_MODEL].astype(F32) * proj
        merged = term if merged is None else merged + term
    y = _dot(merged.astype(BF16), wo_ref[...])
    o_ref[0] = x_ref[0] + _rms_rows(y, g_ref[...])


def _mixer_merge(x, a_t, b_t, c_t, gates, wb, wo, g):
    B, S, D = x.shape
    tm = min(MERGE_TM, S)
    xspec = pl.BlockSpec((1, tm, D), lambda b, i: (b, i, 0))
    brspec = pl.BlockSpec((1, BRANCH_WIDTH, tm), lambda b, i: (b, 0, i))
    return pl.pallas_call(
        _mixer_merge_kernel, grid=(B, S // tm),
        in_specs=[xspec, brspec, brspec, brspec,
                  pl.BlockSpec((1, tm, N_GATE), lambda b, i: (b, i, 0)),
                  _const_spec(wb.shape), _const_spec(wo.shape), _const_spec((1, D))],
        out_specs=xspec, out_shape=jax.ShapeDtypeStruct(x.shape, x.dtype),
        compiler_params=_cparams(("parallel", "parallel")), name="mixer_merge",
    )(x, a_t, b_t, c_t, gates, wb, wo, g)


def _norm_matmul_kernel(x_ref, g_ref, w_ref, o_ref):
    hn = _rms_rows(x_ref[...], g_ref[...]).astype(BF16)
    o_ref[...] = _dot(hn, w_ref[...]).astype(o_ref.dtype)


def _norm_matmul(x2d, g, w, out_dtype):
    T, D = x2d.shape
    N = w.shape[1]
    tm = min(KV_TM, T)
    return pl.pallas_call(
        _norm_matmul_kernel, grid=(T // tm,),
        in_specs=[pl.BlockSpec((tm, D), lambda i: (i, 0)), _const_spec((1, D)), _const_spec(w.shape)],
        out_specs=pl.BlockSpec((tm, N), lambda i: (i, 0)),
        out_shape=jax.ShapeDtypeStruct((T, N), out_dtype),
        compiler_params=_cparams(("parallel",)), name="mem_kv",
    )(x2d, g, w)


def _mem_attn_kernel(x_ref, kv_ref, gpre_ref, wq_ref, wo_ref, gpost_ref, o_ref):
    x = x_ref[0]
    hn = _rms_rows(x, gpre_ref[...]).astype(BF16)
    q = (_dot(hn, wq_ref[...]) * (MEM_HEAD_DIM ** -0.5 * LOG2E)).astype(BF16)
    outs = []
    for h in range(MEM_HEADS):
        lo = h * MEM_HEAD_DIM
        k_h = kv_ref[0, :, lo:lo + MEM_HEAD_DIM]
        v_h = kv_ref[0, :, D_MODEL + lo:D_MODEL + lo + MEM_HEAD_DIM]
        s = _dot_nt(q[:, lo:lo + MEM_HEAD_DIM], k_h)
        e = jnp.exp2(s - jnp.max(s, axis=-1, keepdims=True))
        p = (e / jnp.sum(e, axis=-1, keepdims=True)).astype(BF16)
        outs.append(_dot(p, v_h))
    o = jnp.concatenate(outs, axis=-1).astype(BF16)
    y = _dot(o, wo_ref[...])
    o_ref[0] = x + _rms_rows(y, gpost_ref[...])


def _mem_attention(x, kv, gpre, wq, wo, gpost):
    B, S, D = x.shape
    M = kv.shape[1]
    tm = min(MEM_TM, S)
    xspec = pl.BlockSpec((1, tm, D), lambda b, i: (b, i, 0))
    return pl.pallas_call(
        _mem_attn_kernel, grid=(B, S // tm),
        in_specs=[xspec, pl.BlockSpec((1, M, 2 * D), lambda b, i: (b, 0, 0)),
                  _const_spec((1, D)), _const_spec(wq.shape), _const_spec(wo.shape), _const_spec((1, D))],
        out_specs=xspec, out_shape=jax.ShapeDtypeStruct(x.shape, x.dtype),
        compiler_params=_cparams(("parallel", "parallel")), name="mem_attn",
    )(x, kv, gpre, wq, wo, gpost)


def _swiglu_chunk(hn, wg, wu, wd):
    gate = _dot(hn, wg)
    act = (gate * _sigmoid(gate) * _dot(hn, wu)).astype(BF16)
    return _dot(act, wd)


def _swiglu(hn, wg_ref, wu_ref, wd_ref, tf):
    y = None
    for lo in range(0, wg_ref.shape[-1], tf):
        part = _swiglu_chunk(hn, wg_ref[:, lo:lo + tf], wu_ref[:, lo:lo + tf], wd_ref[lo:lo + tf, :])
        y = part if y is None else y + part
    return y


def _ffn_kernel(x_ref, gpre_ref, wg_ref, wu_ref, wd_ref, gpost_ref, o_ref):
    x = x_ref[...]
    hn = _rms_rows(x, gpre_ref[...]).astype(BF16)
    o_ref[...] = x + _rms_rows(_swiglu(hn, wg_ref, wu_ref, wd_ref, FFN_TF), gpost_ref[...])


def _ffn(x2d, gpre, wg, wu, wd, gpost):
    T, D = x2d.shape
    tm = min(FFN_TM, T)
    xspec = pl.BlockSpec((tm, D), lambda i: (i, 0))
    return pl.pallas_call(
        _ffn_kernel, grid=(T // tm,),
        in_specs=[xspec, _const_spec((1, D)), _const_spec(wg.shape), _const_spec(wu.shape),
                  _const_spec(wd.shape), _const_spec((1, D))],
        out_specs=xspec, out_shape=jax.ShapeDtypeStruct(x2d.shape, x2d.dtype),
        compiler_params=_cparams(("parallel",)), name="ffn",
    )(x2d, gpre, wg, wu, wd, gpost)


def _router_kernel(x_ref, gpre_ref, wr_ref, br_ref, tri_ref,
                   hn_ref, pos_ref, gate_col_ref, pos_col_ref, cnt_ref, run_ref):
    @pl.when(pl.program_id(0) == 0)
    def _():
        run_ref[...] = jnp.zeros_like(run_ref)

    hn = _rms_rows(x_ref[...], gpre_ref[...])
    hn_ref[...] = hn.astype(hn_ref.dtype)
    logits = lax.dot_general(wr_ref[...], hn, (((1,), (1,)), ((), ())),
                             precision=lax.Precision.HIGHEST,
                             preferred_element_type=F32) + br_ref[...]
    e_idx = lax.broadcasted_iota(jnp.int32, logits.shape, 0)
    m1 = jnp.max(logits, axis=0, keepdims=True)
    i1 = jnp.min(jnp.where(logits == m1, e_idx, N_EXPERTS), axis=0, keepdims=True)
    rest = jnp.where(e_idx == i1, -jnp.inf, logits)
    m2 = jnp.max(rest, axis=0, keepdims=True)
    i2 = jnp.min(jnp.where(rest == m2, e_idx, N_EXPERTS), axis=0, keepdims=True)
    e2 = jnp.exp(m2 - m1)
    w1 = 1.0 / (1.0 + e2)
    w2 = e2 / (1.0 + e2)
    gates = jnp.where(e_idx == i1, w1, 0.0) + jnp.where(e_idx == i2, w2, 0.0)

    sel = (e_idx == i1) | (e_idx == i2)
    sel_f = jnp.where(sel, 1.0, 0.0)
    before = _dot(sel_f.astype(BF16), tri_ref[...])
    run = run_ref[:, 0:1]
    pos = jnp.where(sel, run + before, -1.0)
    pos_ref[...] = pos
    tm = pos.shape[1]
    cols = jnp.concatenate([gates, pos, jnp.zeros((V7X_LANES - 2 * N_EXPERTS, tm), F32)], axis=0).T
    for e in range(N_EXPERTS):
        gate_col_ref[e] = cols[:, e:e + 1]
        pos_col_ref[e] = cols[:, N_EXPERTS + e:N_EXPERTS + e + 1]
    cnt = jnp.sum(sel_f, axis=1, keepdims=True)
    cnt_ref[0] = cnt.astype(jnp.int32)
    run_ref[...] = run_ref[...] + cnt


def _router(x2d, gpre, wr_t, br_col):
    T, D = x2d.shape
    tm = MOE_BLOCK
    nb = T // tm
    tri = (np.arange(tm)[:, None] < np.arange(tm)[None, :]).astype(np.float32)
    col = pl.BlockSpec((N_EXPERTS, tm, 1), lambda i: (0, i, 0))
    return pl.pallas_call(
        _router_kernel, grid=(nb,),
        in_specs=[pl.BlockSpec((tm, D), lambda i: (i, 0)), _const_spec((1, D)),
                  _const_spec(wr_t.shape), _const_spec(br_col.shape), _const_spec((tm, tm))],
        out_specs=[pl.BlockSpec((tm, D), lambda i: (i, 0)),
                   pl.BlockSpec((N_EXPERTS, tm), lambda i: (0, i)), col, col,
                   pl.BlockSpec((1, N_EXPERTS, 1), lambda i: (i, 0, 0))],
        out_shape=[jax.ShapeDtypeStruct((T, D), BF16),
                   jax.ShapeDtypeStruct((N_EXPERTS, T), F32),
                   jax.ShapeDtypeStruct((N_EXPERTS, T, 1), F32),
                   jax.ShapeDtypeStruct((N_EXPERTS, T, 1), F32),
                   jax.ShapeDtypeStruct((nb, N_EXPERTS, 1), jnp.int32)],
        scratch_shapes=[pltpu.VMEM((N_EXPERTS, V7X_LANES), F32)],
        compiler_params=_cparams(("arbitrary",)), name="router",
    )(x2d, gpre, wr_t, br_col, jnp.asarray(tri, BF16))


_FIRST, _LAST = 1, 2
_HALF = (4, 8)


def _moe_schedule_kernel(cnt_ref, ea_ref, ba_ref, ta_ref, ra_ref, fa_ref, eb_ref, bb_ref, tb_ref, rb_ref, fb_ref,
                         tile0_ref, start_ref, *, nb, ne):
    tr = MOE_BLOCK
    shift = tr.bit_length() - 1
    assert tr == 1 << shift
    g_max = ea_ref.shape[0]

    def first_tiles(e, tile):
        rows = lax.fori_loop(0, nb, lambda b, acc: acc + cnt_ref[b * ne + e], 0)
        tile0_ref[e] = tile
        start_ref[e] = 0
        return tile + ((rows + tr - 1) >> shift)

    lax.fori_loop(0, ne, first_tiles, 0)

    def append(lists, by_tile, e, b, start, carry):
        e_ref, b_ref, t_ref, r_ref, f_ref = lists
        c = cnt_ref[b * ne + e]
        first = start >> shift
        n = jnp.where(c > 0, ((start + c - 1) >> shift) - first + 1, 0)

        def visit(j, carry):
            g, prev_group = carry
            local = first + j
            tile = tile0_ref[e] + local
            group = tile if by_tile else b
            lo = start - (local << shift)
            opens = group != prev_group

            @pl.when(opens & (g > 0))
            def _():
                f_ref[g - 1] = f_ref[g - 1] | _LAST

            e_ref[g] = e
            b_ref[g] = b
            t_ref[g] = tile
            r_ref[g] = local << shift
            f_ref[g] = (jnp.where(opens, _FIRST, 0) | jnp.where(lo < tr // 2, _HALF[0], 0)
                        | jnp.where(lo + c > tr // 2, _HALF[1], 0))
            return g + 1, group

        return lax.fori_loop(0, n, visit, carry)

    def finish(lists, g):
        e_ref, b_ref, t_ref, r_ref, f_ref = lists
        f_ref[g - 1] = f_ref[g - 1] | _LAST

        def pad(k, _):
            e_ref[k] = e_ref[g - 1]
            b_ref[k] = b_ref[g - 1]
            t_ref[k] = t_ref[g - 1]
            r_ref[k] = r_ref[g - 1]
            f_ref[k] = 0
            return 0

        lax.fori_loop(g, g_max, pad, 0)

    list_a = (ea_ref, ba_ref, ta_ref, ra_ref, fa_ref)
    list_b = (eb_ref, bb_ref, tb_ref, rb_ref, fb_ref)

    def expert_major(e, carry):
        def block(b, inner):
            start, visits = inner
            return start + cnt_ref[b * ne + e], append(list_a, True, e, b, start, visits)
        return lax.fori_loop(0, nb, block, (0, carry))[1]

    g_a, _ = lax.fori_loop(0, ne, expert_major, (0, -1))
    finish(list_a, g_a)

    def block_major(b, carry):
        def expert(e, visits):
            start = start_ref[e]
            start_ref[e] = start + cnt_ref[b * ne + e]
            return append(list_b, False, e, b, start, visits)
        return lax.fori_loop(0, ne, expert, carry)

    g_b, _ = lax.fori_loop(0, nb, block_major, (0, -1))
    finish(list_b, g_b)


def _moe_schedule(cnt, nb, ne, g_max):
    smem = pl.BlockSpec(memory_space=pltpu.SMEM)
    out = jax.ShapeDtypeStruct((g_max,), jnp.int32)
    lists = pl.pallas_call(
        functools.partial(_moe_schedule_kernel, nb=nb, ne=ne),
        in_specs=[smem], out_specs=[smem] * 10, out_shape=[out] * 10,
        scratch_shapes=[pltpu.SMEM((ne,), jnp.int32), pltpu.SMEM((ne,), jnp.int32)],
        name="moe_schedule",
    )(cnt)
    return lists[:5], lists[5:]


def _moe_expert_kernel(e_ref, b_ref, t_ref, r_ref, f_ref, hn_ref, pos_ref, wg_ref, wu_ref, wd_ref,
                       ys_ref, xs_ref):
    g = pl.program_id(0)
    flags = f_ref[g]
    tr = xs_ref.shape[0]

    @pl.when((flags & _FIRST) != 0)
    def _():
        xs_ref[...] = jnp.zeros_like(xs_ref)

    half = tr // 2
    for h in range(2):
        @pl.when((flags & _HALF[h]) != 0)
        def _():
            rows = (lax.broadcasted_iota(jnp.int32, (half, 1), 0) + (r_ref[g] + h * half)).astype(F32)
            onehot = jnp.where(rows == pos_ref[0, 0], 1.0, 0.0).astype(BF16)
            xs_ref[h * half:(h + 1) * half, :] += _dot(onehot, hn_ref[...])

    @pl.when((flags & _LAST) != 0)
    def _():
        y = _swiglu(xs_ref[...].astype(BF16), wg_ref.at[0], wu_ref.at[0], wd_ref.at[0], FFN_TF)
        ys_ref[...] = y.astype(ys_ref.dtype)


def _moe_expert(list_a, hn, pos, wg, wu, wd, n_tiles):
    T, D = hn.shape
    E, _, F = wg.shape
    tr = MOE_BLOCK
    g_max = list_a[0].shape[0]
    pos4 = pos.reshape(E, T // tr, 1, tr)
    wspec_in = pl.BlockSpec((1, D, F), lambda g, e, b, t, r, f: (e[g], 0, 0), pipeline_mode=pl.Buffered(1))
    wspec_out = pl.BlockSpec((1, F, D), lambda g, e, b, t, r, f: (e[g], 0, 0), pipeline_mode=pl.Buffered(1))
    grid_spec = pltpu.PrefetchScalarGridSpec(
        num_scalar_prefetch=5, grid=(g_max,),
        in_specs=[pl.BlockSpec((tr, D), lambda g, e, b, t, r, f: (b[g], 0)),
                  pl.BlockSpec((1, 1, 1, tr), lambda g, e, b, t, r, f: (e[g], b[g], 0, 0)),
                  wspec_in, wspec_in, wspec_out],
        out_specs=pl.BlockSpec((tr, D), lambda g, e, b, t, r, f: (t[g], 0)),
        scratch_shapes=[pltpu.VMEM((tr, D), F32)])
    return pl.pallas_call(
        _moe_expert_kernel, grid_spec=grid_spec,
        out_shape=jax.ShapeDtypeStruct((n_tiles * tr, D), BF16),
        compiler_params=_cparams(("arbitrary",)), name="moe_expert",
    )(*list_a, hn, pos4, wg, wu, wd)


def _moe_combine_kernel(e_ref, b_ref, t_ref, r_ref, f_ref, x_ref, ys_ref, pos_ref, gate_ref, gpost_ref,
                        o_ref, acc_ref):
    g = pl.program_id(0)
    flags = f_ref[g]
    tr = ys_ref.shape[0]

    @pl.when((flags & _FIRST) != 0)
    def _():
        acc_ref[...] = jnp.zeros_like(acc_ref)

    half = tr // 2
    for h in range(2):
        @pl.when((flags & _HALF[h]) != 0)
        def _():
            rows = (lax.broadcasted_iota(jnp.int32, (1, half), 1) + (r_ref[g] + h * half)).astype(F32)
            onehot = jnp.where(pos_ref[0] == rows, 1.0, 0.0).astype(BF16)
            acc_ref[...] += gate_ref[0] * _dot(onehot, ys_ref[h * half:(h + 1) * half, :])

    @pl.when((flags & _LAST) != 0)
    def _():
        o_ref[...] = x_ref[...] + _rms_rows(acc_ref[...], gpost_ref[...])


def _moe_combine(list_b, x2d, ys, pos_col, gate_col, gpost):
    T, D = x2d.shape
    tr = MOE_BLOCK
    g_max = list_b[0].shape[0]
    xspec = pl.BlockSpec((tr, D), lambda g, e, b, t, r, f: (b[g], 0))
    colspec = pl.BlockSpec((1, tr, 1), lambda g, e, b, t, r, f: (e[g], b[g], 0))
    grid_spec = pltpu.PrefetchScalarGridSpec(
        num_scalar_prefetch=5, grid=(g_max,),
        in_specs=[xspec, pl.BlockSpec((tr, D), lambda g, e, b, t, r, f: (t[g], 0)), colspec, colspec,
                  pl.BlockSpec((1, D), lambda g, e, b, t, r, f: (0, 0))],
        out_specs=xspec,
        scratch_shapes=[pltpu.VMEM((tr, D), F32)])
    return pl.pallas_call(
        _moe_combine_kernel, grid_spec=grid_spec,
        out_shape=jax.ShapeDtypeStruct(x2d.shape, x2d.dtype),
        compiler_params=_cparams(("arbitrary",)), name="moe_combine",
    )(*list_b, x2d, ys, pos_col, gate_col, gpost)


def _moe(x2d, gpre, wr_t, br_col, wg, wu, wd, gpost):
    T, D = x2d.shape
    E = wg.shape[0]
    nb = T // MOE_BLOCK
    n_tiles = TOP_K * nb + E
    g_max = E * nb + n_tiles
    hn, pos, gate_col, pos_col, cnt = _router(x2d, gpre, wr_t, br_col)
    list_a, list_b = _moe_schedule(cnt.reshape(nb * E), nb, E, g_max)
    ys = _moe_expert(list_a, hn, pos, wg, wu, wd, n_tiles)
    return _moe_combine(list_b, x2d, ys, pos_col, gate_col, gpost)


def _feature_row_order():
    def diff_block(base):
        x1, x2, rest = [], [], []
        half = DIFF_ROPE_DIM // 2
        for h in range(DIFF_HEADS):
            for c in range(2):
                lo = base + h * 2 * DIFF_QK_DIM + c * DIFF_QK_DIM
                x1 += range(lo, lo + half)
                x2 += range(lo + half, lo + DIFF_ROPE_DIM)
                rest += range(lo + DIFF_ROPE_DIM, lo + DIFF_QK_DIM)
        return x1 + x2 + rest

    order = list(range(0, _S_DQ)) + diff_block(_S_DQ) + diff_block(_S_DK) + list(range(_S_DV, N_FEAT))
    return np.asarray(order, np.int32)


def _mla_q_up_order():
    w = MLA_NOPE_DIM + MLA_ROPE_DIM
    half = MLA_ROPE_DIM // 2
    nope = [h * w + j for h in range(MLA_HEADS) for j in range(MLA_NOPE_DIM)]
    x1 = [h * w + MLA_NOPE_DIM + j for h in range(MLA_HEADS) for j in range(half)]
    x2 = [h * w + MLA_NOPE_DIM + half + j for h in range(MLA_HEADS) for j in range(half)]
    return np.asarray(nope + x1 + x2, np.int32)


def _mla_kv_up_order():
    w = MLA_NOPE_DIM + MLA_V_DIM
    k = [h * w + j for h in range(MLA_HEADS) for j in range(MLA_NOPE_DIM)]
    v = [h * w + MLA_NOPE_DIM + j for h in range(MLA_HEADS) for j in range(MLA_V_DIM)]
    return np.asarray(k + v, np.int32)


def _rope_tables(S):
    def cos_sin(pos, rot_dim):
        inv = ROPE_THETA ** (-jnp.arange(0, rot_dim, 2, dtype=F32) / rot_dim)
        ang = pos.astype(F32)[:, None] * inv[None, :]
        return jnp.cos(ang).T, jnp.sin(ang).T

    rows = S // GRID_W
    pos = jnp.arange(S)
    row = jnp.repeat(jnp.arange(rows), GRID_W)
    col = jnp.tile(jnp.arange(GRID_W), rows)
    cm, sm = cos_sin(pos, MLA_ROPE_DIM)
    cr, sr = cos_sin(row, GQA_HEAD_DIM // 2)
    cc, sc = cos_sin(col, GQA_HEAD_DIM // 2)
    cp, sp = cos_sin(pos, DIFF_ROPE_DIM)
    return (jnp.tile(cm, (MLA_HEADS, 1)), jnp.tile(sm, (MLA_HEADS, 1)),
            jnp.concatenate([cr, cc], axis=0), jnp.concatenate([sr, sc], axis=0),
            jnp.tile(cp, (2 * DIFF_HEADS, 1)), jnp.tile(sp, (2 * DIFF_HEADS, 1)))


def _row(v):
    return v.reshape(1, -1).astype(F32)


def _col(v):
    return v.reshape(-1, 1).astype(F32)


def kernel(x, mem, mix_pre_g, mix_post_g, w_in, mla_q_norm_g, mla_w_q_up, mla_kv_norm_g, mla_w_kv_up,
           gqa_q_norm_g, gqa_k_norm_g, diff_lambda_q1, diff_lambda_k1, diff_lambda_q2, diff_lambda_k2,
           diff_subln_g, w_branch, w_out, mem_pre_g, mem_post_g, mem_norm_g, mem_wq, mem_wkv, mem_wo,
           ffn_pre_g, ffn_post_g, dense_w_gate, dense_w_up, dense_w_down, moe_w_router, moe_b_router,
           moe_w_gate, moe_w_up, moe_w_down):
    B, S, D = x.shape
    M = mem.shape[1]
    depth = w_in.shape[0]
    tables = _rope_tables(S)
    feat_order = _feature_row_order()
    q_up_order = _mla_q_up_order()
    kv_up_order = _mla_kv_up_order()

    for layer in range(depth):
        lambda_init = 0.8 - 0.6 * math.exp(-0.3 * layer)

        w_l = w_in[layer]
        wt = w_l[:, feat_order].T.astype(BF16)
        wg = w_l[:, N_FEAT:].astype(BF16)
        wqu = mla_w_q_up[layer][:, q_up_order].T.astype(BF16)
        wkvu = mla_w_kv_up[layer][:, kv_up_order].T.astype(BF16)
        (qm, km, vm, qg, kg, vg, qd, kd, vd, gates) = _mixer_prep(
            x, _row(mix_pre_g[layer]), wt, wg, wqu, wkvu,
            _col(mla_q_norm_g[layer]), _col(mla_kv_norm_g[layer]),
            _col(gqa_q_norm_g[layer]), _col(gqa_k_norm_g[layer]), tables)
        a_t = _attention(qm, km, vm, kv_of_head=lambda h: h, kcol_of_head=lambda h: h, name="mla_attn")
        b_t = _attention(qg, kg, vg, kv_of_head=lambda h: h // GQA_GROUP, kcol_of_head=lambda h: 0,
                         name="gqa_attn")
        c_t = _diff_attention(qd, kd, vd, _row(diff_lambda_q1[layer]), _row(diff_lambda_k1[layer]),
                              _row(diff_lambda_q2[layer]), _row(diff_lambda_k2[layer]),
                              _col(diff_subln_g[layer]), lambda_init)
        x = _mixer_merge(x, a_t, b_t, c_t, gates, w_branch[layer].astype(BF16),
                         w_out[layer].astype(BF16), _row(mix_post_g[layer]))

        kv = _norm_matmul(mem.reshape(B * M, D), _row(mem_norm_g[layer]), mem_wkv[layer].astype(BF16), BF16)
        x = _mem_attention(x, kv.reshape(B, M, 2 * D), _row(mem_pre_g[layer]), mem_wq[layer].astype(BF16),
                           mem_wo[layer].astype(BF16), _row(mem_post_g[layer]))

        x2d = x.reshape(B * S, D)
        i = layer // 2
        if layer % 2 == 0:
            x2d = _ffn(x2d, _row(ffn_pre_g[layer]), dense_w_gate[i].astype(BF16), dense_w_up[i].astype(BF16),
                       dense_w_down[i].astype(BF16), _row(ffn_post_g[layer]))
        else:
            x2d = _moe(x2d, _row(ffn_pre_g[layer]), moe_w_router[i].T.astype(F32), _col(moe_b_router[i]),
                       moe_w_gate[i].astype(BF16), moe_w_up[i].astype(BF16), moe_w_down[i].astype(BF16),
                       _row(ffn_post_g[layer]))
        x = x2d.reshape(B, S, D)
    return x
```

```python
import functools
import math

import numpy as np
import jax
import jax.numpy as jnp
from jax import lax
from jax.experimental import pallas as pl
from jax.experimental.pallas import tpu as pltpu

F32 = jnp.float32
BF16 = jnp.bfloat16

D_MODEL = 1024
GRID_W = 64
ROPE_THETA = 500000.0
NORM_EPS = 1e-6

MLA_HEADS = 8
MLA_Q_LORA = 384
MLA_KV_LORA = 256
MLA_NOPE_DIM = 64
MLA_ROPE_DIM = 32
MLA_V_DIM = 64

GQA_Q_HEADS = 8
GQA_KV_HEADS = 2
GQA_GROUP = GQA_Q_HEADS // GQA_KV_HEADS
GQA_HEAD_DIM = 64

DIFF_HEADS = 8
DIFF_QK_DIM = 32
DIFF_V_DIM = 2 * DIFF_QK_DIM
DIFF_ROPE_DIM = DIFF_QK_DIM // 4

N_BRANCHES = 3
BRANCH_WIDTH = 512
HEAD_V = 64

MEM_HEADS = 4
MEM_HEAD_DIM = D_MODEL // MEM_HEADS

D_FF = 2816
N_EXPERTS = 8
TOP_K = 2

IN_SPLITS = (MLA_Q_LORA, MLA_KV_LORA, MLA_ROPE_DIM,
             GQA_Q_HEADS * GQA_HEAD_DIM, GQA_KV_HEADS * GQA_HEAD_DIM, GQA_KV_HEADS * GQA_HEAD_DIM,
             DIFF_HEADS * 2 * DIFF_QK_DIM, DIFF_HEADS * 2 * DIFF_QK_DIM, DIFF_HEADS * DIFF_V_DIM,
             N_BRANCHES * D_MODEL)
IN_OFFS = tuple(int(v) for v in np.cumsum((0,) + IN_SPLITS))
N_FEAT = IN_OFFS[9]
N_GATE = IN_SPLITS[9]

LOG2E = 1.4426950408889634
V7X_LANES = 128
QK_PAD = V7X_LANES
ONES_ROWS = 16

V7X_VMEM_LIMIT_BYTES = 56 * 1024 * 1024

PREP_TM = 512
ATTN_TQ = 1024
ATTN_TK = 256
ATTN_UNROLL = 32
MERGE_TM = 512
MEM_TM = 1024
FFN_TM = 512
FFN_TF = 2816
MOE_BLOCK = 512
KV_TM = 256


def _cparams(sem):
    return pltpu.CompilerParams(dimension_semantics=sem, vmem_limit_bytes=V7X_VMEM_LIMIT_BYTES)


def _const_spec(shape):
    nd = len(shape)
    return pl.BlockSpec(shape, lambda *_: (0,) * nd, pipeline_mode=pl.Buffered(1))


def _rms_rows(x, g_row):
    ms = jnp.mean(x * x, axis=-1, keepdims=True)
    return x * lax.rsqrt(ms + NORM_EPS) * g_row


def _rms_cols(x, g_col):
    ms = jnp.mean(x * x, axis=0, keepdims=True)
    return x * lax.rsqrt(ms + NORM_EPS) * g_col


def _sigmoid(x):
    return 0.5 * jnp.tanh(0.5 * x) + 0.5


def _dot(a, b):
    return jnp.dot(a, b, preferred_element_type=F32)


def _dot_nt(a, b):
    return lax.dot_general(a, b, (((1,), (1,)), ((), ())), preferred_element_type=F32)


_S_CQ, _S_CKV, _S_KR, _S_GQ, _S_GK, _S_GV, _S_DQ, _S_DK, _S_DV = IN_OFFS[:9]


def _rope_rows(x1, x2, c, s):
    return x1 * c - x2 * s, x2 * c + x1 * s


def _mixer_prep_kernel(x_ref, g_ref, wt_ref, wg_ref, wqu_ref, wkvu_ref,
                       mqg_ref, mkvg_ref, gqg_ref, gkg_ref,
                       cm_ref, sm_ref, ca_ref, sa_ref, cp_ref, sp_ref,
                       qm_ref, km_ref, vm_ref, qg_ref, kg_ref, vg_ref,
                       qd_ref, kd_ref, vd_ref, gate_ref):
    tm = x_ref.shape[1]
    hn = _rms_rows(x_ref[0], g_ref[...]).astype(BF16)

    z_all = _dot_nt(wt_ref[...], hn)

    def feat(lo, n):
        return z_all[lo:lo + n]

    gate_ref[0] = _sigmoid(_dot(hn, wg_ref[...])).astype(gate_ref.dtype)

    zeros32 = jnp.zeros((32, tm), F32)
    zeros64 = jnp.zeros((64, tm), F32)

    c_mla = (MLA_NOPE_DIM + MLA_ROPE_DIM) ** -0.5 * LOG2E
    cm = cm_ref[...]
    sm = sm_ref[...]
    cqn = _rms_cols(feat(_S_CQ, MLA_Q_LORA), mqg_ref[...]).astype(BF16)
    q_all = _dot(wqu_ref[...], cqn)
    q_nope = q_all[0:512] * c_mla
    q_r1, q_r2 = _rope_rows(q_all[512:640], q_all[640:768], cm, sm)
    q_r1 = q_r1 * c_mla
    q_r2 = q_r2 * c_mla
    for h in range(MLA_HEADS):
        qm_ref[0, h] = jnp.concatenate(
            [q_nope[64 * h:64 * h + 64], q_r1[16 * h:16 * h + 16], q_r2[16 * h:16 * h + 16], zeros32],
            axis=0).astype(qm_ref.dtype)

    ckvn = _rms_cols(feat(_S_CKV, MLA_KV_LORA), mkvg_ref[...]).astype(BF16)
    kv_all = _dot(wkvu_ref[...], ckvn)
    vm_ref[0] = kv_all[512:1024].astype(vm_ref.dtype)
    kr = feat(_S_KR, MLA_ROPE_DIM)
    k_r1, k_r2 = _rope_rows(kr[0:16], kr[16:32], cm[0:16], sm[0:16])
    for h in range(MLA_HEADS):
        kt = jnp.concatenate([kv_all[64 * h:64 * h + 64], k_r1, k_r2, zeros32], axis=0)
        km_ref[0, :, QK_PAD * h:QK_PAD * (h + 1)] = kt.T.astype(km_ref.dtype)

    c_gqa = GQA_HEAD_DIM ** -0.5 * LOG2E
    ca = ca_ref[...]
    sa = sa_ref[...]
    gq = feat(_S_GQ, GQA_Q_HEADS * GQA_HEAD_DIM)
    for h in range(GQA_Q_HEADS):
        qn = _rms_cols(gq[64 * h:64 * h + 64], gqg_ref[...])
        r1, r2 = _rope_rows(qn[0:32], qn[32:64], ca, sa)
        q64 = jnp.concatenate([r1, r2], axis=0) * c_gqa
        parts = [q64, zeros64] if h // GQA_GROUP == 0 else [zeros64, q64]
        qg_ref[0, h] = jnp.concatenate(parts, axis=0).astype(qg_ref.dtype)
    gk = feat(_S_GK, GQA_KV_HEADS * GQA_HEAD_DIM)
    kparts = []
    for g in range(GQA_KV_HEADS):
        kn = _rms_cols(gk[64 * g:64 * g + 64], gkg_ref[...])
        r1, r2 = _rope_rows(kn[0:32], kn[32:64], ca, sa)
        kparts += [r1, r2]
    kg_ref[0] = jnp.concatenate(kparts, axis=0).T.astype(kg_ref.dtype)
    vg_ref[0] = feat(_S_GV, GQA_KV_HEADS * GQA_HEAD_DIM).astype(vg_ref.dtype)

    c_diff = DIFF_QK_DIM ** -0.5 * LOG2E
    cp = cp_ref[...]
    sp = sp_ref[...]
    row = lax.broadcasted_iota(jnp.int32, (64, 1), 0)
    in_c0 = (row < 4) | ((row >= 8) & (row < 12)) | ((row >= 16) & (row < 40))

    def diff_heads(lo):
        z = feat(lo, DIFF_HEADS * 2 * DIFF_QK_DIM)
        r1, r2 = _rope_rows(z[0:64], z[64:128], cp, sp)
        rest = z[128:512]
        return [jnp.concatenate([r1[8 * h:8 * h + 8], r2[8 * h:8 * h + 8], rest[48 * h:48 * h + 48]], axis=0)
                for h in range(DIFF_HEADS)]

    for h, q64 in enumerate(diff_heads(_S_DQ)):
        q64 = q64 * c_diff
        for c in range(2):
            qc = jnp.where(in_c0 if c == 0 else jnp.logical_not(in_c0), q64, 0.0)
            parts = [qc, zeros64] if h % 2 == 0 else [zeros64, qc]
            qd_ref[0, 2 * h + c] = jnp.concatenate(parts, axis=0).astype(qd_ref.dtype)
    k_heads = diff_heads(_S_DK)
    for p in range(DIFF_HEADS // 2):
        kt = jnp.concatenate([k_heads[2 * p], k_heads[2 * p + 1]], axis=0)
        kd_ref[0, :, QK_PAD * p:QK_PAD * (p + 1)] = kt.T.astype(kd_ref.dtype)
    vd_ref[0] = feat(_S_DV, DIFF_HEADS * DIFF_V_DIM).astype(vd_ref.dtype)


def _mixer_prep(x, g, wt, wg, wqu, wkvu, mqg, mkvg, gqg, gkg, tables):
    B, S, D = x.shape
    tm = min(PREP_TM, S)
    cm, sm, ca, sa, cp, sp = tables
    grid = (B, S // tm)

    def tok(width):
        return pl.BlockSpec((1, tm, width), lambda b, i: (b, i, 0))

    def featm(rows):
        return pl.BlockSpec((1, rows, tm), lambda b, i: (b, 0, i))

    def heads(n):
        return pl.BlockSpec((1, n, QK_PAD, tm), lambda b, i: (b, 0, 0, i))

    def table(rows):
        return pl.BlockSpec((rows, tm), lambda b, i: (0, i))

    in_specs = [
        pl.BlockSpec((1, tm, D), lambda b, i: (b, i, 0)),
        _const_spec((1, D)),
        _const_spec(wt.shape), _const_spec(wg.shape), _const_spec(wqu.shape), _const_spec(wkvu.shape),
        _const_spec(mqg.shape), _const_spec(mkvg.shape), _const_spec(gqg.shape), _const_spec(gkg.shape),
        table(128), table(128), table(32), table(32), table(64), table(64),
    ]
    out_shape = [
        jax.ShapeDtypeStruct((B, MLA_HEADS, QK_PAD, S), BF16),
        jax.ShapeDtypeStruct((B, S, MLA_HEADS * QK_PAD), BF16),
        jax.ShapeDtypeStruct((B, MLA_HEADS * HEAD_V, S), BF16),
        jax.ShapeDtypeStruct((B, GQA_Q_HEADS, QK_PAD, S), BF16),
        jax.ShapeDtypeStruct((B, S, QK_PAD), BF16),
        jax.ShapeDtypeStruct((B, GQA_KV_HEADS * HEAD_V, S), BF16),
        jax.ShapeDtypeStruct((B, 2 * DIFF_HEADS, QK_PAD, S), BF16),
        jax.ShapeDtypeStruct((B, S, DIFF_HEADS // 2 * QK_PAD), BF16),
        jax.ShapeDtypeStruct((B, DIFF_HEADS * HEAD_V, S), BF16),
        jax.ShapeDtypeStruct((B, S, N_GATE), BF16),
    ]
    out_specs = [
        heads(MLA_HEADS), tok(MLA_HEADS * QK_PAD), featm(MLA_HEADS * HEAD_V),
        heads(GQA_Q_HEADS), tok(QK_PAD), featm(GQA_KV_HEADS * HEAD_V),
        heads(2 * DIFF_HEADS), tok(DIFF_HEADS // 2 * QK_PAD), featm(DIFF_HEADS * HEAD_V),
        tok(N_GATE),
    ]
    return pl.pallas_call(
        _mixer_prep_kernel, grid=grid, in_specs=in_specs, out_specs=out_specs, out_shape=out_shape,
        compiler_params=_cparams(("parallel", "parallel")), name="mixer_prep",
    )(x, g, wt, wg, wqu, wkvu, mqg, mkvg, gqg, gkg, cm, sm, ca, sa, cp, sp)


def _online_softmax_attention(q_t, load_q_next, k_ref, v_ref, s_ref, cmax_ref, qq_ref, *, tk):
    S = k_ref.shape[1]
    lanes = q_t.shape[1]
    n_chunks = S // tk
    unroll = min(ATTN_UNROLL, n_chunks)
    assert unroll % 2 == 0 and n_chunks % unroll == 0
    ones = jnp.ones((ONES_ROWS, tk), BF16)

    def scores(c, slot, q=q_t):
        s_t = _dot(k_ref[0, pl.ds(pl.multiple_of(c * tk, tk), tk), :], q)
        s_ref[slot] = s_t
        return jnp.max(s_t, axis=0, keepdims=True)

    def update(c, slot, col_max, m, acc):
        v_aug = jnp.concatenate([v_ref[0, :, pl.ds(pl.multiple_of(c * tk, tk), tk)], ones], axis=0)
        m_new = jnp.maximum(m, col_max)
        p = jnp.exp2(s_ref[slot] - m_new).astype(BF16)
        return m_new, acc * jnp.exp2(m - m_new) + _dot(v_aug, p)

    n_trips = n_chunks // unroll
    qq_ref[0] = q_t
    qq_ref[1] = load_q_next()

    def body(t, carry):
        cmax, m, acc = carry
        c = unroll * t
        last = t == n_trips - 1
        for u in range(unroll):
            if u == unroll - 1:
                if isinstance(last, bool):
                    chunk, tile = (0, 1) if last else (c + unroll, 0)
                else:
                    chunk, tile = jnp.where(last, 0, c + unroll), jnp.where(last, 1, 0)
                cmax_next = scores(chunk, 0, qq_ref[tile])
            else:
                cmax_next = scores(c + u + 1, (u + 1) % 2)
            m, acc = update(c + u, u % 2, cmax, m, acc)
            cmax = cmax_next
        return cmax, m, acc

    @pl.when(pl.program_id(2) == 0)
    def _():
        cmax_ref[...] = jnp.broadcast_to(scores(0, 0), cmax_ref.shape)

    cmax0 = jnp.max(cmax_ref[...], axis=0, keepdims=True)
    m0 = jnp.full((1, lanes), -jnp.inf, F32)
    acc0 = jnp.zeros((HEAD_V + ONES_ROWS, lanes), F32)
    carry = (cmax0, m0, acc0)
    cmax_next, _, acc = body(0, carry) if n_trips == 1 else lax.fori_loop(0, n_trips, body, carry)
    cmax_ref[...] = jnp.broadcast_to(cmax_next, cmax_ref.shape)
    return acc[:HEAD_V] / acc[HEAD_V:HEAD_V + 1]


def _attn_kernel(q_ref, qn_ref, k_ref, v_ref, o_ref, s_ref, cmax_ref, qq_ref, *, tk):
    o = _online_softmax_attention(q_ref[0, 0], lambda: qn_ref[0, 0], k_ref, v_ref, s_ref, cmax_ref, qq_ref,
                                  tk=tk)
    o_ref[0] = o.astype(o_ref.dtype)


def _next_tile(n_tiles):
    return lambda i: jnp.minimum(i + 1, n_tiles - 1)


def _attention(q_t, k, v_t, *, kv_of_head, kcol_of_head, name):
    B, H, _, S = q_t.shape
    tq = min(ATTN_TQ, S)
    tk = min(ATTN_TK, S // 2)
    if name == "gqa_attn":
        tq, tk = 512, 512
    nxt = _next_tile(S // tq)
    return pl.pallas_call(
        functools.partial(_attn_kernel, tk=tk),
        grid=(B, H, S // tq),
        in_specs=[
            pl.BlockSpec((1, 1, QK_PAD, tq), lambda b, h, i: (b, h, 0, i)),
            pl.BlockSpec((1, 1, QK_PAD, tq), lambda b, h, i: (b, h, 0, nxt(i))),
            pl.BlockSpec((1, S, QK_PAD), lambda b, h, i: (b, 0, kcol_of_head(h))),
            pl.BlockSpec((1, HEAD_V, S), lambda b, h, i: (b, kv_of_head(h), 0)),
        ],
        out_specs=pl.BlockSpec((1, HEAD_V, tq), lambda b, h, i: (b, h, i)),
        out_shape=jax.ShapeDtypeStruct((B, H * HEAD_V, S), F32),
        scratch_shapes=[pltpu.VMEM((2, tk, tq), F32), pltpu.VMEM((8, tq), F32),
                        pltpu.VMEM((2, QK_PAD, tq), BF16)],
        compiler_params=_cparams(("parallel", "parallel", "arbitrary")), name=name,
    )(q_t, q_t, k, v_t)


def _diff_attn_kernel(lq1_ref, lk1_ref, lq2_ref, lk2_ref, g_ref, q_ref, qn_ref, k_ref, v_ref, o_ref,
                      s_ref, cmax_ref, qq_ref, *, tk, lambda_init):
    tq = q_ref.shape[3]
    q_both = jnp.concatenate([q_ref[0, 0], q_ref[0, 1]], axis=1)
    o_maps = _online_softmax_attention(
        q_both, lambda: jnp.concatenate([qn_ref[0, 0], qn_ref[0, 1]], axis=1), k_ref, v_ref, s_ref, cmax_ref,
        qq_ref, tk=tk)
    lam = (jnp.exp(jnp.sum(lq1_ref[...] * lk1_ref[...], axis=-1, keepdims=True))
           - jnp.exp(jnp.sum(lq2_ref[...] * lk2_ref[...], axis=-1, keepdims=True)) + lambda_init)
    o = o_maps[:, :tq] - lam * o_maps[:, tq:]
    o = _rms_cols(o, g_ref[...]) * (1.0 - lambda_init)
    o_ref[0] = o.astype(o_ref.dtype)


def _diff_attention(q_t, k, v_t, lq1, lk1, lq2, lk2, subln_g, lambda_init):
    B, H2, _, S = q_t.shape
    H = H2 // 2
    tq = min(ATTN_TQ // 2, S)
    tk = min(ATTN_TK, S // 2)
    nxt = _next_tile(S // tq)
    vec = _const_spec((1, DIFF_QK_DIM))
    return pl.pallas_call(
        functools.partial(_diff_attn_kernel, tk=tk, lambda_init=lambda_init),
        grid=(B, H, S // tq),
        in_specs=[
            vec, vec, vec, vec, _const_spec((DIFF_V_DIM, 1)),
            pl.BlockSpec((1, 2, QK_PAD, tq), lambda b, h, i: (b, h, 0, i)),
            pl.BlockSpec((1, 2, QK_PAD, tq), lambda b, h, i: (b, h, 0, nxt(i))),
            pl.BlockSpec((1, S, QK_PAD), lambda b, h, i: (b, 0, h // 2)),
            pl.BlockSpec((1, HEAD_V, S), lambda b, h, i: (b, h, 0)),
        ],
        out_specs=pl.BlockSpec((1, HEAD_V, tq), lambda b, h, i: (b, h, i)),
        out_shape=jax.ShapeDtypeStruct((B, H * HEAD_V, S), F32),
        scratch_shapes=[pltpu.VMEM((2, tk, 2 * tq), F32), pltpu.VMEM((8, 2 * tq), F32),
                        pltpu.VMEM((2, QK_PAD, 2 * tq), BF16)],
        compiler_params=_cparams(("parallel", "parallel", "arbitrary")), name="diff_attn",
    )(lq1, lk1, lq2, lk2, subln_g, q_t, q_t, k, v_t)


def _mixer_merge_kernel(x_ref, a_ref, b_ref, c_ref, gate_ref, wb_ref, wo_ref, g_ref, o_ref):
    merged = None
    for n, br_ref in enumerate((a_ref, b_ref, c_ref)):
        br = br_ref[0].T.astype(BF16)
        proj = _dot(br, wb_ref[n])
        term = gate_ref[0, :, n * D_MODEL:(n + 1) * D_MODEL].astype(F32) * proj
        merged = term if merged is None else merged + term
    y = _dot(merged.astype(BF16), wo_ref[...])
    o_ref[0] = x_ref[0] + _rms_rows(y, g_ref[...])


def _mixer_merge(x, a_t, b_t, c_t, gates, wb, wo, g):
    B, S, D = x.shape
    tm = min(MERGE_TM, S)
    xspec = pl.BlockSpec((1, tm, D), lambda b, i: (b, i, 0))
    brspec = pl.BlockSpec((1, BRANCH_WIDTH, tm), lambda b, i: (b, 0, i))
    return pl.pallas_call(
        _mixer_merge_kernel, grid=(B, S // tm),
        in_specs=[xspec, brspec, brspec, brspec,
                  pl.BlockSpec((1, tm, N_GATE), lambda b, i: (b, i, 0)),
                  _const_spec(wb.shape), _const_spec(wo.shape), _const_spec((1, D))],
        out_specs=xspec, out_shape=jax.ShapeDtypeStruct(x.shape, x.dtype),
        compiler_params=_cparams(("parallel", "parallel")), name="mixer_merge",
    )(x, a_t, b_t, c_t, gates, wb, wo, g)


def _norm_matmul_kernel(x_ref, g_ref, w_ref, o_ref):
    hn = _rms_rows(x_ref[...], g_ref[...]).astype(BF16)
    o_ref[...] = _dot(hn, w_ref[...]).astype(o_ref.dtype)


def _norm_matmul(x2d, g, w, out_dtype):
    T, D = x2d.shape
    N = w.shape[1]
    tm = min(KV_TM, T)
    return pl.pallas_call(
        _norm_matmul_kernel, grid=(T // tm,),
        in_specs=[pl.BlockSpec((tm, D), lambda i: (i, 0)), _const_spec((1, D)), _const_spec(w.shape)],
        out_specs=pl.BlockSpec((tm, N), lambda i: (i, 0)),
        out_shape=jax.ShapeDtypeStruct((T, N), out_dtype),
        compiler_params=_cparams(("parallel",)), name="mem_kv",
    )(x2d, g, w)


def _mem_attn_kernel(x_ref, kv_ref, gpre_ref, wq_ref, wo_ref, gpost_ref, o_ref):
    x = x_ref[0]
    hn = _rms_rows(x, gpre_ref[...]).astype(BF16)
    q = (_dot(hn, wq_ref[...]) * (MEM_HEAD_DIM ** -0.5 * LOG2E)).astype(BF16)
    outs = []
    for h in range(MEM_HEADS):
        lo = h * MEM_HEAD_DIM
        k_h = kv_ref[0, :, lo:lo + MEM_HEAD_DIM]
        v_h = kv_ref[0, :, D_MODEL + lo:D_MODEL + lo + MEM_HEAD_DIM]
        s = _dot_nt(q[:, lo:lo + MEM_HEAD_DIM], k_h)
        e = jnp.exp2(s - jnp.max(s, axis=-1, keepdims=True))
        p = (e / jnp.sum(e, axis=-1, keepdims=True)).astype(BF16)
        outs.append(_dot(p, v_h))
    o = jnp.concatenate(outs, axis=-1).astype(BF16)
    y = _dot(o, wo_ref[...])
    o_ref[0] = x + _rms_rows(y, gpost_ref[...])


def _mem_attention(x, kv, gpre, wq, wo, gpost):
    B, S, D = x.shape
    M = kv.shape[1]
    tm = min(MEM_TM, S)
    xspec = pl.BlockSpec((1, tm, D), lambda b, i: (b, i, 0))
    return pl.pallas_call(
        _mem_attn_kernel, grid=(B, S // tm),
        in_specs=[xspec, pl.BlockSpec((1, M, 2 * D), lambda b, i: (b, 0, 0)),
                  _const_spec((1, D)), _const_spec(wq.shape), _const_spec(wo.shape), _const_spec((1, D))],
        out_specs=xspec, out_shape=jax.ShapeDtypeStruct(x.shape, x.dtype),
        compiler_params=_cparams(("parallel", "parallel")), name="mem_attn",
    )(x, kv, gpre, wq, wo, gpost)


def _swiglu_chunk(hn, wg, wu, wd):
    gate = _dot(hn, wg)
    act = (gate * _sigmoid(gate) * _dot(hn, wu)).astype(BF16)
    return _dot(act, wd)


def _swiglu(hn, wg_ref, wu_ref, wd_ref, tf):
    y = None
    for lo in range(0, wg_ref.shape[-1], tf):
        part = _swiglu_chunk(hn, wg_ref[:, lo:lo + tf], wu_ref[:, lo:lo + tf], wd_ref[lo:lo + tf, :])
        y = part if y is None else y + part
    return y


def _ffn_kernel(x_ref, gpre_ref, wg_ref, wu_ref, wd_ref, gpost_ref, o_ref):
    x = x_ref[...]
    hn = _rms_rows(x, gpre_ref[...]).astype(BF16)
    o_ref[...] = x + _rms_rows(_swiglu(hn, wg_ref, wu_ref, wd_ref, FFN_TF), gpost_ref[...])


def _ffn(x2d, gpre, wg, wu, wd, gpost):
    T, D = x2d.shape
    tm = min(FFN_TM, T)
    xspec = pl.BlockSpec((tm, D), lambda i: (i, 0))
    return pl.pallas_call(
        _ffn_kernel, grid=(T // tm,),
        in_specs=[xspec, _const_spec((1, D)), _const_spec(wg.shape), _const_spec(wu.shape),
                  _const_spec(wd.shape), _const_spec((1, D))],
        out_specs=xspec, out_shape=jax.ShapeDtypeStruct(x2d.shape, x2d.dtype),
        compiler_params=_cparams(("parallel",)), name="ffn",
    )(x2d, gpre, wg, wu, wd, gpost)


def _router_kernel(x_ref, gpre_ref, wr_ref, br_ref, tri_ref,
                   hn_ref, pos_ref, gate_col_ref, pos_col_ref, cnt_ref, run_ref):
    @pl.when(pl.program_id(0) == 0)
    def _():
        run_ref[...] = jnp.zeros_like(run_ref)

    hn = _rms_rows(x_ref[...], gpre_ref[...])
    hn_ref[...] = hn.astype(hn_ref.dtype)
    logits = lax.dot_general(wr_ref[...], hn, (((1,), (1,)), ((), ())),
                             precision=lax.Precision.HIGHEST,
                             preferred_element_type=F32) + br_ref[...]
    e_idx = lax.broadcasted_iota(jnp.int32, logits.shape, 0)
    m1 = jnp.max(logits, axis=0, keepdims=True)
    i1 = jnp.min(jnp.where(logits == m1, e_idx, N_EXPERTS), axis=0, keepdims=True)
    rest = jnp.where(e_idx == i1, -jnp.inf, logits)
    m2 = jnp.max(rest, axis=0, keepdims=True)
    i2 = jnp.min(jnp.where(rest == m2, e_idx, N_EXPERTS), axis=0, keepdims=True)
    e2 = jnp.exp(m2 - m1)
    w1 = 1.0 / (1.0 + e2)
    w2 = e2 / (1.0 + e2)
    gates = jnp.where(e_idx == i1, w1, 0.0) + jnp.where(e_idx == i2, w2, 0.0)

    sel = (e_idx == i1) | (e_idx == i2)
    sel_f = jnp.where(sel, 1.0, 0.0)
    before = _dot(sel_f.astype(BF16), tri_ref[...])
    run = run_ref[:, 0:1]
    pos = jnp.where(sel, run + before, -1.0)
    pos_ref[...] = pos
    tm = pos.shape[1]
    cols = jnp.concatenate([gates, pos, jnp.zeros((V7X_LANES - 2 * N_EXPERTS, tm), F32)], axis=0).T
    for e in range(N_EXPERTS):
        gate_col_ref[e] = cols[:, e:e + 1]
        pos_col_ref[e] = cols[:, N_EXPERTS + e:N_EXPERTS + e + 1]
    cnt = jnp.sum(sel_f, axis=1, keepdims=True)
    cnt_ref[0] = cnt.astype(jnp.int32)
    run_ref[...] = run_ref[...] + cnt


def _router(x2d, gpre, wr_t, br_col):
    T, D = x2d.shape
    tm = MOE_BLOCK
    nb = T // tm
    tri = (np.arange(tm)[:, None] < np.arange(tm)[None, :]).astype(np.float32)
    col = pl.BlockSpec((N_EXPERTS, tm, 1), lambda i: (0, i, 0))
    return pl.pallas_call(
        _router_kernel, grid=(nb,),
        in_specs=[pl.BlockSpec((tm, D), lambda i: (i, 0)), _const_spec((1, D)),
                  _const_spec(wr_t.shape), _const_spec(br_col.shape), _const_spec((tm, tm))],
        out_specs=[pl.BlockSpec((tm, D), lambda i: (i, 0)),
                   pl.BlockSpec((N_EXPERTS, tm), lambda i: (0, i)), col, col,
                   pl.BlockSpec((1, N_EXPERTS, 1), lambda i: (i, 0, 0))],
        out_shape=[jax.ShapeDtypeStruct((T, D), BF16),
                   jax.ShapeDtypeStruct((N_EXPERTS, T), F32),
                   jax.ShapeDtypeStruct((N_EXPERTS, T, 1), F32),
                   jax.ShapeDtypeStruct((N_EXPERTS, T, 1), F32),
                   jax.ShapeDtypeStruct((nb, N_EXPERTS, 1), jnp.int32)],
        scratch_shapes=[pltpu.VMEM((N_EXPERTS, V7X_LANES), F32)],
        compiler_params=_cparams(("arbitrary",)), name="router",
    )(x2d, gpre, wr_t, br_col, jnp.asarray(tri, BF16))


_FIRST, _LAST = 1, 2
_HALF = (4, 8)


def _moe_schedule_kernel(cnt_ref, ea_ref, ba_ref, ta_ref, ra_ref, fa_ref, eb_ref, bb_ref, tb_ref, rb_ref, fb_ref,
                         tile0_ref, start_ref, *, nb, ne):
    tr = MOE_BLOCK
    shift = tr.bit_length() - 1
    assert tr == 1 << shift
    g_max = ea_ref.shape[0]

    def first_tiles(e, tile):
        rows = lax.fori_loop(0, nb, lambda b, acc: acc + cnt_ref[b * ne + e], 0)
        tile0_ref[e] = tile
        start_ref[e] = 0
        return tile + ((rows + tr - 1) >> shift)

    lax.fori_loop(0, ne, first_tiles, 0)

    def append(lists, by_tile, e, b, start, carry):
        e_ref, b_ref, t_ref, r_ref, f_ref = lists
        c = cnt_ref[b * ne + e]
        first = start >> shift
        n = jnp.where(c > 0, ((start + c - 1) >> shift) - first + 1, 0)

        def visit(j, carry):
            g, prev_group = carry
            local = first + j
            tile = tile0_ref[e] + local
            group = tile if by_tile else b
            lo = start - (local << shift)
            opens = group != prev_group

            @pl.when(opens & (g > 0))
            def _():
                f_ref[g - 1] = f_ref[g - 1] | _LAST

            e_ref[g] = e
            b_ref[g] = b
            t_ref[g] = tile
            r_ref[g] = local << shift
            f_ref[g] = (jnp.where(opens, _FIRST, 0) | jnp.where(lo < tr // 2, _HALF[0], 0)
                        | jnp.where(lo + c > tr // 2, _HALF[1], 0))
            return g + 1, group

        return lax.fori_loop(0, n, visit, carry)

    def finish(lists, g):
        e_ref, b_ref, t_ref, r_ref, f_ref = lists
        f_ref[g - 1] = f_ref[g - 1] | _LAST

        def pad(k, _):
            e_ref[k] = e_ref[g - 1]
            b_ref[k] = b_ref[g - 1]
            t_ref[k] = t_ref[g - 1]
            r_ref[k] = r_ref[g - 1]
            f_ref[k] = 0
            return 0

        lax.fori_loop(g, g_max, pad, 0)

    list_a = (ea_ref, ba_ref, ta_ref, ra_ref, fa_ref)
    list_b = (eb_ref, bb_ref, tb_ref, rb_ref, fb_ref)

    def expert_major(e, carry):
        def block(b, inner):
            start, visits = inner
            return start + cnt_ref[b * ne + e], append(list_a, True, e, b, start, visits)
        return lax.fori_loop(0, nb, block, (0, carry))[1]

    g_a, _ = lax.fori_loop(0, ne, expert_major, (0, -1))
    finish(list_a, g_a)

    def block_major(b, carry):
        def expert(e, visits):
            start = start_ref[e]
            start_ref[e] = start + cnt_ref[b * ne + e]
            return append(list_b, False, e, b, start, visits)
        return lax.fori_loop(0, ne, expert, carry)

    g_b, _ = lax.fori_loop(0, nb, block_major, (0, -1))
    finish(list_b, g_b)


def _moe_schedule(cnt, nb, ne, g_max):
    smem = pl.BlockSpec(memory_space=pltpu.SMEM)
    out = jax.ShapeDtypeStruct((g_max,), jnp.int32)
    lists = pl.pallas_call(
        functools.partial(_moe_schedule_kernel, nb=nb, ne=ne),
        in_specs=[smem], out_specs=[smem] * 10, out_shape=[out] * 10,
        scratch_shapes=[pltpu.SMEM((ne,), jnp.int32), pltpu.SMEM((ne,), jnp.int32)],
        name="moe_schedule",
    )(cnt)
    return lists[:5], lists[5:]


def _moe_expert_kernel(e_ref, b_ref, t_ref, r_ref, f_ref, hn_ref, pos_ref, wg_ref, wu_ref, wd_ref,
                       ys_ref, xs_ref):
    g = pl.program_id(0)
    flags = f_ref[g]
    tr = xs_ref.shape[0]

    @pl.when((flags & _FIRST) != 0)
    def _():
        xs_ref[...] = jnp.zeros_like(xs_ref)

    half = tr // 2
    for h in range(2):
        @pl.when((flags & _HALF[h]) != 0)
        def _():
            rows = (lax.broadcasted_iota(jnp.int32, (half, 1), 0) + (r_ref[g] + h * half)).astype(F32)
            onehot = jnp.where(rows == pos_ref[0, 0], 1.0, 0.0).astype(BF16)
            xs_ref[h * half:(h + 1) * half, :] += _dot(onehot, hn_ref[...])

    @pl.when((flags & _LAST) != 0)
    def _():
        y = _swiglu(xs_ref[...].astype(BF16), wg_ref.at[0], wu_ref.at[0], wd_ref.at[0], FFN_TF)
        ys_ref[...] = y.astype(ys_ref.dtype)


def _moe_expert(list_a, hn, pos, wg, wu, wd, n_tiles):
    T, D = hn.shape
    E, _, F = wg.shape
    tr = MOE_BLOCK
    g_max = list_a[0].shape[0]
    pos4 = pos.reshape(E, T // tr, 1, tr)
    wspec_in = pl.BlockSpec((1, D, F), lambda g, e, b, t, r, f: (e[g], 0, 0), pipeline_mode=pl.Buffered(1))
    wspec_out = pl.BlockSpec((1, F, D), lambda g, e, b, t, r, f: (e[g], 0, 0), pipeline_mode=pl.Buffered(1))
    grid_spec = pltpu.PrefetchScalarGridSpec(
        num_scalar_prefetch=5, grid=(g_max,),
        in_specs=[pl.BlockSpec((tr, D), lambda g, e, b, t, r, f: (b[g], 0)),
                  pl.BlockSpec((1, 1, 1, tr), lambda g, e, b, t, r, f: (e[g], b[g], 0, 0)),
                  wspec_in, wspec_in, wspec_out],
        out_specs=pl.BlockSpec((tr, D), lambda g, e, b, t, r, f: (t[g], 0)),
        scratch_shapes=[pltpu.VMEM((tr, D), F32)])
    return pl.pallas_call(
        _moe_expert_kernel, grid_spec=grid_spec,
        out_shape=jax.ShapeDtypeStruct((n_tiles * tr, D), BF16),
        compiler_params=_cparams(("arbitrary",)), name="moe_expert",
    )(*list_a, hn, pos4, wg, wu, wd)


def _moe_combine_kernel(e_ref, b_ref, t_ref, r_ref, f_ref, x_ref, ys_ref, pos_ref, gate_ref, gpost_ref,
                        o_ref, acc_ref):
    g = pl.program_id(0)
    flags = f_ref[g]
    tr = ys_ref.shape[0]

    @pl.when((flags & _FIRST) != 0)
    def _():
        acc_ref[...] = jnp.zeros_like(acc_ref)

    half = tr // 2
    for h in range(2):
        @pl.when((flags & _HALF[h]) != 0)
        def _():
            rows = (lax.broadcasted_iota(jnp.int32, (1, half), 1) + (r_ref[g] + h * half)).astype(F32)
            onehot = jnp.where(pos_ref[0] == rows, 1.0, 0.0).astype(BF16)
            acc_ref[...] += gate_ref[0] * _dot(onehot, ys_ref[h * half:(h + 1) * half, :])

    @pl.when((flags & _LAST) != 0)
    def _():
        o_ref[...] = x_ref[...] + _rms_rows(acc_ref[...], gpost_ref[...])


def _moe_combine(list_b, x2d, ys, pos_col, gate_col, gpost):
    T, D = x2d.shape
    tr = MOE_BLOCK
    g_max = list_b[0].shape[0]
    xspec = pl.BlockSpec((tr, D), lambda g, e, b, t, r, f: (b[g], 0))
    colspec = pl.BlockSpec((1, tr, 1), lambda g, e, b, t, r, f: (e[g], b[g], 0))
    grid_spec = pltpu.PrefetchScalarGridSpec(
        num_scalar_prefetch=5, grid=(g_max,),
        in_specs=[xspec, pl.BlockSpec((tr, D), lambda g, e, b, t, r, f: (t[g], 0)), colspec, colspec,
                  pl.BlockSpec((1, D), lambda g, e, b, t, r, f: (0, 0))],
        out_specs=xspec,
        scratch_shapes=[pltpu.VMEM((tr, D), F32)])
    return pl.pallas_call(
        _moe_combine_kernel, grid_spec=grid_spec,
        out_shape=jax.ShapeDtypeStruct(x2d.shape, x2d.dtype),
        compiler_params=_cparams(("arbitrary",)), name="moe_combine",
    )(*list_b, x2d, ys, pos_col, gate_col, gpost)


def _moe(x2d, gpre, wr_t, br_col, wg, wu, wd, gpost):
    T, D = x2d.shape
    E = wg.shape[0]
    nb = T // MOE_BLOCK
    n_tiles = TOP_K * nb + E
    g_max = E * nb + n_tiles
    hn, pos, gate_col, pos_col, cnt = _router(x2d, gpre, wr_t, br_col)
    list_a, list_b = _moe_schedule(cnt.reshape(nb * E), nb, E, g_max)
    ys = _moe_expert(list_a, hn, pos, wg, wu, wd, n_tiles)
    return _moe_combine(list_b, x2d, ys, pos_col, gate_col, gpost)


def _feature_row_order():
    def diff_block(base):
        x1, x2, rest = [], [], []
        half = DIFF_ROPE_DIM // 2
        for h in range(DIFF_HEADS):
            for c in range(2):
                lo = base + h * 2 * DIFF_QK_DIM + c * DIFF_QK_DIM
                x1 += range(lo, lo + half)
                x2 += range(lo + half, lo + DIFF_ROPE_DIM)
                rest += range(lo + DIFF_ROPE_DIM, lo + DIFF_QK_DIM)
        return x1 + x2 + rest

    order = list(range(0, _S_DQ)) + diff_block(_S_DQ) + diff_block(_S_DK) + list(range(_S_DV, N_FEAT))
    return np.asarray(order, np.int32)


def _mla_q_up_order():
    w = MLA_NOPE_DIM + MLA_ROPE_DIM
    half = MLA_ROPE_DIM // 2
    nope = [h * w + j for h in range(MLA_HEADS) for j in range(MLA_NOPE_DIM)]
    x1 = [h * w + MLA_NOPE_DIM + j for h in range(MLA_HEADS) for j in range(half)]
    x2 = [h * w + MLA_NOPE_DIM + half + j for h in range(MLA_HEADS) for j in range(half)]
    return np.asarray(nope + x1 + x2, np.int32)


def _mla_kv_up_order():
    w = MLA_NOPE_DIM + MLA_V_DIM
    k = [h * w + j for h in range(MLA_HEADS) for j in range(MLA_NOPE_DIM)]
    v = [h * w + MLA_NOPE_DIM + j for h in range(MLA_HEADS) for j in range(MLA_V_DIM)]
    return np.asarray(k + v, np.int32)


def _rope_tables(S):
    def cos_sin(pos, rot_dim):
        inv = ROPE_THETA ** (-jnp.arange(0, rot_dim, 2, dtype=F32) / rot_dim)
        ang = pos.astype(F32)[:, None] * inv[None, :]
        return jnp.cos(ang).T, jnp.sin(ang).T

    rows = S // GRID_W
    pos = jnp.arange(S)
    row = jnp.repeat(jnp.arange(rows), GRID_W)
    col = jnp.tile(jnp.arange(GRID_W), rows)
    cm, sm = cos_sin(pos, MLA_ROPE_DIM)
    cr, sr = cos_sin(row, GQA_HEAD_DIM // 2)
    cc, sc = cos_sin(col, GQA_HEAD_DIM // 2)
    cp, sp = cos_sin(pos, DIFF_ROPE_DIM)
    return (jnp.tile(cm, (MLA_HEADS, 1)), jnp.tile(sm, (MLA_HEADS, 1)),
            jnp.concatenate([cr, cc], axis=0), jnp.concatenate([sr, sc], axis=0),
            jnp.tile(cp, (2 * DIFF_HEADS, 1)), jnp.tile(sp, (2 * DIFF_HEADS, 1)))


def _row(v):
    return v.reshape(1, -1).astype(F32)


def _col(v):
    return v.reshape(-1, 1).astype(F32)


def kernel(x, mem, mix_pre_g, mix_post_g, w_in, mla_q_norm_g, mla_w_q_up, mla_kv_norm_g, mla_w_kv_up,
           gqa_q_norm_g, gqa_k_norm_g, diff_lambda_q1, diff_lambda_k1, diff_lambda_q2, diff_lambda_k2,
           diff_subln_g, w_branch, w_out, mem_pre_g, mem_post_g, mem_norm_g, mem_wq, mem_wkv, mem_wo,
           ffn_pre_g, ffn_post_g, dense_w_gate, dense_w_up, dense_w_down, moe_w_router, moe_b_router,
           moe_w_gate, moe_w_up, moe_w_down):
    B, S, D = x.shape
    M = mem.shape[1]
    depth = w_in.shape[0]
    tables = _rope_tables(S)
    feat_order = _feature_row_order()
    q_up_order = _mla_q_up_order()
    kv_up_order = _mla_kv_up_order()

    for layer in range(depth):
        lambda_init = 0.8 - 0.6 * math.exp(-0.3 * layer)

        w_l = w_in[layer]
        wt = w_l[:, feat_order].T.astype(BF16)
        wg = w_l[:, N_FEAT:].astype(BF16)
        wqu = mla_w_q_up[layer][:, q_up_order].T.astype(BF16)
        wkvu = mla_w_kv_up[layer][:, kv_up_order].T.astype(BF16)
        (qm, km, vm, qg, kg, vg, qd, kd, vd, gates) = _mixer_prep(
            x, _row(mix_pre_g[layer]), wt, wg, wqu, wkvu,
            _col(mla_q_norm_g[layer]), _col(mla_kv_norm_g[layer]),
            _col(gqa_q_norm_g[layer]), _col(gqa_k_norm_g[layer]), tables)
        a_t = _attention(qm, km, vm, kv_of_head=lambda h: h, kcol_of_head=lambda h: h, name="mla_attn")
        b_t = _attention(qg, kg, vg, kv_of_head=lambda h: h // GQA_GROUP, kcol_of_head=lambda h: 0,
                         name="gqa_attn")
        c_t = _diff_attention(qd, kd, vd, _row(diff_lambda_q1[layer]), _row(diff_lambda_k1[layer]),
                              _row(diff_lambda_q2[layer]), _row(diff_lambda_k2[layer]),
                              _col(diff_subln_g[layer]), lambda_init)
        x = _mixer_merge(x, a_t, b_t, c_t, gates, w_branch[layer].astype(BF16),
                         w_out[layer].astype(BF16), _row(mix_post_g[layer]))

        kv = _norm_matmul(mem.reshape(B * M, D), _row(mem_norm_g[layer]), mem_wkv[layer].astype(BF16), BF16)
        x = _mem_attention(x, kv.reshape(B, M, 2 * D), _row(mem_pre_g[layer]), mem_wq[layer].astype(BF16),
                           mem_wo[layer].astype(BF16), _row(mem_post_g[layer]))

        x2d = x.reshape(B * S, D)
        i = layer // 2
        if layer % 2 == 0:
            x2d = _ffn(x2d, _row(ffn_pre_g[layer]), dense_w_gate[i].astype(BF16), dense_w_up[i].astype(BF16),
                       dense_w_down[i].astype(BF16), _row(ffn_post_g[layer]))
        else:
            x2d = _moe(x2d, _row(ffn_pre_g[layer]), moe_w_router[i].T.astype(F32), _col(moe_b_router[i]),
                       moe_w_gate[i].astype(BF16), moe_w_up[i].astype(BF16), moe_w_down[i].astype(BF16),
                       _row(ffn_post_g[layer]))
        x = x2d.reshape(B, S, D)
    return x
```

```python
import functools
import math

import numpy as np
import jax
import jax.numpy as jnp
from jax import lax
from jax.experimental import pallas as pl
from jax.experimental.pallas import tpu as pltpu

F32 = jnp.float32
BF16 = jnp.bfloat16

D_MODEL = 1024
GRID_W = 64
ROPE_THETA = 500000.0
NORM_EPS = 1e-6

MLA_HEADS = 8
MLA_Q_LORA = 384
MLA_KV_LORA = 256
MLA_NOPE_DIM = 64
MLA_ROPE_DIM = 32
MLA_V_DIM = 64

GQA_Q_HEADS = 8
GQA_KV_HEADS = 2
GQA_GROUP = GQA_Q_HEADS // GQA_KV_HEADS
GQA_HEAD_DIM = 64

DIFF_HEADS = 8
DIFF_QK_DIM = 32
DIFF_V_DIM = 2 * DIFF_QK_DIM
DIFF_ROPE_DIM = DIFF_QK_DIM // 4

N_BRANCHES = 3
BRANCH_WIDTH = 512
HEAD_V = 64

MEM_HEADS = 4
MEM_HEAD_DIM = D_MODEL // MEM_HEADS

D_FF = 2816
N_EXPERTS = 8
TOP_K = 2

IN_SPLITS = (MLA_Q_LORA, MLA_KV_LORA, MLA_ROPE_DIM,
             GQA_Q_HEADS * GQA_HEAD_DIM, GQA_KV_HEADS * GQA_HEAD_DIM, GQA_KV_HEADS * GQA_HEAD_DIM,
             DIFF_HEADS * 2 * DIFF_QK_DIM, DIFF_HEADS * 2 * DIFF_QK_DIM, DIFF_HEADS * DIFF_V_DIM,
             N_BRANCHES * D_MODEL)
IN_OFFS = tuple(int(v) for v in np.cumsum((0,) + IN_SPLITS))
N_FEAT = IN_OFFS[9]
N_GATE = IN_SPLITS[9]

LOG2E = 1.4426950408889634
V7X_LANES = 128
QK_PAD = V7X_LANES
ONES_ROWS = 16

V7X_VMEM_LIMIT_BYTES = 56 * 1024 * 1024

PREP_TM = 512
ATTN_TQ = 1024
ATTN_TK = 256
ATTN_UNROLL = 32
MERGE_TM = 512
FFN_TM = 512
FFN_TF = 2816
MOE_BLOCK = 512
KV_TM = 256


def _cparams(sem):
    return pltpu.CompilerParams(dimension_semantics=sem, vmem_limit_bytes=V7X_VMEM_LIMIT_BYTES)


def _const_spec(shape):
    nd = len(shape)
    return pl.BlockSpec(shape, lambda *_: (0,) * nd, pipeline_mode=pl.Buffered(1))


def _rms_rows(x, g_row):
    ms = jnp.mean(x * x, axis=-1, keepdims=True)
    return x * lax.rsqrt(ms + NORM_EPS) * g_row


def _rms_cols(x, g_col):
    ms = jnp.mean(x * x, axis=0, keepdims=True)
    return x * lax.rsqrt(ms + NORM_EPS) * g_col


def _sigmoid(x):
    return 0.5 * jnp.tanh(0.5 * x) + 0.5


def _dot(a, b):
    return jnp.dot(a, b, preferred_element_type=F32)


def _dot_nt(a, b):
    return lax.dot_general(a, b, (((1,), (1,)), ((), ())), preferred_element_type=F32)


_S_CQ, _S_CKV, _S_KR, _S_GQ, _S_GK, _S_GV, _S_DQ, _S_DK, _S_DV = IN_OFFS[:9]


def _rope_rows(x1, x2, c, s):
    return x1 * c - x2 * s, x2 * c + x1 * s


def _mixer_prep_kernel(x_ref, g_ref, wt_ref, wg_ref, wqu_ref, wkvu_ref,
                       mqg_ref, mkvg_ref, gqg_ref, gkg_ref,
                       cm_ref, sm_ref, ca_ref, sa_ref, cp_ref, sp_ref,
                       qm_ref, km_ref, vm_ref, qg_ref, kg_ref, vg_ref,
                       qd_ref, kd_ref, vd_ref, gate_ref):
    tm = x_ref.shape[1]
    hn = _rms_rows(x_ref[0], g_ref[...]).astype(BF16)

    z_all = _dot_nt(wt_ref[...], hn)

    def feat(lo, n):
        return z_all[lo:lo + n]

    gate_ref[0] = _sigmoid(_dot(hn, wg_ref[...])).astype(gate_ref.dtype)

    zeros32 = jnp.zeros((32, tm), F32)
    zeros64 = jnp.zeros((64, tm), F32)

    c_mla = (MLA_NOPE_DIM + MLA_ROPE_DIM) ** -0.5 * LOG2E
    cm = cm_ref[...]
    sm = sm_ref[...]
    cqn = _rms_cols(feat(_S_CQ, MLA_Q_LORA), mqg_ref[...]).astype(BF16)
    q_all = _dot(wqu_ref[...], cqn)
    q_nope = q_all[0:512] * c_mla
    q_r1, q_r2 = _rope_rows(q_all[512:640], q_all[640:768], cm, sm)
    q_r1 = q_r1 * c_mla
    q_r2 = q_r2 * c_mla
    for h in range(MLA_HEADS):
        qm_ref[0, h] = jnp.concatenate(
            [q_nope[64 * h:64 * h + 64], q_r1[16 * h:16 * h + 16], q_r2[16 * h:16 * h + 16], zeros32],
            axis=0).astype(qm_ref.dtype)

    ckvn = _rms_cols(feat(_S_CKV, MLA_KV_LORA), mkvg_ref[...]).astype(BF16)
    kv_all = _dot(wkvu_ref[...], ckvn)
    vm_ref[0] = kv_all[512:1024].astype(vm_ref.dtype)
    kr = feat(_S_KR, MLA_ROPE_DIM)
    k_r1, k_r2 = _rope_rows(kr[0:16], kr[16:32], cm[0:16], sm[0:16])
    for h in range(MLA_HEADS):
        kt = jnp.concatenate([kv_all[64 * h:64 * h + 64], k_r1, k_r2, zeros32], axis=0)
        km_ref[0, :, QK_PAD * h:QK_PAD * (h + 1)] = kt.T.astype(km_ref.dtype)

    c_gqa = GQA_HEAD_DIM ** -0.5 * LOG2E
    ca = ca_ref[...]
    sa = sa_ref[...]
    gq = feat(_S_GQ, GQA_Q_HEADS * GQA_HEAD_DIM)
    for h in range(GQA_Q_HEADS):
        qn = _rms_cols(gq[64 * h:64 * h + 64], gqg_ref[...])
        r1, r2 = _rope_rows(qn[0:32], qn[32:64], ca, sa)
        q64 = jnp.concatenate([r1, r2], axis=0) * c_gqa
        parts = [q64, zeros64] if h // GQA_GROUP == 0 else [zeros64, q64]
        qg_ref[0, h] = jnp.concatenate(parts, axis=0).astype(qg_ref.dtype)
    gk = feat(_S_GK, GQA_KV_HEADS * GQA_HEAD_DIM)
    kparts = []
    for g in range(GQA_KV_HEADS):
        kn = _rms_cols(gk[64 * g:64 * g + 64], gkg_ref[...])
        r1, r2 = _rope_rows(kn[0:32], kn[32:64], ca, sa)
        kparts += [r1, r2]
    kg_ref[0] = jnp.concatenate(kparts, axis=0).T.astype(kg_ref.dtype)
    vg_ref[0] = feat(_S_GV, GQA_KV_HEADS * GQA_HEAD_DIM).astype(vg_ref.dtype)

    c_diff = DIFF_QK_DIM ** -0.5 * LOG2E
    cp = cp_ref[...]
    sp = sp_ref[...]
    row = lax.broadcasted_iota(jnp.int32, (64, 1), 0)
    in_c0 = (row < 4) | ((row >= 8) & (row < 12)) | ((row >= 16) & (row < 40))

    def diff_heads(lo):
        z = feat(lo, DIFF_HEADS * 2 * DIFF_QK_DIM)
        r1, r2 = _rope_rows(z[0:64], z[64:128], cp, sp)
        rest = z[128:512]
        return [jnp.concatenate([r1[8 * h:8 * h + 8], r2[8 * h:8 * h + 8], rest[48 * h:48 * h + 48]], axis=0)
                for h in range(DIFF_HEADS)]

    for h, q64 in enumerate(diff_heads(_S_DQ)):
        q64 = q64 * c_diff
        for c in range(2):
            qc = jnp.where(in_c0 if c == 0 else jnp.logical_not(in_c0), q64, 0.0)
            parts = [qc, zeros64] if h % 2 == 0 else [zeros64, qc]
            qd_ref[0, 2 * h + c] = jnp.concatenate(parts, axis=0).astype(qd_ref.dtype)
    k_heads = diff_heads(_S_DK)
    for p in range(DIFF_HEADS // 2):
        kt = jnp.concatenate([k_heads[2 * p], k_heads[2 * p + 1]], axis=0)
        kd_ref[0, :, QK_PAD * p:QK_PAD * (p + 1)] = kt.T.astype(kd_ref.dtype)
    vd_ref[0] = feat(_S_DV, DIFF_HEADS * DIFF_V_DIM).astype(vd_ref.dtype)


def _mixer_prep(x, g, wt, wg, wqu, wkvu, mqg, mkvg, gqg, gkg, tables):
    B, S, D = x.shape
    tm = min(PREP_TM, S)
    cm, sm, ca, sa, cp, sp = tables
    grid = (B, S // tm)

    def tok(width):
        return pl.BlockSpec((1, tm, width), lambda b, i: (b, i, 0))

    def featm(rows):
        return pl.BlockSpec((1, rows, tm), lambda b, i: (b, 0, i))

    def heads(n):
        return pl.BlockSpec((1, n, QK_PAD, tm), lambda b, i: (b, 0, 0, i))

    def table(rows):
        return pl.BlockSpec((rows, tm), lambda b, i: (0, i))

    in_specs = [
        pl.BlockSpec((1, tm, D), lambda b, i: (b, i, 0)),
        _const_spec((1, D)),
        _const_spec(wt.shape), _const_spec(wg.shape), _const_spec(wqu.shape), _const_spec(wkvu.shape),
        _const_spec(mqg.shape), _const_spec(mkvg.shape), _const_spec(gqg.shape), _const_spec(gkg.shape),
        table(128), table(128), table(32), table(32), table(64), table(64),
    ]
    out_shape = [
        jax.ShapeDtypeStruct((B, MLA_HEADS, QK_PAD, S), BF16),
        jax.ShapeDtypeStruct((B, S, MLA_HEADS * QK_PAD), BF16),
        jax.ShapeDtypeStruct((B, MLA_HEADS * HEAD_V, S), BF16),
        jax.ShapeDtypeStruct((B, GQA_Q_HEADS, QK_PAD, S), BF16),
        jax.ShapeDtypeStruct((B, S, QK_PAD), BF16),
        jax.ShapeDtypeStruct((B, GQA_KV_HEADS * HEAD_V, S), BF16),
        jax.ShapeDtypeStruct((B, 2 * DIFF_HEADS, QK_PAD, S), BF16),
        jax.ShapeDtypeStruct((B, S, DIFF_HEADS // 2 * QK_PAD), BF16),
        jax.ShapeDtypeStruct((B, DIFF_HEADS * HEAD_V, S), BF16),
        jax.ShapeDtypeStruct((B, S, N_GATE), BF16),
    ]
    out_specs = [
        heads(MLA_HEADS), tok(MLA_HEADS * QK_PAD), featm(MLA_HEADS * HEAD_V),
        heads(GQA_Q_HEADS), tok(QK_PAD), featm(GQA_KV_HEADS * HEAD_V),
        heads(2 * DIFF_HEADS), tok(DIFF_HEADS // 2 * QK_PAD), featm(DIFF_HEADS * HEAD_V),
        tok(N_GATE),
    ]
    return pl.pallas_call(
        _mixer_prep_kernel, grid=grid, in_specs=in_specs, out_specs=out_specs, out_shape=out_shape,
        compiler_params=_cparams(("parallel", "parallel")), name="mixer_prep",
    )(x, g, wt, wg, wqu, wkvu, mqg, mkvg, gqg, gkg, cm, sm, ca, sa, cp, sp)


def _online_softmax_attention(q_t, load_q_next, k_ref, v_ref, s_ref, cmax_ref, qq_ref, *, tk):
    S = k_ref.shape[1]
    lanes = q_t.shape[1]
    n_chunks = S // tk
    unroll = min(ATTN_UNROLL, n_chunks)
    assert unroll % 2 == 0 and n_chunks % unroll == 0
    ones = jnp.ones((ONES_ROWS, tk), BF16)

    def scores(c, slot, q=q_t):
        s_t = _dot(k_ref[0, pl.ds(pl.multiple_of(c * tk, tk), tk), :], q)
        s_ref[slot] = s_t
        return jnp.max(s_t, axis=0, keepdims=True)

    def update(c, slot, col_max, m, acc):
        v_aug = jnp.concatenate([v_ref[0, :, pl.ds(pl.multiple_of(c * tk, tk), tk)], ones], axis=0)
        m_new = jnp.maximum(m, col_max)
        p = jnp.exp2(s_ref[slot] - m_new).astype(BF16)
        return m_new, acc * jnp.exp2(m - m_new) + _dot(v_aug, p)

    n_trips = n_chunks // unroll
    qq_ref[0] = q_t
    qq_ref[1] = load_q_next()

    def body(t, carry):
        cmax, m, acc = carry
        c = unroll * t
        last = t == n_trips - 1
        for u in range(unroll):
            if u == unroll - 1:
                if isinstance(last, bool):
                    chunk, tile = (0, 1) if last else (c + unroll, 0)
                else:
                    chunk, tile = jnp.where(last, 0, c + unroll), jnp.where(last, 1, 0)
                cmax_next = scores(chunk, 0, qq_ref[tile])
            else:
                cmax_next = scores(c + u + 1, (u + 1) % 2)
            m, acc = update(c + u, u % 2, cmax, m, acc)
            cmax = cmax_next
        return cmax, m, acc

    @pl.when(pl.program_id(2) == 0)
    def _():
        cmax_ref[...] = jnp.broadcast_to(scores(0, 0), cmax_ref.shape)

    cmax0 = jnp.max(cmax_ref[...], axis=0, keepdims=True)
    m0 = jnp.full((1, lanes), -jnp.inf, F32)
    acc0 = jnp.zeros((HEAD_V + ONES_ROWS, lanes), F32)
    carry = (cmax0, m0, acc0)
    cmax_next, _, acc = body(0, carry) if n_trips == 1 else lax.fori_loop(0, n_trips, body, carry)
    cmax_ref[...] = jnp.broadcast_to(cmax_next, cmax_ref.shape)
    return acc[:HEAD_V] / acc[HEAD_V:HEAD_V + 1]


def _attn_kernel(q_ref, qn_ref, k_ref, v_ref, o_ref, s_ref, cmax_ref, qq_ref, *, tk):
    o = _online_softmax_attention(q_ref[0, 0], lambda: qn_ref[0, 0], k_ref, v_ref, s_ref, cmax_ref, qq_ref,
                                  tk=tk)
    o_ref[0] = o.astype(o_ref.dtype)


def _next_tile(n_tiles):
    return lambda i: jnp.minimum(i + 1, n_tiles - 1)


def _attention(q_t, k, v_t, *, kv_of_head, kcol_of_head, name):
    B, H, _, S = q_t.shape
    tq = min(ATTN_TQ, S)
    tk = min(ATTN_TK, S // 2)
    nxt = _next_tile(S // tq)
    return pl.pallas_call(
        functools.partial(_attn_kernel, tk=tk),
        grid=(B, H, S // tq),
        in_specs=[
            pl.BlockSpec((1, 1, QK_PAD, tq), lambda b, h, i: (b, h, 0, i)),
            pl.BlockSpec((1, 1, QK_PAD, tq), lambda b, h, i: (b, h, 0, nxt(i))),
            pl.BlockSpec((1, S, QK_PAD), lambda b, h, i: (b, 0, kcol_of_head(h))),
            pl.BlockSpec((1, HEAD_V, S), lambda b, h, i: (b, kv_of_head(h), 0)),
        ],
        out_specs=pl.BlockSpec((1, HEAD_V, tq), lambda b, h, i: (b, h, i)),
        out_shape=jax.ShapeDtypeStruct((B, H * HEAD_V, S), F32),
        scratch_shapes=[pltpu.VMEM((2, tk, tq), F32), pltpu.VMEM((8, tq), F32),
                        pltpu.VMEM((2, QK_PAD, tq), BF16)],
        compiler_params=_cparams(("parallel", "parallel", "arbitrary")), name=name,
    )(q_t, q_t, k, v_t)


def _diff_attn_kernel(lq1_ref, lk1_ref, lq2_ref, lk2_ref, g_ref, q_ref, qn_ref, k_ref, v_ref, o_ref,
                      s_ref, cmax_ref, qq_ref, *, tk, lambda_init):
    tq = q_ref.shape[3]
    q_both = jnp.concatenate([q_ref[0, 0], q_ref[0, 1]], axis=1)
    o_maps = _online_softmax_attention(
        q_both, lambda: jnp.concatenate([qn_ref[0, 0], qn_ref[0, 1]], axis=1), k_ref, v_ref, s_ref, cmax_ref,
        qq_ref, tk=tk)
    lam = (jnp.exp(jnp.sum(lq1_ref[...] * lk1_ref[...], axis=-1, keepdims=True))
           - jnp.exp(jnp.sum(lq2_ref[...] * lk2_ref[...], axis=-1, keepdims=True)) + lambda_init)
    o = o_maps[:, :tq] - lam * o_maps[:, tq:]
    o = _rms_cols(o, g_ref[...]) * (1.0 - lambda_init)
    o_ref[0] = o.astype(o_ref.dtype)


def _diff_attention(q_t, k, v_t, lq1, lk1, lq2, lk2, subln_g, lambda_init):
    B, H2, _, S = q_t.shape
    H = H2 // 2
    tq = min(ATTN_TQ // 2, S)
    tk = min(ATTN_TK, S // 2)
    nxt = _next_tile(S // tq)
    vec = _const_spec((1, DIFF_QK_DIM))
    return pl.pallas_call(
        functools.partial(_diff_attn_kernel, tk=tk, lambda_init=lambda_init),
        grid=(B, H, S // tq),
        in_specs=[
            vec, vec, vec, vec, _const_spec((DIFF_V_DIM, 1)),
            pl.BlockSpec((1, 2, QK_PAD, tq), lambda b, h, i: (b, h, 0, i)),
            pl.BlockSpec((1, 2, QK_PAD, tq), lambda b, h, i: (b, h, 0, nxt(i))),
            pl.BlockSpec((1, S, QK_PAD), lambda b, h, i: (b, 0, h // 2)),
            pl.BlockSpec((1, HEAD_V, S), lambda b, h, i: (b, h, 0)),
        ],
        out_specs=pl.BlockSpec((1, HEAD_V, tq), lambda b, h, i: (b, h, i)),
        out_shape=jax.ShapeDtypeStruct((B, H * HEAD_V, S), F32),
        scratch_shapes=[pltpu.VMEM((2, tk, 2 * tq), F32), pltpu.VMEM((8, 2 * tq), F32),
                        pltpu.VMEM((2, QK_PAD, 2 * tq), BF16)],
        compiler_params=_cparams(("parallel", "parallel", "arbitrary")), name="diff_attn",
    )(lq1, lk1, lq2, lk2, subln_g, q_t, q_t, k, v_t)


def _merge_tile(x, a_ref, b_ref, c_ref, gate_ref, wb_ref, wo_ref, g_ref):
    merged = None
    for n, br_ref in enumerate((a_ref, b_ref, c_ref)):
        br = br_ref[0].T.astype(BF16)
        proj = _dot(br, wb_ref[n])
        term = gate_ref[0, :, n * D_MODEL:(n + 1) * D_MODEL].astype(F32) * proj
        merged = term if merged is None else merged + term
    y = _dot(merged.astype(BF16), wo_ref[...])
    return x + _rms_rows(y, g_ref[...])


def _norm_matmul_kernel(x_ref, g_ref, w_ref, o_ref):
    hn = _rms_rows(x_ref[...], g_ref[...]).astype(BF16)
    o_ref[...] = _dot(hn, w_ref[...]).astype(o_ref.dtype)


def _norm_matmul(x2d, g, w, out_dtype):
    T, D = x2d.shape
    N = w.shape[1]
    tm = min(KV_TM, T)
    return pl.pallas_call(
        _norm_matmul_kernel, grid=(T // tm,),
        in_specs=[pl.BlockSpec((tm, D), lambda i: (i, 0)), _const_spec((1, D)), _const_spec(w.shape)],
        out_specs=pl.BlockSpec((tm, N), lambda i: (i, 0)),
        out_shape=jax.ShapeDtypeStruct((T, N), out_dtype),
        compiler_params=_cparams(("parallel",)), name="mem_kv",
    )(x2d, g, w)


def _mem_tile(x, kv_ref, gpre_ref, wq_ref, wo_ref, gpost_ref):
    hn = _rms_rows(x, gpre_ref[...]).astype(BF16)
    q = (_dot(hn, wq_ref[...]) * (MEM_HEAD_DIM ** -0.5 * LOG2E)).astype(BF16)
    outs = []
    for h in range(MEM_HEADS):
        lo = h * MEM_HEAD_DIM
        k_h = kv_ref[0, :, lo:lo + MEM_HEAD_DIM]
        v_h = kv_ref[0, :, D_MODEL + lo:D_MODEL + lo + MEM_HEAD_DIM]
        s = _dot_nt(q[:, lo:lo + MEM_HEAD_DIM], k_h)
        e = jnp.exp2(s - jnp.max(s, axis=-1, keepdims=True))
        p = (e / jnp.sum(e, axis=-1, keepdims=True)).astype(BF16)
        outs.append(_dot(p, v_h))
    o = jnp.concatenate(outs, axis=-1).astype(BF16)
    return x + _rms_rows(_dot(o, wo_ref[...]), gpost_ref[...])


def _merge_mem_kernel(x_ref, a_ref, b_ref, c_ref, gate_ref, wb_ref, wo_ref, g_ref,
                      kv_ref, gpre_ref, wq_ref, mwo_ref, gpost_ref, o_ref):
    x = _merge_tile(x_ref[0], a_ref, b_ref, c_ref, gate_ref, wb_ref, wo_ref, g_ref)
    o_ref[0] = _mem_tile(x, kv_ref, gpre_ref, wq_ref, mwo_ref, gpost_ref)


def _merge_mem(x, a_t, b_t, c_t, gates, wb, wo, g, kv, gpre, wq, mwo, gpost):
    B, S, D = x.shape
    M = kv.shape[1]
    tm = min(MERGE_TM, S)
    xspec = pl.BlockSpec((1, tm, D), lambda b, i: (b, i, 0))
    brspec = pl.BlockSpec((1, BRANCH_WIDTH, tm), lambda b, i: (b, 0, i))
    return pl.pallas_call(
        _merge_mem_kernel, grid=(B, S // tm),
        in_specs=[xspec, brspec, brspec, brspec,
                  pl.BlockSpec((1, tm, N_GATE), lambda b, i: (b, i, 0)),
                  _const_spec(wb.shape), _const_spec(wo.shape), _const_spec((1, D)),
                  pl.BlockSpec((1, M, 2 * D), lambda b, i: (b, 0, 0)),
                  _const_spec((1, D)), _const_spec(wq.shape), _const_spec(mwo.shape), _const_spec((1, D))],
        out_specs=xspec, out_shape=jax.ShapeDtypeStruct(x.shape, x.dtype),
        compiler_params=_cparams(("parallel", "parallel")), name="merge_mem",
    )(x, a_t, b_t, c_t, gates, wb, wo, g, kv, gpre, wq, mwo, gpost)


def _swiglu_chunk(hn, wg, wu, wd):
    gate = _dot(hn, wg)
    act = (gate * _sigmoid(gate) * _dot(hn, wu)).astype(BF16)
    return _dot(act, wd)


def _swiglu(hn, wg_ref, wu_ref, wd_ref, tf):
    y = None
    for lo in range(0, wg_ref.shape[-1], tf):
        part = _swiglu_chunk(hn, wg_ref[:, lo:lo + tf], wu_ref[:, lo:lo + tf], wd_ref[lo:lo + tf, :])
        y = part if y is None else y + part
    return y


def _ffn_kernel(x_ref, gpre_ref, wg_ref, wu_ref, wd_ref, gpost_ref, o_ref):
    x = x_ref[...]
    hn = _rms_rows(x, gpre_ref[...]).astype(BF16)
    o_ref[...] = x + _rms_rows(_swiglu(hn, wg_ref, wu_ref, wd_ref, FFN_TF), gpost_ref[...])


def _ffn(x2d, gpre, wg, wu, wd, gpost):
    T, D = x2d.shape
    tm = min(FFN_TM, T)
    xspec = pl.BlockSpec((tm, D), lambda i: (i, 0))
    return pl.pallas_call(
        _ffn_kernel, grid=(T // tm,),
        in_specs=[xspec, _const_spec((1, D)), _const_spec(wg.shape), _const_spec(wu.shape),
                  _const_spec(wd.shape), _const_spec((1, D))],
        out_specs=xspec, out_shape=jax.ShapeDtypeStruct(x2d.shape, x2d.dtype),
        compiler_params=_cparams(("parallel",)), name="ffn",
    )(x2d, gpre, wg, wu, wd, gpost)


def _router_kernel(x_ref, gpre_ref, wr_ref, br_ref, tri_ref,
                   hn_ref, pos_ref, gate_col_ref, pos_col_ref, cnt_ref, run_ref):
    @pl.when(pl.program_id(0) == 0)
    def _():
        run_ref[...] = jnp.zeros_like(run_ref)

    hn = _rms_rows(x_ref[...], gpre_ref[...])
    hn_ref[...] = hn.astype(hn_ref.dtype)
    logits = lax.dot_general(wr_ref[...], hn, (((1,), (1,)), ((), ())),
                             precision=lax.Precision.HIGHEST,
                             preferred_element_type=F32) + br_ref[...]
    e_idx = lax.broadcasted_iota(jnp.int32, logits.shape, 0)
    m1 = jnp.max(logits, axis=0, keepdims=True)
    i1 = jnp.min(jnp.where(logits == m1, e_idx, N_EXPERTS), axis=0, keepdims=True)
    rest = jnp.where(e_idx == i1, -jnp.inf, logits)
    m2 = jnp.max(rest, axis=0, keepdims=True)
    i2 = jnp.min(jnp.where(rest == m2, e_idx, N_EXPERTS), axis=0, keepdims=True)
    e2 = jnp.exp(m2 - m1)
    w1 = 1.0 / (1.0 + e2)
    w2 = e2 / (1.0 + e2)
    gates = jnp.where(e_idx == i1, w1, 0.0) + jnp.where(e_idx == i2, w2, 0.0)

    sel = (e_idx == i1) | (e_idx == i2)
    sel_f = jnp.where(sel, 1.0, 0.0)
    before = _dot(sel_f.astype(BF16), tri_ref[...])
    run = run_ref[:, 0:1]
    pos = jnp.where(sel, run + before, -1.0)
    pos_ref[...] = pos
    tm = pos.shape[1]
    cols = jnp.concatenate([gates, pos, jnp.zeros((V7X_LANES - 2 * N_EXPERTS, tm), F32)], axis=0).T
    for e in range(N_EXPERTS):
        gate_col_ref[e] = cols[:, e:e + 1]
        pos_col_ref[e] = cols[:, N_EXPERTS + e:N_EXPERTS + e + 1]
    cnt = jnp.sum(sel_f, axis=1, keepdims=True)
    cnt_ref[0] = cnt.astype(jnp.int32)
    run_ref[...] = run_ref[...] + cnt


def _router(x2d, gpre, wr_t, br_col):
    T, D = x2d.shape
    tm = MOE_BLOCK
    nb = T // tm
    tri = (np.arange(tm)[:, None] < np.arange(tm)[None, :]).astype(np.float32)
    col = pl.BlockSpec((N_EXPERTS, tm, 1), lambda i: (0, i, 0))
    return pl.pallas_call(
        _router_kernel, grid=(nb,),
        in_specs=[pl.BlockSpec((tm, D), lambda i: (i, 0)), _const_spec((1, D)),
                  _const_spec(wr_t.shape), _const_spec(br_col.shape), _const_spec((tm, tm))],
        out_specs=[pl.BlockSpec((tm, D), lambda i: (i, 0)),
                   pl.BlockSpec((N_EXPERTS, tm), lambda i: (0, i)), col, col,
                   pl.BlockSpec((1, N_EXPERTS, 1), lambda i: (i, 0, 0))],
        out_shape=[jax.ShapeDtypeStruct((T, D), BF16),
                   jax.ShapeDtypeStruct((N_EXPERTS, T), F32),
                   jax.ShapeDtypeStruct((N_EXPERTS, T, 1), F32),
                   jax.ShapeDtypeStruct((N_EXPERTS, T, 1), F32),
                   jax.ShapeDtypeStruct((nb, N_EXPERTS, 1), jnp.int32)],
        scratch_shapes=[pltpu.VMEM((N_EXPERTS, V7X_LANES), F32)],
        compiler_params=_cparams(("arbitrary",)), name="router",
    )(x2d, gpre, wr_t, br_col, jnp.asarray(tri, BF16))


_FIRST, _LAST = 1, 2
_HALF = (4, 8)


def _moe_schedule_kernel(cnt_ref, ea_ref, ba_ref, ta_ref, ra_ref, fa_ref, eb_ref, bb_ref, tb_ref, rb_ref, fb_ref,
                         tile0_ref, start_ref, *, nb, ne):
    tr = MOE_BLOCK
    shift = tr.bit_length() - 1
    assert tr == 1 << shift
    g_max = ea_ref.shape[0]

    def first_tiles(e, tile):
        rows = lax.fori_loop(0, nb, lambda b, acc: acc + cnt_ref[b * ne + e], 0)
        tile0_ref[e] = tile
        start_ref[e] = 0
        return tile + ((rows + tr - 1) >> shift)

    lax.fori_loop(0, ne, first_tiles, 0)

    def append(lists, by_tile, e, b, start, carry):
        e_ref, b_ref, t_ref, r_ref, f_ref = lists
        c = cnt_ref[b * ne + e]
        first = start >> shift
        n = jnp.where(c > 0, ((start + c - 1) >> shift) - first + 1, 0)

        def visit(j, carry):
            g, prev_group = carry
            local = first + j
            tile = tile0_ref[e] + local
            group = tile if by_tile else b
            lo = start - (local << shift)
            opens = group != prev_group

            @pl.when(opens & (g > 0))
            def _():
                f_ref[g - 1] = f_ref[g - 1] | _LAST

            e_ref[g] = e
            b_ref[g] = b
            t_ref[g] = tile
            r_ref[g] = local << shift
            f_ref[g] = (jnp.where(opens, _FIRST, 0) | jnp.where(lo < tr // 2, _HALF[0], 0)
                        | jnp.where(lo + c > tr // 2, _HALF[1], 0))
            return g + 1, group

        return lax.fori_loop(0, n, visit, carry)

    def finish(lists, g):
        e_ref, b_ref, t_ref, r_ref, f_ref = lists
        f_ref[g - 1] = f_ref[g - 1] | _LAST

        def pad(k, _):
            e_ref[k] = e_ref[g - 1]
            b_ref[k] = b_ref[g - 1]
            t_ref[k] = t_ref[g - 1]
            r_ref[k] = r_ref[g - 1]
            f_ref[k] = 0
            return 0

        lax.fori_loop(g, g_max, pad, 0)

    list_a = (ea_ref, ba_ref, ta_ref, ra_ref, fa_ref)
    list_b = (eb_ref, bb_ref, tb_ref, rb_ref, fb_ref)

    def expert_major(e, carry):
        def block(b, inner):
            start, visits = inner
            return start + cnt_ref[b * ne + e], append(list_a, True, e, b, start, visits)
        return lax.fori_loop(0, nb, block, (0, carry))[1]

    g_a, _ = lax.fori_loop(0, ne, expert_major, (0, -1))
    finish(list_a, g_a)

    def block_major(b, carry):
        def expert(e, visits):
            start = start_ref[e]
            start_ref[e] = start + cnt_ref[b * ne + e]
            return append(list_b, False, e, b, start, visits)
        return lax.fori_loop(0, ne, expert, carry)

    g_b, _ = lax.fori_loop(0, nb, block_major, (0, -1))
    finish(list_b, g_b)


def _moe_schedule(cnt, nb, ne, g_max):
    smem = pl.BlockSpec(memory_space=pltpu.SMEM)
    out = jax.ShapeDtypeStruct((g_max,), jnp.int32)
    lists = pl.pallas_call(
        functools.partial(_moe_schedule_kernel, nb=nb, ne=ne),
        in_specs=[smem], out_specs=[smem] * 10, out_shape=[out] * 10,
        scratch_shapes=[pltpu.SMEM((ne,), jnp.int32), pltpu.SMEM((ne,), jnp.int32)],
        name="moe_schedule",
    )(cnt)
    return lists[:5], lists[5:]


def _moe_expert_kernel(e_ref, b_ref, t_ref, r_ref, f_ref, hn_ref, pos_ref, wg_ref, wu_ref, wd_ref,
                       ys_ref, xs_ref):
    g = pl.program_id(0)
    flags = f_ref[g]
    tr = xs_ref.shape[0]

    @pl.when((flags & _FIRST) != 0)
    def _():
        xs_ref[...] = jnp.zeros_like(xs_ref)

    half = tr // 2
    for h in range(2):
        @pl.when((flags & _HALF[h]) != 0)
        def _():
            rows = (lax.broadcasted_iota(jnp.int32, (half, 1), 0) + (r_ref[g] + h * half)).astype(F32)
            onehot = jnp.where(rows == pos_ref[0, 0], 1.0, 0.0).astype(BF16)
            xs_ref[h * half:(h + 1) * half, :] += _dot(onehot, hn_ref[...])

    @pl.when((flags & _LAST) != 0)
    def _():
        y = _swiglu(xs_ref[...].astype(BF16), wg_ref.at[0], wu_ref.at[0], wd_ref.at[0], FFN_TF)
        ys_ref[...] = y.astype(ys_ref.dtype)


def _moe_expert(list_a, hn, pos, wg, wu, wd, n_tiles):
    T, D = hn.shape
    E, _, F = wg.shape
    tr = MOE_BLOCK
    g_max = list_a[0].shape[0]
    pos4 = pos.reshape(E, T // tr, 1, tr)
    wspec_in = pl.BlockSpec((1, D, F), lambda g, e, b, t, r, f: (e[g], 0, 0), pipeline_mode=pl.Buffered(1))
    wspec_out = pl.BlockSpec((1, F, D), lambda g, e, b, t, r, f: (e[g], 0, 0), pipeline_mode=pl.Buffered(1))
    grid_spec = pltpu.PrefetchScalarGridSpec(
        num_scalar_prefetch=5, grid=(g_max,),
        in_specs=[pl.BlockSpec((tr, D), lambda g, e, b, t, r, f: (b[g], 0)),
                  pl.BlockSpec((1, 1, 1, tr), lambda g, e, b, t, r, f: (e[g], b[g], 0, 0)),
                  wspec_in, wspec_in, wspec_out],
        out_specs=pl.BlockSpec((tr, D), lambda g, e, b, t, r, f: (t[g], 0)),
        scratch_shapes=[pltpu.VMEM((tr, D), F32)])
    return pl.pallas_call(
        _moe_expert_kernel, grid_spec=grid_spec,
        out_shape=jax.ShapeDtypeStruct((n_tiles * tr, D), BF16),
        compiler_params=_cparams(("arbitrary",)), name="moe_expert",
    )(*list_a, hn, pos4, wg, wu, wd)


def _moe_combine_kernel(e_ref, b_ref, t_ref, r_ref, f_ref, x_ref, ys_ref, pos_ref, gate_ref, gpost_ref,
                        o_ref, acc_ref):
    g = pl.program_id(0)
    flags = f_ref[g]
    tr = ys_ref.shape[0]

    @pl.when((flags & _FIRST) != 0)
    def _():
        acc_ref[...] = jnp.zeros_like(acc_ref)

    half = tr // 2
    for h in range(2):
        @pl.when((flags & _HALF[h]) != 0)
        def _():
            rows = (lax.broadcasted_iota(jnp.int32, (1, half), 1) + (r_ref[g] + h * half)).astype(F32)
            onehot = jnp.where(pos_ref[0] == rows, 1.0, 0.0).astype(BF16)
            acc_ref[...] += gate_ref[0] * _dot(onehot, ys_ref[h * half:(h + 1) * half, :])

    @pl.when((flags & _LAST) != 0)
    def _():
        o_ref[...] = x_ref[...] + _rms_rows(acc_ref[...], gpost_ref[...])


def _moe_combine(list_b, x2d, ys, pos_col, gate_col, gpost):
    T, D = x2d.shape
    tr = MOE_BLOCK
    g_max = list_b[0].shape[0]
    xspec = pl.BlockSpec((tr, D), lambda g, e, b, t, r, f: (b[g], 0))
    colspec = pl.BlockSpec((1, tr, 1), lambda g, e, b, t, r, f: (e[g], b[g], 0))
    grid_spec = pltpu.PrefetchScalarGridSpec(
        num_scalar_prefetch=5, grid=(g_max,),
        in_specs=[xspec, pl.BlockSpec((tr, D), lambda g, e, b, t, r, f: (t[g], 0)), colspec, colspec,
                  pl.BlockSpec((1, D), lambda g, e, b, t, r, f: (0, 0))],
        out_specs=xspec,
        scratch_shapes=[pltpu.VMEM((tr, D), F32)])
    return pl.pallas_call(
        _moe_combine_kernel, grid_spec=grid_spec,
        out_shape=jax.ShapeDtypeStruct(x2d.shape, x2d.dtype),
        compiler_params=_cparams(("arbitrary",)), name="moe_combine",
    )(*list_b, x2d, ys, pos_col, gate_col, gpost)


def _moe(x2d, gpre, wr_t, br_col, wg, wu, wd, gpost):
    T, D = x2d.shape
    E = wg.shape[0]
    nb = T // MOE_BLOCK
    n_tiles = TOP_K * nb + E
    g_max = E * nb + n_tiles
    hn, pos, gate_col, pos_col, cnt = _router(x2d, gpre, wr_t, br_col)
    list_a, list_b = _moe_schedule(cnt.reshape(nb * E), nb, E, g_max)
    ys = _moe_expert(list_a, hn, pos, wg, wu, wd, n_tiles)
    return _moe_combine(list_b, x2d, ys, pos_col, gate_col, gpost)


def _feature_row_order():
    def diff_block(base):
        x1, x2, rest = [], [], []
        half = DIFF_ROPE_DIM // 2
        for h in range(DIFF_HEADS):
            for c in range(2):
                lo = base + h * 2 * DIFF_QK_DIM + c * DIFF_QK_DIM
                x1 += range(lo, lo + half)
                x2 += range(lo + half, lo + DIFF_ROPE_DIM)
                rest += range(lo + DIFF_ROPE_DIM, lo + DIFF_QK_DIM)
        return x1 + x2 + rest

    order = list(range(0, _S_DQ)) + diff_block(_S_DQ) + diff_block(_S_DK) + list(range(_S_DV, N_FEAT))
    return np.asarray(order, np.int32)


def _mla_q_up_order():
    w = MLA_NOPE_DIM + MLA_ROPE_DIM
    half = MLA_ROPE_DIM // 2
    nope = [h * w + j for h in range(MLA_HEADS) for j in range(MLA_NOPE_DIM)]
    x1 = [h * w + MLA_NOPE_DIM + j for h in range(MLA_HEADS) for j in range(half)]
    x2 = [h * w + MLA_NOPE_DIM + half + j for h in range(MLA_HEADS) for j in range(half)]
    return np.asarray(nope + x1 + x2, np.int32)


def _mla_kv_up_order():
    w = MLA_NOPE_DIM + MLA_V_DIM
    k = [h * w + j for h in range(MLA_HEADS) for j in range(MLA_NOPE_DIM)]
    v = [h * w + MLA_NOPE_DIM + j for h in range(MLA_HEADS) for j in range(MLA_V_DIM)]
    return np.asarray(k + v, np.int32)


def _rope_tables(S):
    def cos_sin(pos, rot_dim):
        inv = ROPE_THETA ** (-jnp.arange(0, rot_dim, 2, dtype=F32) / rot_dim)
        ang = pos.astype(F32)[:, None] * inv[None, :]
        return jnp.cos(ang).T, jnp.sin(ang).T

    rows = S // GRID_W
    pos = jnp.arange(S)
    row = jnp.repeat(jnp.arange(rows), GRID_W)
    col = jnp.tile(jnp.arange(GRID_W), rows)
    cm, sm = cos_sin(pos, MLA_ROPE_DIM)
    cr, sr = cos_sin(row, GQA_HEAD_DIM // 2)
    cc, sc = cos_sin(col, GQA_HEAD_DIM // 2)
    cp, sp = cos_sin(pos, DIFF_ROPE_DIM)
    return (jnp.tile(cm, (MLA_HEADS, 1)), jnp.tile(sm, (MLA_HEADS, 1)),
            jnp.concatenate([cr, cc], axis=0), jnp.concatenate([sr, sc], axis=0),
            jnp.tile(cp, (2 * DIFF_HEADS, 1)), jnp.tile(sp, (2 * DIFF_HEADS, 1)))


def _row(v):
    return v.reshape(1, -1).astype(F32)


def _col(v):
    return v.reshape(-1, 1).astype(F32)


def kernel(x, mem, mix_pre_g, mix_post_g, w_in, mla_q_norm_g, mla_w_q_up, mla_kv_norm_g, mla_w_kv_up,
           gqa_q_norm_g, gqa_k_norm_g, diff_lambda_q1, diff_lambda_k1, diff_lambda_q2, diff_lambda_k2,
           diff_subln_g, w_branch, w_out, mem_pre_g, mem_post_g, mem_norm_g, mem_wq, mem_wkv, mem_wo,
           ffn_pre_g, ffn_post_g, dense_w_gate, dense_w_up, dense_w_down, moe_w_router, moe_b_router,
           moe_w_gate, moe_w_up, moe_w_down):
    B, S, D = x.shape
    M = mem.shape[1]
    depth = w_in.shape[0]
    tables = _rope_tables(S)
    feat_order = _feature_row_order()
    q_up_order = _mla_q_up_order()
    kv_up_order = _mla_kv_up_order()

    for layer in range(depth):
        lambda_init = 0.8 - 0.6 * math.exp(-0.3 * layer)

        w_l = w_in[layer]
        wt = w_l[:, feat_order].T.astype(BF16)
        wg = w_l[:, N_FEAT:].astype(BF16)
        wqu = mla_w_q_up[layer][:, q_up_order].T.astype(BF16)
        wkvu = mla_w_kv_up[layer][:, kv_up_order].T.astype(BF16)
        (qm, km, vm, qg, kg, vg, qd, kd, vd, gates) = _mixer_prep(
            x, _row(mix_pre_g[layer]), wt, wg, wqu, wkvu,
            _col(mla_q_norm_g[layer]), _col(mla_kv_norm_g[layer]),
            _col(gqa_q_norm_g[layer]), _col(gqa_k_norm_g[layer]), tables)
        a_t = _attention(qm, km, vm, kv_of_head=lambda h: h, kcol_of_head=lambda h: h, name="mla_attn")
        b_t = _attention(qg, kg, vg, kv_of_head=lambda h: h // GQA_GROUP, kcol_of_head=lambda h: 0,
                         name="gqa_attn")
        c_t = _diff_attention(qd, kd, vd, _row(diff_lambda_q1[layer]), _row(diff_lambda_k1[layer]),
                              _row(diff_lambda_q2[layer]), _row(diff_lambda_k2[layer]),
                              _col(diff_subln_g[layer]), lambda_init)
        kv = _norm_matmul(mem.reshape(B * M, D), _row(mem_norm_g[layer]), mem_wkv[layer].astype(BF16), BF16)
        x = _merge_mem(x, a_t, b_t, c_t, gates, w_branch[layer].astype(BF16), w_out[layer].astype(BF16),
                       _row(mix_post_g[layer]), kv.reshape(B, M, 2 * D), _row(mem_pre_g[layer]),
                       mem_wq[layer].astype(BF16), mem_wo[layer].astype(BF16), _row(mem_post_g[layer]))

        x2d = x.reshape(B * S, D)
        i = layer // 2
        if layer % 2 == 0:
            x2d = _ffn(x2d, _row(ffn_pre_g[layer]), dense_w_gate[i].astype(BF16), dense_w_up[i].astype(BF16),
                       dense_w_down[i].astype(BF16), _row(ffn_post_g[layer]))
        else:
            x2d = _moe(x2d, _row(ffn_pre_g[layer]), moe_w_router[i].T.astype(F32), _col(moe_b_router[i]),
                       moe_w_gate[i].astype(BF16), moe_w_up[i].astype(BF16), moe_w_down[i].astype(BF16),
                       _row(ffn_post_g[layer]))
        x = x2d.reshape(B, S, D)
    return x
```

```python
import functools
import math

import numpy as np
import jax
import jax.numpy as jnp
from jax import lax
from jax.experimental import pallas as pl
from jax.experimental.pallas import tpu as pltpu

F32 = jnp.float32
BF16 = jnp.bfloat16

D_MODEL = 1024
GRID_W = 64
ROPE_THETA = 500000.0
NORM_EPS = 1e-6

MLA_HEADS = 8
MLA_Q_LORA = 384
MLA_KV_LORA = 256
MLA_NOPE_DIM = 64
MLA_ROPE_DIM = 32
MLA_V_DIM = 64

GQA_Q_HEADS = 8
GQA_KV_HEADS = 2
GQA_GROUP = GQA_Q_HEADS // GQA_KV_HEADS
GQA_HEAD_DIM = 64

DIFF_HEADS = 8
DIFF_QK_DIM = 32
DIFF_V_DIM = 2 * DIFF_QK_DIM
DIFF_ROPE_DIM = DIFF_QK_DIM // 4

N_BRANCHES = 3
BRANCH_WIDTH = 512
HEAD_V = 64

MEM_HEADS = 4
MEM_HEAD_DIM = D_MODEL // MEM_HEADS

D_FF = 2816
N_EXPERTS = 8
TOP_K = 2

IN_SPLITS = (MLA_Q_LORA, MLA_KV_LORA, MLA_ROPE_DIM,
             GQA_Q_HEADS * GQA_HEAD_DIM, GQA_KV_HEADS * GQA_HEAD_DIM, GQA_KV_HEADS * GQA_HEAD_DIM,
             DIFF_HEADS * 2 * DIFF_QK_DIM, DIFF_HEADS * 2 * DIFF_QK_DIM, DIFF_HEADS * DIFF_V_DIM,
             N_BRANCHES * D_MODEL)
IN_OFFS = tuple(int(v) for v in np.cumsum((0,) + IN_SPLITS))
N_FEAT = IN_OFFS[9]
N_GATE = IN_SPLITS[9]

LOG2E = 1.4426950408889634
V7X_LANES = 128
QK_PAD = V7X_LANES
ONES_ROWS = 16

V7X_VMEM_LIMIT_BYTES = 56 * 1024 * 1024

PREP_TM = 512
ATTN_TQ = 1024
ATTN_TK = 256
ATTN_UNROLL = 32
MERGE_TM = 512
FFN_TM = 512
FFN_TF = 2816
MOE_BLOCK = 512
MOE_COMBINE_BLOCKS = 2
KV_TM = 256


def _cparams(sem):
    return pltpu.CompilerParams(dimension_semantics=sem, vmem_limit_bytes=V7X_VMEM_LIMIT_BYTES)


def _const_spec(shape):
    nd = len(shape)
    return pl.BlockSpec(shape, lambda *_: (0,) * nd, pipeline_mode=pl.Buffered(1))


def _rms_rows(x, g_row):
    ms = jnp.mean(x * x, axis=-1, keepdims=True)
    return x * lax.rsqrt(ms + NORM_EPS) * g_row


def _rms_cols(x, g_col):
    ms = jnp.mean(x * x, axis=0, keepdims=True)
    return x * lax.rsqrt(ms + NORM_EPS) * g_col


def _sigmoid(x):
    return 0.5 * jnp.tanh(0.5 * x) + 0.5


def _dot(a, b):
    return jnp.dot(a, b, preferred_element_type=F32)


def _dot_nt(a, b):
    return lax.dot_general(a, b, (((1,), (1,)), ((), ())), preferred_element_type=F32)


_S_CQ, _S_CKV, _S_KR, _S_GQ, _S_GK, _S_GV, _S_DQ, _S_DK, _S_DV = IN_OFFS[:9]


def _rope_rows(x1, x2, c, s):
    return x1 * c - x2 * s, x2 * c + x1 * s


def _mixer_prep_kernel(x_ref, g_ref, wt_ref, wg_ref, wqu_ref, wkvu_ref,
                       mqg_ref, mkvg_ref, gqg_ref, gkg_ref,
                       cm_ref, sm_ref, ca_ref, sa_ref, cp_ref, sp_ref,
                       qm_ref, km_ref, vm_ref, qg_ref, kg_ref, vg_ref,
                       qd_ref, kd_ref, vd_ref, gate_ref):
    tm = x_ref.shape[1]
    hn = _rms_rows(x_ref[0], g_ref[...]).astype(BF16)

    z_all = _dot_nt(wt_ref[...], hn)

    def feat(lo, n):
        return z_all[lo:lo + n]

    gate_ref[0] = _sigmoid(_dot(hn, wg_ref[...])).astype(gate_ref.dtype)

    zeros32 = jnp.zeros((32, tm), F32)
    zeros64 = jnp.zeros((64, tm), F32)

    c_mla = (MLA_NOPE_DIM + MLA_ROPE_DIM) ** -0.5 * LOG2E
    cm = cm_ref[...]
    sm = sm_ref[...]
    cqn = _rms_cols(feat(_S_CQ, MLA_Q_LORA), mqg_ref[...]).astype(BF16)
    q_all = _dot(wqu_ref[...], cqn)
    q_nope = q_all[0:512] * c_mla
    q_r1, q_r2 = _rope_rows(q_all[512:640], q_all[640:768], cm, sm)
    q_r1 = q_r1 * c_mla
    q_r2 = q_r2 * c_mla
    for h in range(MLA_HEADS):
        qm_ref[0, h] = jnp.concatenate(
            [q_nope[64 * h:64 * h + 64], q_r1[16 * h:16 * h + 16], q_r2[16 * h:16 * h + 16], zeros32],
            axis=0).astype(qm_ref.dtype)

    ckvn = _rms_cols(feat(_S_CKV, MLA_KV_LORA), mkvg_ref[...]).astype(BF16)
    kv_all = _dot(wkvu_ref[...], ckvn)
    vm_ref[0] = kv_all[512:1024].astype(vm_ref.dtype)
    kr = feat(_S_KR, MLA_ROPE_DIM)
    k_r1, k_r2 = _rope_rows(kr[0:16], kr[16:32], cm[0:16], sm[0:16])
    for h in range(MLA_HEADS):
        kt = jnp.concatenate([kv_all[64 * h:64 * h + 64], k_r1, k_r2, zeros32], axis=0)
        km_ref[0, :, QK_PAD * h:QK_PAD * (h + 1)] = kt.T.astype(km_ref.dtype)

    c_gqa = GQA_HEAD_DIM ** -0.5 * LOG2E
    ca = ca_ref[...]
    sa = sa_ref[...]
    gq = feat(_S_GQ, GQA_Q_HEADS * GQA_HEAD_DIM)
    for h in range(GQA_Q_HEADS):
        qn = _rms_cols(gq[64 * h:64 * h + 64], gqg_ref[...])
        r1, r2 = _rope_rows(qn[0:32], qn[32:64], ca, sa)
        q64 = jnp.concatenate([r1, r2], axis=0) * c_gqa
        parts = [q64, zeros64] if h // GQA_GROUP == 0 else [zeros64, q64]
        qg_ref[0, h] = jnp.concatenate(parts, axis=0).astype(qg_ref.dtype)
    gk = feat(_S_GK, GQA_KV_HEADS * GQA_HEAD_DIM)
    kparts = []
    for g in range(GQA_KV_HEADS):
        kn = _rms_cols(gk[64 * g:64 * g + 64], gkg_ref[...])
        r1, r2 = _rope_rows(kn[0:32], kn[32:64], ca, sa)
        kparts += [r1, r2]
    kg_ref[0] = jnp.concatenate(kparts, axis=0).T.astype(kg_ref.dtype)
    vg_ref[0] = feat(_S_GV, GQA_KV_HEADS * GQA_HEAD_DIM).astype(vg_ref.dtype)

    c_diff = DIFF_QK_DIM ** -0.5 * LOG2E
    cp = cp_ref[...]
    sp = sp_ref[...]
    row = lax.broadcasted_iota(jnp.int32, (64, 1), 0)
    in_c0 = (row < 4) | ((row >= 8) & (row < 12)) | ((row >= 16) & (row < 40))

    def diff_heads(lo):
        z = feat(lo, DIFF_HEADS * 2 * DIFF_QK_DIM)
        r1, r2 = _rope_rows(z[0:64], z[64:128], cp, sp)
        rest = z[128:512]
        return [jnp.concatenate([r1[8 * h:8 * h + 8], r2[8 * h:8 * h + 8], rest[48 * h:48 * h + 48]], axis=0)
                for h in range(DIFF_HEADS)]

    for h, q64 in enumerate(diff_heads(_S_DQ)):
        q64 = q64 * c_diff
        for c in range(2):
            qc = jnp.where(in_c0 if c == 0 else jnp.logical_not(in_c0), q64, 0.0)
            parts = [qc, zeros64] if h % 2 == 0 else [zeros64, qc]
            qd_ref[0, 2 * h + c] = jnp.concatenate(parts, axis=0).astype(qd_ref.dtype)
    k_heads = diff_heads(_S_DK)
    for p in range(DIFF_HEADS // 2):
        kt = jnp.concatenate([k_heads[2 * p], k_heads[2 * p + 1]], axis=0)
        kd_ref[0, :, QK_PAD * p:QK_PAD * (p + 1)] = kt.T.astype(kd_ref.dtype)
    vd_ref[0] = feat(_S_DV, DIFF_HEADS * DIFF_V_DIM).astype(vd_ref.dtype)


def _mixer_prep(x, g, wt, wg, wqu, wkvu, mqg, mkvg, gqg, gkg, tables):
    B, S, D = x.shape
    tm = min(PREP_TM, S)
    cm, sm, ca, sa, cp, sp = tables
    grid = (B, S // tm)

    def tok(width):
        return pl.BlockSpec((1, tm, width), lambda b, i: (b, i, 0))

    def featm(rows):
        return pl.BlockSpec((1, rows, tm), lambda b, i: (b, 0, i))

    def heads(n):
        return pl.BlockSpec((1, n, QK_PAD, tm), lambda b, i: (b, 0, 0, i))

    def table(rows):
        return pl.BlockSpec((rows, tm), lambda b, i: (0, i))

    in_specs = [
        pl.BlockSpec((1, tm, D), lambda b, i: (b, i, 0)),
        _const_spec((1, D)),
        _const_spec(wt.shape), _const_spec(wg.shape), _const_spec(wqu.shape), _const_spec(wkvu.shape),
        _const_spec(mqg.shape), _const_spec(mkvg.shape), _const_spec(gqg.shape), _const_spec(gkg.shape),
        table(128), table(128), table(32), table(32), table(64), table(64),
    ]
    out_shape = [
        jax.ShapeDtypeStruct((B, MLA_HEADS, QK_PAD, S), BF16),
        jax.ShapeDtypeStruct((B, S, MLA_HEADS * QK_PAD), BF16),
        jax.ShapeDtypeStruct((B, MLA_HEADS * HEAD_V, S), BF16),
        jax.ShapeDtypeStruct((B, GQA_Q_HEADS, QK_PAD, S), BF16),
        jax.ShapeDtypeStruct((B, S, QK_PAD), BF16),
        jax.ShapeDtypeStruct((B, GQA_KV_HEADS * HEAD_V, S), BF16),
        jax.ShapeDtypeStruct((B, 2 * DIFF_HEADS, QK_PAD, S), BF16),
        jax.ShapeDtypeStruct((B, S, DIFF_HEADS // 2 * QK_PAD), BF16),
        jax.ShapeDtypeStruct((B, DIFF_HEADS * HEAD_V, S), BF16),
        jax.ShapeDtypeStruct((B, S, N_GATE), BF16),
    ]
    out_specs = [
        heads(MLA_HEADS), tok(MLA_HEADS * QK_PAD), featm(MLA_HEADS * HEAD_V),
        heads(GQA_Q_HEADS), tok(QK_PAD), featm(GQA_KV_HEADS * HEAD_V),
        heads(2 * DIFF_HEADS), tok(DIFF_HEADS // 2 * QK_PAD), featm(DIFF_HEADS * HEAD_V),
        tok(N_GATE),
    ]
    return pl.pallas_call(
        _mixer_prep_kernel, grid=grid, in_specs=in_specs, out_specs=out_specs, out_shape=out_shape,
        compiler_params=_cparams(("parallel", "parallel")), name="mixer_prep",
    )(x, g, wt, wg, wqu, wkvu, mqg, mkvg, gqg, gkg, cm, sm, ca, sa, cp, sp)


def _online_softmax_attention(q_t, load_q_next, k_ref, v_ref, s_ref, cmax_ref, qq_ref, *, tk):
    S = k_ref.shape[1]
    lanes = q_t.shape[1]
    n_chunks = S // tk
    unroll = min(ATTN_UNROLL, n_chunks)
    assert unroll % 2 == 0 and n_chunks % unroll == 0
    ones = jnp.ones((ONES_ROWS, tk), BF16)

    def scores(c, slot, q=q_t):
        s_t = _dot(k_ref[0, pl.ds(pl.multiple_of(c * tk, tk), tk), :], q)
        s_ref[slot] = s_t
        return jnp.max(s_t, axis=0, keepdims=True)

    def update(c, slot, col_max, m, acc):
        v_aug = jnp.concatenate([v_ref[0, :, pl.ds(pl.multiple_of(c * tk, tk), tk)], ones], axis=0)
        m_new = jnp.maximum(m, col_max)
        p = jnp.exp2(s_ref[slot] - m_new).astype(BF16)
        return m_new, acc * jnp.exp2(m - m_new) + _dot(v_aug, p)

    n_trips = n_chunks // unroll
    qq_ref[0] = q_t
    qq_ref[1] = load_q_next()

    def body(t, carry):
        cmax, m, acc = carry
        c = unroll * t
        last = t == n_trips - 1
        for u in range(unroll):
            if u == unroll - 1:
                if isinstance(last, bool):
                    chunk, tile = (0, 1) if last else (c + unroll, 0)
                else:
                    chunk, tile = jnp.where(last, 0, c + unroll), jnp.where(last, 1, 0)
                cmax_next = scores(chunk, 0, qq_ref[tile])
            else:
                cmax_next = scores(c + u + 1, (u + 1) % 2)
            m, acc = update(c + u, u % 2, cmax, m, acc)
            cmax = cmax_next
        return cmax, m, acc

    @pl.when(pl.program_id(2) == 0)
    def _():
        cmax_ref[...] = jnp.broadcast_to(scores(0, 0), cmax_ref.shape)

    cmax0 = jnp.max(cmax_ref[...], axis=0, keepdims=True)
    m0 = jnp.full((1, lanes), -jnp.inf, F32)
    acc0 = jnp.zeros((HEAD_V + ONES_ROWS, lanes), F32)
    carry = (cmax0, m0, acc0)
    cmax_next, _, acc = body(0, carry) if n_trips == 1 else lax.fori_loop(0, n_trips, body, carry)
    cmax_ref[...] = jnp.broadcast_to(cmax_next, cmax_ref.shape)
    return acc[:HEAD_V] / acc[HEAD_V:HEAD_V + 1]


def _attn_kernel(q_ref, qn_ref, k_ref, v_ref, o_ref, s_ref, cmax_ref, qq_ref, *, tk):
    o = _online_softmax_attention(q_ref[0, 0], lambda: qn_ref[0, 0], k_ref, v_ref, s_ref, cmax_ref, qq_ref,
                                  tk=tk)
    o_ref[0] = o.astype(o_ref.dtype)


def _next_tile(n_tiles):
    return lambda i: jnp.minimum(i + 1, n_tiles - 1)


def _attention(q_t, k, v_t, *, kv_of_head, kcol_of_head, name):
    B, H, _, S = q_t.shape
    tq = min(ATTN_TQ, S)
    tk = min(ATTN_TK, S // 2)
    nxt = _next_tile(S // tq)
    return pl.pallas_call(
        functools.partial(_attn_kernel, tk=tk),
        grid=(B, H, S // tq),
        in_specs=[
            pl.BlockSpec((1, 1, QK_PAD, tq), lambda b, h, i: (b, h, 0, i)),
            pl.BlockSpec((1, 1, QK_PAD, tq), lambda b, h, i: (b, h, 0, nxt(i))),
            pl.BlockSpec((1, S, QK_PAD), lambda b, h, i: (b, 0, kcol_of_head(h))),
            pl.BlockSpec((1, HEAD_V, S), lambda b, h, i: (b, kv_of_head(h), 0)),
        ],
        out_specs=pl.BlockSpec((1, HEAD_V, tq), lambda b, h, i: (b, h, i)),
        out_shape=jax.ShapeDtypeStruct((B, H * HEAD_V, S), F32),
        scratch_shapes=[pltpu.VMEM((2, tk, tq), F32), pltpu.VMEM((8, tq), F32),
                        pltpu.VMEM((2, QK_PAD, tq), BF16)],
        compiler_params=_cparams(("parallel", "parallel", "arbitrary")), name=name,
    )(q_t, q_t, k, v_t)


def _diff_attn_kernel(lq1_ref, lk1_ref, lq2_ref, lk2_ref, g_ref, q_ref, qn_ref, k_ref, v_ref, o_ref,
                      s_ref, cmax_ref, qq_ref, *, tk, lambda_init):
    tq = q_ref.shape[3]
    q_both = jnp.concatenate([q_ref[0, 0], q_ref[0, 1]], axis=1)
    o_maps = _online_softmax_attention(
        q_both, lambda: jnp.concatenate([qn_ref[0, 0], qn_ref[0, 1]], axis=1), k_ref, v_ref, s_ref, cmax_ref,
        qq_ref, tk=tk)
    lam = (jnp.exp(jnp.sum(lq1_ref[...] * lk1_ref[...], axis=-1, keepdims=True))
           - jnp.exp(jnp.sum(lq2_ref[...] * lk2_ref[...], axis=-1, keepdims=True)) + lambda_init)
    o = o_maps[:, :tq] - lam * o_maps[:, tq:]
    o = _rms_cols(o, g_ref[...]) * (1.0 - lambda_init)
    o_ref[0] = o.astype(o_ref.dtype)


def _diff_attention(q_t, k, v_t, lq1, lk1, lq2, lk2, subln_g, lambda_init):
    B, H2, _, S = q_t.shape
    H = H2 // 2
    tq = min(ATTN_TQ // 2, S)
    tk = min(ATTN_TK, S // 2)
    nxt = _next_tile(S // tq)
    vec = _const_spec((1, DIFF_QK_DIM))
    return pl.pallas_call(
        functools.partial(_diff_attn_kernel, tk=tk, lambda_init=lambda_init),
        grid=(B, H, S // tq),
        in_specs=[
            vec, vec, vec, vec, _const_spec((DIFF_V_DIM, 1)),
            pl.BlockSpec((1, 2, QK_PAD, tq), lambda b, h, i: (b, h, 0, i)),
            pl.BlockSpec((1, 2, QK_PAD, tq), lambda b, h, i: (b, h, 0, nxt(i))),
            pl.BlockSpec((1, S, QK_PAD), lambda b, h, i: (b, 0, h // 2)),
            pl.BlockSpec((1, HEAD_V, S), lambda b, h, i: (b, h, 0)),
        ],
        out_specs=pl.BlockSpec((1, HEAD_V, tq), lambda b, h, i: (b, h, i)),
        out_shape=jax.ShapeDtypeStruct((B, H * HEAD_V, S), F32),
        scratch_shapes=[pltpu.VMEM((2, tk, 2 * tq), F32), pltpu.VMEM((8, 2 * tq), F32),
                        pltpu.VMEM((2, QK_PAD, 2 * tq), BF16)],
        compiler_params=_cparams(("parallel", "parallel", "arbitrary")), name="diff_attn",
    )(lq1, lk1, lq2, lk2, subln_g, q_t, q_t, k, v_t)


def _merge_tile(x, a_ref, b_ref, c_ref, gate_ref, wb_ref, wo_ref, g_ref):
    merged = None
    for n, br_ref in enumerate((a_ref, b_ref, c_ref)):
        br = br_ref[0].T.astype(BF16)
        proj = _dot(br, wb_ref[n])
        term = gate_ref[0, :, n * D_MODEL:(n + 1) * D_MODEL].astype(F32) * proj
        merged = term if merged is None else merged + term
    y = _dot(merged.astype(BF16), wo_ref[...])
    return x + _rms_rows(y, g_ref[...])


def _norm_matmul_kernel(x_ref, g_ref, w_ref, o_ref):
    hn = _rms_rows(x_ref[...], g_ref[...]).astype(BF16)
    o_ref[...] = _dot(hn, w_ref[...]).astype(o_ref.dtype)


def _norm_matmul(x2d, g, w, out_dtype):
    T, D = x2d.shape
    N = w.shape[1]
    tm = min(KV_TM, T)
    return pl.pallas_call(
        _norm_matmul_kernel, grid=(T // tm,),
        in_specs=[pl.BlockSpec((tm, D), lambda i: (i, 0)), _const_spec((1, D)), _const_spec(w.shape)],
        out_specs=pl.BlockSpec((tm, N), lambda i: (i, 0)),
        out_shape=jax.ShapeDtypeStruct((T, N), out_dtype),
        compiler_params=_cparams(("parallel",)), name="mem_kv",
    )(x2d, g, w)


def _mem_tile(x, kv_ref, gpre_ref, wq_ref, wo_ref, gpost_ref):
    hn = _rms_rows(x, gpre_ref[...]).astype(BF16)
    q = (_dot(hn, wq_ref[...]) * (MEM_HEAD_DIM ** -0.5 * LOG2E)).astype(BF16)
    outs = []
    for h in range(MEM_HEADS):
        lo = h * MEM_HEAD_DIM
        k_h = kv_ref[0, :, lo:lo + MEM_HEAD_DIM]
        v_h = kv_ref[0, :, D_MODEL + lo:D_MODEL + lo + MEM_HEAD_DIM]
        s = _dot_nt(q[:, lo:lo + MEM_HEAD_DIM], k_h)
        e = jnp.exp2(s - jnp.max(s, axis=-1, keepdims=True))
        p = (e / jnp.sum(e, axis=-1, keepdims=True)).astype(BF16)
        outs.append(_dot(p, v_h))
    o = jnp.concatenate(outs, axis=-1).astype(BF16)
    return x + _rms_rows(_dot(o, wo_ref[...]), gpost_ref[...])


def _merge_mem_kernel(x_ref, a_ref, b_ref, c_ref, gate_ref, wb_ref, wo_ref, g_ref,
                      kv_ref, gpre_ref, wq_ref, mwo_ref, gpost_ref, o_ref):
    x = _merge_tile(x_ref[0], a_ref, b_ref, c_ref, gate_ref, wb_ref, wo_ref, g_ref)
    o_ref[0] = _mem_tile(x, kv_ref, gpre_ref, wq_ref, mwo_ref, gpost_ref)


def _merge_mem(x, a_t, b_t, c_t, gates, wb, wo, g, kv, gpre, wq, mwo, gpost):
    B, S, D = x.shape
    M = kv.shape[1]
    tm = min(MERGE_TM, S)
    xspec = pl.BlockSpec((1, tm, D), lambda b, i: (b, i, 0))
    brspec = pl.BlockSpec((1, BRANCH_WIDTH, tm), lambda b, i: (b, 0, i))
    return pl.pallas_call(
        _merge_mem_kernel, grid=(B, S // tm),
        in_specs=[xspec, brspec, brspec, brspec,
                  pl.BlockSpec((1, tm, N_GATE), lambda b, i: (b, i, 0)),
                  _const_spec(wb.shape), _const_spec(wo.shape), _const_spec((1, D)),
                  pl.BlockSpec((1, M, 2 * D), lambda b, i: (b, 0, 0)),
                  _const_spec((1, D)), _const_spec(wq.shape), _const_spec(mwo.shape), _const_spec((1, D))],
        out_specs=xspec, out_shape=jax.ShapeDtypeStruct(x.shape, x.dtype),
        compiler_params=_cparams(("parallel", "parallel")), name="merge_mem",
    )(x, a_t, b_t, c_t, gates, wb, wo, g, kv, gpre, wq, mwo, gpost)


def _swiglu_chunk(hn, wg, wu, wd):
    gate = _dot(hn, wg)
    act = (gate * _sigmoid(gate) * _dot(hn, wu)).astype(BF16)
    return _dot(act, wd)


def _swiglu(hn, wg_ref, wu_ref, wd_ref, tf):
    y = None
    for lo in range(0, wg_ref.shape[-1], tf):
        part = _swiglu_chunk(hn, wg_ref[:, lo:lo + tf], wu_ref[:, lo:lo + tf], wd_ref[lo:lo + tf, :])
        y = part if y is None else y + part
    return y


def _ffn_kernel(x_ref, gpre_ref, wg_ref, wu_ref, wd_ref, gpost_ref, o_ref):
    x = x_ref[...]
    hn = _rms_rows(x, gpre_ref[...]).astype(BF16)
    o_ref[...] = x + _rms_rows(_swiglu(hn, wg_ref, wu_ref, wd_ref, FFN_TF), gpost_ref[...])


def _ffn(x2d, gpre, wg, wu, wd, gpost):
    T, D = x2d.shape
    tm = min(FFN_TM, T)
    xspec = pl.BlockSpec((tm, D), lambda i: (i, 0))
    return pl.pallas_call(
        _ffn_kernel, grid=(T // tm,),
        in_specs=[xspec, _const_spec((1, D)), _const_spec(wg.shape), _const_spec(wu.shape),
                  _const_spec(wd.shape), _const_spec((1, D))],
        out_specs=xspec, out_shape=jax.ShapeDtypeStruct(x2d.shape, x2d.dtype),
        compiler_params=_cparams(("parallel",)), name="ffn",
    )(x2d, gpre, wg, wu, wd, gpost)


def _router_kernel(x_ref, gpre_ref, wr_ref, br_ref, tri_ref,
                   hn_ref, pos_ref, gate_col_ref, pos_col_ref, cnt_ref, run_ref):
    @pl.when(pl.program_id(0) == 0)
    def _():
        run_ref[...] = jnp.zeros_like(run_ref)

    hn = _rms_rows(x_ref[...], gpre_ref[...])
    hn_ref[...] = hn.astype(hn_ref.dtype)
    logits = lax.dot_general(wr_ref[...], hn, (((1,), (1,)), ((), ())),
                             precision=lax.Precision.HIGHEST,
                             preferred_element_type=F32) + br_ref[...]
    e_idx = lax.broadcasted_iota(jnp.int32, logits.shape, 0)
    m1 = jnp.max(logits, axis=0, keepdims=True)
    i1 = jnp.min(jnp.where(logits == m1, e_idx, N_EXPERTS), axis=0, keepdims=True)
    rest = jnp.where(e_idx == i1, -jnp.inf, logits)
    m2 = jnp.max(rest, axis=0, keepdims=True)
    i2 = jnp.min(jnp.where(rest == m2, e_idx, N_EXPERTS), axis=0, keepdims=True)
    e2 = jnp.exp(m2 - m1)
    w1 = 1.0 / (1.0 + e2)
    w2 = e2 / (1.0 + e2)
    gates = jnp.where(e_idx == i1, w1, 0.0) + jnp.where(e_idx == i2, w2, 0.0)

    sel = (e_idx == i1) | (e_idx == i2)
    sel_f = jnp.where(sel, 1.0, 0.0)
    before = _dot(sel_f.astype(BF16), tri_ref[...])
    run = run_ref[:, 0:1]
    pos = jnp.where(sel, run + before, -1.0)
    pos_ref[...] = pos
    tm = pos.shape[1]
    cols = jnp.concatenate([gates, pos, jnp.zeros((V7X_LANES - 2 * N_EXPERTS, tm), F32)], axis=0).T
    for e in range(N_EXPERTS):
        gate_col_ref[e] = cols[:, e:e + 1]
        pos_col_ref[e] = cols[:, N_EXPERTS + e:N_EXPERTS + e + 1]
    cnt = jnp.sum(sel_f, axis=1, keepdims=True)
    cnt_ref[0] = cnt.astype(jnp.int32)
    run_ref[...] = run_ref[...] + cnt


def _router(x2d, gpre, wr_t, br_col):
    T, D = x2d.shape
    tm = MOE_BLOCK
    nb = T // tm
    tri = (np.arange(tm)[:, None] < np.arange(tm)[None, :]).astype(np.float32)
    col = pl.BlockSpec((N_EXPERTS, tm, 1), lambda i: (0, i, 0))
    return pl.pallas_call(
        _router_kernel, grid=(nb,),
        in_specs=[pl.BlockSpec((tm, D), lambda i: (i, 0)), _const_spec((1, D)),
                  _const_spec(wr_t.shape), _const_spec(br_col.shape), _const_spec((tm, tm))],
        out_specs=[pl.BlockSpec((tm, D), lambda i: (i, 0)),
                   pl.BlockSpec((N_EXPERTS, tm), lambda i: (0, i)), col, col,
                   pl.BlockSpec((1, N_EXPERTS, 1), lambda i: (i, 0, 0))],
        out_shape=[jax.ShapeDtypeStruct((T, D), BF16),
                   jax.ShapeDtypeStruct((N_EXPERTS, T), F32),
                   jax.ShapeDtypeStruct((N_EXPERTS, T, 1), F32),
                   jax.ShapeDtypeStruct((N_EXPERTS, T, 1), F32),
                   jax.ShapeDtypeStruct((nb, N_EXPERTS, 1), jnp.int32)],
        scratch_shapes=[pltpu.VMEM((N_EXPERTS, V7X_LANES), F32)],
        compiler_params=_cparams(("arbitrary",)), name="router",
    )(x2d, gpre, wr_t, br_col, jnp.asarray(tri, BF16))


_FIRST, _LAST = 1, 2
_HALF = (4, 8)


def _moe_schedule_kernel(cnt_ref, ea_ref, ba_ref, ta_ref, ra_ref, fa_ref, eb_ref, bb_ref, tb_ref, rb_ref, fb_ref,
                         tile0_ref, start_ref, *, nb, ne, gb):
    tr = MOE_BLOCK
    shift = tr.bit_length() - 1
    assert tr == 1 << shift
    g_max = ea_ref.shape[0]

    def first_tiles(e, tile):
        rows = lax.fori_loop(0, nb, lambda b, acc: acc + cnt_ref[b * ne + e], 0)
        tile0_ref[e] = tile
        start_ref[e] = 0
        return tile + ((rows + tr - 1) >> shift)

    lax.fori_loop(0, ne, first_tiles, 0)

    def append(lists, by_tile, e, b, start, c, carry):
        e_ref, b_ref, t_ref, r_ref, f_ref = lists
        first = start >> shift
        n = jnp.where(c > 0, ((start + c - 1) >> shift) - first + 1, 0)

        def visit(j, carry):
            g, prev_group = carry
            local = first + j
            tile = tile0_ref[e] + local
            group = tile if by_tile else b
            lo = start - (local << shift)
            opens = group != prev_group

            @pl.when(opens & (g > 0))
            def _():
                f_ref[g - 1] = f_ref[g - 1] | _LAST

            e_ref[g] = e
            b_ref[g] = b
            t_ref[g] = tile
            r_ref[g] = local << shift
            f_ref[g] = (jnp.where(opens, _FIRST, 0) | jnp.where(lo < tr // 2, _HALF[0], 0)
                        | jnp.where(lo + c > tr // 2, _HALF[1], 0))
            return g + 1, group

        return lax.fori_loop(0, n, visit, carry)

    def finish(lists, g):
        e_ref, b_ref, t_ref, r_ref, f_ref = lists
        f_ref[g - 1] = f_ref[g - 1] | _LAST

        def pad(k, _):
            e_ref[k] = e_ref[g - 1]
            b_ref[k] = b_ref[g - 1]
            t_ref[k] = t_ref[g - 1]
            r_ref[k] = r_ref[g - 1]
            f_ref[k] = 0
            return 0

        lax.fori_loop(g, g_max, pad, 0)

    list_a = (ea_ref, ba_ref, ta_ref, ra_ref, fa_ref)
    list_b = (eb_ref, bb_ref, tb_ref, rb_ref, fb_ref)

    def expert_major(e, carry):
        def block(b, inner):
            start, visits = inner
            c = cnt_ref[b * ne + e]
            return start + c, append(list_a, True, e, b, start, c, visits)
        return lax.fori_loop(0, nb, block, (0, carry))[1]

    g_a, _ = lax.fori_loop(0, ne, expert_major, (0, -1))
    finish(list_a, g_a)

    def block_major(b, carry):
        def expert(e, visits):
            start = start_ref[e]
            c = sum(cnt_ref[(b * gb + j) * ne + e] for j in range(gb))
            start_ref[e] = start + c
            return append(list_b, False, e, b, start, c, visits)
        return lax.fori_loop(0, ne, expert, carry)

    g_b, _ = lax.fori_loop(0, nb // gb, block_major, (0, -1))
    finish(list_b, g_b)


def _moe_schedule(cnt, nb, ne, gb, g_max):
    smem = pl.BlockSpec(memory_space=pltpu.SMEM)
    out = jax.ShapeDtypeStruct((g_max,), jnp.int32)
    lists = pl.pallas_call(
        functools.partial(_moe_schedule_kernel, nb=nb, ne=ne, gb=gb),
        in_specs=[smem], out_specs=[smem] * 10, out_shape=[out] * 10,
        scratch_shapes=[pltpu.SMEM((ne,), jnp.int32), pltpu.SMEM((ne,), jnp.int32)],
        name="moe_schedule",
    )(cnt)
    return lists[:5], lists[5:]


def _moe_expert_kernel(e_ref, b_ref, t_ref, r_ref, f_ref, hn_ref, pos_ref, wg_ref, wu_ref, wd_ref,
                       ys_ref, xs_ref):
    g = pl.program_id(0)
    flags = f_ref[g]
    tr = xs_ref.shape[0]

    @pl.when((flags & _FIRST) != 0)
    def _():
        xs_ref[...] = jnp.zeros_like(xs_ref)

    half = tr // 2
    for h in range(2):
        @pl.when((flags & _HALF[h]) != 0)
        def _():
            rows = (lax.broadcasted_iota(jnp.int32, (half, 1), 0) + (r_ref[g] + h * half)).astype(F32)
            onehot = jnp.where(rows == pos_ref[0, 0], 1.0, 0.0).astype(BF16)
            xs_ref[h * half:(h + 1) * half, :] += _dot(onehot, hn_ref[...])

    @pl.when((flags & _LAST) != 0)
    def _():
        y = _swiglu(xs_ref[...].astype(BF16), wg_ref.at[0], wu_ref.at[0], wd_ref.at[0], FFN_TF)
        ys_ref[...] = y.astype(ys_ref.dtype)


def _moe_expert(list_a, hn, pos, wg, wu, wd, n_tiles):
    T, D = hn.shape
    E, _, F = wg.shape
    tr = MOE_BLOCK
    g_max = list_a[0].shape[0]
    pos4 = pos.reshape(E, T // tr, 1, tr)
    wspec_in = pl.BlockSpec((1, D, F), lambda g, e, b, t, r, f: (e[g], 0, 0), pipeline_mode=pl.Buffered(1))
    wspec_out = pl.BlockSpec((1, F, D), lambda g, e, b, t, r, f: (e[g], 0, 0), pipeline_mode=pl.Buffered(1))
    grid_spec = pltpu.PrefetchScalarGridSpec(
        num_scalar_prefetch=5, grid=(g_max,),
        in_specs=[pl.BlockSpec((tr, D), lambda g, e, b, t, r, f: (b[g], 0)),
                  pl.BlockSpec((1, 1, 1, tr), lambda g, e, b, t, r, f: (e[g], b[g], 0, 0)),
                  wspec_in, wspec_in, wspec_out],
        out_specs=pl.BlockSpec((tr, D), lambda g, e, b, t, r, f: (t[g], 0)),
        scratch_shapes=[pltpu.VMEM((tr, D), F32)])
    return pl.pallas_call(
        _moe_expert_kernel, grid_spec=grid_spec,
        out_shape=jax.ShapeDtypeStruct((n_tiles * tr, D), BF16),
        compiler_params=_cparams(("arbitrary",)), name="moe_expert",
    )(*list_a, hn, pos4, wg, wu, wd)


def _moe_combine_kernel(e_ref, b_ref, t_ref, r_ref, f_ref, x_ref, ys_ref, pos_ref, gate_ref, gpost_ref,
                        o_ref, acc_ref):
    g = pl.program_id(0)
    flags = f_ref[g]
    tr = ys_ref.shape[0]

    @pl.when((flags & _FIRST) != 0)
    def _():
        acc_ref[...] = jnp.zeros_like(acc_ref)

    half = tr // 2
    for h in range(2):
        @pl.when((flags & _HALF[h]) != 0)
        def _():
            rows = (lax.broadcasted_iota(jnp.int32, (1, half), 1) + (r_ref[g] + h * half)).astype(F32)
            onehot = jnp.where(pos_ref[0] == rows, 1.0, 0.0).astype(BF16)
            acc_ref[...] += gate_ref[0] * _dot(onehot, ys_ref[h * half:(h + 1) * half, :])

    @pl.when((flags & _LAST) != 0)
    def _():
        o_ref[...] = x_ref[...] + _rms_rows(acc_ref[...], gpost_ref[...])


def _moe_combine(list_b, x2d, ys, pos_col, gate_col, gpost, gb):
    T, D = x2d.shape
    tr = MOE_BLOCK
    tc = gb * MOE_BLOCK
    g_max = list_b[0].shape[0]
    xspec = pl.BlockSpec((tc, D), lambda g, e, b, t, r, f: (b[g], 0))
    colspec = pl.BlockSpec((1, tc, 1), lambda g, e, b, t, r, f: (e[g], b[g], 0))
    grid_spec = pltpu.PrefetchScalarGridSpec(
        num_scalar_prefetch=5, grid=(g_max,),
        in_specs=[xspec, pl.BlockSpec((tr, D), lambda g, e, b, t, r, f: (t[g], 0)), colspec, colspec,
                  pl.BlockSpec((1, D), lambda g, e, b, t, r, f: (0, 0))],
        out_specs=xspec,
        scratch_shapes=[pltpu.VMEM((tc, D), F32)])
    return pl.pallas_call(
        _moe_combine_kernel, grid_spec=grid_spec,
        out_shape=jax.ShapeDtypeStruct(x2d.shape, x2d.dtype),
        compiler_params=_cparams(("arbitrary",)), name="moe_combine",
    )(*list_b, x2d, ys, pos_col, gate_col, gpost)


def _moe(x2d, gpre, wr_t, br_col, wg, wu, wd, gpost):
    T, D = x2d.shape
    E = wg.shape[0]
    nb = T // MOE_BLOCK
    n_tiles = TOP_K * nb + E
    g_max = E * nb + n_tiles
    hn, pos, gate_col, pos_col, cnt = _router(x2d, gpre, wr_t, br_col)
    gb = MOE_COMBINE_BLOCKS if nb % MOE_COMBINE_BLOCKS == 0 else 1
    list_a, list_b = _moe_schedule(cnt.reshape(nb * E), nb, E, gb, g_max)
    ys = _moe_expert(list_a, hn, pos, wg, wu, wd, n_tiles)
    return _moe_combine(list_b, x2d, ys, pos_col, gate_col, gpost, gb)


def _feature_row_order():
    def diff_block(base):
        x1, x2, rest = [], [], []
        half = DIFF_ROPE_DIM // 2
        for h in range(DIFF_HEADS):
            for c in range(2):
                lo = base + h * 2 * DIFF_QK_DIM + c * DIFF_QK_DIM
                x1 += range(lo, lo + half)
                x2 += range(lo + half, lo + DIFF_ROPE_DIM)
                rest += range(lo + DIFF_ROPE_DIM, lo + DIFF_QK_DIM)
        return x1 + x2 + rest

    order = list(range(0, _S_DQ)) + diff_block(_S_DQ) + diff_block(_S_DK) + list(range(_S_DV, N_FEAT))
    return np.asarray(order, np.int32)


def _mla_q_up_order():
    w = MLA_NOPE_DIM + MLA_ROPE_DIM
    half = MLA_ROPE_DIM // 2
    nope = [h * w + j for h in range(MLA_HEADS) for j in range(MLA_NOPE_DIM)]
    x1 = [h * w + MLA_NOPE_DIM + j for h in range(MLA_HEADS) for j in range(half)]
    x2 = [h * w + MLA_NOPE_DIM + half + j for h in range(MLA_HEADS) for j in range(half)]
    return np.asarray(nope + x1 + x2, np.int32)


def _mla_kv_up_order():
    w = MLA_NOPE_DIM + MLA_V_DIM
    k = [h * w + j for h in range(MLA_HEADS) for j in range(MLA_NOPE_DIM)]
    v = [h * w + MLA_NOPE_DIM + j for h in range(MLA_HEADS) for j in range(MLA_V_DIM)]
    return np.asarray(k + v, np.int32)


def _rope_tables(S):
    def cos_sin(pos, rot_dim):
        inv = ROPE_THETA ** (-jnp.arange(0, rot_dim, 2, dtype=F32) / rot_dim)
        ang = pos.astype(F32)[:, None] * inv[None, :]
        return jnp.cos(ang).T, jnp.sin(ang).T

    rows = S // GRID_W
    pos = jnp.arange(S)
    row = jnp.repeat(jnp.arange(rows), GRID_W)
    col = jnp.tile(jnp.arange(GRID_W), rows)
    cm, sm = cos_sin(pos, MLA_ROPE_DIM)
    cr, sr = cos_sin(row, GQA_HEAD_DIM // 2)
    cc, sc = cos_sin(col, GQA_HEAD_DIM // 2)
    cp, sp = cos_sin(pos, DIFF_ROPE_DIM)
    return (jnp.tile(cm, (MLA_HEADS, 1)), jnp.tile(sm, (MLA_HEADS, 1)),
            jnp.concatenate([cr, cc], axis=0), jnp.concatenate([sr, sc], axis=0),
            jnp.tile(cp, (2 * DIFF_HEADS, 1)), jnp.tile(sp, (2 * DIFF_HEADS, 1)))


def _row(v):
    return v.reshape(1, -1).astype(F32)


def _col(v):
    return v.reshape(-1, 1).astype(F32)


def kernel(x, mem, mix_pre_g, mix_post_g, w_in, mla_q_norm_g, mla_w_q_up, mla_kv_norm_g, mla_w_kv_up,
           gqa_q_norm_g, gqa_k_norm_g, diff_lambda_q1, diff_lambda_k1, diff_lambda_q2, diff_lambda_k2,
           diff_subln_g, w_branch, w_out, mem_pre_g, mem_post_g, mem_norm_g, mem_wq, mem_wkv, mem_wo,
           ffn_pre_g, ffn_post_g, dense_w_gate, dense_w_up, dense_w_down, moe_w_router, moe_b_router,
           moe_w_gate, moe_w_up, moe_w_down):
    B, S, D = x.shape
    M = mem.shape[1]
    depth = w_in.shape[0]
    tables = _rope_tables(S)
    feat_order = _feature_row_order()
    q_up_order = _mla_q_up_order()
    kv_up_order = _mla_kv_up_order()

    for layer in range(depth):
        lambda_init = 0.8 - 0.6 * math.exp(-0.3 * layer)

        w_l = w_in[layer]
        wt = w_l[:, feat_order].T.astype(BF16)
        wg = w_l[:, N_FEAT:].astype(BF16)
        wqu = mla_w_q_up[layer][:, q_up_order].T.astype(BF16)
        wkvu = mla_w_kv_up[layer][:, kv_up_order].T.astype(BF16)
        (qm, km, vm, qg, kg, vg, qd, kd, vd, gates) = _mixer_prep(
            x, _row(mix_pre_g[layer]), wt, wg, wqu, wkvu,
            _col(mla_q_norm_g[layer]), _col(mla_kv_norm_g[layer]),
            _col(gqa_q_norm_g[layer]), _col(gqa_k_norm_g[layer]), tables)
        a_t = _attention(qm, km, vm, kv_of_head=lambda h: h, kcol_of_head=lambda h: h, name="mla_attn")
        b_t = _attention(qg, kg, vg, kv_of_head=lambda h: h // GQA_GROUP, kcol_of_head=lambda h: 0,
                         name="gqa_attn")
        c_t = _diff_attention(qd, kd, vd, _row(diff_lambda_q1[layer]), _row(diff_lambda_k1[layer]),
                              _row(diff_lambda_q2[layer]), _row(diff_lambda_k2[layer]),
                              _col(diff_subln_g[layer]), lambda_init)
        kv = _norm_matmul(mem.reshape(B * M, D), _row(mem_norm_g[layer]), mem_wkv[layer].astype(BF16), BF16)
        x = _merge_mem(x, a_t, b_t, c_t, gates, w_branch[layer].astype(BF16), w_out[layer].astype(BF16),
                       _row(mix_post_g[layer]), kv.reshape(B, M, 2 * D), _row(mem_pre_g[layer]),
                       mem_wq[layer].astype(BF16), mem_wo[layer].astype(BF16), _row(mem_post_g[layer]))

        x2d = x.reshape(B * S, D)
        i = layer // 2
        if layer % 2 == 0:
            x2d = _ffn(x2d, _row(ffn_pre_g[layer]), dense_w_gate[i].astype(BF16), dense_w_up[i].astype(BF16),
                       dense_w_down[i].astype(BF16), _row(ffn_post_g[layer]))
        else:
            x2d = _moe(x2d, _row(ffn_pre_g[layer]), moe_w_router[i].T.astype(F32), _col(moe_b_router[i]),
                       moe_w_gate[i].astype(BF16), moe_w_up[i].astype(BF16), moe_w_down[i].astype(BF16),
                       _row(ffn_post_g[layer]))
        x = x2d.reshape(B, S, D)
    return x
```
